```python
import math
import jax
import jax.numpy as jnp
from jax import lax
import numpy as np

D_MODEL = 1024
BATCH = 8
SEQ = 2048
DEPTH = 2
DEC_BATCH = 32
DEC_SEQ = 1
PAST_LEN = 8192
PAGE_SIZE = 128

N_EVEN = (DEPTH + 1) // 2
N_ODD = DEPTH // 2
H_A = 8
DK_A = 64
DV_A = 64
CONV_W = 4
CHUNK = 64
H_B = 4
DQK_B = 64
DV_B = 128
QKV_A = H_A * (2 * DK_A + DV_A)
AB_SIZES = (QKV_A, H_A * DV_A, H_A, H_A, H_B * DQK_B, H_B * DQK_B, H_B * DV_B, H_B * DV_B, H_B, H_B)
IN_AB = QKV_A + H_A * DV_A + 2 * H_A + 2 * H_B * DQK_B + 2 * H_B * DV_B + 2 * H_B
MIX_AB = H_A * DV_A + H_B * DV_B
H_C = 8
DH_C = 64
VD_C = 2 * DH_C
C_W = H_C * 2 * DH_C
Q_BLOCK = 128
N_BUCKETS = 32
MAX_DISTANCE = 128
N_MEM = 256
H_X = 4
DH_X = 128
X_W = H_X * DH_X
N_EXPERTS = 32
TOP_K = 4
D_FF = D_MODEL
SWIGLU_ALPHA = 1.702
SWIGLU_LIMIT = 7.0
MOE_BLOCK = 128
DN_ALPHA = (2 * DEPTH) ** 0.25
DN_BETA = (8 * DEPTH) ** -0.25
LN_EPS = 1e-5
NORM_EPS = 1e-6

kernel_name = 'hybrid_deltanet_mlstm_diffattn_moe_step'


def layer_norm(x, g, b):
    xf = x.astype(jnp.float32)
    mu = jnp.mean(xf, axis=-1, keepdims=True)
    var = jnp.mean(jnp.square(xf - mu), axis=-1, keepdims=True)
    return ((xf - mu) * lax.rsqrt(var + LN_EPS) * g.astype(jnp.float32) + b.astype(jnp.float32)).astype(x.dtype)


def rms_norm(x, g):
    xf = x.astype(jnp.float32)
    return xf * lax.rsqrt(jnp.mean(jnp.square(xf), axis=-1, keepdims=True) + NORM_EPS) * g.astype(jnp.float32)


def l2_normalize(x):
    return x * lax.rsqrt(jnp.sum(jnp.square(x), axis=-1, keepdims=True) + NORM_EPS)


def split_cols(x, sizes):
    idx, acc = [], 0
    for s in sizes[:-1]:
        acc += s
        idx.append(acc)
    return jnp.split(x, idx, axis=-1)


def depthwise_causal_conv(u_ext, w):
    return lax.conv_general_dilated(u_ext, w[:, None, :].astype(u_ext.dtype), window_strides=(1,),
                                    padding='VALID', dimension_numbers=('NWC', 'WIO', 'NWC'),
                                    feature_group_count=u_ext.shape[-1])


def to_chunks(t):
    bsz, n = t.shape[:2]
    t = t.reshape((bsz, n // CHUNK, CHUNK) + t.shape[2:])
    return jnp.transpose(t, (0, 3, 1, 2) + tuple(range(4, t.ndim)))


def from_chunks(t):
    bsz, h, nc, l, d = t.shape
    return jnp.transpose(t, (0, 2, 3, 1, 4)).reshape(bsz, nc * l, h, d)


def time_major(t):
    return jnp.moveaxis(t, 1, 0)


def gated_delta_chunked(q, k, v, beta, g, s0):
    L = q.shape[-2]
    dv = v.shape[-1]
    incl = jnp.tril(jnp.ones((L, L), dtype=bool))
    strict = jnp.tril(jnp.ones((L, L), dtype=bool), -1)
    gc = jnp.cumsum(g, axis=-1)
    decay = jnp.where(incl, jnp.exp(jnp.where(incl, gc[..., :, None] - gc[..., None, :], 0.0)), 0.0)
    kk = jnp.einsum('bhcid,bhcjd->bhcij', k, k)
    a_mat = jnp.eye(L, dtype=q.dtype) + jnp.where(strict, beta[..., :, None] * kk * decay, 0.0)
    rhs = jnp.concatenate([v * beta[..., None], k * (beta * jnp.exp(gc))[..., None]], axis=-1)
    sol = lax.linalg.triangular_solve(a_mat, rhs, left_side=True, lower=True)
    u, w = sol[..., :dv], sol[..., dv:]
    qk = jnp.einsum('bhcid,bhcjd->bhcij', q, k) * decay
    q_dec = q * jnp.exp(gc)[..., None]
    k_tail = k * jnp.exp(gc[..., -1:] - gc)[..., None]
    g_tail = jnp.exp(gc[..., -1])

    def step(s, xs):
        u_c, w_c, qk_c, qd_c, kt_c, gt_c = xs
        delta = u_c - jnp.einsum('bhid,bhde->bhie', w_c, s)
        o = jnp.einsum('bhid,bhde->bhie', qd_c, s) + jnp.einsum('bhij,bhje->bhie', qk_c, delta)
        s = s * gt_c[..., None, None] + jnp.einsum('bhid,bhie->bhde', kt_c, delta)
        return s, o

    xs = tuple(jnp.moveaxis(t, 2, 0) for t in (u, w, qk, q_dec, k_tail, g_tail))
    s, o = lax.scan(step, s0, xs)
    return jnp.moveaxis(o, 0, 2), s


def gated_delta_step(s, inp):
    q_t, k_t, v_t, beta_t, g_t = inp
    s = s * jnp.exp(g_t)[..., None, None]
    err = v_t - jnp.einsum('bhd,bhde->bhe', k_t, s)
    s = s + jnp.einsum('bhd,bhe->bhde', k_t, beta_t[..., None] * err)
    return s, jnp.einsum('bhd,bhde->bhe', q_t, s)


def mlstm_chunked(q, k, v, i_pre, logf, c0, n0, m0):
    L = q.shape[-2]
    incl = jnp.tril(jnp.ones((L, L), dtype=bool))
    b = jnp.cumsum(logf, axis=-1)
    dmat = jnp.where(incl, b[..., :, None] - b[..., None, :] + i_pre[..., None, :], -jnp.inf)
    m_intra = jnp.max(dmat, axis=-1)
    w_intra = jnp.exp(dmat - m_intra[..., None]) * jnp.einsum('bhcid,bhcjd->bhcij', q, k)
    num_intra = jnp.einsum('bhcij,bhcje->bhcie', w_intra, v)
    den_intra = jnp.sum(w_intra, axis=-1)
    e_end = b[..., -1:] - b + i_pre
    e_max = jnp.max(e_end, axis=-1)
    w_end = jnp.exp(e_end - e_max[..., None])
    kv_end = jnp.einsum('bhcj,bhcjd,bhcje->bhcde', w_end, k, v)
    k_end = jnp.einsum('bhcj,bhcjd->bhcd', w_end, k)
    b_end = b[..., -1]

    def step(carry, xs):
        c, n, m = carry
        q_c, b_c, mi_c, num_c, den_c, kv_c, ke_c, em_c, be_c = xs
        inter = b_c + m[..., None]
        m_t = jnp.maximum(inter, mi_c)
        s_inter = jnp.exp(inter - m_t)
        s_intra = jnp.exp(mi_c - m_t)
        num = s_inter[..., None] * jnp.einsum('bhid,bhde->bhie', q_c, c) + s_intra[..., None] * num_c
        den = s_inter * jnp.einsum('bhid,bhd->bhi', q_c, n) + s_intra * den_c
        h = num / jnp.maximum(jnp.abs(den), jnp.exp(-m_t))[..., None]
        m_new = jnp.maximum(be_c + m, em_c)
        f_sc = jnp.exp(be_c + m - m_new)
        i_sc = jnp.exp(em_c - m_new)
        c = f_sc[..., None, None] * c + i_sc[..., None, None] * kv_c
        n = f_sc[..., None] * n + i_sc[..., None] * ke_c
        return (c, n, m_new), h

    xs = tuple(jnp.moveaxis(t, 2, 0) for t in (q, b, m_intra, num_intra, den_intra, kv_end, k_end, e_max, b_end))
    (c, n, m), h = lax.scan(step, (c0, n0, m0), xs)
    return jnp.moveaxis(h, 0, 2), (c, n, m)


def mlstm_step(carry, inp):
    c, n, m = carry
    q_t, k_t, v_t, i_t, lf_t = inp
    m_new = jnp.maximum(lf_t + m, i_t)
    f_sc = jnp.exp(lf_t + m - m_new)
    i_sc = jnp.exp(i_t - m_new)
    c = f_sc[..., None, None] * c + i_sc[..., None, None] * jnp.einsum('bhd,bhe->bhde', k_t, v_t)
    n = f_sc[..., None] * n + i_sc[..., None] * k_t
    num = jnp.einsum('bhd,bhde->bhe', q_t, c)
    den = jnp.einsum('bhd,bhd->bh', q_t, n)
    h = num / jnp.maximum(jnp.abs(den), jnp.exp(-m_new))[..., None]
    return (c, n, m_new), h


def ab_project(x, conv_prev, w_in, conv_w, a_log, dt_bias, b_i, b_f):
    f32 = jnp.float32
    bsz, t = x.shape[:2]
    qkv_a, z_a, beta_pre, a_pre, q_b, k_b, v_b, o_b, i_b, f_b = split_cols(x @ w_in, AB_SIZES)
    ext = jnp.concatenate([conv_prev.astype(qkv_a.dtype), qkv_a], axis=1)
    new_conv = ext[:, -(CONV_W - 1):]
    c = jax.nn.silu(depthwise_causal_conv(ext, conv_w).astype(f32))
    q_a, k_a, v_a = split_cols(c, (H_A * DK_A, H_A * DK_A, H_A * DV_A))
    q_a = l2_normalize(q_a.reshape(bsz, t, H_A, DK_A)) * DK_A ** -0.5
    k_a = l2_normalize(k_a.reshape(bsz, t, H_A, DK_A))
    v_a = v_a.reshape(bsz, t, H_A, DV_A)
    beta = jax.nn.sigmoid(beta_pre.astype(f32))
    g = -jnp.exp(a_log.astype(f32)) * jax.nn.softplus(a_pre.astype(f32) + dt_bias.astype(f32))
    q_b = q_b.astype(f32).reshape(bsz, t, H_B, DQK_B)
    k_b = k_b.astype(f32).reshape(bsz, t, H_B, DQK_B) * DQK_B ** -0.5
    v_b = v_b.astype(f32).reshape(bsz, t, H_B, DV_B)
    i_pre = i_b.astype(f32) + b_i.astype(f32)
    logf = jax.nn.log_sigmoid(f_b.astype(f32) + b_f.astype(f32))
    return (q_a, k_a, v_a, beta, g, z_a), (q_b, k_b, v_b, i_pre, logf, o_b), new_conv


def ab_output(o_a, z_a, h_b, o_b, norm_g_a, norm_g_b, w_out):
    bsz, t = o_a.shape[:2]
    out_a = rms_norm(o_a, norm_g_a) * jax.nn.silu(z_a.astype(jnp.float32)).reshape(bsz, t, H_A, DV_A)
    out_b = jax.nn.sigmoid(o_b.astype(jnp.float32)).reshape(bsz, t, H_B, DV_B) * rms_norm(h_b, norm_g_b)
    mix = jnp.concatenate([out_a.reshape(bsz, t, H_A * DV_A), out_b.reshape(bsz, t, H_B * DV_B)], axis=-1)
    return mix.astype(w_out.dtype) @ w_out


def ab_mixer(x, conv_prev, s_prev, c_prev, n_prev, m_prev, w_in, conv_w, a_log, dt_bias, norm_g_a,
             b_i, b_f, norm_g_b, w_out, chunked):
    (q_a, k_a, v_a, beta, g, z_a), (q_b, k_b, v_b, i_pre, logf, o_b), new_conv = ab_project(
        x, conv_prev, w_in, conv_w, a_log, dt_bias, b_i, b_f)
    if chunked:
        o_a, s_new = gated_delta_chunked(to_chunks(q_a), to_chunks(k_a), to_chunks(v_a),
                                         to_chunks(beta), to_chunks(g), s_prev)
        h_b, (c_new, n_new, m_new) = mlstm_chunked(to_chunks(q_b), to_chunks(k_b), to_chunks(v_b),
                                                   to_chunks(i_pre), to_chunks(logf), c_prev, n_prev, m_prev)
        o_a, h_b = from_chunks(o_a), from_chunks(h_b)
    else:
        s_new, o_a = lax.scan(gated_delta_step, s_prev,
                              (time_major(q_a), time_major(k_a), time_major(v_a), time_major(beta), time_major(g)))
        (c_new, n_new, m_new), h_b = lax.scan(mlstm_step, (c_prev, n_prev, m_prev),
                                              (time_major(q_b), time_major(k_b), time_major(v_b),
                                               time_major(i_pre), time_major(logf)))
        o_a, h_b = time_major(o_a), time_major(h_b)
    y = ab_output(o_a, z_a, h_b, o_b, norm_g_a, norm_g_b, w_out)
    return y, s_new, new_conv, c_new, n_new, m_new


def t5_causal_bucket(dist):
    n = jnp.maximum(dist, 0)
    max_exact = N_BUCKETS // 2
    nf = jnp.maximum(n, max_exact).astype(jnp.float32)
    large = max_exact + (jnp.log(nf / max_exact) / math.log(MAX_DISTANCE / max_exact)
                         * (N_BUCKETS - max_exact)).astype(jnp.int32)
    large = jnp.minimum(large, N_BUCKETS - 1)
    return jnp.where(n < max_exact, n, large)


def diff_lambda(lq1, lk1, lq2, lk2, lam_init):
    f32 = jnp.float32
    return (jnp.exp(jnp.sum(lq1.astype(f32) * lk1.astype(f32)))
            - jnp.exp(jnp.sum(lq2.astype(f32) * lk2.astype(f32))) + lam_init)


def diff_qkv(x, w_qkv):
    bsz, t = x.shape[:2]
    q, k, v = split_cols(x @ w_qkv, (C_W, C_W, C_W))
    return (q.reshape(bsz, t, H_C, 2 * DH_C), k.reshape(bsz, t, H_C, 2 * DH_C), v.reshape(bsz, t, H_C, VD_C))


def diff_weights(s, qpos, kpos, rel_bias, lam):
    bias = rel_bias[t5_causal_bucket(qpos[:, None] - kpos[None, :])]
    s = s + jnp.transpose(bias, (2, 0, 1)).astype(jnp.float32)
    s = jnp.where(kpos[None, :] <= qpos[:, None], s, -jnp.inf)
    p = jax.nn.softmax(s, axis=-1)
    return p[0] - lam * p[1]


def diff_out(o, subln_g, lam_init, w_o):
    bsz, t = o.shape[:2]
    o = rms_norm(o, subln_g) * (1.0 - lam_init)
    return o.reshape(bsz, t, C_W).astype(w_o.dtype) @ w_o


def diff_attn_prompt(x, w_qkv, rel_bias, lam, lam_init, subln_g, w_o):
    bsz, t = x.shape[:2]
    q, k, v = diff_qkv(x, w_qkv)
    nqb = t // Q_BLOCK
    q_blocks = jnp.moveaxis(q.reshape(bsz, nqb, Q_BLOCK, H_C, 2, DH_C), 1, 0)
    k5 = k.reshape(bsz, t, H_C, 2, DH_C)
    kpos = jnp.arange(t)

    def block(args):
        qb, bi = args
        qpos = bi * Q_BLOCK + jnp.arange(Q_BLOCK)
        s = jnp.einsum('bqhmd,bkhmd->mbhqk', qb, k5).astype(jnp.float32) * DH_C ** -0.5
        a = diff_weights(s, qpos, kpos, rel_bias, lam)
        return jnp.einsum('bhqk,bkhe->bqhe', a.astype(v.dtype), v)

    o = lax.map(block, (q_blocks, jnp.arange(nqb)))
    o = jnp.moveaxis(o, 0, 1).reshape(bsz, t, H_C, VD_C)
    return diff_out(o, subln_g, lam_init, w_o), k, v


def diff_attn_sample(x, cache_k, cache_v, page_table, j, w_qkv, rel_bias, lam, lam_init, subln_g, w_o):
    bsz, t = x.shape[:2]
    q, k, v = diff_qkv(x, w_qkv)
    past_len = page_table.shape[1] * PAGE_SIZE
    past_k = cache_k[page_table, j].reshape(bsz, past_len, H_C, 2, DH_C)
    past_v = cache_v[page_table, j].reshape(bsz, past_len, H_C, VD_C)
    q5 = q.reshape(bsz, t, H_C, 2, DH_C)
    s_past = jnp.einsum('bqhmd,bkhmd->mbhqk', q5, past_k)
    s_new = jnp.einsum('bqhmd,bkhmd->mbhqk', q5, k.reshape(bsz, t, H_C, 2, DH_C))
    s = jnp.concatenate([s_past, s_new], axis=-1).astype(jnp.float32) * DH_C ** -0.5
    qpos = past_len + jnp.arange(t)
    kpos = jnp.arange(past_len + t)
    a = diff_weights(s, qpos, kpos, rel_bias, lam).astype(v.dtype)
    o = (jnp.einsum('bhqk,bkhe->bqhe', a[..., :past_len], past_v)
         + jnp.einsum('bhqk,bkhe->bqhe', a[..., past_len:], v))
    return diff_out(o, subln_g, lam_init, w_o), k, v


def mem_kv(mem, w_kv):
    bsz, n = mem.shape[:2]
    k, v = split_cols(mem @ w_kv, (X_W, X_W))
    return k.reshape(bsz, n, H_X, DH_X), v.reshape(bsz, n, H_X, DH_X)


def cross_attn(x, mk, mv, w_q, w_o):
    bsz, t = x.shape[:2]
    q = (x @ w_q).reshape(bsz, t, H_X, DH_X)
    s = jnp.einsum('bqhd,bkhd->bhqk', q, mk.astype(q.dtype)).astype(jnp.float32) * DH_X ** -0.5
    p = jax.nn.softmax(s, axis=-1)
    o = jnp.einsum('bhqk,bkhd->bqhd', p.astype(x.dtype), mv.astype(x.dtype)).reshape(bsz, t, X_W)
    return o @ w_o


def clamped_swiglu(h):
    glu, lin = jnp.split(h, 2, axis=-1)
    glu = jnp.minimum(glu, SWIGLU_LIMIT)
    lin = jnp.clip(lin, -SWIGLU_LIMIT, SWIGLU_LIMIT)
    return glu * jax.nn.sigmoid(SWIGLU_ALPHA * glu) * (lin + 1.0)


def moe(x, w_router, b_router, w_in, b_in, w_out, b_out):
    shp = x.shape
    xt = x.reshape(-1, shp[-1])
    t = xt.shape[0]
    logits = (xt @ w_router + b_router).astype(jnp.float32)
    top_val, top_idx = lax.top_k(logits, TOP_K)
    gates = jax.nn.softmax(top_val, axis=-1)
    tk = t * TOP_K
    flat_e = top_idx.reshape(tk)
    flat_tok = jnp.repeat(jnp.arange(t, dtype=jnp.int32), TOP_K)
    flat_g = gates.reshape(tk)
    order = jnp.argsort(flat_e)
    se = flat_e[order]
    counts = jnp.bincount(flat_e, length=N_EXPERTS)
    padded = (counts + MOE_BLOCK - 1) // MOE_BLOCK * MOE_BLOCK
    pad_end = jnp.cumsum(padded)
    pad_start = pad_end - padded
    sort_start = jnp.cumsum(counts) - counts
    dest = pad_start[se] + jnp.arange(tk, dtype=jnp.int32) - sort_start[se]
    n_blocks = -(-tk // MOE_BLOCK) + N_EXPERTS
    n_rows = n_blocks * MOE_BLOCK
    row_tok = jnp.zeros((n_rows,), jnp.int32).at[dest].set(flat_tok[order])
    row_gate = jnp.zeros((n_rows,), jnp.float32).at[dest].set(flat_g[order])
    block_start = jnp.arange(n_blocks, dtype=jnp.int32) * MOE_BLOCK
    block_expert = jnp.minimum(jnp.searchsorted(pad_end, block_start, side='right'), N_EXPERTS - 1)
    xb = xt[row_tok].reshape(n_blocks, MOE_BLOCK, shp[-1])

    def expert_block(args):
        xblk, e = args
        h = clamped_swiglu((xblk @ w_in[e] + b_in[e]).astype(jnp.float32))
        return h.astype(x.dtype) @ w_out[e] + b_out[e]

    yb = lax.map(expert_block, (xb, block_expert)).reshape(n_rows, shp[-1])
    out = jnp.zeros((t, shp[-1]), jnp.float32).at[row_tok].add(yb.astype(jnp.float32) * row_gate[:, None])
    return out.astype(x.dtype).reshape(shp)


def setup_inputs(seed: int = 0) -> dict:
    key = jax.random.key(seed)
    ks = iter(jax.random.split(key, 64))
    f32 = jnp.float32

    def nrm(shape, scale=1.0):
        return jax.random.normal(next(ks), shape, f32) * scale

    def gain(shape):
        return 1.0 + nrm(shape, 0.02)

    n_pages = PAST_LEN // PAGE_SIZE
    n_used = DEC_BATCH * n_pages
    n_pool = n_used + n_used // 4
    page_table = jax.random.permutation(next(ks), n_pool)[:n_used].reshape(DEC_BATCH, n_pages).astype(jnp.int32)

    x_prompt = nrm((BATCH, SEQ, D_MODEL))
    x_sample = nrm((DEC_BATCH, DEC_SEQ, D_MODEL))
    state_delta_S = nrm((N_EVEN, DEC_BATCH, H_A, DK_A, DV_A), 0.3)
    state_delta_conv = nrm((N_EVEN, DEC_BATCH, CONV_W - 1, QKV_A))
    state_mlstm_C = nrm((N_EVEN, DEC_BATCH, H_B, DQK_B, DV_B), 0.1)
    state_mlstm_n = nrm((N_EVEN, DEC_BATCH, H_B, DQK_B), 0.1)
    state_mlstm_m = nrm((N_EVEN, DEC_BATCH, H_B))
    cache_diff_k = nrm((n_pool, N_ODD, PAGE_SIZE, H_C, 2 * DH_C))
    cache_diff_v = nrm((n_pool, N_ODD, PAGE_SIZE, H_C, VD_C))
    cache_mem_k = nrm((DEPTH, DEC_BATCH, N_MEM, H_X, DH_X))
    cache_mem_v = nrm((DEPTH, DEC_BATCH, N_MEM, H_X, DH_X))
    mem_prompt = nrm((BATCH, N_MEM, D_MODEL))

    w_in_ab = nrm((N_EVEN, D_MODEL, IN_AB), D_MODEL ** -0.5)
    conv_w_a = nrm((N_EVEN, CONV_W, QKV_A), CONV_W ** -0.5)
    a_log_a = jnp.log(jax.random.uniform(next(ks), (N_EVEN, H_A), f32, 1.0, 16.0))
    dt = jnp.exp(jax.random.uniform(next(ks), (N_EVEN, H_A), f32, math.log(1e-3), math.log(1e-1)))
    dt_bias_a = dt + jnp.log(-jnp.expm1(-dt))
    norm_g_a = gain((N_EVEN, DV_A))
    b_i_b = nrm((N_EVEN, H_B), 0.1)
    b_f_b = jax.random.uniform(next(ks), (N_EVEN, H_B), f32, 3.0, 6.0)
    norm_g_b = gain((N_EVEN, DV_B))
    w_out_ab = nrm((N_EVEN, MIX_AB, D_MODEL), MIX_AB ** -0.5 * DN_BETA)

    w_qkv_c = nrm((N_ODD, D_MODEL, 3 * C_W), D_MODEL ** -0.5)
    lam_q1 = nrm((N_ODD, DH_C), 0.1)
    lam_k1 = nrm((N_ODD, DH_C), 0.1)
    lam_q2 = nrm((N_ODD, DH_C), 0.1)
    lam_k2 = nrm((N_ODD, DH_C), 0.1)
    subln_g_c = gain((N_ODD, VD_C))
    w_o_c = nrm((N_ODD, C_W, D_MODEL), C_W ** -0.5 * DN_BETA)
    rel_bias = nrm((N_BUCKETS, H_C), 0.5)

    w_xq = nrm((DEPTH, D_MODEL, X_W), D_MODEL ** -0.5)
    w_xkv = nrm((DEPTH, D_MODEL, 2 * X_W), D_MODEL ** -0.5)
    w_xo = nrm((DEPTH, X_W, D_MODEL), X_W ** -0.5 * DN_BETA)
    ln_g = gain((DEPTH, 3, D_MODEL))
    ln_b = nrm((DEPTH, 3, D_MODEL), 0.02)
    w_router = nrm((DEPTH, D_MODEL, N_EXPERTS), D_MODEL ** -0.5)
    b_router = nrm((DEPTH, N_EXPERTS), 0.01)
    w_moe_in = nrm((DEPTH, N_EXPERTS, D_MODEL, 2 * D_FF), D_MODEL ** -0.5)
    b_moe_in = nrm((DEPTH, N_EXPERTS, 2 * D_FF), 0.02)
    w_moe_out = nrm((DEPTH, N_EXPERTS, D_FF, D_MODEL), D_FF ** -0.5 * DN_BETA)
    b_moe_out = nrm((DEPTH, N_EXPERTS, D_MODEL), 0.02)

    return {'x_prompt': x_prompt, 'x_sample': x_sample,
            'state_delta_S': state_delta_S, 'state_delta_conv': state_delta_conv,
            'state_mlstm_C': state_mlstm_C, 'state_mlstm_n': state_mlstm_n, 'state_mlstm_m': state_mlstm_m,
            'cache_diff_k': cache_diff_k, 'cache_diff_v': cache_diff_v,
            'cache_mem_k': cache_mem_k, 'cache_mem_v': cache_mem_v,
            'page_table': page_table, 'mem_prompt': mem_prompt,
            'w_in_ab': w_in_ab, 'conv_w_a': conv_w_a, 'a_log_a': a_log_a, 'dt_bias_a': dt_bias_a,
            'norm_g_a': norm_g_a, 'b_i_b': b_i_b, 'b_f_b': b_f_b, 'norm_g_b': norm_g_b, 'w_out_ab': w_out_ab,
            'w_qkv_c': w_qkv_c, 'lam_q1': lam_q1, 'lam_k1': lam_k1, 'lam_q2': lam_q2, 'lam_k2': lam_k2,
            'subln_g_c': subln_g_c, 'w_o_c': w_o_c, 'rel_bias': rel_bias,
            'w_xq': w_xq, 'w_xkv': w_xkv, 'w_xo': w_xo, 'ln_g': ln_g, 'ln_b': ln_b,
            'w_router': w_router, 'b_router': b_router, 'w_moe_in': w_moe_in, 'b_moe_in': b_moe_in,
            'w_moe_out': w_moe_out, 'b_moe_out': b_moe_out}


def reference(x_prompt, x_sample, state_delta_S, state_delta_conv, state_mlstm_C, state_mlstm_n, state_mlstm_m,
              cache_diff_k, cache_diff_v, cache_mem_k, cache_mem_v, page_table, mem_prompt,
              w_in_ab, conv_w_a, a_log_a, dt_bias_a, norm_g_a, b_i_b, b_f_b, norm_g_b, w_out_ab,
              w_qkv_c, lam_q1, lam_k1, lam_q2, lam_k2, subln_g_c, w_o_c, rel_bias,
              w_xq, w_xkv, w_xo, ln_g, ln_b, w_router, b_router, w_moe_in, b_moe_in, w_moe_out, b_moe_out):
    f32 = jnp.float32
    xp, xs = x_prompt, x_sample
    bp = xp.shape[0]
    p_S, p_conv, p_C, p_n, p_m, p_k, p_v, p_mk, p_mv = [], [], [], [], [], [], [], [], []
    s_S, s_conv, s_C, s_n, s_m, s_k, s_v = [], [], [], [], [], [], []
    for layer in range(DEPTH):
        j = layer // 2
        if layer % 2 == 0:
            hp, st_S, st_conv, st_C, st_n, st_m = ab_mixer(
                xp, jnp.zeros((bp, CONV_W - 1, QKV_A), xp.dtype), jnp.zeros((bp, H_A, DK_A, DV_A), f32),
                jnp.zeros((bp, H_B, DQK_B, DV_B), f32), jnp.zeros((bp, H_B, DQK_B), f32), jnp.zeros((bp, H_B), f32),
                w_in_ab[j], conv_w_a[j], a_log_a[j], dt_bias_a[j], norm_g_a[j], b_i_b[j], b_f_b[j], norm_g_b[j],
                w_out_ab[j], True)
            p_S.append(st_S); p_conv.append(st_conv); p_C.append(st_C); p_n.append(st_n); p_m.append(st_m)
            hs, st_S, st_conv, st_C, st_n, st_m = ab_mixer(
                xs, state_delta_conv[j], state_delta_S[j].astype(f32), state_mlstm_C[j].astype(f32),
                state_mlstm_n[j].astype(f32), state_mlstm_m[j].astype(f32),
                w_in_ab[j], conv_w_a[j], a_log_a[j], dt_bias_a[j], norm_g_a[j], b_i_b[j], b_f_b[j], norm_g_b[j],
                w_out_ab[j], False)
            s_S.append(st_S); s_conv.append(st_conv); s_C.append(st_C); s_n.append(st_n); s_m.append(st_m)
        else:
            lam_init = 0.8 - 0.6 * math.exp(-0.3 * layer)
            lam = diff_lambda(lam_q1[j], lam_k1[j], lam_q2[j], lam_k2[j], lam_init)
            hp, kp, vp = diff_attn_prompt(xp, w_qkv_c[j], rel_bias, lam, lam_init, subln_g_c[j], w_o_c[j])
            p_k.append(kp); p_v.append(vp)
            hs, kn, vn = diff_attn_sample(xs, cache_diff_k, cache_diff_v, page_table, j, w_qkv_c[j], rel_bias,
                                          lam, lam_init, subln_g_c[j], w_o_c[j])
            s_k.append(kn); s_v.append(vn)
        xp = layer_norm(DN_ALPHA * xp + hp, ln_g[layer, 0], ln_b[layer, 0])
        xs = layer_norm(DN_ALPHA * xs + hs, ln_g[layer, 0], ln_b[layer, 0])
        mk, mv = mem_kv(mem_prompt, w_xkv[layer])
        p_mk.append(mk); p_mv.append(mv)
        xp = layer_norm(DN_ALPHA * xp + cross_attn(xp, mk, mv, w_xq[layer], w_xo[layer]),
                        ln_g[layer, 1], ln_b[layer, 1])
        xs = layer_norm(DN_ALPHA * xs + cross_attn(xs, cache_mem_k[layer], cache_mem_v[layer], w_xq[layer], w_xo[layer]),
                        ln_g[layer, 1], ln_b[layer, 1])
        moe_w = (w_router[layer], b_router[layer], w_moe_in[layer], b_moe_in[layer], w_moe_out[layer], b_moe_out[layer])
        xp = layer_norm(DN_ALPHA * xp + moe(xp, *moe_w), ln_g[layer, 2], ln_b[layer, 2])
        xs = layer_norm(DN_ALPHA * xs + moe(xs, *moe_w), ln_g[layer, 2], ln_b[layer, 2])

    prompt_delta_S = jnp.stack(p_S)
    prompt_delta_conv = jnp.stack(p_conv)
    prompt_mlstm_C = jnp.stack(p_C)
    prompt_mlstm_n = jnp.stack(p_n)
    prompt_mlstm_m = jnp.stack(p_m)
    prompt_diff_k = jnp.stack(p_k, axis=1)
    prompt_diff_v = jnp.stack(p_v, axis=1)
    prompt_mem_k = jnp.stack(p_mk)
    prompt_mem_v = jnp.stack(p_mv)
    sample_delta_S = jnp.stack(s_S)
    sample_delta_conv = jnp.stack(s_conv)
    sample_mlstm_C = jnp.stack(s_C)
    sample_mlstm_n = jnp.stack(s_n)
    sample_mlstm_m = jnp.stack(s_m)
    sample_diff_k = jnp.stack(s_k, axis=1)
    sample_diff_v = jnp.stack(s_v, axis=1)
    return (xp, xs, prompt_delta_S, prompt_delta_conv, prompt_mlstm_C, prompt_mlstm_n, prompt_mlstm_m,
            prompt_diff_k, prompt_diff_v, prompt_mem_k, prompt_mem_v,
            sample_delta_S, sample_delta_conv, sample_mlstm_C, sample_mlstm_n, sample_mlstm_m,
            sample_diff_k, sample_diff_v)
```

```python
import functools
import math

import jax
import jax.numpy as jnp
from jax import lax
from jax.experimental import pallas as pl
from jax.experimental.pallas import tpu as pltpu

F32 = jnp.float32
BF16 = jnp.bfloat16
HI = lax.Precision.HIGHEST

D_MODEL = 1024
DEPTH = 2
H_A, DK_A, DV_A, CONV_W, CHUNK = 8, 64, 64, 4, 64
H_B, DQK_B, DV_B = 4, 64, 128
QKV_A = H_A * (2 * DK_A + DV_A)
H_C, DH_C = 8, 64
VD_C = 2 * DH_C
C_W = H_C * 2 * DH_C
N_BUCKETS, MAX_DISTANCE = 32, 128
N_MEM, H_X, DH_X = 256, 4, 128
X_W = H_X * DH_X
N_EXPERTS, TOP_K = 32, 4
D_FF = D_MODEL
SWIGLU_ALPHA, SWIGLU_LIMIT = 1.702, 7.0
DN_ALPHA = (2 * DEPTH) ** 0.25
LN_EPS = 1e-5
NORM_EPS = 1e-6

LANES = 128
VMEM_LIMIT = 48 * 1024 * 1024


def _cparams(*sem):
    return pltpu.CompilerParams(dimension_semantics=tuple(sem), vmem_limit_bytes=VMEM_LIMIT)


def _layer_norm(y, g, b):
    mu = jnp.mean(y, axis=-1, keepdims=True)
    d = y - mu
    var = jnp.mean(d * d, axis=-1, keepdims=True)
    return d * lax.rsqrt(var + LN_EPS) * g + b


def _mm_multi_kernel(x_ref, *refs, n_lo, n_hi):
    n = n_lo + n_hi
    ws, outs = refs[:n], refs[n:]
    x = x_ref[...]
    xb = x.astype(BF16)
    for i in range(n_lo):
        outs[i][...] = jnp.dot(xb, ws[i][...], preferred_element_type=F32)
    for i in range(n_lo, n):
        outs[i][...] = jnp.dot(x, ws[i][...], preferred_element_type=F32, precision=HI)


def mm_multi(x, w_lo, w_hi=(), tm=256):
    m, k = x.shape
    tm = min(tm, m)
    assert m % tm == 0
    ws = tuple(w_lo) + tuple(w_hi)
    in_specs = [pl.BlockSpec((tm, k), lambda i: (i, 0))]
    in_specs += [pl.BlockSpec(w.shape, lambda i: (0, 0)) for w in ws]
    out_specs = [pl.BlockSpec((tm, w.shape[1]), lambda i: (i, 0)) for w in ws]
    out_shape = [jax.ShapeDtypeStruct((m, w.shape[1]), F32) for w in ws]
    return pl.pallas_call(
        functools.partial(_mm_multi_kernel, n_lo=len(w_lo), n_hi=len(w_hi)),
        grid=(m // tm,), in_specs=in_specs, out_specs=out_specs, out_shape=out_shape,
        compiler_params=_cparams("parallel"), name="mm_multi")(x, *ws)


def _proj_ln_kernel(*refs, n):
    a_refs, w_refs = refs[:n], refs[n:2 * n]
    x_ref, g_ref, b_ref, o_ref = refs[2 * n:]
    h = jnp.dot(a_refs[0][...].astype(BF16), w_refs[0][...], preferred_element_type=F32)
    for a_ref, w_ref in zip(a_refs[1:], w_refs[1:]):
        h = h + jnp.dot(a_ref[...].astype(BF16), w_ref[...], preferred_element_type=F32)
    o_ref[...] = _layer_norm(DN_ALPHA * x_ref[...] + h, g_ref[...], b_ref[...])


def proj_ln(a_list, w_list, x_res, g, b, tm=512):
    m, d = x_res.shape
    tm = min(tm, m)
    assert m % tm == 0
    n = len(a_list)
    return pl.pallas_call(
        functools.partial(_proj_ln_kernel, n=n), grid=(m // tm,),
        in_specs=[pl.BlockSpec((tm, a.shape[1]), lambda i: (i, 0)) for a in a_list]
        + [pl.BlockSpec(w.shape, lambda i: (0, 0)) for w in w_list]
        + [pl.BlockSpec((tm, d), lambda i: (i, 0)), pl.BlockSpec((1, d), lambda i: (0, 0)),
           pl.BlockSpec((1, d), lambda i: (0, 0))],
        out_specs=pl.BlockSpec((tm, d), lambda i: (i, 0)),
        out_shape=jax.ShapeDtypeStruct((m, d), F32),
        compiler_params=_cparams("parallel"), name="proj_ln")(*a_list, *w_list, x_res, g, b)


def _mem_attention(q, mk, mv):
    outs = []
    for h in range(H_X):
        sl = slice(h * DH_X, (h + 1) * DH_X)
        s = lax.dot_general(q[:, sl].astype(BF16), mk[:, sl].astype(BF16), (((1,), (1,)), ((), ())),
                            preferred_element_type=F32) * (DH_X ** -0.5)
        s = s - jnp.max(s, axis=-1, keepdims=True)
        p = jnp.exp(s)
        p = p / jnp.sum(p, axis=-1, keepdims=True)
        outs.append(jnp.dot(p.astype(BF16), mv[:, sl].astype(BF16), preferred_element_type=F32))
    return jnp.concatenate(outs, axis=-1)


def _route(x, wr, br):
    logits = jnp.dot(x, wr, preferred_element_type=F32, precision=HI) + br
    lane = lax.broadcasted_iota(jnp.int32, logits.shape, 1)
    work = jnp.where(lane < N_EXPERTS, logits, -jnp.inf)
    idx_out = jnp.zeros(logits.shape, jnp.int32)
    val_out = jnp.full(logits.shape, -jnp.inf, F32)
    for k in range(TOP_K):
        m = jnp.max(work, axis=-1, keepdims=True)
        sel = jnp.min(jnp.where(work == m, lane, LANES), axis=-1, keepdims=True)
        idx_out = jnp.where(lane == k, sel, idx_out)
        val_out = jnp.where(lane == k, m, val_out)
        work = jnp.where(lane == sel, -jnp.inf, work)
    e = jnp.exp(val_out - jnp.max(val_out, axis=-1, keepdims=True))
    gates = e / jnp.sum(e, axis=-1, keepdims=True)
    return idx_out, gates


def _xattn_prompt_kernel(x_ref, mk_ref, mv_ref, wq_ref, wo_ref, g_ref, b_ref, wr_ref, br_ref,
                         o_ref, idx_ref, gate_ref):
    x = x_ref[...]
    q = jnp.dot(x.astype(BF16), wq_ref[...], preferred_element_type=F32)
    o = _mem_attention(q, mk_ref[0], mv_ref[0])
    h = jnp.dot(o.astype(BF16), wo_ref[...], preferred_element_type=F32)
    y = _layer_norm(DN_ALPHA * x + h, g_ref[...], b_ref[...])
    o_ref[...] = y
    idx, gates = _route(y, wr_ref[...], br_ref[...])
    idx_ref[...] = idx
    gate_ref[...] = gates


def xattn_prompt(x, mk, mv, wq, wo, g, b, wr, br, tq=512):
    m, d = x.shape
    bsz = mk.shape[0]
    t = m // bsz
    nq = t // tq
    full = lambda shape: pl.BlockSpec(shape, lambda bi, qi: (0,) * len(shape))
    row = lambda w: pl.BlockSpec((tq, w), lambda bi, qi: (bi * nq + qi, 0))
    return pl.pallas_call(
        _xattn_prompt_kernel, grid=(bsz, nq),
        in_specs=[row(d), pl.BlockSpec((1, N_MEM, X_W), lambda bi, qi: (bi, 0, 0)),
                  pl.BlockSpec((1, N_MEM, X_W), lambda bi, qi: (bi, 0, 0)),
                  full(wq.shape), full(wo.shape), full(g.shape), full(b.shape), full(wr.shape), full(br.shape)],
        out_specs=[row(d), row(LANES), row(LANES)],
        out_shape=[jax.ShapeDtypeStruct((m, d), F32), jax.ShapeDtypeStruct((m, LANES), jnp.int32),
                   jax.ShapeDtypeStruct((m, LANES), F32)],
        compiler_params=_cparams("parallel", "parallel"), name="xattn_prompt")(x, mk, mv, wq, wo, g, b, wr, br)


def _xattn_sample_kernel(q_ref, mk_ref, mv_ref, o_ref):
    q = jnp.broadcast_to(q_ref[0], (8, X_W))
    o = _mem_attention(q, mk_ref[0], mv_ref[0])
    o_ref[0] = o[0:1]


def xattn_sample_core(q, mk, mv, off=0):
    bsz = q.shape[0]
    spec3 = pl.BlockSpec((1, N_MEM, X_W), lambda bi: (off + bi, 0, 0))
    out = pl.pallas_call(
        _xattn_sample_kernel, grid=(bsz,),
        in_specs=[pl.BlockSpec((1, 1, X_W), lambda bi: (bi, 0, 0)), spec3, spec3],
        out_specs=pl.BlockSpec((1, 1, X_W), lambda bi: (bi, 0, 0)),
        out_shape=jax.ShapeDtypeStruct((bsz, 1, X_W), F32),
        compiler_params=_cparams("parallel"), name="xattn_sample")(q.reshape(bsz, 1, X_W), mk, mv)
    return out.reshape(bsz, X_W)


def _ln_route_kernel(a_ref, w_ref, x_ref, g_ref, b_ref, wr_ref, br_ref, o_ref, idx_ref, gate_ref):
    h = jnp.dot(a_ref[...].astype(BF16), w_ref[...], preferred_element_type=F32)
    y = _layer_norm(DN_ALPHA * x_ref[...] + h, g_ref[...], b_ref[...])
    o_ref[...] = y
    idx, gates = _route(y, wr_ref[...], br_ref[...])
    idx_ref[...] = idx
    gate_ref[...] = gates


def proj_ln_route(a, w, x_res, g, b, wr, br):
    m, d = x_res.shape
    return pl.pallas_call(
        _ln_route_kernel,
        out_shape=[jax.ShapeDtypeStruct((m, d), F32), jax.ShapeDtypeStruct((m, LANES), jnp.int32),
                   jax.ShapeDtypeStruct((m, LANES), F32)],
        compiler_params=pltpu.CompilerParams(vmem_limit_bytes=VMEM_LIMIT),
        name="proj_ln_route")(a, w, x_res, g, b, wr, br)


def _rank_kernel(idx_ref, rank_ref, cnt_ref, base_ref):
    @pl.when(pl.program_id(0) == 0)
    def _():
        base_ref[...] = jnp.zeros_like(base_ref)

    idx = idx_ref[...]
    tr = idx.shape[0]
    lane = lax.broadcasted_iota(jnp.int32, idx.shape, 1)
    onehots = [(idx[:, k:k + 1] == lane).astype(F32) for k in range(TOP_K)]
    tot = onehots[0]
    for k in range(1, TOP_K):
        tot = tot + onehots[k]
    r = lax.broadcasted_iota(jnp.int32, (tr, tr), 0)
    c = lax.broadcasted_iota(jnp.int32, (tr, tr), 1)
    strict = (c < r).astype(BF16)
    before = jnp.dot(strict, tot.astype(BF16), preferred_element_type=F32) + base_ref[...]
    out = jnp.zeros(idx.shape, F32)
    for k in range(TOP_K):
        rk = jnp.sum(onehots[k] * before, axis=-1, keepdims=True)
        out = jnp.where(lane == k, rk, out)
    rank_ref[...] = out.astype(jnp.int32)
    base_ref[...] = base_ref[...] + jnp.sum(tot, axis=0, keepdims=True)
    cnt_ref[...] = base_ref[...].astype(jnp.int32)


def route_rank(idx, tr=256):
    t = idx.shape[0]
    tr = min(tr, t)
    assert t % tr == 0
    return pl.pallas_call(
        _rank_kernel, grid=(t // tr,),
        in_specs=[pl.BlockSpec((tr, LANES), lambda i: (i, 0))],
        out_specs=[pl.BlockSpec((tr, LANES), lambda i: (i, 0)), pl.BlockSpec((1, LANES), lambda i: (0, 0))],
        out_shape=[jax.ShapeDtypeStruct((t, LANES), jnp.int32), jax.ShapeDtypeStruct((1, LANES), jnp.int32)],
        scratch_shapes=[pltpu.VMEM((1, LANES), F32)],
        compiler_params=_cparams("arbitrary"), name="route_rank")(idx)


def _clamped_swiglu(h):
    glu = jnp.minimum(h[:, :D_FF], SWIGLU_LIMIT)
    lin = jnp.clip(h[:, D_FF:], -SWIGLU_LIMIT, SWIGLU_LIMIT)
    return glu * jax.nn.sigmoid(SWIGLU_ALPHA * glu) * (lin + 1.0)


def _moe_gmm_kernel(bexp_ref, nused_ref, tok_ref, tok_next_ref, dst_ref, x_hbm, win_ref, bin_ref, wout_ref,
                    bout_ref, y_hbm, xbuf, ybuf, winb, woutb, gsem, ssem, *, bm):
    i = pl.program_id(0)
    n = pl.num_programs(0)
    nused = nused_ref[0]
    slot = lax.rem(i, 2)

    def gather_copy(tok, r, s):
        return pltpu.make_async_copy(x_hbm.at[pl.ds(tok, 1)], xbuf.at[s, pl.ds(r, 1)], gsem.at[s])

    def scatter_copy(dst, r, s):
        return pltpu.make_async_copy(ybuf.at[s, pl.ds(r, 1)], y_hbm.at[pl.ds(dst, 1)], ssem.at[s])

    def start_gather(ref, s):
        def body(r, carry):
            gather_copy(ref[0, 0, r], r, s).start()
            return carry
        lax.fori_loop(0, bm, body, 0)

    def wait_rows(copy_fn, s):
        def body(r, carry):
            copy_fn(0, r, s).wait()
            return carry
        lax.fori_loop(0, bm, body, 0)

    @pl.when(i == 0)
    def _():
        start_gather(tok_ref, 0)
        ybuf[...] = jnp.zeros_like(ybuf)
        n_real = y_hbm.shape[0] - 2 * bm
        for s in range(2):
            cp = pltpu.make_async_copy(ybuf.at[s], y_hbm.at[pl.ds(n_real + s * bm, bm)], ssem.at[s])
            cp.start()
            cp.wait()

    @pl.when(jnp.logical_and(i + 1 < n, i + 1 < nused))
    def _():
        start_gather(tok_next_ref, 1 - slot)

    active = i < nused

    @pl.when(jnp.logical_and(active, jnp.logical_or(i == 0, bexp_ref[i] != bexp_ref[jnp.maximum(i - 1, 0)])))
    def _():
        winb[...] = win_ref[0].astype(BF16)
        woutb[...] = wout_ref[0].astype(BF16)

    @pl.when(active)
    def _():
        wait_rows(gather_copy, slot)

        @pl.when(i >= 2)
        def _():
            wait_rows(scatter_copy, slot)

        h = jnp.dot(xbuf[slot].astype(BF16), winb[...], preferred_element_type=F32) + bin_ref[0]
        act = _clamped_swiglu(h)
        ybuf[slot] = jnp.dot(act.astype(BF16), woutb[...], preferred_element_type=F32) + bout_ref[0]

        def body(r, carry):
            scatter_copy(dst_ref[0, 0, r], r, slot).start()
            return carry
        lax.fori_loop(0, bm, body, 0)

    @pl.when(i == n - 1)
    def _():
        last = nused - 1

        @pl.when(nused >= 2)
        def _():
            wait_rows(scatter_copy, lax.rem(last + 1, 2))

        @pl.when(nused >= 1)
        def _():
            wait_rows(scatter_copy, lax.rem(last, 2))


def moe_gmm(x, row_tok, row_dst, block_expert, n_used, w_in, b_in, w_out, b_out, n_out_rows, bm):
    n_blocks = row_tok.shape[0]
    d = x.shape[1]
    idx_spec = lambda off: pl.BlockSpec(
        (1, 1, bm), lambda i, be, nu: (jnp.minimum(i + off, n_blocks - 1), 0, 0), memory_space=pltpu.SMEM)
    ex = lambda i, be, nu: (be[i], 0, 0)
    grid_spec = pltpu.PrefetchScalarGridSpec(
        num_scalar_prefetch=2, grid=(n_blocks,),
        in_specs=[idx_spec(0), idx_spec(1), idx_spec(0),
                  pl.BlockSpec(memory_space=pl.ANY),
                  pl.BlockSpec((1, d, 2 * D_FF), ex), pl.BlockSpec((1, 1, 2 * D_FF), ex),
                  pl.BlockSpec((1, D_FF, d), ex), pl.BlockSpec((1, 1, d), ex)],
        out_specs=pl.BlockSpec(memory_space=pl.ANY),
        scratch_shapes=[pltpu.VMEM((2, bm, d), F32), pltpu.VMEM((2, bm, d), F32),
                        pltpu.VMEM((d, 2 * D_FF), BF16), pltpu.VMEM((D_FF, d), BF16),
                        pltpu.SemaphoreType.DMA((2,)), pltpu.SemaphoreType.DMA((2,))])
    return pl.pallas_call(
        functools.partial(_moe_gmm_kernel, bm=bm), grid_spec=grid_spec,
        out_shape=jax.ShapeDtypeStruct((n_out_rows, d), F32),
        compiler_params=_cparams("arbitrary"), name="moe_gmm")(
            block_expert, n_used, row_tok, row_tok, row_dst, x, w_in, b_in.reshape(b_in.shape[0], 1, -1),
            w_out, b_out.reshape(b_out.shape[0], 1, -1))


def _combine_ln_kernel(y_ref, gate_ref, x_ref, g_ref, b_ref, o_ref):
    d = x_ref.shape[1]
    gates = gate_ref[...]
    acc = gates[:, 0:1] * y_ref[:, 0:d]
    for k in range(1, TOP_K):
        acc = acc + gates[:, k:k + 1] * y_ref[:, k * d:(k + 1) * d]
    o_ref[...] = _layer_norm(DN_ALPHA * x_ref[...] + acc, g_ref[...], b_ref[...])


def combine_ln(y4, gates, x_res, g, b, tc=256):
    t, d = x_res.shape
    tc = min(tc, t)
    assert t % tc == 0
    return pl.pallas_call(
        _combine_ln_kernel, grid=(t // tc,),
        in_specs=[pl.BlockSpec((tc, TOP_K * d), lambda i: (i, 0)), pl.BlockSpec((tc, LANES), lambda i: (i, 0)),
                  pl.BlockSpec((tc, d), lambda i: (i, 0)), pl.BlockSpec((1, d), lambda i: (0, 0)),
                  pl.BlockSpec((1, d), lambda i: (0, 0))],
        out_specs=pl.BlockSpec((tc, d), lambda i: (i, 0)),
        out_shape=jax.ShapeDtypeStruct((t, d), F32),
        compiler_params=_cparams("parallel"), name="combine_ln")(y4, gates, x_res, g, b)


def moe_ln(x, idx, gates, w_in, b_in, w_out, b_out, g, b, bm, e_off=0):
    t, d = x.shape
    tk = t * TOP_K
    rank, counts = route_rank(idx)
    counts = counts[0, :N_EXPERTS]
    padded = (counts + bm - 1) // bm * bm
    pad_end = jnp.cumsum(padded)
    pad_start = pad_end - padded
    n_blocks = -(-tk // bm) + N_EXPERTS
    n_rows = n_blocks * bm
    e = idx[:, :TOP_K]
    dest = (pad_start[e] + rank[:, :TOP_K]).reshape(tk)
    src = jnp.arange(tk, dtype=jnp.int32)
    row_src = jnp.full((n_rows,), -1, jnp.int32).at[dest].set(src)
    slot_in_pair = (jnp.arange(n_rows, dtype=jnp.int32) // bm % 2) * bm + jnp.arange(n_rows, dtype=jnp.int32) % bm
    row_tok = jnp.where(row_src >= 0, row_src // TOP_K, 0).reshape(n_blocks, 1, bm)
    row_dst = jnp.where(row_src >= 0, row_src, tk + slot_in_pair).reshape(n_blocks, 1, bm)
    block_start = jnp.arange(n_blocks, dtype=jnp.int32) * bm
    block_expert = jnp.minimum(jnp.searchsorted(pad_end, block_start, side='right'), N_EXPERTS - 1).astype(jnp.int32)
    block_expert = block_expert + e_off
    n_used = (pad_end[-1] // bm).astype(jnp.int32).reshape(1)
    y = moe_gmm(x, row_tok, row_dst, block_expert, n_used, w_in, b_in, w_out, b_out, tk + 2 * bm, bm)
    y4 = y.reshape((tk + 2 * bm) // TOP_K, TOP_K * d)
    return combine_ln(y4, gates, x, g, b)


def _t5_causal_bucket(dist):
    n = jnp.maximum(dist, 0)
    max_exact = N_BUCKETS // 2
    nf = jnp.maximum(n, max_exact).astype(F32)
    large = max_exact + (jnp.log(nf / max_exact) / math.log(MAX_DISTANCE / max_exact)
                         * (N_BUCKETS - max_exact)).astype(jnp.int32)
    large = jnp.minimum(large, N_BUCKETS - 1)
    return jnp.where(n < max_exact, n, large)


def _split_maps(q):
    lane = lax.broadcasted_iota(jnp.int32, q.shape, 1)
    qs = q * (DH_C ** -0.5)
    return (jnp.where(lane < DH_C, qs, 0.0).astype(BF16), jnp.where(lane >= DH_C, qs, 0.0).astype(BF16))


def _diff_finish(acc1, l1, acc2, l2, lam, out_scale, g):
    o = acc1 / l1 - lam * (acc2 / l2)
    o = o * lax.rsqrt(jnp.mean(o * o, axis=-1, keepdims=True) + NORM_EPS) * g
    return o * out_scale


def _diff_prompt_kernel(sc_ref, far_ref, q_ref, k_ref, v_ref, bias_ref, g_ref, o_ref,
                        m1, l1, a1, m2, l2, a2, *, tq):
    h = pl.program_id(1)
    qi = pl.program_id(2)
    qm = _split_maps(q_ref[...])
    stats = ((m1, l1, a1), (m2, l2, a2))
    for m_ref, l_ref, a_ref in stats:
        m_ref[...] = jnp.full_like(m_ref, -jnp.inf)
        l_ref[...] = jnp.zeros_like(l_ref)
        a_ref[...] = jnp.zeros_like(a_ref)

    def tile(kidx, bias, causal):
        start = pl.multiple_of(kidx * tq, tq)
        kt = k_ref[0, pl.ds(start, tq), :].astype(BF16)
        vt = v_ref[0, pl.ds(start, tq), :].astype(BF16)
        for qmap, (m_ref, l_ref, a_ref) in zip(qm, stats):
            s = lax.dot_general(qmap, kt, (((1,), (1,)), ((), ())), preferred_element_type=F32) + bias
            if causal:
                r = lax.broadcasted_iota(jnp.int32, s.shape, 0)
                c = lax.broadcasted_iota(jnp.int32, s.shape, 1)
                s = jnp.where(c <= r, s, -jnp.inf)
            m_old = m_ref[...]
            m_new = jnp.maximum(m_old, jnp.max(s, axis=-1, keepdims=True))
            alpha = jnp.exp(m_old - m_new)
            p = jnp.exp(s - m_new)
            l_ref[...] = alpha * l_ref[...] + jnp.sum(p, axis=-1, keepdims=True)
            a_ref[...] = alpha * a_ref[...] + jnp.dot(p.astype(BF16), vt, preferred_element_type=F32)
            m_ref[...] = m_new

    far_bias = far_ref[h]

    def far_body(kidx, carry):
        tile(kidx, far_bias, False)
        return carry

    lax.fori_loop(0, jnp.maximum(qi - 1, 0), far_body, 0)

    @pl.when(qi >= 1)
    def _():
        tile(qi - 1, bias_ref[0, 1], False)

    tile(qi, bias_ref[0, 0], True)
    o_ref[...] = _diff_finish(a1[...], l1[...], a2[...], l2[...], sc_ref[0], sc_ref[1], g_ref[...])


def diff_attn_prompt_core(q, k, v, bsz, rel_bias, lam, lam_init, subln_g, tq=256):
    m = q.shape[0]
    t = m // bsz
    nq = t // tq
    tab = rel_bias[_t5_causal_bucket(jnp.arange(2 * tq))].T.astype(F32)
    ii = jnp.arange(tq)[:, None]
    jj = jnp.arange(tq)[None, :]
    bias = jnp.stack([tab[:, jnp.maximum(ii - jj, 0)], tab[:, tq + ii - jj]], axis=1)
    far = rel_bias[_t5_causal_bucket(jnp.array(2 * tq))].astype(F32)
    scal = jnp.stack([lam, 1.0 - lam_init]).astype(F32)
    k3 = k.reshape(bsz, t, C_W)
    v3 = v.reshape(bsz, t, C_W)
    smem = pl.BlockSpec(memory_space=pltpu.SMEM)
    kv_spec = pl.BlockSpec((1, t, VD_C), lambda b, h, i: (b, 0, h))
    row = pl.BlockSpec((tq, VD_C), lambda b, h, i: (b * nq + i, h))
    stat = pltpu.VMEM((tq, 1), F32)
    acc = pltpu.VMEM((tq, VD_C), F32)
    return pl.pallas_call(
        functools.partial(_diff_prompt_kernel, tq=tq), grid=(bsz, H_C, nq),
        in_specs=[smem, smem, row, kv_spec, kv_spec,
                  pl.BlockSpec((1, 2, tq, tq), lambda b, h, i: (h, 0, 0, 0)),
                  pl.BlockSpec((1, VD_C), lambda b, h, i: (0, 0))],
        out_specs=row, out_shape=jax.ShapeDtypeStruct((m, C_W), F32),
        scratch_shapes=[stat, stat, acc, stat, stat, acc],
        compiler_params=_cparams("parallel", "parallel", "parallel"), name="diff_attn_prompt")(
            scal, far, q, k3, v3, bias, subln_g.reshape(1, VD_C))


def _diff_sample_kernel(pt_ref, sc_ref, q_ref, kn_ref, vn_ref, bm_ref, bnew_ref, g_ref, *refs, n_pp):
    k_refs, v_refs = refs[:n_pp], refs[n_pp:2 * n_pp]
    o_ref, m_ref, l_ref, a_ref = refs[2 * n_pp:]
    p = pl.program_id(1)
    n_p = pl.num_programs(1)
    q8 = q_ref[0]
    qm = jnp.concatenate(_split_maps(q8), axis=0)

    @pl.when(p == 0)
    def _():
        m_ref[...] = jnp.full_like(m_ref, -jnp.inf)
        l_ref[...] = jnp.zeros_like(l_ref)
        a_ref[...] = jnp.zeros_like(a_ref)

    for j in range(n_pp):
        bias = bm_ref[jnp.where(p == n_p - 1, 1, 0)] if j == n_pp - 1 else bm_ref[0]
        k2 = k_refs[j][0, 0].reshape(-1, 2 * DH_C).astype(BF16)
        v2 = v_refs[j][0, 0].reshape(-1, VD_C).astype(BF16)
        s = lax.dot_general(qm, k2, (((1,), (1,)), ((), ())), preferred_element_type=F32) + bias
        m_old = m_ref[...]
        m_new = jnp.maximum(m_old, jnp.max(s, axis=-1, keepdims=True))
        alpha = jnp.exp(m_old - m_new)
        pr = jnp.exp(s - m_new)
        l_ref[...] = alpha * l_ref[...] + jnp.sum(pr, axis=-1, keepdims=True)
        a_ref[...] = alpha * a_ref[...] + jnp.dot(pr.astype(BF16), v2, preferred_element_type=F32)
        m_ref[...] = m_new

    @pl.when(p == n_p - 1)
    def _():
        kn = jnp.concatenate([kn_ref[0], kn_ref[0]], axis=0)
        vn = jnp.concatenate([vn_ref[0], vn_ref[0]], axis=0)
        s = jnp.sum(qm.astype(F32) * kn, axis=-1, keepdims=True) + bnew_ref[:, 0:1]
        m_old = m_ref[...]
        m_new = jnp.maximum(m_old, s)
        alpha = jnp.exp(m_old - m_new)
        pr = jnp.exp(s - m_new)
        l = alpha * l_ref[...] + pr
        a = alpha * a_ref[...] + pr * vn
        o_ref[0] = _diff_finish(a[:H_C], l[:H_C], a[H_C:], l[H_C:], sc_ref[0], sc_ref[1], g_ref[...])


def diff_attn_sample_core(q, k_new, v_new, cache_k, cache_v, page_table, layer_j, rel_bias, lam, lam_init, subln_g,
                          n_pp=4):
    bsz = q.shape[0]
    n_pages = page_table.shape[1]
    page = cache_k.shape[2]
    past = n_pages * page
    assert n_pages % n_pp == 0
    dist_last = past - ((n_pages - 1) * page + jnp.arange(page))
    b_last = rel_bias[_t5_causal_bucket(dist_last)].astype(F32)
    b_far = jnp.broadcast_to(rel_bias[_t5_causal_bucket(jnp.array(page + 1))].astype(F32), (page, H_C))
    eye = jnp.eye(H_C, dtype=bool)

    def expand(bt):
        full = jnp.where(eye[:, None, :], bt.T[:, :, None], -jnp.inf).reshape(H_C, page * H_C)
        return jnp.concatenate([full, full], axis=0)

    bm = jnp.stack([expand(b_far), expand(b_last)])
    b_new = rel_bias[_t5_causal_bucket(jnp.array(0))].astype(F32)
    b_new = jnp.broadcast_to(jnp.concatenate([b_new, b_new])[:, None], (2 * H_C, LANES))
    scal = jnp.stack([lam, 1.0 - lam_init]).astype(F32)
    smem = pl.BlockSpec(memory_space=pltpu.SMEM)
    head3 = pl.BlockSpec((1, H_C, VD_C), lambda b, p, pt: (b, 0, 0))
    full = lambda shape: pl.BlockSpec(shape, lambda b, p, pt: (0,) * len(shape))

    def page_spec(j):
        return pl.BlockSpec((1, 1, page, H_C, VD_C), lambda b, p, pt: (pt[b, p * n_pp + j], layer_j, 0, 0, 0))

    grid_spec = pltpu.PrefetchScalarGridSpec(
        num_scalar_prefetch=1, grid=(bsz, n_pages // n_pp),
        in_specs=[smem, head3, head3, head3, full(bm.shape), full(b_new.shape), full((1, VD_C))]
        + [page_spec(j) for j in range(n_pp)] * 2,
        out_specs=head3,
        scratch_shapes=[pltpu.VMEM((2 * H_C, 1), F32), pltpu.VMEM((2 * H_C, 1), F32),
                        pltpu.VMEM((2 * H_C, VD_C), F32)])
    out = pl.pallas_call(
        functools.partial(_diff_sample_kernel, n_pp=n_pp), grid_spec=grid_spec,
        out_shape=jax.ShapeDtypeStruct((bsz, H_C, VD_C), F32),
        compiler_params=_cparams("parallel", "arbitrary"), name="diff_attn_sample")(
            page_table, scal, q.reshape(bsz, H_C, VD_C), k_new.reshape(bsz, H_C, VD_C),
            v_new.reshape(bsz, H_C, VD_C), bm, b_new, subln_g.reshape(1, VD_C),
            *([cache_k] * n_pp), *([cache_v] * n_pp))
    return out.reshape(bsz, C_W)


LANE_BETA, LANE_A, LANE_I, LANE_F = 0, H_A, 2 * H_A, 2 * H_A + H_B


def _softplus(x):
    return jnp.maximum(x, 0.0) + jnp.log1p(jnp.exp(-jnp.abs(x)))


def _silu(x):
    return x * jax.nn.sigmoid(x)


def _lanes(shape, lo, n):
    lane = lax.broadcasted_iota(jnp.int32, shape, 1)
    return jnp.logical_and(lane >= lo, lane < lo + n)


def _gate_tile(gt, alog_row, prow):
    z = gt + prow
    return jax.nn.sigmoid(gt), -jnp.exp(alog_row) * _softplus(z), z, -_softplus(-z)


def _nt(a, b, precision=None):
    return lax.dot_general(a, b, (((1,), (1,)), ((), ())), preferred_element_type=F32, precision=precision)


def _tn(a, b):
    return lax.dot_general(a, b, (((0,), (0,)), ((), ())), preferred_element_type=F32)


def _row_selector(n_heads, length, lanes_of_head):
    r = lax.broadcasted_iota(jnp.int32, (n_heads * length, LANES), 0) // length
    lane = lax.broadcasted_iota(jnp.int32, (n_heads * length, LANES), 1)
    sel = jnp.zeros((n_heads * length, LANES), F32)
    for lo in lanes_of_head:
        sel = sel + (lane == r + lo).astype(F32)
    return sel


def _tri_masks(n):
    r = lax.broadcasted_iota(jnp.int32, (n, n), 0)
    c = lax.broadcasted_iota(jnp.int32, (n, n), 1)
    return r, c


def _unit_lower_inverse(nmat, r, c):
    mm = lambda a, b: jnp.dot(a, b, preferred_element_type=F32, precision=HI)
    eye = (r == c).astype(F32)
    same = (r // 16) == (c // 16)
    nd = jnp.where(same, nmat, 0.0)
    off = nmat - nd
    dinv = eye - nd
    p = nd
    for _ in range(3):
        p = mm(p, p)
        dinv = dinv + mm(dinv, p)
    m = mm(dinv, off)
    m2 = mm(m, m)
    left = eye - m
    left = left + mm(left, m2)
    return mm(left, dinv)


def _gdn_prompt_kernel(qkv_ref, z_ref, gt_ref, convw_ref, alog_ref, prow_ref, ng_ref, mix_ref, s_ref, ext_ref):
    cidx = pl.program_id(1)
    L = CHUNK

    @pl.when(cidx == 0)
    def _():
        ext_ref[0:8, :] = jnp.zeros((8, QKV_A), F32)
        s_ref[...] = jnp.zeros_like(s_ref)

    ext_ref[8:8 + L, :] = qkv_ref[...]
    acc = convw_ref[0:1, :] * ext_ref[pl.ds(8 - (CONV_W - 1), L), :]
    for i in range(1, CONV_W):
        acc = acc + convw_ref[i:i + 1, :] * ext_ref[pl.ds(8 - (CONV_W - 1) + i, L), :]
    ext_ref[0:8, :] = ext_ref[L:L + 8, :]
    cs = _silu(acc)

    beta, g, _, _ = _gate_tile(gt_ref[...], alog_ref[...], prow_ref[...])
    r, c = _tri_masks(L)
    incl = c <= r
    strict = c < r
    gsel = jnp.where(_lanes(g.shape, LANE_A, H_A), g, 0.0)
    gcum = jnp.dot(incl.astype(F32), gsel, preferred_element_type=F32, precision=HI)
    gc_rows = _nt(_row_selector(H_A, L, (LANE_A,)), gcum, HI)
    z = z_ref[...]
    ng = ng_ref[...]
    outs = []
    for h in range(H_A):
        qh = cs[:, h * DK_A:(h + 1) * DK_A]
        kh = cs[:, H_A * DK_A + h * DK_A:H_A * DK_A + (h + 1) * DK_A]
        vh = cs[:, 2 * H_A * DK_A + h * DV_A:2 * H_A * DK_A + (h + 1) * DV_A]
        qh = qh * lax.rsqrt(jnp.sum(qh * qh, axis=-1, keepdims=True) + NORM_EPS) * (DK_A ** -0.5)
        kh = kh * lax.rsqrt(jnp.sum(kh * kh, axis=-1, keepdims=True) + NORM_EPS)
        beta_c = beta[:, LANE_BETA + h:LANE_BETA + h + 1]
        gc_c = gcum[:, LANE_A + h:LANE_A + h + 1]
        diff = gc_c - gc_rows[h * L:(h + 1) * L]
        decay = jnp.where(incl, jnp.exp(jnp.where(incl, diff, 0.0)), 0.0)
        kb = kh.astype(BF16)
        nmat = jnp.where(strict, beta_c * _nt(kb, kb) * decay, 0.0)
        egc = jnp.exp(gc_c)
        rhs = jnp.concatenate([vh * beta_c, kh * (beta_c * egc)], axis=-1)
        tinv = _unit_lower_inverse(nmat, r, c)
        sol = jnp.dot(tinv.astype(BF16), rhs.astype(BF16), preferred_element_type=F32)
        u, w = sol[:, :DV_A], sol[:, DV_A:]
        qk = _nt(qh.astype(BF16), kb) * decay
        gc_last = gc_c[L - 1:L]
        k_tail = kh * jnp.exp(gc_last - gc_c)
        s_old = s_ref[0, h]
        ws = jnp.dot(jnp.concatenate([w, qh * egc], axis=0).astype(BF16), s_old.astype(BF16),
                     preferred_element_type=F32)
        delta = u - ws[:L]
        o = ws[L:] + jnp.dot(qk.astype(BF16), delta.astype(BF16), preferred_element_type=F32)
        s_ref[0, h] = s_old * jnp.exp(gc_last) + _tn(k_tail.astype(BF16), delta.astype(BF16))
        o = o * lax.rsqrt(jnp.mean(o * o, axis=-1, keepdims=True) + NORM_EPS) * ng
        outs.append(o * _silu(z[:, h * DV_A:(h + 1) * DV_A]))
    mix_ref[...] = jnp.concatenate(outs, axis=-1)


def gdn_prompt(qkv, z, gates, conv_w, alog_row, prow, norm_g, bsz):
    m = qkv.shape[0]
    nc = m // bsz // CHUNK
    row = lambda w: pl.BlockSpec((CHUNK, w), lambda b, c: (b * nc + c, 0))
    full = lambda shape: pl.BlockSpec(shape, lambda b, c: (0,) * len(shape))
    return pl.pallas_call(
        _gdn_prompt_kernel, grid=(bsz, nc),
        in_specs=[row(QKV_A), row(H_A * DV_A), row(LANES), full(conv_w.shape), full((1, LANES)), full((1, LANES)),
                  full((1, DV_A))],
        out_specs=[row(H_A * DV_A), pl.BlockSpec((1, H_A, DK_A, DV_A), lambda b, c: (b, 0, 0, 0))],
        out_shape=[jax.ShapeDtypeStruct((m, H_A * DV_A), F32), jax.ShapeDtypeStruct((bsz, H_A, DK_A, DV_A), F32)],
        scratch_shapes=[pltpu.VMEM((CHUNK + 8, QKV_A), F32)],
        compiler_params=_cparams("parallel", "arbitrary"), name="gdn_prompt")(
            qkv, z, gates, conv_w, alog_row, prow, norm_g.reshape(1, DV_A))


def _mlstm_prompt_kernel(q_ref, k_ref, v_ref, og_ref, gt_ref, alog_ref, prow_ref, ng_ref, mix_ref, c_ref, m_ref):
    cidx = pl.program_id(1)
    L = CHUNK

    @pl.when(cidx == 0)
    def _():
        c_ref[...] = jnp.zeros_like(c_ref)
        m_ref[...] = jnp.zeros_like(m_ref)

    _, _, ipre, logf = _gate_tile(gt_ref[...], alog_ref[...], prow_ref[...])
    r, c = _tri_masks(L)
    incl = c <= r
    fsel = _lanes(logf.shape, LANE_F, H_B)
    bcum = jnp.dot(incl.astype(F32), jnp.where(fsel, logf, 0.0), preferred_element_type=F32, precision=HI)
    rowvals = jnp.where(_lanes(ipre.shape, LANE_I, H_B), ipre, 0.0) - jnp.where(fsel, bcum, 0.0)
    rows = _nt(_row_selector(H_B, L, (LANE_I, LANE_F)), rowvals, HI)
    mrow = m_ref[0]
    lane_row = lax.broadcasted_iota(jnp.int32, mrow.shape, 1)
    q_all, k_all, v_all, og = q_ref[...], k_ref[...], v_ref[...], og_ref[...]
    ng = ng_ref[...]
    ones_col = (lax.broadcasted_iota(jnp.int32, (L, LANES), 1) == 0).astype(F32)
    outs = []
    for h in range(H_B):
        qb = q_all[:, h * DQK_B:(h + 1) * DQK_B].astype(BF16)
        ks = k_all[:, h * DQK_B:(h + 1) * DQK_B] * (DQK_B ** -0.5)
        v_ext = jnp.concatenate([v_all[:, h * DV_B:(h + 1) * DV_B], ones_col], axis=-1).astype(BF16)
        b_c = bcum[:, LANE_F + h:LANE_F + h + 1]
        i_c = ipre[:, LANE_I + h:LANE_I + h + 1]
        dmat = jnp.where(incl, b_c + rows[h * L:(h + 1) * L], -jnp.inf)
        m_intra = jnp.max(dmat, axis=-1, keepdims=True)
        w_intra = jnp.exp(dmat - m_intra) * _nt(qb, ks.astype(BF16))
        nd_intra = jnp.dot(w_intra.astype(BF16), v_ext, preferred_element_type=F32)
        b_last = b_c[L - 1:L]
        e_end = b_last - b_c + i_c
        e_max = jnp.max(e_end, axis=0, keepdims=True)
        kv_end = _tn((ks * jnp.exp(e_end - e_max)).astype(BF16), v_ext)
        m_prev = mrow[:, h:h + 1]
        inter = b_c + m_prev
        m_t = jnp.maximum(inter, m_intra)
        c_old = c_ref[0, h]
        nd = (jnp.exp(inter - m_t) * jnp.dot(qb, c_old.astype(BF16), preferred_element_type=F32)
              + jnp.exp(m_intra - m_t) * nd_intra)
        hh = nd[:, :DV_B] / jnp.maximum(jnp.abs(nd[:, DV_B:DV_B + 1]), jnp.exp(-m_t))
        m_new = jnp.maximum(b_last + m_prev, e_max)
        c_ref[0, h] = jnp.exp(b_last + m_prev - m_new) * c_old + jnp.exp(e_max - m_new) * kv_end
        mrow = jnp.where(lane_row == h, m_new, mrow)
        hh = hh * lax.rsqrt(jnp.mean(hh * hh, axis=-1, keepdims=True) + NORM_EPS) * ng
        outs.append(jax.nn.sigmoid(og[:, h * DV_B:(h + 1) * DV_B]) * hh)
    m_ref[0] = mrow
    mix_ref[...] = jnp.concatenate(outs, axis=-1)


def mlstm_prompt(q, k, v, og, gates, alog_row, prow, norm_g, bsz):
    m = q.shape[0]
    nc = m // bsz // CHUNK
    row = lambda w: pl.BlockSpec((CHUNK, w), lambda b, c: (b * nc + c, 0))
    full = lambda shape: pl.BlockSpec(shape, lambda b, c: (0,) * len(shape))
    return pl.pallas_call(
        _mlstm_prompt_kernel, grid=(bsz, nc),
        in_specs=[row(H_B * DQK_B), row(H_B * DQK_B), row(H_B * DV_B), row(H_B * DV_B), row(LANES),
                  full((1, LANES)), full((1, LANES)), full((1, DV_B))],
        out_specs=[row(H_B * DV_B), pl.BlockSpec((1, H_B, DQK_B, DV_B + LANES), lambda b, c: (b, 0, 0, 0)),
                   pl.BlockSpec((1, 1, LANES), lambda b, c: (b, 0, 0))],
        out_shape=[jax.ShapeDtypeStruct((m, H_B * DV_B), F32),
                   jax.ShapeDtypeStruct((bsz, H_B, DQK_B, DV_B + LANES), F32),
                   jax.ShapeDtypeStruct((bsz, 1, LANES), F32)],
        compiler_params=_cparams("parallel", "arbitrary"), name="mlstm_prompt")(
            q, k, v, og, gates, alog_row, prow, norm_g.reshape(1, DV_B))


def _columns(x8):
    n = x8.shape[1]
    r, c = _tri_masks(n)
    return _nt((r == c).astype(F32), x8, HI)


def _ab_sample_kernel(qn_ref, kn_ref, vn_ref, cq_ref, ck_ref, cv_ref, wq_ref, wk_ref, wv_ref, z_ref,
                      qb_ref, kb_ref, vb_ref, og_ref, gt_ref, alog_ref, prow_ref, nga_ref, ngb_ref,
                      s_in, c_in, n_in, m_in,
                      oa_ref, ob_ref, s_out, c_out, n_out, m_out):
    def conv(new_ref, prev_ref, w_ref):
        acc = w_ref[CONV_W - 1] * new_ref[0]
        for i in range(CONV_W - 1):
            acc = acc + w_ref[i] * prev_ref[0, i]
        return _silu(acc)

    q8 = conv(qn_ref, cq_ref, wq_ref)
    k8 = conv(kn_ref, ck_ref, wk_ref)
    v8 = conv(vn_ref, cv_ref, wv_ref)
    q8 = q8 * lax.rsqrt(jnp.sum(q8 * q8, axis=-1, keepdims=True) + NORM_EPS) * (DK_A ** -0.5)
    k8 = k8 * lax.rsqrt(jnp.sum(k8 * k8, axis=-1, keepdims=True) + NORM_EPS)
    beta, g, ipre, logf = _gate_tile(gt_ref[0], alog_ref[...], prow_ref[...])
    q_cols, k_cols = _columns(q8), _columns(k8)
    z8 = z_ref[0]
    outs = []
    for h in range(H_A):
        s = s_in[0, h] * jnp.exp(g[:, LANE_A + h:LANE_A + h + 1])
        kc = k_cols[:, h:h + 1]
        err = v8[h:h + 1] - jnp.sum(kc * s, axis=0, keepdims=True)
        s = s + kc * (beta[:, LANE_BETA + h:LANE_BETA + h + 1] * err)
        s_out[0, h] = s
        outs.append(jnp.sum(q_cols[:, h:h + 1] * s, axis=0, keepdims=True))
    o = jnp.concatenate(outs, axis=0)
    o = o * lax.rsqrt(jnp.mean(o * o, axis=-1, keepdims=True) + NORM_EPS) * nga_ref[...]
    oa_ref[0] = o * _silu(z8)

    zeros4 = jnp.zeros((8 - H_B, DQK_B), F32)
    qb_cols = _columns(jnp.concatenate([qb_ref[0], zeros4], axis=0))
    kb_cols = _columns(jnp.concatenate([kb_ref[0] * (DQK_B ** -0.5), zeros4], axis=0))
    vb = vb_ref[0]
    n_cols = n_in[0]
    m_row = m_in[0]
    lane_n = lax.broadcasted_iota(jnp.int32, n_cols.shape, 1)
    lane_m = lax.broadcasted_iota(jnp.int32, m_row.shape, 1)
    outs = []
    for h in range(H_B):
        lf = logf[:, LANE_F + h:LANE_F + h + 1]
        it = ipre[:, LANE_I + h:LANE_I + h + 1]
        m_prev = m_row[:, h:h + 1]
        m_new = jnp.maximum(lf + m_prev, it)
        f_sc = jnp.exp(lf + m_prev - m_new)
        i_sc = jnp.exp(it - m_new)
        kc = kb_cols[:, h:h + 1]
        qc = qb_cols[:, h:h + 1]
        cm = f_sc * c_in[0, h] + i_sc * (kc * vb[h:h + 1])
        nn = f_sc * n_cols[:, h:h + 1] + i_sc * kc
        c_out[0, h] = cm
        n_cols = jnp.where(lane_n == h, nn, n_cols)
        m_row = jnp.where(lane_m == h, m_new, m_row)
        num = jnp.sum(qc * cm, axis=0, keepdims=True)
        den = jnp.sum(qc * nn, axis=0, keepdims=True)
        outs.append(num / jnp.maximum(jnp.abs(den), jnp.exp(-m_new)))
    hb = jnp.concatenate(outs, axis=0)
    hb = hb * lax.rsqrt(jnp.mean(hb * hb, axis=-1, keepdims=True) + NORM_EPS) * ngb_ref[...]
    ob_ref[0] = jax.nn.sigmoid(og_ref[0]) * hb
    n_out[0] = n_cols
    m_out[0] = m_row


def ab_sample(qkv, z, q_b, k_b, v_b, o_b, gates, conv_prev, conv_w, alog_row, prow, norm_g_a, norm_g_b,
              s_prev, c_prev, n_prev, m_prev):
    bsz = qkv.shape[0]
    hk = H_A * DK_A
    part = lambda x, i, w: x[..., i * hk:i * hk + H_A * w].reshape(x.shape[:-1] + (H_A, w))
    new_parts = [part(qkv, 0, DK_A), part(qkv, 1, DK_A), part(qkv, 2, DV_A)]
    prev_parts = [part(conv_prev, 0, DK_A), part(conv_prev, 1, DK_A), part(conv_prev, 2, DV_A)]
    w_parts = [part(conv_w, 0, DK_A), part(conv_w, 1, DK_A), part(conv_w, 2, DV_A)]
    args = new_parts + prev_parts + w_parts + [
        z.reshape(bsz, H_A, DV_A), q_b.reshape(bsz, H_B, DQK_B), k_b.reshape(bsz, H_B, DQK_B),
        v_b.reshape(bsz, H_B, DV_B), o_b.reshape(bsz, H_B, DV_B), gates.reshape(bsz, 1, LANES),
        alog_row, prow, norm_g_a.reshape(1, DV_A), norm_g_b.reshape(1, DV_B),
        s_prev, c_prev, jnp.swapaxes(n_prev, 1, 2), m_prev.reshape(bsz, 1, H_B)]

    def spec(x, batched):
        nd = x.ndim
        if batched:
            return pl.BlockSpec((1,) + x.shape[1:], lambda b: (b,) + (0,) * (nd - 1))
        return pl.BlockSpec(x.shape, lambda b: (0,) * nd)

    batched = [True] * 6 + [False] * 3 + [True] * 6 + [False] * 4 + [True] * 4
    out_shape = [jax.ShapeDtypeStruct((bsz, H_A, DV_A), F32), jax.ShapeDtypeStruct((bsz, H_B, DV_B), F32),
                 jax.ShapeDtypeStruct(s_prev.shape, F32), jax.ShapeDtypeStruct(c_prev.shape, F32),
                 jax.ShapeDtypeStruct((bsz, DQK_B, H_B), F32), jax.ShapeDtypeStruct((bsz, 1, H_B), F32)]
    oa, ob, s_new, c_new, n_new, m_new = pl.pallas_call(
        _ab_sample_kernel, grid=(bsz,),
        in_specs=[spec(x, bt) for x, bt in zip(args, batched)],
        out_specs=[spec(x, True) for x in out_shape], out_shape=out_shape,
        compiler_params=_cparams("parallel"), name="ab_sample")(*args)
    mix = jnp.concatenate([oa.reshape(bsz, H_A * DV_A), ob.reshape(bsz, H_B * DV_B)], axis=-1)
    return mix, s_new, c_new, jnp.swapaxes(n_new, 1, 2), m_new.reshape(bsz, H_B)


MOE_BLOCK_PROMPT = 256
MOE_BLOCK_SAMPLE = 32


def _ab_weights(w_in, a_log, dt_bias, b_i, b_f):
    sizes = (QKV_A, H_A * DV_A, H_A, H_A, H_B * DQK_B, H_B * DQK_B, H_B * DV_B, H_B * DV_B, H_B, H_B)
    offs = [0]
    for s in sizes:
        offs.append(offs[-1] + s)
    col = lambda i: w_in[:, offs[i]:offs[i + 1]]
    w_lo = tuple(col(i).astype(BF16) for i in (0, 1, 4, 5, 6, 7))
    w_gate = jnp.concatenate([col(2), col(3), col(8), col(9)], axis=1)
    w_gate = jnp.pad(w_gate, ((0, 0), (0, LANES - w_gate.shape[1])))
    zeros = lambda n: jnp.zeros((n,), F32)
    pad = LANES - 2 * H_A - 2 * H_B
    alog_row = jnp.concatenate([zeros(H_A), a_log.astype(F32), zeros(2 * H_B + pad)]).reshape(1, LANES)
    prow = jnp.concatenate([zeros(H_A), dt_bias.astype(F32), b_i.astype(F32), b_f.astype(F32),
                            zeros(pad)]).reshape(1, LANES)
    return w_lo, w_gate, alog_row, prow


def kernel(x_prompt, x_sample, state_delta_S, state_delta_conv, state_mlstm_C, state_mlstm_n, state_mlstm_m,
           cache_diff_k, cache_diff_v, cache_mem_k, cache_mem_v, page_table, mem_prompt,
           w_in_ab, conv_w_a, a_log_a, dt_bias_a, norm_g_a, b_i_b, b_f_b, norm_g_b, w_out_ab,
           w_qkv_c, lam_q1, lam_k1, lam_q2, lam_k2, subln_g_c, w_o_c, rel_bias,
           w_xq, w_xkv, w_xo, ln_g, ln_b, w_router, b_router, w_moe_in, b_moe_in, w_moe_out, b_moe_out):
    bp, t, d = x_prompt.shape
    bs = x_sample.shape[0]
    xp = x_prompt.reshape(bp * t, d)
    xs = x_sample.reshape(bs, d)
    mem2d = mem_prompt.reshape(bp * N_MEM, d)
    cmk = cache_mem_k.reshape(DEPTH * bs, N_MEM, X_W)
    cmv = cache_mem_v.reshape(DEPTH * bs, N_MEM, X_W)
    wm_in = w_moe_in.reshape(DEPTH * N_EXPERTS, d, 2 * D_FF)
    bm_in = b_moe_in.reshape(DEPTH * N_EXPERTS, 2 * D_FF)
    wm_out = w_moe_out.reshape(DEPTH * N_EXPERTS, D_FF, d)
    bm_out = b_moe_out.reshape(DEPTH * N_EXPERTS, d)
    p_S, p_conv, p_C, p_n, p_m, p_k, p_v, p_mk, p_mv = [], [], [], [], [], [], [], [], []
    s_S, s_conv, s_C, s_n, s_m, s_k, s_v = [], [], [], [], [], [], []
    for layer in range(DEPTH):
        j = layer // 2
        g0, b0 = ln_g[layer, 0].reshape(1, d), ln_b[layer, 0].reshape(1, d)
        g1, b1 = ln_g[layer, 1].reshape(1, d), ln_b[layer, 1].reshape(1, d)
        g2, b2 = ln_g[layer, 2].reshape(1, d), ln_b[layer, 2].reshape(1, d)
        if layer % 2 == 0:
            w_lo, w_gate, alog_row, prow = _ab_weights(w_in_ab[j], a_log_a[j], dt_bias_a[j], b_i_b[j], b_f_b[j])
            w_out = w_out_ab[j].astype(BF16)
            w_out_a, w_out_b = w_out[:H_A * DV_A], w_out[H_A * DV_A:]
            qkv, z, q_b, k_b, v_b, o_b, gates = mm_multi(xp, w_lo, (w_gate,), tm=256)
            mix_a, st_S = gdn_prompt(qkv, z, gates, conv_w_a[j], alog_row, prow, norm_g_a[j], bp)
            mix_b, c_ext, m_row = mlstm_prompt(q_b, k_b, v_b, o_b, gates, alog_row, prow, norm_g_b[j], bp)
            xp = proj_ln([mix_a, mix_b], [w_out_a, w_out_b], xp, g0, b0)
            p_S.append(st_S)
            p_conv.append(qkv.reshape(bp, t, QKV_A)[:, t - (CONV_W - 1):])
            p_C.append(c_ext[..., :DV_B])
            p_n.append(c_ext[..., DV_B])
            p_m.append(m_row[:, 0, :H_B])
            qkv, z, q_b, k_b, v_b, o_b, gates = mm_multi(xs, w_lo, (w_gate,))
            mix, st_S, st_C, st_n, st_m = ab_sample(
                qkv, z, q_b, k_b, v_b, o_b, gates, state_delta_conv[j], conv_w_a[j], alog_row, prow,
                norm_g_a[j], norm_g_b[j], state_delta_S[j].astype(F32), state_mlstm_C[j].astype(F32),
                state_mlstm_n[j].astype(F32), state_mlstm_m[j].astype(F32))
            xs = proj_ln([mix], [w_out], xs, g0, b0)
            s_S.append(st_S)
            s_conv.append(jnp.concatenate([state_delta_conv[j][:, 1:].astype(F32), qkv[:, None, :]], axis=1))
            s_C.append(st_C)
            s_n.append(st_n)
            s_m.append(st_m)
        else:
            lam_init = 0.8 - 0.6 * math.exp(-0.3 * layer)
            lam = (jnp.exp(jnp.sum(lam_q1[j].astype(F32) * lam_k1[j].astype(F32)))
                   - jnp.exp(jnp.sum(lam_q2[j].astype(F32) * lam_k2[j].astype(F32))) + lam_init)
            w_qkv = tuple(w_qkv_c[j][:, i * C_W:(i + 1) * C_W].astype(BF16) for i in range(3))
            w_o = w_o_c[j].astype(BF16)
            q, k, v = mm_multi(xp, w_qkv, tm=512)
            o = diff_attn_prompt_core(q, k, v, bp, rel_bias, lam, lam_init, subln_g_c[j])
            xp = proj_ln([o], [w_o], xp, g0, b0)
            p_k.append(k.reshape(bp, t, H_C, 2 * DH_C))
            p_v.append(v.reshape(bp, t, H_C, VD_C))
            q, k, v = mm_multi(xs, w_qkv)
            o = diff_attn_sample_core(q, k, v, cache_diff_k, cache_diff_v, page_table, j, rel_bias, lam, lam_init,
                                      subln_g_c[j])
            xs = proj_ln([o], [w_o], xs, g0, b0)
            s_k.append(k.reshape(bs, 1, H_C, 2 * DH_C))
            s_v.append(v.reshape(bs, 1, H_C, VD_C))
        w_q, w_o = w_xq[layer].astype(BF16), w_xo[layer].astype(BF16)
        w_r = jnp.pad(w_router[layer].astype(F32), ((0, 0), (0, LANES - N_EXPERTS)))
        b_r = jnp.pad(b_router[layer].astype(F32), (0, LANES - N_EXPERTS)).reshape(1, LANES)
        mk, mv = mm_multi(mem2d, (w_xkv[layer][:, :X_W].astype(BF16), w_xkv[layer][:, X_W:].astype(BF16)))
        p_mk.append(mk.reshape(bp, N_MEM, H_X, DH_X))
        p_mv.append(mv.reshape(bp, N_MEM, H_X, DH_X))
        xp, idx_p, gate_p = xattn_prompt(xp, mk.reshape(bp, N_MEM, X_W), mv.reshape(bp, N_MEM, X_W),
                                         w_q, w_o, g1, b1, w_r, b_r)
        (q,) = mm_multi(xs, (w_q,))
        o = xattn_sample_core(q, cmk, cmv, off=layer * bs)
        xs, idx_s, gate_s = proj_ln_route(o, w_o, xs, g1, b1, w_r, b_r)
        xp = moe_ln(xp, idx_p, gate_p, wm_in, bm_in, wm_out, bm_out, g2, b2, MOE_BLOCK_PROMPT,
                    e_off=layer * N_EXPERTS)
        xs = moe_ln(xs, idx_s, gate_s, wm_in, bm_in, wm_out, bm_out, g2, b2, MOE_BLOCK_SAMPLE,
                    e_off=layer * N_EXPERTS)

    return (xp.reshape(bp, t, d), xs.reshape(bs, 1, d),
            jnp.stack(p_S), jnp.stack(p_conv), jnp.stack(p_C), jnp.stack(p_n), jnp.stack(p_m),
            jnp.stack(p_k, axis=1), jnp.stack(p_v, axis=1), jnp.stack(p_mk), jnp.stack(p_mv),
            jnp.stack(s_S), jnp.stack(s_conv), jnp.stack(s_C), jnp.stack(s_n), jnp.stack(s_m),
            jnp.stack(s_k, axis=1), jnp.stack(s_v, axis=1))
```

```python
import functools
import math

import jax
import jax.numpy as jnp
from jax import lax
from jax.experimental import pallas as pl
from jax.experimental.pallas import tpu as pltpu

F32 = jnp.float32
BF16 = jnp.bfloat16
HI = lax.Precision.HIGHEST

D_MODEL = 1024
DEPTH = 2
H_A, DK_A, DV_A, CONV_W, CHUNK = 8, 64, 64, 4, 64
H_B, DQK_B, DV_B = 4, 64, 128
QKV_A = H_A * (2 * DK_A + DV_A)
H_C, DH_C = 8, 64
VD_C = 2 * DH_C
C_W = H_C * 2 * DH_C
N_BUCKETS, MAX_DISTANCE = 32, 128
N_MEM, H_X, DH_X = 256, 4, 128
X_W = H_X * DH_X
N_EXPERTS, TOP_K = 32, 4
D_FF = D_MODEL
SWIGLU_ALPHA, SWIGLU_LIMIT = 1.702, 7.0
DN_ALPHA = (2 * DEPTH) ** 0.25
LN_EPS = 1e-5
NORM_EPS = 1e-6

LANES = 128
VMEM_LIMIT = 48 * 1024 * 1024


def _cparams(*sem):
    return pltpu.CompilerParams(dimension_semantics=tuple(sem), vmem_limit_bytes=VMEM_LIMIT)


def _layer_norm(y, g, b):
    mu = jnp.mean(y, axis=-1, keepdims=True)
    d = y - mu
    var = jnp.mean(d * d, axis=-1, keepdims=True)
    return d * lax.rsqrt(var + LN_EPS) * g + b


def _mm_multi_kernel(x_ref, *refs, n_lo, n_hi):
    n = n_lo + n_hi
    ws, outs = refs[:n], refs[n:]
    x = x_ref[...]
    xb = x.astype(BF16)
    for i in range(n_lo):
        outs[i][...] = jnp.dot(xb, ws[i][...], preferred_element_type=F32)
    for i in range(n_lo, n):
        outs[i][...] = jnp.dot(x, ws[i][...], preferred_element_type=F32, precision=HI)


def mm_multi(x, w_lo, w_hi=(), tm=256):
    m, k = x.shape
    tm = min(tm, m)
    assert m % tm == 0
    ws = tuple(w_lo) + tuple(w_hi)
    in_specs = [pl.BlockSpec((tm, k), lambda i: (i, 0))]
    in_specs += [pl.BlockSpec(w.shape, lambda i: (0, 0)) for w in ws]
    out_specs = [pl.BlockSpec((tm, w.shape[1]), lambda i: (i, 0)) for w in ws]
    out_shape = [jax.ShapeDtypeStruct((m, w.shape[1]), F32) for w in ws]
    return pl.pallas_call(
        functools.partial(_mm_multi_kernel, n_lo=len(w_lo), n_hi=len(w_hi)),
        grid=(m // tm,), in_specs=in_specs, out_specs=out_specs, out_shape=out_shape,
        compiler_params=_cparams("parallel"), name="mm_multi")(x, *ws)


def _proj_ln_kernel(*refs, n):
    a_refs, w_refs = refs[:n], refs[n:2 * n]
    x_ref, g_ref, b_ref, o_ref = refs[2 * n:]
    h = jnp.dot(a_refs[0][...].astype(BF16), w_refs[0][...], preferred_element_type=F32)
    for a_ref, w_ref in zip(a_refs[1:], w_refs[1:]):
        h = h + jnp.dot(a_ref[...].astype(BF16), w_ref[...], preferred_element_type=F32)
    o_ref[...] = _layer_norm(DN_ALPHA * x_ref[...] + h, g_ref[...], b_ref[...])


def proj_ln(a_list, w_list, x_res, g, b, tm=512):
    m, d = x_res.shape
    tm = min(tm, m)
    assert m % tm == 0
    n = len(a_list)
    return pl.pallas_call(
        functools.partial(_proj_ln_kernel, n=n), grid=(m // tm,),
        in_specs=[pl.BlockSpec((tm, a.shape[1]), lambda i: (i, 0)) for a in a_list]
        + [pl.BlockSpec(w.shape, lambda i: (0, 0)) for w in w_list]
        + [pl.BlockSpec((tm, d), lambda i: (i, 0)), pl.BlockSpec((1, d), lambda i: (0, 0)),
           pl.BlockSpec((1, d), lambda i: (0, 0))],
        out_specs=pl.BlockSpec((tm, d), lambda i: (i, 0)),
        out_shape=jax.ShapeDtypeStruct((m, d), F32),
        compiler_params=_cparams("parallel"), name="proj_ln")(*a_list, *w_list, x_res, g, b)


def _mem_attention(q, mk, mv):
    outs = []
    for h in range(H_X):
        sl = slice(h * DH_X, (h + 1) * DH_X)
        s = lax.dot_general(q[:, sl].astype(BF16), mk[:, sl].astype(BF16), (((1,), (1,)), ((), ())),
                            preferred_element_type=F32) * (DH_X ** -0.5)
        s = s - jnp.max(s, axis=-1, keepdims=True)
        p = jnp.exp(s)
        p = p / jnp.sum(p, axis=-1, keepdims=True)
        outs.append(jnp.dot(p.astype(BF16), mv[:, sl].astype(BF16), preferred_element_type=F32))
    return jnp.concatenate(outs, axis=-1)


def _route(x, wr, br):
    logits = jnp.dot(x, wr, preferred_element_type=F32, precision=HI) + br
    lane = lax.broadcasted_iota(jnp.int32, logits.shape, 1)
    work = jnp.where(lane < N_EXPERTS, logits, -jnp.inf)
    idx_out = jnp.zeros(logits.shape, jnp.int32)
    val_out = jnp.full(logits.shape, -jnp.inf, F32)
    for k in range(TOP_K):
        m = jnp.max(work, axis=-1, keepdims=True)
        sel = jnp.min(jnp.where(work == m, lane, LANES), axis=-1, keepdims=True)
        idx_out = jnp.where(lane == k, sel, idx_out)
        val_out = jnp.where(lane == k, m, val_out)
        work = jnp.where(lane == sel, -jnp.inf, work)
    e = jnp.exp(val_out - jnp.max(val_out, axis=-1, keepdims=True))
    gates = e / jnp.sum(e, axis=-1, keepdims=True)
    return idx_out, gates


def _xattn_prompt_kernel(x_ref, mk_ref, mv_ref, wq_ref, wo_ref, g_ref, b_ref, wr_ref, br_ref,
                         o_ref, idx_ref, gate_ref):
    x = x_ref[...]
    q = jnp.dot(x.astype(BF16), wq_ref[...], preferred_element_type=F32)
    o = _mem_attention(q, mk_ref[0], mv_ref[0])
    h = jnp.dot(o.astype(BF16), wo_ref[...], preferred_element_type=F32)
    y = _layer_norm(DN_ALPHA * x + h, g_ref[...], b_ref[...])
    o_ref[...] = y
    idx, gates = _route(y, wr_ref[...], br_ref[...])
    idx_ref[...] = idx
    gate_ref[...] = gates


def xattn_prompt(x, mk, mv, wq, wo, g, b, wr, br, tq=512):
    m, d = x.shape
    bsz = mk.shape[0]
    t = m // bsz
    nq = t // tq
    full = lambda shape: pl.BlockSpec(shape, lambda bi, qi: (0,) * len(shape))
    row = lambda w: pl.BlockSpec((tq, w), lambda bi, qi: (bi * nq + qi, 0))
    return pl.pallas_call(
        _xattn_prompt_kernel, grid=(bsz, nq),
        in_specs=[row(d), pl.BlockSpec((1, N_MEM, X_W), lambda bi, qi: (bi, 0, 0)),
                  pl.BlockSpec((1, N_MEM, X_W), lambda bi, qi: (bi, 0, 0)),
                  full(wq.shape), full(wo.shape), full(g.shape), full(b.shape), full(wr.shape), full(br.shape)],
        out_specs=[row(d), row(LANES), row(LANES)],
        out_shape=[jax.ShapeDtypeStruct((m, d), F32), jax.ShapeDtypeStruct((m, LANES), jnp.int32),
                   jax.ShapeDtypeStruct((m, LANES), F32)],
        compiler_params=_cparams("parallel", "parallel"), name="xattn_prompt")(x, mk, mv, wq, wo, g, b, wr, br)


def _xattn_sample_kernel(q_ref, mk_ref, mv_ref, o_ref):
    q = jnp.broadcast_to(q_ref[0], (8, X_W))
    o = _mem_attention(q, mk_ref[0], mv_ref[0])
    o_ref[0] = o[0:1]


def xattn_sample_core(q, mk, mv, off=0):
    bsz = q.shape[0]
    spec3 = pl.BlockSpec((1, N_MEM, X_W), lambda bi: (off + bi, 0, 0))
    out = pl.pallas_call(
        _xattn_sample_kernel, grid=(bsz,),
        in_specs=[pl.BlockSpec((1, 1, X_W), lambda bi: (bi, 0, 0)), spec3, spec3],
        out_specs=pl.BlockSpec((1, 1, X_W), lambda bi: (bi, 0, 0)),
        out_shape=jax.ShapeDtypeStruct((bsz, 1, X_W), F32),
        compiler_params=_cparams("parallel"), name="xattn_sample")(q.reshape(bsz, 1, X_W), mk, mv)
    return out.reshape(bsz, X_W)


def _ln_route_kernel(a_ref, w_ref, x_ref, g_ref, b_ref, wr_ref, br_ref, o_ref, idx_ref, gate_ref):
    h = jnp.dot(a_ref[...].astype(BF16), w_ref[...], preferred_element_type=F32)
    y = _layer_norm(DN_ALPHA * x_ref[...] + h, g_ref[...], b_ref[...])
    o_ref[...] = y
    idx, gates = _route(y, wr_ref[...], br_ref[...])
    idx_ref[...] = idx
    gate_ref[...] = gates


def proj_ln_route(a, w, x_res, g, b, wr, br):
    m, d = x_res.shape
    return pl.pallas_call(
        _ln_route_kernel,
        out_shape=[jax.ShapeDtypeStruct((m, d), F32), jax.ShapeDtypeStruct((m, LANES), jnp.int32),
                   jax.ShapeDtypeStruct((m, LANES), F32)],
        compiler_params=pltpu.CompilerParams(vmem_limit_bytes=VMEM_LIMIT),
        name="proj_ln_route")(a, w, x_res, g, b, wr, br)


def _expert_onehots(idx):
    lane = lax.broadcasted_iota(jnp.int32, idx.shape, 1)
    onehots = [(idx[:, k:k + 1] == lane).astype(F32) for k in range(TOP_K)]
    tot = onehots[0]
    for k in range(1, TOP_K):
        tot = tot + onehots[k]
    return lane, onehots, tot


def _count_kernel(idx_ref, cnt_ref):
    @pl.when(pl.program_id(0) == 0)
    def _():
        cnt_ref[...] = jnp.zeros_like(cnt_ref)

    _, _, tot = _expert_onehots(idx_ref[...])
    cnt_ref[...] = cnt_ref[...] + jnp.sum(tot, axis=0, keepdims=True)


def _dest_kernel(idx_ref, base0_ref, dest_ref, base_ref):
    @pl.when(pl.program_id(0) == 0)
    def _():
        base_ref[...] = base0_ref[...]

    idx = idx_ref[...]
    tr = idx.shape[0]
    lane, onehots, tot = _expert_onehots(idx)
    r = lax.broadcasted_iota(jnp.int32, (tr, tr), 0)
    c = lax.broadcasted_iota(jnp.int32, (tr, tr), 1)
    strict = (c < r).astype(BF16)
    before = jnp.dot(strict, tot.astype(BF16), preferred_element_type=F32) + base_ref[...]
    out = jnp.zeros(idx.shape, F32)
    for k in range(TOP_K):
        out = jnp.where(lane == k, jnp.sum(onehots[k] * before, axis=-1, keepdims=True), out)
    dest_ref[...] = out.astype(jnp.int32)
    base_ref[...] = base_ref[...] + jnp.sum(tot, axis=0, keepdims=True)


def route_slots(idx, bm, tr=256):
    t = idx.shape[0]
    tr = min(tr, t)
    assert t % tr == 0
    rows = pl.BlockSpec((tr, LANES), lambda i: (i, 0))
    one = pl.BlockSpec((1, LANES), lambda i: (0, 0))
    counts = pl.pallas_call(
        _count_kernel, grid=(t // tr,), in_specs=[rows], out_specs=one,
        out_shape=jax.ShapeDtypeStruct((1, LANES), F32),
        compiler_params=_cparams("arbitrary"), name="route_count")(idx)
    counts = counts[0].astype(jnp.int32)
    padded = (counts + bm - 1) // bm * bm
    pad_end = jnp.cumsum(padded)
    base0 = (pad_end - padded).astype(F32).reshape(1, LANES)
    dest = pl.pallas_call(
        _dest_kernel, grid=(t // tr,), in_specs=[rows, one], out_specs=rows,
        out_shape=jax.ShapeDtypeStruct((t, LANES), jnp.int32),
        scratch_shapes=[pltpu.VMEM((1, LANES), F32)],
        compiler_params=_cparams("arbitrary"), name="route_dest")(idx, base0)
    return dest, pad_end[:N_EXPERTS]


def _clamped_swiglu(h):
    glu = jnp.minimum(h[:, :D_FF], SWIGLU_LIMIT)
    lin = jnp.clip(h[:, D_FF:], -SWIGLU_LIMIT, SWIGLU_LIMIT)
    return glu * jax.nn.sigmoid(SWIGLU_ALPHA * glu) * (lin + 1.0)


def _moe_gmm_kernel(bexp_ref, nused_ref, tok_ref, tok_next_ref, dst_ref, x_hbm, win_ref, bin_ref, wout_ref,
                    bout_ref, y_hbm, xbuf, ybuf, winb, woutb, gsem, ssem, *, bm):
    i = pl.program_id(0)
    n = pl.num_programs(0)
    nused = nused_ref[0]
    slot = lax.rem(i, 2)

    d = x_hbm.shape[1]

    def gather_copy(tok, r, s):
        return pltpu.make_async_copy(x_hbm.at[pl.ds(tok, 1)], xbuf.at[s, pl.ds(r, 1)], gsem.at[s])

    def scatter_copy(dst, r, s):
        row = lax.shift_right_logical(dst, 2)
        col = pl.multiple_of(lax.bitwise_and(dst, TOP_K - 1) * d, d)
        return pltpu.make_async_copy(ybuf.at[s, pl.ds(r, 1)], y_hbm.at[pl.ds(row, 1), pl.ds(col, d)], ssem.at[s])

    def start_rows(copy_fn, idx_ref, s):
        def body(g, carry):
            for u in range(DMA_UNROLL):
                r = g * DMA_UNROLL + u
                copy_fn(idx_ref[0, 0, r], r, s).start(priority=u % 2)
            return carry
        lax.fori_loop(0, bm // DMA_UNROLL, body, 0)

    def wait_gather(s):
        pltpu.make_async_copy(x_hbm.at[pl.ds(0, bm)], xbuf.at[s], gsem.at[s]).wait()

    def wait_scatter(s):
        pltpu.make_async_copy(ybuf.at[s], y_hbm.at[pl.ds(0, bm), pl.ds(0, d)], ssem.at[s]).wait()

    @pl.when(i == 0)
    def _():
        start_rows(gather_copy, tok_ref, 0)
        ybuf[...] = jnp.zeros_like(ybuf)
        q = bm // TOP_K
        n_real = y_hbm.shape[0] - 2 * q
        for s in range(2):
            cps = [pltpu.make_async_copy(ybuf.at[s, pl.ds(k * q, q)],
                                         y_hbm.at[pl.ds(n_real + s * q, q), pl.ds(k * d, d)], ssem.at[s])
                   for k in range(TOP_K)]
            for cp in cps:
                cp.start()
            for cp in cps:
                cp.wait()

    @pl.when(jnp.logical_and(i + 1 < n, i + 1 < nused))
    def _():
        start_rows(gather_copy, tok_next_ref, 1 - slot)

    active = i < nused

    @pl.when(jnp.logical_and(active, jnp.logical_or(i == 0, bexp_ref[i] != bexp_ref[jnp.maximum(i - 1, 0)])))
    def _():
        winb[...] = win_ref[0].astype(BF16)
        woutb[...] = wout_ref[0].astype(BF16)

    @pl.when(active)
    def _():
        wait_gather(slot)

        @pl.when(i >= 2)
        def _():
            wait_scatter(slot)

        h = jnp.dot(xbuf[slot].astype(BF16), winb[...], preferred_element_type=F32) + bin_ref[0]
        act = _clamped_swiglu(h)
        ybuf[slot] = jnp.dot(act.astype(BF16), woutb[...], preferred_element_type=F32) + bout_ref[0]
        start_rows(scatter_copy, dst_ref, slot)

    @pl.when(i == n - 1)
    def _():
        last = nused - 1

        @pl.when(nused >= 2)
        def _():
            wait_scatter(lax.rem(last + 1, 2))

        @pl.when(nused >= 1)
        def _():
            wait_scatter(lax.rem(last, 2))


DMA_UNROLL = 8


def moe_gmm(x, row_tok, row_dst, block_expert, n_used, w_in, b_in, w_out, b_out, n_out_rows, bm):
    n_blocks = row_tok.shape[0]
    d = x.shape[1]
    idx_spec = lambda off: pl.BlockSpec(
        (1, 1, bm), lambda i, be, nu: (jnp.minimum(i + off, n_blocks - 1), 0, 0), memory_space=pltpu.SMEM)
    ex = lambda i, be, nu: (be[i], 0, 0)
    grid_spec = pltpu.PrefetchScalarGridSpec(
        num_scalar_prefetch=2, grid=(n_blocks,),
        in_specs=[idx_spec(0), idx_spec(1), idx_spec(0),
                  pl.BlockSpec(memory_space=pl.ANY),
                  pl.BlockSpec((1, d, 2 * D_FF), ex), pl.BlockSpec((1, 1, 2 * D_FF), ex),
                  pl.BlockSpec((1, D_FF, d), ex), pl.BlockSpec((1, 1, d), ex)],
        out_specs=pl.BlockSpec(memory_space=pl.ANY),
        scratch_shapes=[pltpu.VMEM((2, bm, d), F32), pltpu.VMEM((2, bm, d), F32),
                        pltpu.VMEM((d, 2 * D_FF), BF16), pltpu.VMEM((D_FF, d), BF16),
                        pltpu.SemaphoreType.DMA((2,)), pltpu.SemaphoreType.DMA((2,))])
    return pl.pallas_call(
        functools.partial(_moe_gmm_kernel, bm=bm), grid_spec=grid_spec,
        out_shape=jax.ShapeDtypeStruct((n_out_rows // TOP_K, TOP_K * d), F32),
        compiler_params=_cparams("arbitrary"), name="moe_gmm")(
            block_expert, n_used, row_tok, row_tok, row_dst, x, w_in, b_in.reshape(b_in.shape[0], 1, -1),
            w_out, b_out.reshape(b_out.shape[0], 1, -1))


def _combine_ln_kernel(y_ref, gate_ref, x_ref, g_ref, b_ref, o_ref):
    d = x_ref.shape[1]
    gates = gate_ref[...]
    acc = gates[:, 0:1] * y_ref[:, 0:d]
    for k in range(1, TOP_K):
        acc = acc + gates[:, k:k + 1] * y_ref[:, k * d:(k + 1) * d]
    o_ref[...] = _layer_norm(DN_ALPHA * x_ref[...] + acc, g_ref[...], b_ref[...])


def combine_ln(y4, gates, x_res, g, b, tc=256):
    t, d = x_res.shape
    tc = min(tc, t)
    assert t % tc == 0
    return pl.pallas_call(
        _combine_ln_kernel, grid=(t // tc,),
        in_specs=[pl.BlockSpec((tc, TOP_K * d), lambda i: (i, 0)), pl.BlockSpec((tc, LANES), lambda i: (i, 0)),
                  pl.BlockSpec((tc, d), lambda i: (i, 0)), pl.BlockSpec((1, d), lambda i: (0, 0)),
                  pl.BlockSpec((1, d), lambda i: (0, 0))],
        out_specs=pl.BlockSpec((tc, d), lambda i: (i, 0)),
        out_shape=jax.ShapeDtypeStruct((t, d), F32),
        compiler_params=_cparams("parallel"), name="combine_ln")(y4, gates, x_res, g, b)


def moe_ln(x, idx, gates, w_in, b_in, w_out, b_out, g, b, bm, e_off=0):
    t, d = x.shape
    tk = t * TOP_K
    dest, pad_end = route_slots(idx, bm)
    n_blocks = -(-tk // bm) + N_EXPERTS
    n_rows = n_blocks * bm
    src = jnp.arange(tk, dtype=jnp.int32)
    row_src = jnp.full((n_rows,), -1, jnp.int32).at[dest[:, :TOP_K].reshape(tk)].set(
        src, unique_indices=True, mode='promise_in_bounds')
    slot = jnp.arange(n_rows, dtype=jnp.int32)
    slot_in_pair = (slot // bm % 2) * bm + slot % bm
    row_tok = jnp.where(row_src >= 0, row_src // TOP_K, 0).reshape(n_blocks, 1, bm)
    row_dst = jnp.where(row_src >= 0, row_src, tk + slot_in_pair).reshape(n_blocks, 1, bm)
    block_start = jnp.arange(n_blocks, dtype=jnp.int32) * bm
    block_expert = jnp.sum((pad_end[None, :] <= block_start[:, None]).astype(jnp.int32), axis=1)
    block_expert = jnp.minimum(block_expert, N_EXPERTS - 1) + e_off
    n_used = (pad_end[-1] // bm).astype(jnp.int32).reshape(1)
    y4 = moe_gmm(x, row_tok, row_dst, block_expert, n_used, w_in, b_in, w_out, b_out, tk + 2 * bm, bm)
    return combine_ln(y4, gates, x, g, b)


def _t5_causal_bucket(dist):
    n = jnp.maximum(dist, 0)
    max_exact = N_BUCKETS // 2
    nf = jnp.maximum(n, max_exact).astype(F32)
    large = max_exact + (jnp.log(nf / max_exact) / math.log(MAX_DISTANCE / max_exact)
                         * (N_BUCKETS - max_exact)).astype(jnp.int32)
    large = jnp.minimum(large, N_BUCKETS - 1)
    return jnp.where(n < max_exact, n, large)


def _split_maps(q):
    lane = lax.broadcasted_iota(jnp.int32, q.shape, 1)
    qs = q * (DH_C ** -0.5)
    return (jnp.where(lane < DH_C, qs, 0.0).astype(BF16), jnp.where(lane >= DH_C, qs, 0.0).astype(BF16))


def _diff_finish(acc1, l1, acc2, l2, lam, out_scale, g):
    o = acc1 / l1 - lam * (acc2 / l2)
    o = o * lax.rsqrt(jnp.mean(o * o, axis=-1, keepdims=True) + NORM_EPS) * g
    return o * out_scale


def _diff_prompt_kernel(sc_ref, far_ref, q_ref, k_ref, v_ref, bias_ref, g_ref, o_ref, m1, a1, m2, a2, *, tq):
    h = pl.program_id(1)
    qi = pl.program_id(2)
    qm = _split_maps(q_ref[...])
    stats = ((m1, a1), (m2, a2))
    for m_ref, a_ref in stats:
        m_ref[...] = jnp.full_like(m_ref, -jnp.inf)
        a_ref[...] = jnp.zeros_like(a_ref)
    ones_col = (lax.broadcasted_iota(jnp.int32, (tq, LANES), 1) == 0).astype(BF16)
    reps = tq // LANES

    def tile(kidx, bias, shift, causal):
        start = pl.multiple_of(kidx * tq, tq)
        kt = k_ref[0, pl.ds(start, tq), :].astype(BF16)
        vt = jnp.concatenate([v_ref[0, pl.ds(start, tq), :].astype(BF16), ones_col], axis=-1)
        for qmap, (m_ref, a_ref) in zip(qm, stats):
            s = lax.dot_general(qmap, kt, (((1,), (1,)), ((), ())), preferred_element_type=F32)
            if bias is not None:
                s = s + bias
            if causal:
                r = lax.broadcasted_iota(jnp.int32, s.shape, 0)
                c = lax.broadcasted_iota(jnp.int32, s.shape, 1)
                s = jnp.where(c <= r, s, -jnp.inf)
            m_old = m_ref[...]
            m_new = jnp.maximum(m_old, jnp.max(s, axis=-1, keepdims=True) + shift)
            alpha = jnp.exp(m_old - m_new)
            off = m_new - shift
            p = jnp.exp(s - jnp.concatenate([off] * reps, axis=-1))
            a_ref[...] = (jnp.concatenate([alpha, alpha], axis=-1) * a_ref[...]
                          + jnp.dot(p.astype(BF16), vt, preferred_element_type=F32))
            m_ref[...] = m_new

    far_bias = far_ref[h]

    def far_body(kidx, carry):
        tile(kidx, None, far_bias, False)
        return carry

    lax.fori_loop(0, jnp.maximum(qi - 1, 0), far_body, 0)

    @pl.when(qi >= 1)
    def _():
        tile(qi - 1, bias_ref[0, 1], 0.0, False)

    tile(qi, bias_ref[0, 0], 0.0, True)
    acc1, acc2 = a1[...], a2[...]
    o_ref[...] = _diff_finish(acc1[:, :VD_C], acc1[:, VD_C:VD_C + 1], acc2[:, :VD_C], acc2[:, VD_C:VD_C + 1],
                              sc_ref[0], sc_ref[1], g_ref[...])


def diff_attn_prompt_core(q, k, v, bsz, rel_bias, lam, lam_init, subln_g, tq=256):
    m = q.shape[0]
    t = m // bsz
    nq = t // tq
    tab = rel_bias[_t5_causal_bucket(jnp.arange(2 * tq))].T.astype(F32)
    ii = jnp.arange(tq)[:, None]
    jj = jnp.arange(tq)[None, :]
    bias = jnp.stack([tab[:, jnp.maximum(ii - jj, 0)], tab[:, tq + ii - jj]], axis=1)
    far = rel_bias[_t5_causal_bucket(jnp.array(2 * tq))].astype(F32)
    scal = jnp.stack([lam, 1.0 - lam_init]).astype(F32)
    k3 = k.reshape(bsz, t, C_W)
    v3 = v.reshape(bsz, t, C_W)
    smem = pl.BlockSpec(memory_space=pltpu.SMEM)
    kv_spec = pl.BlockSpec((1, t, VD_C), lambda b, h, i: (b, 0, h))
    row = pl.BlockSpec((tq, VD_C), lambda b, h, i: (b * nq + i, h))
    stat = pltpu.VMEM((tq, LANES), F32)
    acc = pltpu.VMEM((tq, VD_C + LANES), F32)
    return pl.pallas_call(
        functools.partial(_diff_prompt_kernel, tq=tq), grid=(bsz, H_C, nq),
        in_specs=[smem, smem, row, kv_spec, kv_spec,
                  pl.BlockSpec((1, 2, tq, tq), lambda b, h, i: (h, 0, 0, 0)),
                  pl.BlockSpec((1, VD_C), lambda b, h, i: (0, 0))],
        out_specs=row, out_shape=jax.ShapeDtypeStruct((m, C_W), F32),
        scratch_shapes=[stat, acc, stat, acc],
        compiler_params=_cparams("parallel", "parallel", "parallel"), name="diff_attn_prompt")(
            scal, far, q, k3, v3, bias, subln_g.reshape(1, VD_C))


def _diff_sample_kernel(pt_ref, sc_ref, q_ref, kn_ref, vn_ref, bm_ref, bnew_ref, g_ref, *refs, n_pp):
    k_refs, v_refs = refs[:n_pp], refs[n_pp:2 * n_pp]
    o_ref, m_ref, l_ref, a_ref = refs[2 * n_pp:]
    p = pl.program_id(1)
    n_p = pl.num_programs(1)
    q8 = q_ref[0]
    qm = jnp.concatenate(_split_maps(q8), axis=0)

    @pl.when(p == 0)
    def _():
        m_ref[...] = jnp.full_like(m_ref, -jnp.inf)
        l_ref[...] = jnp.zeros_like(l_ref)
        a_ref[...] = jnp.zeros_like(a_ref)

    scores = []
    for j in range(n_pp):
        bias = bm_ref[jnp.where(p == n_p - 1, 1, 0)] if j == n_pp - 1 else bm_ref[0]
        k2 = k_refs[j][0, 0].reshape(-1, 2 * DH_C).astype(BF16)
        scores.append(lax.dot_general(qm, k2, (((1,), (1,)), ((), ())), preferred_element_type=F32) + bias)
    m_old = m_ref[...]
    m_new = m_old
    for s in scores:
        m_new = jnp.maximum(m_new, jnp.max(s, axis=-1, keepdims=True))
    alpha = jnp.exp(m_old - m_new)
    l_new = alpha * l_ref[...]
    a_new = alpha * a_ref[...]
    for j, s in enumerate(scores):
        pr = jnp.exp(s - m_new)
        l_new = l_new + jnp.sum(pr, axis=-1, keepdims=True)
        v2 = v_refs[j][0, 0].reshape(-1, VD_C).astype(BF16)
        a_new = a_new + jnp.dot(pr.astype(BF16), v2, preferred_element_type=F32)
    l_ref[...] = l_new
    a_ref[...] = a_new
    m_ref[...] = m_new

    @pl.when(p == n_p - 1)
    def _():
        kn = jnp.concatenate([kn_ref[0], kn_ref[0]], axis=0)
        vn = jnp.concatenate([vn_ref[0], vn_ref[0]], axis=0)
        s = jnp.sum(qm.astype(F32) * kn, axis=-1, keepdims=True) + bnew_ref[:, 0:1]
        m_old = m_ref[...]
        m_new = jnp.maximum(m_old, s)
        alpha = jnp.exp(m_old - m_new)
        pr = jnp.exp(s - m_new)
        l = alpha * l_ref[...] + pr
        a = alpha * a_ref[...] + pr * vn
        o_ref[0] = _diff_finish(a[:H_C], l[:H_C], a[H_C:], l[H_C:], sc_ref[0], sc_ref[1], g_ref[...])


def diff_attn_sample_core(q, k_new, v_new, cache_k, cache_v, page_table, layer_j, rel_bias, lam, lam_init, subln_g,
                          n_pp=8):
    bsz = q.shape[0]
    n_pages = page_table.shape[1]
    page = cache_k.shape[2]
    past = n_pages * page
    assert n_pages % n_pp == 0
    dist_last = past - ((n_pages - 1) * page + jnp.arange(page))
    b_last = rel_bias[_t5_causal_bucket(dist_last)].astype(F32)
    b_far = jnp.broadcast_to(rel_bias[_t5_causal_bucket(jnp.array(page + 1))].astype(F32), (page, H_C))
    eye = jnp.eye(H_C, dtype=bool)

    def expand(bt):
        full = jnp.where(eye[:, None, :], bt.T[:, :, None], -jnp.inf).reshape(H_C, page * H_C)
        return jnp.concatenate([full, full], axis=0)

    bm = jnp.stack([expand(b_far), expand(b_last)])
    b_new = rel_bias[_t5_causal_bucket(jnp.array(0))].astype(F32)
    b_new = jnp.broadcast_to(jnp.concatenate([b_new, b_new])[:, None], (2 * H_C, LANES))
    scal = jnp.stack([lam, 1.0 - lam_init]).astype(F32)
    smem = pl.BlockSpec(memory_space=pltpu.SMEM)
    head3 = pl.BlockSpec((1, H_C, VD_C), lambda b, p, pt: (b, 0, 0))
    full = lambda shape: pl.BlockSpec(shape, lambda b, p, pt: (0,) * len(shape))

    def page_spec(j):
        return pl.BlockSpec((1, 1, page, H_C, VD_C), lambda b, p, pt: (pt[b, p * n_pp + j], layer_j, 0, 0, 0))

    grid_spec = pltpu.PrefetchScalarGridSpec(
        num_scalar_prefetch=1, grid=(bsz, n_pages // n_pp),
        in_specs=[smem, head3, head3, head3, full(bm.shape), full(b_new.shape), full((1, VD_C))]
        + [page_spec(j) for j in range(n_pp)] * 2,
        out_specs=head3,
        scratch_shapes=[pltpu.VMEM((2 * H_C, 1), F32), pltpu.VMEM((2 * H_C, 1), F32),
                        pltpu.VMEM((2 * H_C, VD_C), F32)])
    out = pl.pallas_call(
        functools.partial(_diff_sample_kernel, n_pp=n_pp), grid_spec=grid_spec,
        out_shape=jax.ShapeDtypeStruct((bsz, H_C, VD_C), F32),
        compiler_params=_cparams("parallel", "arbitrary"), name="diff_attn_sample")(
            page_table, scal, q.reshape(bsz, H_C, VD_C), k_new.reshape(bsz, H_C, VD_C),
            v_new.reshape(bsz, H_C, VD_C), bm, b_new, subln_g.reshape(1, VD_C),
            *([cache_k] * n_pp), *([cache_v] * n_pp))
    return out.reshape(bsz, C_W)


LANE_BETA, LANE_A, LANE_I, LANE_F = 0, H_A, 2 * H_A, 2 * H_A + H_B


def _softplus(x):
    return jnp.maximum(x, 0.0) + jnp.log1p(jnp.exp(-jnp.abs(x)))


def _silu(x):
    return x * jax.nn.sigmoid(x)


def _lanes(shape, lo, n):
    lane = lax.broadcasted_iota(jnp.int32, shape, 1)
    return jnp.logical_and(lane >= lo, lane < lo + n)


def _gate_tile(gt, alog_row, prow):
    z = gt + prow
    return jax.nn.sigmoid(gt), -jnp.exp(alog_row) * _softplus(z), z, -_softplus(-z)


def _nt(a, b, precision=None):
    return lax.dot_general(a, b, (((1,), (1,)), ((), ())), preferred_element_type=F32, precision=precision)


def _tn(a, b):
    return lax.dot_general(a, b, (((0,), (0,)), ((), ())), preferred_element_type=F32)


def _row_selector(n_heads, length, lanes_of_head):
    r = lax.broadcasted_iota(jnp.int32, (n_heads * length, LANES), 0) // length
    lane = lax.broadcasted_iota(jnp.int32, (n_heads * length, LANES), 1)
    sel = jnp.zeros((n_heads * length, LANES), F32)
    for lo in lanes_of_head:
        sel = sel + (lane == r + lo).astype(F32)
    return sel


def _tri_masks(n):
    r = lax.broadcasted_iota(jnp.int32, (n, n), 0)
    c = lax.broadcasted_iota(jnp.int32, (n, n), 1)
    return r, c


def _bmm(a, b):
    return lax.dot_general(a.astype(BF16), b.astype(BF16), (((2,), (1,)), ((0,), (0,))),
                           preferred_element_type=F32)


def _bmm_nt(a, b):
    return lax.dot_general(a.astype(BF16), b.astype(BF16), (((2,), (2,)), ((0,), (0,))),
                           preferred_element_type=F32)


def _heads(x, n_heads, width, offset=0):
    return jnp.stack([x[:, offset + h * width:offset + (h + 1) * width] for h in range(n_heads)], axis=0)


def _head_cols(x, n_heads, lane0):
    return jnp.stack([x[:, lane0 + h:lane0 + h + 1] for h in range(n_heads)], axis=0)


def _unit_lower_inverse(nmat, r, c):
    mm = _bmm
    eye = (r == c).astype(F32)
    same = (r // 16) == (c // 16)
    nd = jnp.where(same, nmat, 0.0)
    off = nmat - nd
    dinv = eye - nd
    p = nd
    for _ in range(3):
        p = mm(p, p)
        dinv = dinv + mm(dinv, p)
    m = mm(dinv, off)
    m2 = mm(m, m)
    left = eye - m
    left = left + mm(left, m2)
    return mm(left, dinv)


def _gdn_prompt_kernel(qkv_ref, z_ref, gt_ref, convw_ref, alog_ref, prow_ref, ng_ref, mix_ref, s_ref, ext_ref):
    cidx = pl.program_id(1)
    L = CHUNK

    @pl.when(cidx == 0)
    def _():
        ext_ref[0:8, :] = jnp.zeros((8, QKV_A), F32)
        s_ref[...] = jnp.zeros_like(s_ref)

    ext_ref[8:8 + L, :] = qkv_ref[...]
    acc = convw_ref[0:1, :] * ext_ref[pl.ds(8 - (CONV_W - 1), L), :]
    for i in range(1, CONV_W):
        acc = acc + convw_ref[i:i + 1, :] * ext_ref[pl.ds(8 - (CONV_W - 1) + i, L), :]
    ext_ref[0:8, :] = ext_ref[L:L + 8, :]
    cs = _silu(acc)

    beta, g, _, _ = _gate_tile(gt_ref[...], alog_ref[...], prow_ref[...])
    r, c = _tri_masks(L)
    incl = c <= r
    strict = c < r
    gsel = jnp.where(_lanes(g.shape, LANE_A, H_A), g, 0.0)
    gcum = jnp.dot(incl.astype(F32), gsel, preferred_element_type=F32, precision=HI)
    gc_rows = _nt(_row_selector(H_A, L, (LANE_A,)), gcum, HI)
    q = _heads(cs, H_A, DK_A)
    k = _heads(cs, H_A, DK_A, H_A * DK_A)
    v = _heads(cs, H_A, DV_A, 2 * H_A * DK_A)
    q = q * lax.rsqrt(jnp.sum(q * q, axis=-1, keepdims=True) + NORM_EPS) * (DK_A ** -0.5)
    k = k * lax.rsqrt(jnp.sum(k * k, axis=-1, keepdims=True) + NORM_EPS)
    beta_c = _head_cols(beta, H_A, LANE_BETA)
    gc_c = _head_cols(gcum, H_A, LANE_A)
    diff = gc_c - gc_rows.reshape(H_A, L, L)
    decay = jnp.where(incl, jnp.exp(jnp.where(incl, diff, 0.0)), 0.0)
    kb = k.astype(BF16)
    nmat = jnp.where(strict, beta_c * _bmm_nt(kb, kb) * decay, 0.0)
    egc = jnp.exp(gc_c)
    rhs = jnp.concatenate([v * beta_c, k * (beta_c * egc)], axis=-1)
    sol = _bmm(_unit_lower_inverse(nmat, r, c), rhs)
    u, w = sol[:, :, :DV_A], sol[:, :, DV_A:]
    qk = _bmm_nt(q, kb) * decay
    gc_last = gc_c[:, L - 1:L]
    k_tail = (k * jnp.exp(gc_last - gc_c)).astype(BF16)
    s_old = s_ref[0]
    ws = _bmm(jnp.concatenate([w, q * egc], axis=1), s_old)
    delta = u - ws[:, :L]
    o = ws[:, L:] + _bmm(qk, delta)
    delta_b = delta.astype(BF16)
    s_ref[0] = s_old * jnp.exp(gc_last) + jnp.stack([_tn(k_tail[h], delta_b[h]) for h in range(H_A)], axis=0)
    o = o * lax.rsqrt(jnp.mean(o * o, axis=-1, keepdims=True) + NORM_EPS) * ng_ref[...]
    o = o * _silu(_heads(z_ref[...], H_A, DV_A))
    mix_ref[...] = jnp.concatenate([o[h] for h in range(H_A)], axis=-1)


def gdn_prompt(qkv, z, gates, conv_w, alog_row, prow, norm_g, bsz):
    m = qkv.shape[0]
    nc = m // bsz // CHUNK
    row = lambda w: pl.BlockSpec((CHUNK, w), lambda b, c: (b * nc + c, 0))
    full = lambda shape: pl.BlockSpec(shape, lambda b, c: (0,) * len(shape))
    return pl.pallas_call(
        _gdn_prompt_kernel, grid=(bsz, nc),
        in_specs=[row(QKV_A), row(H_A * DV_A), row(LANES), full(conv_w.shape), full((1, LANES)), full((1, LANES)),
                  full((1, DV_A))],
        out_specs=[row(H_A * DV_A), pl.BlockSpec((1, H_A, DK_A, DV_A), lambda b, c: (b, 0, 0, 0))],
        out_shape=[jax.ShapeDtypeStruct((m, H_A * DV_A), F32), jax.ShapeDtypeStruct((bsz, H_A, DK_A, DV_A), F32)],
        scratch_shapes=[pltpu.VMEM((CHUNK + 8, QKV_A), F32)],
        compiler_params=_cparams("parallel", "arbitrary"), name="gdn_prompt")(
            qkv, z, gates, conv_w, alog_row, prow, norm_g.reshape(1, DV_A))


def _mlstm_prompt_kernel(q_ref, k_ref, v_ref, og_ref, gt_ref, alog_ref, prow_ref, ng_ref, mix_ref, c_ref, m_ref):
    cidx = pl.program_id(1)
    L = CHUNK

    @pl.when(cidx == 0)
    def _():
        c_ref[...] = jnp.zeros_like(c_ref)
        m_ref[...] = jnp.zeros_like(m_ref)

    _, _, ipre, logf = _gate_tile(gt_ref[...], alog_ref[...], prow_ref[...])
    r, c = _tri_masks(L)
    incl = c <= r
    fsel = _lanes(logf.shape, LANE_F, H_B)
    bcum = jnp.dot(incl.astype(F32), jnp.where(fsel, logf, 0.0), preferred_element_type=F32, precision=HI)
    rowvals = jnp.where(_lanes(ipre.shape, LANE_I, H_B), ipre, 0.0) - jnp.where(fsel, bcum, 0.0)
    rows = _nt(_row_selector(H_B, L, (LANE_I, LANE_F)), rowvals, HI)
    mrow = m_ref[0]
    lane_row = lax.broadcasted_iota(jnp.int32, mrow.shape, 1)
    qb = _heads(q_ref[...], H_B, DQK_B).astype(BF16)
    ks = _heads(k_ref[...], H_B, DQK_B) * (DQK_B ** -0.5)
    ones_col = jnp.broadcast_to((lax.broadcasted_iota(jnp.int32, (L, LANES), 1) == 0).astype(F32), (H_B, L, LANES))
    v_ext = jnp.concatenate([_heads(v_ref[...], H_B, DV_B), ones_col], axis=-1).astype(BF16)
    b_c = _head_cols(bcum, H_B, LANE_F)
    i_c = _head_cols(ipre, H_B, LANE_I)
    dmat = jnp.where(incl, b_c + rows.reshape(H_B, L, L), -jnp.inf)
    m_intra = jnp.max(dmat, axis=-1, keepdims=True)
    w_intra = jnp.exp(dmat - m_intra) * _bmm_nt(qb, ks)
    nd_intra = _bmm(w_intra, v_ext)
    b_last = b_c[:, L - 1:L]
    e_end = b_last - b_c + i_c
    e_max = jnp.max(e_end, axis=1, keepdims=True)
    kw = (ks * jnp.exp(e_end - e_max)).astype(BF16)
    kv_end = jnp.stack([_tn(kw[h], v_ext[h]) for h in range(H_B)], axis=0)
    m_prev = jnp.stack([mrow[:, h:h + 1] for h in range(H_B)], axis=0)
    inter = b_c + m_prev
    m_t = jnp.maximum(inter, m_intra)
    c_old = c_ref[0]
    nd = jnp.exp(inter - m_t) * _bmm(qb, c_old) + jnp.exp(m_intra - m_t) * nd_intra
    hh = nd[:, :, :DV_B] / jnp.maximum(jnp.abs(nd[:, :, DV_B:DV_B + 1]), jnp.exp(-m_t))
    m_new = jnp.maximum(b_last + m_prev, e_max)
    c_ref[0] = jnp.exp(b_last + m_prev - m_new) * c_old + jnp.exp(e_max - m_new) * kv_end
    for h in range(H_B):
        mrow = jnp.where(lane_row == h, m_new[h], mrow)
    m_ref[0] = mrow
    hh = hh * lax.rsqrt(jnp.mean(hh * hh, axis=-1, keepdims=True) + NORM_EPS) * ng_ref[...]
    hh = jax.nn.sigmoid(_heads(og_ref[...], H_B, DV_B)) * hh
    mix_ref[...] = jnp.concatenate([hh[h] for h in range(H_B)], axis=-1)


def mlstm_prompt(q, k, v, og, gates, alog_row, prow, norm_g, bsz):
    m = q.shape[0]
    nc = m // bsz // CHUNK
    row = lambda w: pl.BlockSpec((CHUNK, w), lambda b, c: (b * nc + c, 0))
    full = lambda shape: pl.BlockSpec(shape, lambda b, c: (0,) * len(shape))
    return pl.pallas_call(
        _mlstm_prompt_kernel, grid=(bsz, nc),
        in_specs=[row(H_B * DQK_B), row(H_B * DQK_B), row(H_B * DV_B), row(H_B * DV_B), row(LANES),
                  full((1, LANES)), full((1, LANES)), full((1, DV_B))],
        out_specs=[row(H_B * DV_B), pl.BlockSpec((1, H_B, DQK_B, DV_B + LANES), lambda b, c: (b, 0, 0, 0)),
                   pl.BlockSpec((1, 1, LANES), lambda b, c: (b, 0, 0))],
        out_shape=[jax.ShapeDtypeStruct((m, H_B * DV_B), F32),
                   jax.ShapeDtypeStruct((bsz, H_B, DQK_B, DV_B + LANES), F32),
                   jax.ShapeDtypeStruct((bsz, 1, LANES), F32)],
        compiler_params=_cparams("parallel", "arbitrary"), name="mlstm_prompt")(
            q, k, v, og, gates, alog_row, prow, norm_g.reshape(1, DV_B))


def _columns(x8):
    n = x8.shape[1]
    r, c = _tri_masks(n)
    return _nt((r == c).astype(F32), x8, HI)


def _ab_sample_kernel(qn_ref, kn_ref, vn_ref, cq_ref, ck_ref, cv_ref, wq_ref, wk_ref, wv_ref, z_ref,
                      qb_ref, kb_ref, vb_ref, og_ref, gt_ref, alog_ref, prow_ref, nga_ref, ngb_ref,
                      s_in, c_in, n_in, m_in,
                      oa_ref, ob_ref, s_out, c_out, n_out, m_out):
    def conv(new_ref, prev_ref, w_ref):
        acc = w_ref[CONV_W - 1] * new_ref[0]
        for i in range(CONV_W - 1):
            acc = acc + w_ref[i] * prev_ref[0, i]
        return _silu(acc)

    q8 = conv(qn_ref, cq_ref, wq_ref)
    k8 = conv(kn_ref, ck_ref, wk_ref)
    v8 = conv(vn_ref, cv_ref, wv_ref)
    q8 = q8 * lax.rsqrt(jnp.sum(q8 * q8, axis=-1, keepdims=True) + NORM_EPS) * (DK_A ** -0.5)
    k8 = k8 * lax.rsqrt(jnp.sum(k8 * k8, axis=-1, keepdims=True) + NORM_EPS)
    beta, g, ipre, logf = _gate_tile(gt_ref[0], alog_ref[...], prow_ref[...])
    q_cols, k_cols = _columns(q8), _columns(k8)
    z8 = z_ref[0]
    outs = []
    for h in range(H_A):
        s = s_in[0, h] * jnp.exp(g[:, LANE_A + h:LANE_A + h + 1])
        kc = k_cols[:, h:h + 1]
        err = v8[h:h + 1] - jnp.sum(kc * s, axis=0, keepdims=True)
        s = s + kc * (beta[:, LANE_BETA + h:LANE_BETA + h + 1] * err)
        s_out[0, h] = s
        outs.append(jnp.sum(q_cols[:, h:h + 1] * s, axis=0, keepdims=True))
    o = jnp.concatenate(outs, axis=0)
    o = o * lax.rsqrt(jnp.mean(o * o, axis=-1, keepdims=True) + NORM_EPS) * nga_ref[...]
    oa_ref[0] = o * _silu(z8)

    zeros4 = jnp.zeros((8 - H_B, DQK_B), F32)
    qb_cols = _columns(jnp.concatenate([qb_ref[0], zeros4], axis=0))
    kb_cols = _columns(jnp.concatenate([kb_ref[0] * (DQK_B ** -0.5), zeros4], axis=0))
    vb = vb_ref[0]
    n_cols = n_in[0]
    m_row = m_in[0]
    lane_n = lax.broadcasted_iota(jnp.int32, n_cols.shape, 1)
    lane_m = lax.broadcasted_iota(jnp.int32, m_row.shape, 1)
    outs = []
    for h in range(H_B):
        lf = logf[:, LANE_F + h:LANE_F + h + 1]
        it = ipre[:, LANE_I + h:LANE_I + h + 1]
        m_prev = m_row[:, h:h + 1]
        m_new = jnp.maximum(lf + m_prev, it)
        f_sc = jnp.exp(lf + m_prev - m_new)
        i_sc = jnp.exp(it - m_new)
        kc = kb_cols[:, h:h + 1]
        qc = qb_cols[:, h:h + 1]
        cm = f_sc * c_in[0, h] + i_sc * (kc * vb[h:h + 1])
        nn = f_sc * n_cols[:, h:h + 1] + i_sc * kc
        c_out[0, h] = cm
        n_cols = jnp.where(lane_n == h, nn, n_cols)
        m_row = jnp.where(lane_m == h, m_new, m_row)
        num = jnp.sum(qc * cm, axis=0, keepdims=True)
        den = jnp.sum(qc * nn, axis=0, keepdims=True)
        outs.append(num / jnp.maximum(jnp.abs(den), jnp.exp(-m_new)))
    hb = jnp.concatenate(outs, axis=0)
    hb = hb * lax.rsqrt(jnp.mean(hb * hb, axis=-1, keepdims=True) + NORM_EPS) * ngb_ref[...]
    ob_ref[0] = jax.nn.sigmoid(og_ref[0]) * hb
    n_out[0] = n_cols
    m_out[0] = m_row


def ab_sample(qkv, z, q_b, k_b, v_b, o_b, gates, conv_prev, conv_w, alog_row, prow, norm_g_a, norm_g_b,
              s_prev, c_prev, n_prev, m_prev):
    bsz = qkv.shape[0]
    hk = H_A * DK_A
    part = lambda x, i, w: x[..., i * hk:i * hk + H_A * w].reshape(x.shape[:-1] + (H_A, w))
    new_parts = [part(qkv, 0, DK_A), part(qkv, 1, DK_A), part(qkv, 2, DV_A)]
    prev_parts = [part(conv_prev, 0, DK_A), part(conv_prev, 1, DK_A), part(conv_prev, 2, DV_A)]
    w_parts = [part(conv_w, 0, DK_A), part(conv_w, 1, DK_A), part(conv_w, 2, DV_A)]
    args = new_parts + prev_parts + w_parts + [
        z.reshape(bsz, H_A, DV_A), q_b.reshape(bsz, H_B, DQK_B), k_b.reshape(bsz, H_B, DQK_B),
        v_b.reshape(bsz, H_B, DV_B), o_b.reshape(bsz, H_B, DV_B), gates.reshape(bsz, 1, LANES),
        alog_row, prow, norm_g_a.reshape(1, DV_A), norm_g_b.reshape(1, DV_B),
        s_prev, c_prev, jnp.swapaxes(n_prev, 1, 2), m_prev.reshape(bsz, 1, H_B)]

    def spec(x, batched):
        nd = x.ndim
        if batched:
            return pl.BlockSpec((1,) + x.shape[1:], lambda b: (b,) + (0,) * (nd - 1))
        return pl.BlockSpec(x.shape, lambda b: (0,) * nd)

    batched = [True] * 6 + [False] * 3 + [True] * 6 + [False] * 4 + [True] * 4
    out_shape = [jax.ShapeDtypeStruct((bsz, H_A, DV_A), F32), jax.ShapeDtypeStruct((bsz, H_B, DV_B), F32),
                 jax.ShapeDtypeStruct(s_prev.shape, F32), jax.ShapeDtypeStruct(c_prev.shape, F32),
                 jax.ShapeDtypeStruct((bsz, DQK_B, H_B), F32), jax.ShapeDtypeStruct((bsz, 1, H_B), F32)]
    oa, ob, s_new, c_new, n_new, m_new = pl.pallas_call(
        _ab_sample_kernel, grid=(bsz,),
        in_specs=[spec(x, bt) for x, bt in zip(args, batched)],
        out_specs=[spec(x, True) for x in out_shape], out_shape=out_shape,
        compiler_params=_cparams("parallel"), name="ab_sample")(*args)
    mix = jnp.concatenate([oa.reshape(bsz, H_A * DV_A), ob.reshape(bsz, H_B * DV_B)], axis=-1)
    return mix, s_new, c_new, jnp.swapaxes(n_new, 1, 2), m_new.reshape(bsz, H_B)


MOE_BLOCK_PROMPT = 256
MOE_BLOCK_SAMPLE = 32


def _ab_weights(w_in, a_log, dt_bias, b_i, b_f):
    sizes = (QKV_A, H_A * DV_A, H_A, H_A, H_B * DQK_B, H_B * DQK_B, H_B * DV_B, H_B * DV_B, H_B, H_B)
    offs = [0]
    for s in sizes:
        offs.append(offs[-1] + s)
    col = lambda i: w_in[:, offs[i]:offs[i + 1]]
    w_lo = tuple(col(i).astype(BF16) for i in (0, 1, 4, 5, 6, 7))
    w_gate = jnp.concatenate([col(2), col(3), col(8), col(9)], axis=1)
    w_gate = jnp.pad(w_gate, ((0, 0), (0, LANES - w_gate.shape[1])))
    zeros = lambda n: jnp.zeros((n,), F32)
    pad = LANES - 2 * H_A - 2 * H_B
    alog_row = jnp.concatenate([zeros(H_A), a_log.astype(F32), zeros(2 * H_B + pad)]).reshape(1, LANES)
    prow = jnp.concatenate([zeros(H_A), dt_bias.astype(F32), b_i.astype(F32), b_f.astype(F32),
                            zeros(pad)]).reshape(1, LANES)
    return w_lo, w_gate, alog_row, prow


def kernel(x_prompt, x_sample, state_delta_S, state_delta_conv, state_mlstm_C, state_mlstm_n, state_mlstm_m,
           cache_diff_k, cache_diff_v, cache_mem_k, cache_mem_v, page_table, mem_prompt,
           w_in_ab, conv_w_a, a_log_a, dt_bias_a, norm_g_a, b_i_b, b_f_b, norm_g_b, w_out_ab,
           w_qkv_c, lam_q1, lam_k1, lam_q2, lam_k2, subln_g_c, w_o_c, rel_bias,
           w_xq, w_xkv, w_xo, ln_g, ln_b, w_router, b_router, w_moe_in, b_moe_in, w_moe_out, b_moe_out):
    bp, t, d = x_prompt.shape
    bs = x_sample.shape[0]
    xp = x_prompt.reshape(bp * t, d)
    xs = x_sample.reshape(bs, d)
    mem2d = mem_prompt.reshape(bp * N_MEM, d)
    cmk = cache_mem_k.reshape(DEPTH * bs, N_MEM, X_W)
    cmv = cache_mem_v.reshape(DEPTH * bs, N_MEM, X_W)
    wm_in = w_moe_in.reshape(DEPTH * N_EXPERTS, d, 2 * D_FF)
    bm_in = b_moe_in.reshape(DEPTH * N_EXPERTS, 2 * D_FF)
    wm_out = w_moe_out.reshape(DEPTH * N_EXPERTS, D_FF, d)
    bm_out = b_moe_out.reshape(DEPTH * N_EXPERTS, d)
    p_S, p_conv, p_C, p_n, p_m, p_k, p_v, p_mk, p_mv = [], [], [], [], [], [], [], [], []
    s_S, s_conv, s_C, s_n, s_m, s_k, s_v = [], [], [], [], [], [], []
    for layer in range(DEPTH):
        j = layer // 2
        g0, b0 = ln_g[layer, 0].reshape(1, d), ln_b[layer, 0].reshape(1, d)
        g1, b1 = ln_g[layer, 1].reshape(1, d), ln_b[layer, 1].reshape(1, d)
        g2, b2 = ln_g[layer, 2].reshape(1, d), ln_b[layer, 2].reshape(1, d)
        if layer % 2 == 0:
            w_lo, w_gate, alog_row, prow = _ab_weights(w_in_ab[j], a_log_a[j], dt_bias_a[j], b_i_b[j], b_f_b[j])
            w_out = w_out_ab[j].astype(BF16)
            w_out_a, w_out_b = w_out[:H_A * DV_A], w_out[H_A * DV_A:]
            qkv, z, q_b, k_b, v_b, o_b, gates = mm_multi(xp, w_lo, (w_gate,), tm=256)
            mix_a, st_S = gdn_prompt(qkv, z, gates, conv_w_a[j], alog_row, prow, norm_g_a[j], bp)
            mix_b, c_ext, m_row = mlstm_prompt(q_b, k_b, v_b, o_b, gates, alog_row, prow, norm_g_b[j], bp)
            xp = proj_ln([mix_a, mix_b], [w_out_a, w_out_b], xp, g0, b0)
            p_S.append(st_S)
            p_conv.append(qkv.reshape(bp, t, QKV_A)[:, t - (CONV_W - 1):])
            p_C.append(c_ext[..., :DV_B])
            p_n.append(c_ext[..., DV_B])
            p_m.append(m_row[:, 0, :H_B])
            qkv, z, q_b, k_b, v_b, o_b, gates = mm_multi(xs, w_lo, (w_gate,))
            mix, st_S, st_C, st_n, st_m = ab_sample(
                qkv, z, q_b, k_b, v_b, o_b, gates, state_delta_conv[j], conv_w_a[j], alog_row, prow,
                norm_g_a[j], norm_g_b[j], state_delta_S[j].astype(F32), state_mlstm_C[j].astype(F32),
                state_mlstm_n[j].astype(F32), state_mlstm_m[j].astype(F32))
            xs = proj_ln([mix], [w_out], xs, g0, b0)
            s_S.append(st_S)
            s_conv.append(jnp.concatenate([state_delta_conv[j][:, 1:].astype(F32), qkv[:, None, :]], axis=1))
            s_C.append(st_C)
            s_n.append(st_n)
            s_m.append(st_m)
        else:
            lam_init = 0.8 - 0.6 * math.exp(-0.3 * layer)
            lam = (jnp.exp(jnp.sum(lam_q1[j].astype(F32) * lam_k1[j].astype(F32)))
                   - jnp.exp(jnp.sum(lam_q2[j].astype(F32) * lam_k2[j].astype(F32))) + lam_init)
            w_qkv = tuple(w_qkv_c[j][:, i * C_W:(i + 1) * C_W].astype(BF16) for i in range(3))
            w_o = w_o_c[j].astype(BF16)
            q, k, v = mm_multi(xp, w_qkv, tm=512)
            o = diff_attn_prompt_core(q, k, v, bp, rel_bias, lam, lam_init, subln_g_c[j])
            xp = proj_ln([o], [w_o], xp, g0, b0)
            p_k.append(k.reshape(bp, t, H_C, 2 * DH_C))
            p_v.append(v.reshape(bp, t, H_C, VD_C))
            q, k, v = mm_multi(xs, w_qkv)
            o = diff_attn_sample_core(q, k, v, cache_diff_k, cache_diff_v, page_table, j, rel_bias, lam, lam_init,
                                      subln_g_c[j])
            xs = proj_ln([o], [w_o], xs, g0, b0)
            s_k.append(k.reshape(bs, 1, H_C, 2 * DH_C))
            s_v.append(v.reshape(bs, 1, H_C, VD_C))
        w_q, w_o = w_xq[layer].astype(BF16), w_xo[layer].astype(BF16)
        w_r = jnp.pad(w_router[layer].astype(F32), ((0, 0), (0, LANES - N_EXPERTS)))
        b_r = jnp.pad(b_router[layer].astype(F32), (0, LANES - N_EXPERTS)).reshape(1, LANES)
        mk, mv = mm_multi(mem2d, (w_xkv[layer][:, :X_W].astype(BF16), w_xkv[layer][:, X_W:].astype(BF16)))
        p_mk.append(mk.reshape(bp, N_MEM, H_X, DH_X))
        p_mv.append(mv.reshape(bp, N_MEM, H_X, DH_X))
        xp, idx_p, gate_p = xattn_prompt(xp, mk.reshape(bp, N_MEM, X_W), mv.reshape(bp, N_MEM, X_W),
                                         w_q, w_o, g1, b1, w_r, b_r)
        (q,) = mm_multi(xs, (w_q,))
        o = xattn_sample_core(q, cmk, cmv, off=layer * bs)
        xs, idx_s, gate_s = proj_ln_route(o, w_o, xs, g1, b1, w_r, b_r)
        xp = moe_ln(xp, idx_p, gate_p, wm_in, bm_in, wm_out, bm_out, g2, b2, MOE_BLOCK_PROMPT,
                    e_off=layer * N_EXPERTS)
        xs = moe_ln(xs, idx_s, gate_s, wm_in, bm_in, wm_out, bm_out, g2, b2, MOE_BLOCK_SAMPLE,
                    e_off=layer * N_EXPERTS)

    return (xp.reshape(bp, t, d), xs.reshape(bs, 1, d),
            jnp.stack(p_S), jnp.stack(p_conv), jnp.stack(p_C), jnp.stack(p_n), jnp.stack(p_m),
            jnp.stack(p_k, axis=1), jnp.stack(p_v, axis=1), jnp.stack(p_mk), jnp.stack(p_mv),
            jnp.stack(s_S), jnp.stack(s_conv), jnp.stack(s_C), jnp.stack(s_n), jnp.stack(s_m),
            jnp.stack(s_k, axis=1), jnp.stack(s_v, axis=1))
```

```python
import functools
import math

import jax
import jax.numpy as jnp
from jax import lax
from jax.experimental import pallas as pl
from jax.experimental.pallas import tpu as pltpu

F32 = jnp.float32
BF16 = jnp.bfloat16
HI = lax.Precision.HIGHEST

D_MODEL = 1024
DEPTH = 2
H_A, DK_A, DV_A, CONV_W, CHUNK = 8, 64, 64, 4, 64
H_B, DQK_B, DV_B = 4, 64, 128
QKV_A = H_A * (2 * DK_A + DV_A)
H_C, DH_C = 8, 64
VD_C = 2 * DH_C
C_W = H_C * 2 * DH_C
N_BUCKETS, MAX_DISTANCE = 32, 128
N_MEM, H_X, DH_X = 256, 4, 128
X_W = H_X * DH_X
N_EXPERTS, TOP_K = 32, 4
D_FF = D_MODEL
SWIGLU_ALPHA, SWIGLU_LIMIT = 1.702, 7.0
DN_ALPHA = (2 * DEPTH) ** 0.25
LN_EPS = 1e-5
NORM_EPS = 1e-6

LANES = 128
SUBLANES = 8
VMEM_LIMIT = 48 * 1024 * 1024


def _cparams(*sem):
    return pltpu.CompilerParams(dimension_semantics=tuple(sem), vmem_limit_bytes=VMEM_LIMIT)


def _layer_norm(y, g, b):
    mu = jnp.mean(y, axis=-1, keepdims=True)
    d = y - mu
    var = jnp.mean(d * d, axis=-1, keepdims=True)
    return d * lax.rsqrt(var + LN_EPS) * g + b


def _mm_multi_kernel(x_ref, *refs, n_lo, n_hi):
    n = n_lo + n_hi
    ws, outs = refs[:n], refs[n:]
    x = x_ref[...]
    xb = x.astype(BF16)
    for i in range(n_lo):
        outs[i][...] = jnp.dot(xb, ws[i][...], preferred_element_type=F32)
    for i in range(n_lo, n):
        outs[i][...] = jnp.dot(x, ws[i][...], preferred_element_type=F32, precision=HI)


def mm_multi(x, w_lo, w_hi=(), tm=256):
    m, k = x.shape
    tm = min(tm, m)
    assert m % tm == 0
    ws = tuple(w_lo) + tuple(w_hi)
    in_specs = [pl.BlockSpec((tm, k), lambda i: (i, 0))]
    in_specs += [pl.BlockSpec(w.shape, lambda i: (0, 0)) for w in ws]
    out_specs = [pl.BlockSpec((tm, w.shape[1]), lambda i: (i, 0)) for w in ws]
    out_shape = [jax.ShapeDtypeStruct((m, w.shape[1]), F32) for w in ws]
    return pl.pallas_call(
        functools.partial(_mm_multi_kernel, n_lo=len(w_lo), n_hi=len(w_hi)),
        grid=(m // tm,), in_specs=in_specs, out_specs=out_specs, out_shape=out_shape,
        compiler_params=_cparams("parallel"), name="mm_multi")(x, *ws)


def _proj_ln_kernel(*refs, n):
    a_refs, w_refs = refs[:n], refs[n:2 * n]
    x_ref, g_ref, b_ref, o_ref = refs[2 * n:]
    h = jnp.dot(a_refs[0][...].astype(BF16), w_refs[0][...], preferred_element_type=F32)
    for a_ref, w_ref in zip(a_refs[1:], w_refs[1:]):
        h = h + jnp.dot(a_ref[...].astype(BF16), w_ref[...], preferred_element_type=F32)
    o_ref[...] = _layer_norm(DN_ALPHA * x_ref[...] + h, g_ref[...], b_ref[...])


def proj_ln(a_list, w_list, x_res, g, b, tm=512):
    m, d = x_res.shape
    tm = min(tm, m)
    assert m % tm == 0
    n = len(a_list)
    return pl.pallas_call(
        functools.partial(_proj_ln_kernel, n=n), grid=(m // tm,),
        in_specs=[pl.BlockSpec((tm, a.shape[1]), lambda i: (i, 0)) for a in a_list]
        + [pl.BlockSpec(w.shape, lambda i: (0, 0)) for w in w_list]
        + [pl.BlockSpec((tm, d), lambda i: (i, 0)), pl.BlockSpec((1, d), lambda i: (0, 0)),
           pl.BlockSpec((1, d), lambda i: (0, 0))],
        out_specs=pl.BlockSpec((tm, d), lambda i: (i, 0)),
        out_shape=jax.ShapeDtypeStruct((m, d), F32),
        compiler_params=_cparams("parallel"), name="proj_ln")(*a_list, *w_list, x_res, g, b)


def _mem_attention(q, mk, mv):
    outs = []
    for h in range(H_X):
        sl = slice(h * DH_X, (h + 1) * DH_X)
        s = lax.dot_general(q[:, sl].astype(BF16), mk[:, sl].astype(BF16), (((1,), (1,)), ((), ())),
                            preferred_element_type=F32) * (DH_X ** -0.5)
        s = s - jnp.max(s, axis=-1, keepdims=True)
        p = jnp.exp(s)
        p = p / jnp.sum(p, axis=-1, keepdims=True)
        outs.append(jnp.dot(p.astype(BF16), mv[:, sl].astype(BF16), preferred_element_type=F32))
    return jnp.concatenate(outs, axis=-1)


def _route(x, wr, br):
    logits = jnp.dot(x, wr, preferred_element_type=F32, precision=HI) + br
    lane = lax.broadcasted_iota(jnp.int32, logits.shape, 1)
    work = jnp.where(lane < N_EXPERTS, logits, -jnp.inf)
    idx_out = jnp.zeros(logits.shape, jnp.int32)
    val_out = jnp.full(logits.shape, -jnp.inf, F32)
    for k in range(TOP_K):
        m = jnp.max(work, axis=-1, keepdims=True)
        sel = jnp.min(jnp.where(work == m, lane, LANES), axis=-1, keepdims=True)
        idx_out = jnp.where(lane == k, sel, idx_out)
        val_out = jnp.where(lane == k, m, val_out)
        work = jnp.where(lane == sel, -jnp.inf, work)
    e = jnp.exp(val_out - jnp.max(val_out, axis=-1, keepdims=True))
    gates = e / jnp.sum(e, axis=-1, keepdims=True)
    return idx_out, gates


def _store_token_tiles(o3_ref, y):
    for s in range(y.shape[1] // LANES):
        o3_ref[:, s, :] = y[:, s * LANES:(s + 1) * LANES]


def _xattn_prompt_kernel(x_ref, mk_ref, mv_ref, wq_ref, wo_ref, g_ref, b_ref, wr_ref, br_ref,
                         o_ref, o3_ref, idx_ref, gate_ref):
    x = x_ref[...]
    q = jnp.dot(x.astype(BF16), wq_ref[...], preferred_element_type=F32)
    o = _mem_attention(q, mk_ref[0], mv_ref[0])
    h = jnp.dot(o.astype(BF16), wo_ref[...], preferred_element_type=F32)
    y = _layer_norm(DN_ALPHA * x + h, g_ref[...], b_ref[...])
    o_ref[...] = y
    _store_token_tiles(o3_ref, y)
    idx, gates = _route(y, wr_ref[...], br_ref[...])
    idx_ref[...] = idx
    gate_ref[...] = gates


def xattn_prompt(x, mk, mv, wq, wo, g, b, wr, br, tq=512):
    m, d = x.shape
    bsz = mk.shape[0]
    t = m // bsz
    nq = t // tq
    full = lambda shape: pl.BlockSpec(shape, lambda bi, qi: (0,) * len(shape))
    row = lambda w: pl.BlockSpec((tq, w), lambda bi, qi: (bi * nq + qi, 0))
    return pl.pallas_call(
        _xattn_prompt_kernel, grid=(bsz, nq),
        in_specs=[row(d), pl.BlockSpec((1, N_MEM, X_W), lambda bi, qi: (bi, 0, 0)),
                  pl.BlockSpec((1, N_MEM, X_W), lambda bi, qi: (bi, 0, 0)),
                  full(wq.shape), full(wo.shape), full(g.shape), full(b.shape), full(wr.shape), full(br.shape)],
        out_specs=[row(d), pl.BlockSpec((tq, d // LANES, LANES), lambda bi, qi: (bi * nq + qi, 0, 0)),
                   row(LANES), row(LANES)],
        out_shape=[jax.ShapeDtypeStruct((m, d), F32), jax.ShapeDtypeStruct((m, d // LANES, LANES), F32),
                   jax.ShapeDtypeStruct((m, LANES), jnp.int32), jax.ShapeDtypeStruct((m, LANES), F32)],
        compiler_params=_cparams("parallel", "parallel"), name="xattn_prompt")(x, mk, mv, wq, wo, g, b, wr, br)


def _xattn_sample_kernel(q_ref, mk_ref, mv_ref, o_ref):
    q = jnp.broadcast_to(q_ref[0], (8, X_W))
    o = _mem_attention(q, mk_ref[0], mv_ref[0])
    o_ref[0] = o[0:1]


def xattn_sample_core(q, mk, mv, off=0):
    bsz = q.shape[0]
    spec3 = pl.BlockSpec((1, N_MEM, X_W), lambda bi: (off + bi, 0, 0))
    out = pl.pallas_call(
        _xattn_sample_kernel, grid=(bsz,),
        in_specs=[pl.BlockSpec((1, 1, X_W), lambda bi: (bi, 0, 0)), spec3, spec3],
        out_specs=pl.BlockSpec((1, 1, X_W), lambda bi: (bi, 0, 0)),
        out_shape=jax.ShapeDtypeStruct((bsz, 1, X_W), F32),
        compiler_params=_cparams("parallel"), name="xattn_sample")(q.reshape(bsz, 1, X_W), mk, mv)
    return out.reshape(bsz, X_W)


def _ln_route_kernel(a_ref, w_ref, x_ref, g_ref, b_ref, wr_ref, br_ref, o_ref, o3_ref, idx_ref, gate_ref):
    h = jnp.dot(a_ref[...].astype(BF16), w_ref[...], preferred_element_type=F32)
    y = _layer_norm(DN_ALPHA * x_ref[...] + h, g_ref[...], b_ref[...])
    o_ref[...] = y
    _store_token_tiles(o3_ref, y)
    idx, gates = _route(y, wr_ref[...], br_ref[...])
    idx_ref[...] = idx
    gate_ref[...] = gates


def proj_ln_route(a, w, x_res, g, b, wr, br):
    m, d = x_res.shape
    return pl.pallas_call(
        _ln_route_kernel,
        out_shape=[jax.ShapeDtypeStruct((m, d), F32), jax.ShapeDtypeStruct((m, d // LANES, LANES), F32),
                   jax.ShapeDtypeStruct((m, LANES), jnp.int32), jax.ShapeDtypeStruct((m, LANES), F32)],
        compiler_params=pltpu.CompilerParams(vmem_limit_bytes=VMEM_LIMIT),
        name="proj_ln_route")(a, w, x_res, g, b, wr, br)


def _expert_onehots(idx):
    lane = lax.broadcasted_iota(jnp.int32, idx.shape, 1)
    onehots = [(idx[:, k:k + 1] == lane).astype(F32) for k in range(TOP_K)]
    tot = onehots[0]
    for k in range(1, TOP_K):
        tot = tot + onehots[k]
    return lane, onehots, tot


def _count_kernel(idx_ref, cnt_ref):
    @pl.when(pl.program_id(0) == 0)
    def _():
        cnt_ref[...] = jnp.zeros_like(cnt_ref)

    _, _, tot = _expert_onehots(idx_ref[...])
    cnt_ref[...] = cnt_ref[...] + jnp.sum(tot, axis=0, keepdims=True)


def _dest_kernel(idx_ref, base0_ref, dest_ref, base_ref):
    @pl.when(pl.program_id(0) == 0)
    def _():
        base_ref[...] = base0_ref[...]

    idx = idx_ref[...]
    tr = idx.shape[0]
    lane, onehots, tot = _expert_onehots(idx)
    r = lax.broadcasted_iota(jnp.int32, (tr, tr), 0)
    c = lax.broadcasted_iota(jnp.int32, (tr, tr), 1)
    strict = (c < r).astype(BF16)
    before = jnp.dot(strict, tot.astype(BF16), preferred_element_type=F32) + base_ref[...]
    out = jnp.zeros(idx.shape, F32)
    for k in range(TOP_K):
        out = jnp.where(lane == k, jnp.sum(onehots[k] * before, axis=-1, keepdims=True), out)
    dest_ref[...] = out.astype(jnp.int32)
    base_ref[...] = base_ref[...] + jnp.sum(tot, axis=0, keepdims=True)


def route_slots(idx, bm, tr=256):
    t = idx.shape[0]
    tr = min(tr, t)
    assert t % tr == 0
    rows = pl.BlockSpec((tr, LANES), lambda i: (i, 0))
    one = pl.BlockSpec((1, LANES), lambda i: (0, 0))
    counts = pl.pallas_call(
        _count_kernel, grid=(t // tr,), in_specs=[rows], out_specs=one,
        out_shape=jax.ShapeDtypeStruct((1, LANES), F32),
        compiler_params=_cparams("arbitrary"), name="route_count")(idx)
    counts = counts[0].astype(jnp.int32)
    padded = (counts + bm - 1) // bm * bm
    pad_end = jnp.cumsum(padded)
    base0 = (pad_end - padded).astype(F32).reshape(1, LANES)
    dest = pl.pallas_call(
        _dest_kernel, grid=(t // tr,), in_specs=[rows, one], out_specs=rows,
        out_shape=jax.ShapeDtypeStruct((t, LANES), jnp.int32),
        scratch_shapes=[pltpu.VMEM((1, LANES), F32)],
        compiler_params=_cparams("arbitrary"), name="route_dest")(idx, base0)
    return dest, pad_end[:N_EXPERTS]


def _clamped_swiglu(h):
    glu = jnp.minimum(h[:, :D_FF], SWIGLU_LIMIT)
    lin = jnp.clip(h[:, D_FF:], -SWIGLU_LIMIT, SWIGLU_LIMIT)
    return glu * jax.nn.sigmoid(SWIGLU_ALPHA * glu) * (lin + 1.0)


def _moe_gmm_kernel(bexp_ref, nused_ref, tok_ref, tok_next_ref, dst_ref, x_hbm, win_ref, bin_ref, wout_ref,
                    bout_ref, y_hbm, xbuf, ybuf, zbuf, winb, woutb, gsem, ssem, *, bm):
    i = pl.program_id(0)
    n = pl.num_programs(0)
    nused = nused_ref[0]
    slot = lax.rem(i, 2)
    n_sub = x_hbm.shape[1]
    rows_per_group = xbuf.shape[3]

    def gather_copy(tok, g, u, s):
        return pltpu.make_async_copy(x_hbm.at[tok], xbuf.at[s, g, :, u], gsem.at[s])

    def scatter_copy(dst, g, u, s):
        return pltpu.make_async_copy(ybuf.at[s, g, :, u], y_hbm.at[dst], ssem.at[s])

    def start_rows(copy_fn, idx_ref, s):
        def body(g, carry):
            for u in range(rows_per_group):
                copy_fn(idx_ref[0, 0, g * rows_per_group + u], g, u, s).start(priority=u % 2)
            return carry
        lax.fori_loop(0, bm // rows_per_group, body, 0)

    def wait_gather(s):
        pltpu.make_async_copy(xbuf.at[s], xbuf.at[s], gsem.at[s]).wait()

    def wait_scatter(s):
        pltpu.make_async_copy(ybuf.at[s], ybuf.at[s], ssem.at[s]).wait()

    @pl.when(i == 0)
    def _():
        start_rows(gather_copy, tok_ref, 0)
        zbuf[...] = jnp.zeros_like(zbuf)
        nz = zbuf.shape[0]
        n_real = y_hbm.shape[0] - 2 * bm
        cps = [pltpu.make_async_copy(zbuf, y_hbm.at[pl.ds(n_real + c * nz, nz)], ssem.at[0])
               for c in range(2 * bm // nz)]
        for cp in cps:
            cp.start()
        for cp in cps:
            cp.wait()

    @pl.when(jnp.logical_and(i + 1 < n, i + 1 < nused))
    def _():
        start_rows(gather_copy, tok_next_ref, 1 - slot)

    active = i < nused

    @pl.when(jnp.logical_and(active, jnp.logical_or(i == 0, bexp_ref[i] != bexp_ref[jnp.maximum(i - 1, 0)])))
    def _():
        winb[...] = win_ref[0].astype(BF16)
        woutb[...] = wout_ref[0].astype(BF16)

    @pl.when(active)
    def _():
        wait_gather(slot)

        @pl.when(i >= 2)
        def _():
            wait_scatter(slot)

        xb = jnp.concatenate([xbuf[slot, :, j].reshape(bm, LANES) for j in range(n_sub)], axis=-1).astype(BF16)
        h = jnp.dot(xb, winb[...], preferred_element_type=F32) + bin_ref[0]
        act = _clamped_swiglu(h)
        y = jnp.dot(act.astype(BF16), woutb[...], preferred_element_type=F32) + bout_ref[0]
        for j in range(n_sub):
            ybuf[slot, :, j] = y[:, j * LANES:(j + 1) * LANES].reshape(bm // rows_per_group, rows_per_group, LANES)
        start_rows(scatter_copy, dst_ref, slot)

    @pl.when(i == n - 1)
    def _():
        last = nused - 1

        @pl.when(nused >= 2)
        def _():
            wait_scatter(lax.rem(last + 1, 2))

        @pl.when(nused >= 1)
        def _():
            wait_scatter(lax.rem(last, 2))


def moe_gmm(x, row_tok, row_dst, block_expert, n_used, w_in, b_in, w_out, b_out, n_out_rows, bm):
    n_blocks = row_tok.shape[0]
    n_sub = x.shape[1]
    d = n_sub * LANES
    idx_spec = lambda off: pl.BlockSpec(
        (1, 1, bm), lambda i, be, nu: (jnp.minimum(i + off, n_blocks - 1), 0, 0), memory_space=pltpu.SMEM)
    ex = lambda i, be, nu: (be[i], 0, 0)
    grid_spec = pltpu.PrefetchScalarGridSpec(
        num_scalar_prefetch=2, grid=(n_blocks,),
        in_specs=[idx_spec(0), idx_spec(1), idx_spec(0),
                  pl.BlockSpec(memory_space=pl.ANY),
                  pl.BlockSpec((1, d, 2 * D_FF), ex), pl.BlockSpec((1, 1, 2 * D_FF), ex),
                  pl.BlockSpec((1, D_FF, d), ex), pl.BlockSpec((1, 1, d), ex)],
        out_specs=pl.BlockSpec(memory_space=pl.ANY),
        scratch_shapes=[pltpu.VMEM((2, bm // SUBLANES, n_sub, SUBLANES, LANES), F32),
                        pltpu.VMEM((2, bm // SUBLANES, n_sub, SUBLANES, LANES), F32),
                        pltpu.VMEM((min(bm, 64), n_sub, LANES), F32),
                        pltpu.VMEM((d, 2 * D_FF), BF16), pltpu.VMEM((D_FF, d), BF16),
                        pltpu.SemaphoreType.DMA((2,)), pltpu.SemaphoreType.DMA((2,))])
    return pl.pallas_call(
        functools.partial(_moe_gmm_kernel, bm=bm), grid_spec=grid_spec,
        out_shape=jax.ShapeDtypeStruct((n_out_rows, n_sub, LANES), F32),
        compiler_params=_cparams("arbitrary"), name="moe_gmm")(
            block_expert, n_used, row_tok, row_tok, row_dst, x, w_in, b_in.reshape(b_in.shape[0], 1, -1),
            w_out, b_out.reshape(b_out.shape[0], 1, -1))


def _combine_ln_kernel(y_ref, gate_ref, x_ref, g_ref, b_ref, o_ref):
    gates = gate_ref[...]
    cols = []
    for s in range(y_ref.shape[2]):
        acc = gates[:, 0:1] * y_ref[:, 0, s, :]
        for k in range(1, TOP_K):
            acc = acc + gates[:, k:k + 1] * y_ref[:, k, s, :]
        cols.append(acc)
    o_ref[...] = _layer_norm(DN_ALPHA * x_ref[...] + jnp.concatenate(cols, axis=-1), g_ref[...], b_ref[...])


def combine_ln(y4, gates, x_res, g, b, tc=256):
    t, d = x_res.shape
    tc = min(tc, t)
    assert t % tc == 0
    return pl.pallas_call(
        _combine_ln_kernel, grid=(t // tc,),
        in_specs=[pl.BlockSpec((tc,) + y4.shape[1:], lambda i: (i, 0, 0, 0)),
                  pl.BlockSpec((tc, LANES), lambda i: (i, 0)),
                  pl.BlockSpec((tc, d), lambda i: (i, 0)), pl.BlockSpec((1, d), lambda i: (0, 0)),
                  pl.BlockSpec((1, d), lambda i: (0, 0))],
        out_specs=pl.BlockSpec((tc, d), lambda i: (i, 0)),
        out_shape=jax.ShapeDtypeStruct((t, d), F32),
        compiler_params=_cparams("parallel"), name="combine_ln")(y4, gates, x_res, g, b)


def moe_ln(x, x_tiles, idx, gates, w_in, b_in, w_out, b_out, g, b, bm, e_off=0):
    t, d = x.shape
    tk = t * TOP_K
    dest, pad_end = route_slots(idx, bm)
    n_blocks = -(-tk // bm) + N_EXPERTS
    n_rows = n_blocks * bm
    src = jnp.arange(tk, dtype=jnp.int32)
    row_src = jnp.full((n_rows,), -1, jnp.int32).at[dest[:, :TOP_K].reshape(tk)].set(
        src, unique_indices=True, mode='promise_in_bounds')
    slot = jnp.arange(n_rows, dtype=jnp.int32)
    slot_in_pair = (slot // bm % 2) * bm + slot % bm
    row_tok = jnp.where(row_src >= 0, row_src // TOP_K, 0).reshape(n_blocks, 1, bm)
    row_dst = jnp.where(row_src >= 0, row_src, tk + slot_in_pair).reshape(n_blocks, 1, bm)
    block_start = jnp.arange(n_blocks, dtype=jnp.int32) * bm
    block_expert = jnp.sum((pad_end[None, :] <= block_start[:, None]).astype(jnp.int32), axis=1)
    block_expert = jnp.minimum(block_expert, N_EXPERTS - 1) + e_off
    n_used = (pad_end[-1] // bm).astype(jnp.int32).reshape(1)
    y = moe_gmm(x_tiles, row_tok, row_dst, block_expert, n_used, w_in, b_in, w_out, b_out, tk + 2 * bm, bm)
    y4 = y.reshape((tk + 2 * bm) // TOP_K, TOP_K, d // LANES, LANES)
    return combine_ln(y4, gates, x, g, b)


def _t5_causal_bucket(dist):
    n = jnp.maximum(dist, 0)
    max_exact = N_BUCKETS // 2
    nf = jnp.maximum(n, max_exact).astype(F32)
    large = max_exact + (jnp.log(nf / max_exact) / math.log(MAX_DISTANCE / max_exact)
                         * (N_BUCKETS - max_exact)).astype(jnp.int32)
    large = jnp.minimum(large, N_BUCKETS - 1)
    return jnp.where(n < max_exact, n, large)


def _split_maps(q):
    lane = lax.broadcasted_iota(jnp.int32, q.shape, 1)
    qs = q * (DH_C ** -0.5)
    return (jnp.where(lane < DH_C, qs, 0.0).astype(BF16), jnp.where(lane >= DH_C, qs, 0.0).astype(BF16))


def _diff_finish(acc1, l1, acc2, l2, lam, out_scale, g):
    o = acc1 / l1 - lam * (acc2 / l2)
    o = o * lax.rsqrt(jnp.mean(o * o, axis=-1, keepdims=True) + NORM_EPS) * g
    return o * out_scale


def _diff_prompt_kernel(sc_ref, far_ref, q_ref, k_ref, v_ref, bias_ref, g_ref, o_ref, m1, a1, m2, a2, *, tq):
    h = pl.program_id(1)
    qi = pl.program_id(2)
    qm = _split_maps(q_ref[...])
    stats = ((m1, a1), (m2, a2))
    for m_ref, a_ref in stats:
        m_ref[...] = jnp.full_like(m_ref, -jnp.inf)
        a_ref[...] = jnp.zeros_like(a_ref)
    ones_col = (lax.broadcasted_iota(jnp.int32, (tq, LANES), 1) == 0).astype(BF16)
    reps = tq // LANES

    def tile(kidx, bias, shift, causal):
        start = pl.multiple_of(kidx * tq, tq)
        kt = k_ref[0, pl.ds(start, tq), :].astype(BF16)
        vt = jnp.concatenate([v_ref[0, pl.ds(start, tq), :].astype(BF16), ones_col], axis=-1)
        for qmap, (m_ref, a_ref) in zip(qm, stats):
            s = lax.dot_general(qmap, kt, (((1,), (1,)), ((), ())), preferred_element_type=F32)
            if bias is not None:
                s = s + bias
            if causal:
                r = lax.broadcasted_iota(jnp.int32, s.shape, 0)
                c = lax.broadcasted_iota(jnp.int32, s.shape, 1)
                s = jnp.where(c <= r, s, -jnp.inf)
            m_old = m_ref[...]
            m_new = jnp.maximum(m_old, jnp.max(s, axis=-1, keepdims=True) + shift)
            alpha = jnp.exp(m_old - m_new)
            off = m_new - shift
            p = jnp.exp(s - jnp.concatenate([off] * reps, axis=-1))
            a_ref[...] = (jnp.concatenate([alpha, alpha], axis=-1) * a_ref[...]
                          + jnp.dot(p.astype(BF16), vt, preferred_element_type=F32))
            m_ref[...] = m_new

    far_bias = far_ref[h]

    def far_body(kidx, carry):
        tile(kidx, None, far_bias, False)
        return carry

    lax.fori_loop(0, jnp.maximum(qi - 1, 0), far_body, 0)

    @pl.when(qi >= 1)
    def _():
        tile(qi - 1, bias_ref[0, 1], 0.0, False)

    tile(qi, bias_ref[0, 0], 0.0, True)
    acc1, acc2 = a1[...], a2[...]
    o_ref[...] = _diff_finish(acc1[:, :VD_C], acc1[:, VD_C:VD_C + 1], acc2[:, :VD_C], acc2[:, VD_C:VD_C + 1],
                              sc_ref[0], sc_ref[1], g_ref[...])


def _bias_tiles_kernel(rb_ref, bucket_ref, o_ref):
    h = pl.program_id(0)
    b = bucket_ref[...]
    out = jnp.zeros(b.shape, F32)
    for k in range(N_BUCKETS):
        out = jnp.where(b == k, rb_ref[k * H_C + h], out)
    o_ref[0] = out


def diff_attn_prompt_core(q, k, v, bsz, rel_bias, lam, lam_init, subln_g, tq=256):
    m = q.shape[0]
    t = m // bsz
    nq = t // tq
    ii = jnp.arange(tq)[:, None]
    jj = jnp.arange(tq)[None, :]
    buckets = _t5_causal_bucket(jnp.stack([jnp.maximum(ii - jj, 0), tq + ii - jj])).astype(jnp.int32)
    bias = pl.pallas_call(
        _bias_tiles_kernel, grid=(H_C,),
        in_specs=[pl.BlockSpec(memory_space=pltpu.SMEM), pl.BlockSpec((2, tq, tq), lambda h: (0, 0, 0))],
        out_specs=pl.BlockSpec((1, 2, tq, tq), lambda h: (h, 0, 0, 0)),
        out_shape=jax.ShapeDtypeStruct((H_C, 2, tq, tq), F32),
        compiler_params=_cparams("parallel"), name="t5_bias_tiles")(rel_bias.astype(F32).reshape(-1), buckets)
    far = rel_bias[_t5_causal_bucket(jnp.array(2 * tq))].astype(F32)
    scal = jnp.stack([lam, 1.0 - lam_init]).astype(F32)
    k3 = k.reshape(bsz, t, C_W)
    v3 = v.reshape(bsz, t, C_W)
    smem = pl.BlockSpec(memory_space=pltpu.SMEM)
    kv_spec = pl.BlockSpec((1, t, VD_C), lambda b, h, i: (b, 0, h))
    row = pl.BlockSpec((tq, VD_C), lambda b, h, i: (b * nq + i, h))
    stat = pltpu.VMEM((tq, LANES), F32)
    acc = pltpu.VMEM((tq, VD_C + LANES), F32)
    return pl.pallas_call(
        functools.partial(_diff_prompt_kernel, tq=tq), grid=(bsz, H_C, nq),
        in_specs=[smem, smem, row, kv_spec, kv_spec,
                  pl.BlockSpec((1, 2, tq, tq), lambda b, h, i: (h, 0, 0, 0)),
                  pl.BlockSpec((1, VD_C), lambda b, h, i: (0, 0))],
        out_specs=row, out_shape=jax.ShapeDtypeStruct((m, C_W), F32),
        scratch_shapes=[stat, acc, stat, acc],
        compiler_params=_cparams("parallel", "parallel", "parallel"), name="diff_attn_prompt")(
            scal, far, q, k3, v3, bias, subln_g.reshape(1, VD_C))


def _diff_sample_kernel(pt_ref, sc_ref, q_ref, kn_ref, vn_ref, bm_ref, bnew_ref, g_ref, *refs, n_pp):
    k_refs, v_refs = refs[:n_pp], refs[n_pp:2 * n_pp]
    o_ref, m_ref, l_ref, a_ref = refs[2 * n_pp:]
    p = pl.program_id(1)
    n_p = pl.num_programs(1)
    q8 = q_ref[0]
    qm = jnp.concatenate(_split_maps(q8), axis=0)

    @pl.when(p == 0)
    def _():
        m_ref[...] = jnp.full_like(m_ref, -jnp.inf)
        l_ref[...] = jnp.zeros_like(l_ref)
        a_ref[...] = jnp.zeros_like(a_ref)

    scores = []
    for j in range(n_pp):
        bias = bm_ref[jnp.where(p == n_p - 1, 1, 0)] if j == n_pp - 1 else bm_ref[0]
        k2 = k_refs[j][0, 0].reshape(-1, 2 * DH_C).astype(BF16)
        scores.append(lax.dot_general(qm, k2, (((1,), (1,)), ((), ())), preferred_element_type=F32) + bias)
    m_old = m_ref[...]
    m_new = m_old
    for s in scores:
        m_new = jnp.maximum(m_new, jnp.max(s, axis=-1, keepdims=True))
    alpha = jnp.exp(m_old - m_new)
    l_new = alpha * l_ref[...]
    a_new = alpha * a_ref[...]
    for j, s in enumerate(scores):
        pr = jnp.exp(s - m_new)
        l_new = l_new + jnp.sum(pr, axis=-1, keepdims=True)
        v2 = v_refs[j][0, 0].reshape(-1, VD_C).astype(BF16)
        a_new = a_new + jnp.dot(pr.astype(BF16), v2, preferred_element_type=F32)
    l_ref[...] = l_new
    a_ref[...] = a_new
    m_ref[...] = m_new

    @pl.when(p == n_p - 1)
    def _():
        kn = jnp.concatenate([kn_ref[0], kn_ref[0]], axis=0)
        vn = jnp.concatenate([vn_ref[0], vn_ref[0]], axis=0)
        s = jnp.sum(qm.astype(F32) * kn, axis=-1, keepdims=True) + bnew_ref[:, 0:1]
        m_old = m_ref[...]
        m_new = jnp.maximum(m_old, s)
        alpha = jnp.exp(m_old - m_new)
        pr = jnp.exp(s - m_new)
        l = alpha * l_ref[...] + pr
        a = alpha * a_ref[...] + pr * vn
        o_ref[0] = _diff_finish(a[:H_C], l[:H_C], a[H_C:], l[H_C:], sc_ref[0], sc_ref[1], g_ref[...])


def diff_attn_sample_core(q, k_new, v_new, cache_k, cache_v, page_table, layer_j, rel_bias, lam, lam_init, subln_g,
                          n_pp=8):
    bsz = q.shape[0]
    n_pages = page_table.shape[1]
    page = cache_k.shape[2]
    past = n_pages * page
    assert n_pages % n_pp == 0
    dist_last = past - ((n_pages - 1) * page + jnp.arange(page))
    b_last = rel_bias[_t5_causal_bucket(dist_last)].astype(F32)
    b_far = jnp.broadcast_to(rel_bias[_t5_causal_bucket(jnp.array(page + 1))].astype(F32), (page, H_C))
    eye = jnp.eye(H_C, dtype=bool)

    def expand(bt):
        full = jnp.where(eye[:, None, :], bt.T[:, :, None], -jnp.inf).reshape(H_C, page * H_C)
        return jnp.concatenate([full, full], axis=0)

    bm = jnp.stack([expand(b_far), expand(b_last)])
    b_new = rel_bias[_t5_causal_bucket(jnp.array(0))].astype(F32)
    b_new = jnp.broadcast_to(jnp.concatenate([b_new, b_new])[:, None], (2 * H_C, LANES))
    scal = jnp.stack([lam, 1.0 - lam_init]).astype(F32)
    smem = pl.BlockSpec(memory_space=pltpu.SMEM)
    head3 = pl.BlockSpec((1, H_C, VD_C), lambda b, p, pt: (b, 0, 0))
    full = lambda shape: pl.BlockSpec(shape, lambda b, p, pt: (0,) * len(shape))

    def page_spec(j):
        return pl.BlockSpec((1, 1, page, H_C, VD_C), lambda b, p, pt: (pt[b, p * n_pp + j], layer_j, 0, 0, 0))

    grid_spec = pltpu.PrefetchScalarGridSpec(
        num_scalar_prefetch=1, grid=(bsz, n_pages // n_pp),
        in_specs=[smem, head3, head3, head3, full(bm.shape), full(b_new.shape), full((1, VD_C))]
        + [page_spec(j) for j in range(n_pp)] * 2,
        out_specs=head3,
        scratch_shapes=[pltpu.VMEM((2 * H_C, 1), F32), pltpu.VMEM((2 * H_C, 1), F32),
                        pltpu.VMEM((2 * H_C, VD_C), F32)])
    out = pl.pallas_call(
        functools.partial(_diff_sample_kernel, n_pp=n_pp), grid_spec=grid_spec,
        out_shape=jax.ShapeDtypeStruct((bsz, H_C, VD_C), F32),
        compiler_params=_cparams("parallel", "arbitrary"), name="diff_attn_sample")(
            page_table, scal, q.reshape(bsz, H_C, VD_C), k_new.reshape(bsz, H_C, VD_C),
            v_new.reshape(bsz, H_C, VD_C), bm, b_new, subln_g.reshape(1, VD_C),
            *([cache_k] * n_pp), *([cache_v] * n_pp))
    return out.reshape(bsz, C_W)


LANE_BETA, LANE_A, LANE_I, LANE_F = 0, H_A, 2 * H_A, 2 * H_A + H_B


def _softplus(x):
    return jnp.maximum(x, 0.0) + jnp.log1p(jnp.exp(-jnp.abs(x)))


def _silu(x):
    return x * jax.nn.sigmoid(x)


def _lanes(shape, lo, n):
    lane = lax.broadcasted_iota(jnp.int32, shape, 1)
    return jnp.logical_and(lane >= lo, lane < lo + n)


def _gate_tile(gt, alog_row, prow):
    z = gt + prow
    return jax.nn.sigmoid(gt), -jnp.exp(alog_row) * _softplus(z), z, -_softplus(-z)


def _nt(a, b, precision=None):
    return lax.dot_general(a, b, (((1,), (1,)), ((), ())), preferred_element_type=F32, precision=precision)


def _tn(a, b):
    return lax.dot_general(a, b, (((0,), (0,)), ((), ())), preferred_element_type=F32)


def _row_selector(n_heads, length, lanes_of_head):
    r = lax.broadcasted_iota(jnp.int32, (n_heads * length, LANES), 0) // length
    lane = lax.broadcasted_iota(jnp.int32, (n_heads * length, LANES), 1)
    sel = jnp.zeros((n_heads * length, LANES), F32)
    for lo in lanes_of_head:
        sel = sel + (lane == r + lo).astype(F32)
    return sel


def _tri_masks(n):
    r = lax.broadcasted_iota(jnp.int32, (n, n), 0)
    c = lax.broadcasted_iota(jnp.int32, (n, n), 1)
    return r, c


def _bmm(a, b):
    return lax.dot_general(a.astype(BF16), b.astype(BF16), (((2,), (1,)), ((0,), (0,))),
                           preferred_element_type=F32)


def _bmm_nt(a, b):
    return lax.dot_general(a.astype(BF16), b.astype(BF16), (((2,), (2,)), ((0,), (0,))),
                           preferred_element_type=F32)


def _heads(x, n_heads, width, offset=0):
    return jnp.stack([x[:, offset + h * width:offset + (h + 1) * width] for h in range(n_heads)], axis=0)


def _head_cols(x, n_heads, lane0):
    return jnp.stack([x[:, lane0 + h:lane0 + h + 1] for h in range(n_heads)], axis=0)


def _unit_lower_inverse(nmat, r, c):
    mm = _bmm
    eye = (r == c).astype(F32)
    same = (r // 16) == (c // 16)
    nd = jnp.where(same, nmat, 0.0)
    off = nmat - nd
    dinv = eye - nd
    p = nd
    for _ in range(3):
        p = mm(p, p)
        dinv = dinv + mm(dinv, p)
    m = mm(dinv, off)
    m2 = mm(m, m)
    left = eye - m
    left = left + mm(left, m2)
    return mm(left, dinv)


def _gdn_prompt_kernel(qkv_ref, z_ref, gt_ref, convw_ref, alog_ref, prow_ref, ng_ref, mix_ref, s_ref, ext_ref):
    cidx = pl.program_id(1)
    L = CHUNK

    @pl.when(cidx == 0)
    def _():
        ext_ref[0:8, :] = jnp.zeros((8, QKV_A), F32)
        s_ref[...] = jnp.zeros_like(s_ref)

    ext_ref[8:8 + L, :] = qkv_ref[...]
    acc = convw_ref[0:1, :] * ext_ref[pl.ds(8 - (CONV_W - 1), L), :]
    for i in range(1, CONV_W):
        acc = acc + convw_ref[i:i + 1, :] * ext_ref[pl.ds(8 - (CONV_W - 1) + i, L), :]
    ext_ref[0:8, :] = ext_ref[L:L + 8, :]
    cs = _silu(acc)

    beta, g, _, _ = _gate_tile(gt_ref[...], alog_ref[...], prow_ref[...])
    r, c = _tri_masks(L)
    incl = c <= r
    strict = c < r
    gsel = jnp.where(_lanes(g.shape, LANE_A, H_A), g, 0.0)
    gcum = jnp.dot(incl.astype(F32), gsel, preferred_element_type=F32, precision=HI)
    gc_rows = _nt(_row_selector(H_A, L, (LANE_A,)), gcum, HI)
    q = _heads(cs, H_A, DK_A)
    k = _heads(cs, H_A, DK_A, H_A * DK_A)
    v = _heads(cs, H_A, DV_A, 2 * H_A * DK_A)
    q = q * lax.rsqrt(jnp.sum(q * q, axis=-1, keepdims=True) + NORM_EPS) * (DK_A ** -0.5)
    k = k * lax.rsqrt(jnp.sum(k * k, axis=-1, keepdims=True) + NORM_EPS)
    beta_c = _head_cols(beta, H_A, LANE_BETA)
    gc_c = _head_cols(gcum, H_A, LANE_A)
    diff = gc_c - gc_rows.reshape(H_A, L, L)
    decay = jnp.where(incl, jnp.exp(jnp.where(incl, diff, 0.0)), 0.0)
    kb = k.astype(BF16)
    nmat = jnp.where(strict, beta_c * _bmm_nt(kb, kb) * decay, 0.0)
    egc = jnp.exp(gc_c)
    rhs = jnp.concatenate([v * beta_c, k * (beta_c * egc)], axis=-1)
    sol = _bmm(_unit_lower_inverse(nmat, r, c), rhs)
    u, w = sol[:, :, :DV_A], sol[:, :, DV_A:]
    qk = _bmm_nt(q, kb) * decay
    gc_last = gc_c[:, L - 1:L]
    k_tail = (k * jnp.exp(gc_last - gc_c)).astype(BF16)
    s_old = s_ref[0]
    ws = _bmm(jnp.concatenate([w, q * egc], axis=1), s_old)
    delta = u - ws[:, :L]
    o = ws[:, L:] + _bmm(qk, delta)
    delta_b = delta.astype(BF16)
    s_ref[0] = s_old * jnp.exp(gc_last) + jnp.stack([_tn(k_tail[h], delta_b[h]) for h in range(H_A)], axis=0)
    o = o * lax.rsqrt(jnp.mean(o * o, axis=-1, keepdims=True) + NORM_EPS) * ng_ref[...]
    o = o * _silu(_heads(z_ref[...], H_A, DV_A))
    mix_ref[...] = jnp.concatenate([o[h] for h in range(H_A)], axis=-1)


def gdn_prompt(qkv, z, gates, conv_w, alog_row, prow, norm_g, bsz):
    m = qkv.shape[0]
    nc = m // bsz // CHUNK
    row = lambda w: pl.BlockSpec((CHUNK, w), lambda b, c: (b * nc + c, 0))
    full = lambda shape: pl.BlockSpec(shape, lambda b, c: (0,) * len(shape))
    return pl.pallas_call(
        _gdn_prompt_kernel, grid=(bsz, nc),
        in_specs=[row(QKV_A), row(H_A * DV_A), row(LANES), full(conv_w.shape), full((1, LANES)), full((1, LANES)),
                  full((1, DV_A))],
        out_specs=[row(H_A * DV_A), pl.BlockSpec((1, H_A, DK_A, DV_A), lambda b, c: (b, 0, 0, 0))],
        out_shape=[jax.ShapeDtypeStruct((m, H_A * DV_A), F32), jax.ShapeDtypeStruct((bsz, H_A, DK_A, DV_A), F32)],
        scratch_shapes=[pltpu.VMEM((CHUNK + 8, QKV_A), F32)],
        compiler_params=_cparams("parallel", "arbitrary"), name="gdn_prompt")(
            qkv, z, gates, conv_w, alog_row, prow, norm_g.reshape(1, DV_A))


def _mlstm_prompt_kernel(q_ref, k_ref, v_ref, og_ref, gt_ref, alog_ref, prow_ref, ng_ref, mix_ref, c_ref, m_ref):
    cidx = pl.program_id(1)
    L = CHUNK

    @pl.when(cidx == 0)
    def _():
        c_ref[...] = jnp.zeros_like(c_ref)
        m_ref[...] = jnp.zeros_like(m_ref)

    _, _, ipre, logf = _gate_tile(gt_ref[...], alog_ref[...], prow_ref[...])
    r, c = _tri_masks(L)
    incl = c <= r
    fsel = _lanes(logf.shape, LANE_F, H_B)
    bcum = jnp.dot(incl.astype(F32), jnp.where(fsel, logf, 0.0), preferred_element_type=F32, precision=HI)
    rowvals = jnp.where(_lanes(ipre.shape, LANE_I, H_B), ipre, 0.0) - jnp.where(fsel, bcum, 0.0)
    rows = _nt(_row_selector(H_B, L, (LANE_I, LANE_F)), rowvals, HI)
    mrow = m_ref[0]
    lane_row = lax.broadcasted_iota(jnp.int32, mrow.shape, 1)
    qb = _heads(q_ref[...], H_B, DQK_B).astype(BF16)
    ks = _heads(k_ref[...], H_B, DQK_B) * (DQK_B ** -0.5)
    ones_col = jnp.broadcast_to((lax.broadcasted_iota(jnp.int32, (L, LANES), 1) == 0).astype(F32), (H_B, L, LANES))
    v_ext = jnp.concatenate([_heads(v_ref[...], H_B, DV_B), ones_col], axis=-1).astype(BF16)
    b_c = _head_cols(bcum, H_B, LANE_F)
    i_c = _head_cols(ipre, H_B, LANE_I)
    dmat = jnp.where(incl, b_c + rows.reshape(H_B, L, L), -jnp.inf)
    m_intra = jnp.max(dmat, axis=-1, keepdims=True)
    w_intra = jnp.exp(dmat - m_intra) * _bmm_nt(qb, ks)
    nd_intra = _bmm(w_intra, v_ext)
    b_last = b_c[:, L - 1:L]
    e_end = b_last - b_c + i_c
    e_max = jnp.max(e_end, axis=1, keepdims=True)
    kw = (ks * jnp.exp(e_end - e_max)).astype(BF16)
    kv_end = jnp.stack([_tn(kw[h], v_ext[h]) for h in range(H_B)], axis=0)
    m_prev = jnp.stack([mrow[:, h:h + 1] for h in range(H_B)], axis=0)
    inter = b_c + m_prev
    m_t = jnp.maximum(inter, m_intra)
    c_old = c_ref[0]
    nd = jnp.exp(inter - m_t) * _bmm(qb, c_old) + jnp.exp(m_intra - m_t) * nd_intra
    hh = nd[:, :, :DV_B] / jnp.maximum(jnp.abs(nd[:, :, DV_B:DV_B + 1]), jnp.exp(-m_t))
    m_new = jnp.maximum(b_last + m_prev, e_max)
    c_ref[0] = jnp.exp(b_last + m_prev - m_new) * c_old + jnp.exp(e_max - m_new) * kv_end
    for h in range(H_B):
        mrow = jnp.where(lane_row == h, m_new[h], mrow)
    m_ref[0] = mrow
    hh = hh * lax.rsqrt(jnp.mean(hh * hh, axis=-1, keepdims=True) + NORM_EPS) * ng_ref[...]
    hh = jax.nn.sigmoid(_heads(og_ref[...], H_B, DV_B)) * hh
    mix_ref[...] = jnp.concatenate([hh[h] for h in range(H_B)], axis=-1)


def mlstm_prompt(q, k, v, og, gates, alog_row, prow, norm_g, bsz):
    m = q.shape[0]
    nc = m // bsz // CHUNK
    row = lambda w: pl.BlockSpec((CHUNK, w), lambda b, c: (b * nc + c, 0))
    full = lambda shape: pl.BlockSpec(shape, lambda b, c: (0,) * len(shape))
    return pl.pallas_call(
        _mlstm_prompt_kernel, grid=(bsz, nc),
        in_specs=[row(H_B * DQK_B), row(H_B * DQK_B), row(H_B * DV_B), row(H_B * DV_B), row(LANES),
                  full((1, LANES)), full((1, LANES)), full((1, DV_B))],
        out_specs=[row(H_B * DV_B), pl.BlockSpec((1, H_B, DQK_B, DV_B + LANES), lambda b, c: (b, 0, 0, 0)),
                   pl.BlockSpec((1, 1, LANES), lambda b, c: (b, 0, 0))],
        out_shape=[jax.ShapeDtypeStruct((m, H_B * DV_B), F32),
                   jax.ShapeDtypeStruct((bsz, H_B, DQK_B, DV_B + LANES), F32),
                   jax.ShapeDtypeStruct((bsz, 1, LANES), F32)],
        compiler_params=_cparams("parallel", "arbitrary"), name="mlstm_prompt")(
            q, k, v, og, gates, alog_row, prow, norm_g.reshape(1, DV_B))


def _columns(x8):
    n = x8.shape[1]
    r, c = _tri_masks(n)
    return _nt((r == c).astype(F32), x8, HI)


def _ab_sample_kernel(qn_ref, kn_ref, vn_ref, cq_ref, ck_ref, cv_ref, wq_ref, wk_ref, wv_ref, z_ref,
                      qb_ref, kb_ref, vb_ref, og_ref, gt_ref, alog_ref, prow_ref, nga_ref, ngb_ref,
                      s_in, c_in, n_in, m_in,
                      oa_ref, ob_ref, s_out, c_out, n_out, m_out):
    def conv(new_ref, prev_ref, w_ref):
        acc = w_ref[CONV_W - 1] * new_ref[0]
        for i in range(CONV_W - 1):
            acc = acc + w_ref[i] * prev_ref[0, i]
        return _silu(acc)

    q8 = conv(qn_ref, cq_ref, wq_ref)
    k8 = conv(kn_ref, ck_ref, wk_ref)
    v8 = conv(vn_ref, cv_ref, wv_ref)
    q8 = q8 * lax.rsqrt(jnp.sum(q8 * q8, axis=-1, keepdims=True) + NORM_EPS) * (DK_A ** -0.5)
    k8 = k8 * lax.rsqrt(jnp.sum(k8 * k8, axis=-1, keepdims=True) + NORM_EPS)
    beta, g, ipre, logf = _gate_tile(gt_ref[0], alog_ref[...], prow_ref[...])
    q_cols, k_cols = _columns(q8), _columns(k8)
    z8 = z_ref[0]
    outs = []
    for h in range(H_A):
        s = s_in[0, h] * jnp.exp(g[:, LANE_A + h:LANE_A + h + 1])
        kc = k_cols[:, h:h + 1]
        err = v8[h:h + 1] - jnp.sum(kc * s, axis=0, keepdims=True)
        s = s + kc * (beta[:, LANE_BETA + h:LANE_BETA + h + 1] * err)
        s_out[0, h] = s
        outs.append(jnp.sum(q_cols[:, h:h + 1] * s, axis=0, keepdims=True))
    o = jnp.concatenate(outs, axis=0)
    o = o * lax.rsqrt(jnp.mean(o * o, axis=-1, keepdims=True) + NORM_EPS) * nga_ref[...]
    oa_ref[0] = o * _silu(z8)

    zeros4 = jnp.zeros((8 - H_B, DQK_B), F32)
    qb_cols = _columns(jnp.concatenate([qb_ref[0], zeros4], axis=0))
    kb_cols = _columns(jnp.concatenate([kb_ref[0] * (DQK_B ** -0.5), zeros4], axis=0))
    vb = vb_ref[0]
    n_cols = n_in[0]
    m_row = m_in[0]
    lane_n = lax.broadcasted_iota(jnp.int32, n_cols.shape, 1)
    lane_m = lax.broadcasted_iota(jnp.int32, m_row.shape, 1)
    outs = []
    for h in range(H_B):
        lf = logf[:, LANE_F + h:LANE_F + h + 1]
        it = ipre[:, LANE_I + h:LANE_I + h + 1]
        m_prev = m_row[:, h:h + 1]
        m_new = jnp.maximum(lf + m_prev, it)
        f_sc = jnp.exp(lf + m_prev - m_new)
        i_sc = jnp.exp(it - m_new)
        kc = kb_cols[:, h:h + 1]
        qc = qb_cols[:, h:h + 1]
        cm = f_sc * c_in[0, h] + i_sc * (kc * vb[h:h + 1])
        nn = f_sc * n_cols[:, h:h + 1] + i_sc * kc
        c_out[0, h] = cm
        n_cols = jnp.where(lane_n == h, nn, n_cols)
        m_row = jnp.where(lane_m == h, m_new, m_row)
        num = jnp.sum(qc * cm, axis=0, keepdims=True)
        den = jnp.sum(qc * nn, axis=0, keepdims=True)
        outs.append(num / jnp.maximum(jnp.abs(den), jnp.exp(-m_new)))
    hb = jnp.concatenate(outs, axis=0)
    hb = hb * lax.rsqrt(jnp.mean(hb * hb, axis=-1, keepdims=True) + NORM_EPS) * ngb_ref[...]
    ob_ref[0] = jax.nn.sigmoid(og_ref[0]) * hb
    n_out[0] = n_cols
    m_out[0] = m_row


def ab_sample(qkv, z, q_b, k_b, v_b, o_b, gates, conv_prev, conv_w, alog_row, prow, norm_g_a, norm_g_b,
              s_prev, c_prev, n_prev, m_prev):
    bsz = qkv.shape[0]
    hk = H_A * DK_A
    part = lambda x, i, w: x[..., i * hk:i * hk + H_A * w].reshape(x.shape[:-1] + (H_A, w))
    new_parts = [part(qkv, 0, DK_A), part(qkv, 1, DK_A), part(qkv, 2, DV_A)]
    prev_parts = [part(conv_prev, 0, DK_A), part(conv_prev, 1, DK_A), part(conv_prev, 2, DV_A)]
    w_parts = [part(conv_w, 0, DK_A), part(conv_w, 1, DK_A), part(conv_w, 2, DV_A)]
    args = new_parts + prev_parts + w_parts + [
        z.reshape(bsz, H_A, DV_A), q_b.reshape(bsz, H_B, DQK_B), k_b.reshape(bsz, H_B, DQK_B),
        v_b.reshape(bsz, H_B, DV_B), o_b.reshape(bsz, H_B, DV_B), gates.reshape(bsz, 1, LANES),
        alog_row, prow, norm_g_a.reshape(1, DV_A), norm_g_b.reshape(1, DV_B),
        s_prev, c_prev, jnp.swapaxes(n_prev, 1, 2), m_prev.reshape(bsz, 1, H_B)]

    def spec(x, batched):
        nd = x.ndim
        if batched:
            return pl.BlockSpec((1,) + x.shape[1:], lambda b: (b,) + (0,) * (nd - 1))
        return pl.BlockSpec(x.shape, lambda b: (0,) * nd)

    batched = [True] * 6 + [False] * 3 + [True] * 6 + [False] * 4 + [True] * 4
    out_shape = [jax.ShapeDtypeStruct((bsz, H_A, DV_A), F32), jax.ShapeDtypeStruct((bsz, H_B, DV_B), F32),
                 jax.ShapeDtypeStruct(s_prev.shape, F32), jax.ShapeDtypeStruct(c_prev.shape, F32),
                 jax.ShapeDtypeStruct((bsz, DQK_B, H_B), F32), jax.ShapeDtypeStruct((bsz, 1, H_B), F32)]
    oa, ob, s_new, c_new, n_new, m_new = pl.pallas_call(
        _ab_sample_kernel, grid=(bsz,),
        in_specs=[spec(x, bt) for x, bt in zip(args, batched)],
        out_specs=[spec(x, True) for x in out_shape], out_shape=out_shape,
        compiler_params=_cparams("parallel"), name="ab_sample")(*args)
    mix = jnp.concatenate([oa.reshape(bsz, H_A * DV_A), ob.reshape(bsz, H_B * DV_B)], axis=-1)
    return mix, s_new, c_new, jnp.swapaxes(n_new, 1, 2), m_new.reshape(bsz, H_B)


MOE_BLOCK_PROMPT = 256
MOE_BLOCK_SAMPLE = 32


def _ab_weights(w_in, a_log, dt_bias, b_i, b_f):
    sizes = (QKV_A, H_A * DV_A, H_A, H_A, H_B * DQK_B, H_B * DQK_B, H_B * DV_B, H_B * DV_B, H_B, H_B)
    offs = [0]
    for s in sizes:
        offs.append(offs[-1] + s)
    col = lambda i: w_in[:, offs[i]:offs[i + 1]]
    w_lo = tuple(col(i).astype(BF16) for i in (0, 1, 4, 5, 6, 7))
    w_gate = jnp.concatenate([col(2), col(3), col(8), col(9)], axis=1)
    w_gate = jnp.pad(w_gate, ((0, 0), (0, LANES - w_gate.shape[1])))
    zeros = lambda n: jnp.zeros((n,), F32)
    pad = LANES - 2 * H_A - 2 * H_B
    alog_row = jnp.concatenate([zeros(H_A), a_log.astype(F32), zeros(2 * H_B + pad)]).reshape(1, LANES)
    prow = jnp.concatenate([zeros(H_A), dt_bias.astype(F32), b_i.astype(F32), b_f.astype(F32),
                            zeros(pad)]).reshape(1, LANES)
    return w_lo, w_gate, alog_row, prow


def kernel(x_prompt, x_sample, state_delta_S, state_delta_conv, state_mlstm_C, state_mlstm_n, state_mlstm_m,
           cache_diff_k, cache_diff_v, cache_mem_k, cache_mem_v, page_table, mem_prompt,
           w_in_ab, conv_w_a, a_log_a, dt_bias_a, norm_g_a, b_i_b, b_f_b, norm_g_b, w_out_ab,
           w_qkv_c, lam_q1, lam_k1, lam_q2, lam_k2, subln_g_c, w_o_c, rel_bias,
           w_xq, w_xkv, w_xo, ln_g, ln_b, w_router, b_router, w_moe_in, b_moe_in, w_moe_out, b_moe_out):
    bp, t, d = x_prompt.shape
    bs = x_sample.shape[0]
    xp = x_prompt.reshape(bp * t, d)
    xs = x_sample.reshape(bs, d)
    mem2d = mem_prompt.reshape(bp * N_MEM, d)
    cmk = cache_mem_k.reshape(DEPTH * bs, N_MEM, X_W)
    cmv = cache_mem_v.reshape(DEPTH * bs, N_MEM, X_W)
    wm_in = w_moe_in.reshape(DEPTH * N_EXPERTS, d, 2 * D_FF)
    bm_in = b_moe_in.reshape(DEPTH * N_EXPERTS, 2 * D_FF)
    wm_out = w_moe_out.reshape(DEPTH * N_EXPERTS, D_FF, d)
    bm_out = b_moe_out.reshape(DEPTH * N_EXPERTS, d)
    p_S, p_conv, p_C, p_n, p_m, p_k, p_v, p_mk, p_mv = [], [], [], [], [], [], [], [], []
    s_S, s_conv, s_C, s_n, s_m, s_k, s_v = [], [], [], [], [], [], []
    for layer in range(DEPTH):
        j = layer // 2
        g0, b0 = ln_g[layer, 0].reshape(1, d), ln_b[layer, 0].reshape(1, d)
        g1, b1 = ln_g[layer, 1].reshape(1, d), ln_b[layer, 1].reshape(1, d)
        g2, b2 = ln_g[layer, 2].reshape(1, d), ln_b[layer, 2].reshape(1, d)
        if layer % 2 == 0:
            w_lo, w_gate, alog_row, prow = _ab_weights(w_in_ab[j], a_log_a[j], dt_bias_a[j], b_i_b[j], b_f_b[j])
            w_out = w_out_ab[j].astype(BF16)
            w_out_a, w_out_b = w_out[:H_A * DV_A], w_out[H_A * DV_A:]
            qkv, z, q_b, k_b, v_b, o_b, gates = mm_multi(xp, w_lo, (w_gate,), tm=256)
            mix_a, st_S = gdn_prompt(qkv, z, gates, conv_w_a[j], alog_row, prow, norm_g_a[j], bp)
            mix_b, c_ext, m_row = mlstm_prompt(q_b, k_b, v_b, o_b, gates, alog_row, prow, norm_g_b[j], bp)
            xp = proj_ln([mix_a, mix_b], [w_out_a, w_out_b], xp, g0, b0)
            p_S.append(st_S)
            p_conv.append(qkv.reshape(bp, t, QKV_A)[:, t - (CONV_W - 1):])
            p_C.append(c_ext[..., :DV_B])
            p_n.append(c_ext[..., DV_B])
            p_m.append(m_row[:, 0, :H_B])
            qkv, z, q_b, k_b, v_b, o_b, gates = mm_multi(xs, w_lo, (w_gate,))
            mix, st_S, st_C, st_n, st_m = ab_sample(
                qkv, z, q_b, k_b, v_b, o_b, gates, state_delta_conv[j], conv_w_a[j], alog_row, prow,
                norm_g_a[j], norm_g_b[j], state_delta_S[j].astype(F32), state_mlstm_C[j].astype(F32),
                state_mlstm_n[j].astype(F32), state_mlstm_m[j].astype(F32))
            xs = proj_ln([mix], [w_out], xs, g0, b0)
            s_S.append(st_S)
            s_conv.append(jnp.concatenate([state_delta_conv[j][:, 1:].astype(F32), qkv[:, None, :]], axis=1))
            s_C.append(st_C)
            s_n.append(st_n)
            s_m.append(st_m)
        else:
            lam_init = 0.8 - 0.6 * math.exp(-0.3 * layer)
            lam = (jnp.exp(jnp.sum(lam_q1[j].astype(F32) * lam_k1[j].astype(F32)))
                   - jnp.exp(jnp.sum(lam_q2[j].astype(F32) * lam_k2[j].astype(F32))) + lam_init)
            w_qkv = tuple(w_qkv_c[j][:, i * C_W:(i + 1) * C_W].astype(BF16) for i in range(3))
            w_o = w_o_c[j].astype(BF16)
            q, k, v = mm_multi(xp, w_qkv, tm=512)
            o = diff_attn_prompt_core(q, k, v, bp, rel_bias, lam, lam_init, subln_g_c[j])
            xp = proj_ln([o], [w_o], xp, g0, b0)
            p_k.append(k.reshape(bp, t, H_C, 2 * DH_C))
            p_v.append(v.reshape(bp, t, H_C, VD_C))
            q, k, v = mm_multi(xs, w_qkv)
            o = diff_attn_sample_core(q, k, v, cache_diff_k, cache_diff_v, page_table, j, rel_bias, lam, lam_init,
                                      subln_g_c[j])
            xs = proj_ln([o], [w_o], xs, g0, b0)
            s_k.append(k.reshape(bs, 1, H_C, 2 * DH_C))
            s_v.append(v.reshape(bs, 1, H_C, VD_C))
        w_q, w_o = w_xq[layer].astype(BF16), w_xo[layer].astype(BF16)
        w_r = jnp.pad(w_router[layer].astype(F32), ((0, 0), (0, LANES - N_EXPERTS)))
        b_r = jnp.pad(b_router[layer].astype(F32), (0, LANES - N_EXPERTS)).reshape(1, LANES)
        mk, mv = mm_multi(mem2d, (w_xkv[layer][:, :X_W].astype(BF16), w_xkv[layer][:, X_W:].astype(BF16)))
        p_mk.append(mk.reshape(bp, N_MEM, H_X, DH_X))
        p_mv.append(mv.reshape(bp, N_MEM, H_X, DH_X))
        xp, xp_tiles, idx_p, gate_p = xattn_prompt(xp, mk.reshape(bp, N_MEM, X_W), mv.reshape(bp, N_MEM, X_W),
                                                   w_q, w_o, g1, b1, w_r, b_r)
        (q,) = mm_multi(xs, (w_q,))
        o = xattn_sample_core(q, cmk, cmv, off=layer * bs)
        xs, xs_tiles, idx_s, gate_s = proj_ln_route(o, w_o, xs, g1, b1, w_r, b_r)
        xp = moe_ln(xp, xp_tiles, idx_p, gate_p, wm_in, bm_in, wm_out, bm_out, g2, b2, MOE_BLOCK_PROMPT,
                    e_off=layer * N_EXPERTS)
        xs = moe_ln(xs, xs_tiles, idx_s, gate_s, wm_in, bm_in, wm_out, bm_out, g2, b2, MOE_BLOCK_SAMPLE,
                    e_off=layer * N_EXPERTS)

    return (xp.reshape(bp, t, d), xs.reshape(bs, 1, d),
            jnp.stack(p_S), jnp.stack(p_conv), jnp.stack(p_C), jnp.stack(p_n), jnp.stack(p_m),
            jnp.stack(p_k, axis=1), jnp.stack(p_v, axis=1), jnp.stack(p_mk), jnp.stack(p_mv),
            jnp.stack(s_S), jnp.stack(s_conv), jnp.stack(s_C), jnp.stack(s_n), jnp.stack(s_m),
            jnp.stack(s_k, axis=1), jnp.stack(s_v, axis=1))
```

```python
import functools
import math

import jax
import jax.numpy as jnp
from jax import lax
from jax.experimental import pallas as pl
from jax.experimental.pallas import tpu as pltpu

F32 = jnp.float32
BF16 = jnp.bfloat16
HI = lax.Precision.HIGHEST

D_MODEL = 1024
DEPTH = 2
H_A, DK_A, DV_A, CONV_W, CHUNK = 8, 64, 64, 4, 64
H_B, DQK_B, DV_B = 4, 64, 128
QKV_A = H_A * (2 * DK_A + DV_A)
H_C, DH_C = 8, 64
VD_C = 2 * DH_C
C_W = H_C * 2 * DH_C
N_BUCKETS, MAX_DISTANCE = 32, 128
N_MEM, H_X, DH_X = 256, 4, 128
X_W = H_X * DH_X
N_EXPERTS, TOP_K = 32, 4
D_FF = D_MODEL
SWIGLU_ALPHA, SWIGLU_LIMIT = 1.702, 7.0
DN_ALPHA = (2 * DEPTH) ** 0.25
LN_EPS = 1e-5
NORM_EPS = 1e-6

LANES = 128
SUBLANES = 8
VMEM_PHYSICAL = 64 * 1024 * 1024
VMEM_LIMIT = 48 * 1024 * 1024


def _cparams(*sem):
    return pltpu.CompilerParams(dimension_semantics=tuple(sem), vmem_limit_bytes=VMEM_LIMIT)


def _layer_norm(y, g, b):
    mu = jnp.mean(y, axis=-1, keepdims=True)
    d = y - mu
    var = jnp.mean(d * d, axis=-1, keepdims=True)
    return d * lax.rsqrt(var + LN_EPS) * g + b


def _mm_multi_kernel(x_ref, *refs, n_lo, n_hi):
    n = n_lo + n_hi
    ws, outs = refs[:n], refs[n:]
    x = x_ref[...]
    xb = x.astype(BF16)
    for i in range(n_lo):
        outs[i][...] = jnp.dot(xb, ws[i][...], preferred_element_type=F32)
    for i in range(n_lo, n):
        outs[i][...] = jnp.dot(x, ws[i][...], preferred_element_type=F32, precision=HI)


def mm_multi(x, w_lo, w_hi=(), tm=256):
    m, k = x.shape
    tm = min(tm, m)
    assert m % tm == 0
    ws = tuple(w_lo) + tuple(w_hi)
    in_specs = [pl.BlockSpec((tm, k), lambda i: (i, 0))]
    in_specs += [pl.BlockSpec(w.shape, lambda i: (0, 0)) for w in ws]
    out_specs = [pl.BlockSpec((tm, w.shape[1]), lambda i: (i, 0)) for w in ws]
    out_shape = [jax.ShapeDtypeStruct((m, w.shape[1]), F32) for w in ws]
    return pl.pallas_call(
        functools.partial(_mm_multi_kernel, n_lo=len(w_lo), n_hi=len(w_hi)),
        grid=(m // tm,), in_specs=in_specs, out_specs=out_specs, out_shape=out_shape,
        compiler_params=_cparams("parallel"), name="mm_multi")(x, *ws)


def _proj_ln_kernel(*refs, n):
    a_refs, w_refs = refs[:n], refs[n:2 * n]
    x_ref, g_ref, b_ref, o_ref = refs[2 * n:]
    h = jnp.dot(a_refs[0][...].astype(BF16), w_refs[0][...], preferred_element_type=F32)
    for a_ref, w_ref in zip(a_refs[1:], w_refs[1:]):
        h = h + jnp.dot(a_ref[...].astype(BF16), w_ref[...], preferred_element_type=F32)
    o_ref[...] = _layer_norm(DN_ALPHA * x_ref[...] + h, g_ref[...], b_ref[...])


def proj_ln(a_list, w_list, x_res, g, b, tm=512):
    m, d = x_res.shape
    tm = min(tm, m)
    assert m % tm == 0
    n = len(a_list)
    return pl.pallas_call(
        functools.partial(_proj_ln_kernel, n=n), grid=(m // tm,),
        in_specs=[pl.BlockSpec((tm, a.shape[1]), lambda i: (i, 0)) for a in a_list]
        + [pl.BlockSpec(w.shape, lambda i: (0, 0)) for w in w_list]
        + [pl.BlockSpec((tm, d), lambda i: (i, 0)), pl.BlockSpec((1, d), lambda i: (0, 0)),
           pl.BlockSpec((1, d), lambda i: (0, 0))],
        out_specs=pl.BlockSpec((tm, d), lambda i: (i, 0)),
        out_shape=jax.ShapeDtypeStruct((m, d), F32),
        compiler_params=_cparams("parallel"), name="proj_ln")(*a_list, *w_list, x_res, g, b)


def _mem_attention(q, mk, mv):
    outs = []
    for h in range(H_X):
        sl = slice(h * DH_X, (h + 1) * DH_X)
        s = lax.dot_general(q[:, sl].astype(BF16), mk[:, sl].astype(BF16), (((1,), (1,)), ((), ())),
                            preferred_element_type=F32) * (DH_X ** -0.5)
        s = s - jnp.max(s, axis=-1, keepdims=True)
        p = jnp.exp(s)
        p = p / jnp.sum(p, axis=-1, keepdims=True)
        outs.append(jnp.dot(p.astype(BF16), mv[:, sl].astype(BF16), preferred_element_type=F32))
    return jnp.concatenate(outs, axis=-1)


def _route(x, wr, br):
    logits = jnp.dot(x, wr, preferred_element_type=F32, precision=HI) + br
    lane = lax.broadcasted_iota(jnp.int32, logits.shape, 1)
    work = jnp.where(lane < N_EXPERTS, logits, -jnp.inf)
    idx_out = jnp.zeros(logits.shape, jnp.int32)
    val_out = jnp.full(logits.shape, -jnp.inf, F32)
    for k in range(TOP_K):
        m = jnp.max(work, axis=-1, keepdims=True)
        sel = jnp.min(jnp.where(work == m, lane, LANES), axis=-1, keepdims=True)
        idx_out = jnp.where(lane == k, sel, idx_out)
        val_out = jnp.where(lane == k, m, val_out)
        work = jnp.where(lane == sel, -jnp.inf, work)
    e = jnp.exp(val_out - jnp.max(val_out, axis=-1, keepdims=True))
    gates = e / jnp.sum(e, axis=-1, keepdims=True)
    return idx_out, gates


def _store_token_tiles(o3_ref, y):
    for s in range(y.shape[1] // LANES):
        o3_ref[:, s, :] = y[:, s * LANES:(s + 1) * LANES]


def _xattn_prompt_kernel(x_ref, mk_ref, mv_ref, wq_ref, wo_ref, g_ref, b_ref, wr_ref, br_ref,
                         o_ref, o3_ref, idx_ref, gate_ref):
    x = x_ref[...]
    q = jnp.dot(x.astype(BF16), wq_ref[...], preferred_element_type=F32)
    o = _mem_attention(q, mk_ref[0], mv_ref[0])
    h = jnp.dot(o.astype(BF16), wo_ref[...], preferred_element_type=F32)
    y = _layer_norm(DN_ALPHA * x + h, g_ref[...], b_ref[...])
    o_ref[...] = y
    _store_token_tiles(o3_ref, y)
    idx, gates = _route(y, wr_ref[...], br_ref[...])
    idx_ref[...] = idx
    gate_ref[...] = gates


def xattn_prompt(x, mk, mv, wq, wo, g, b, wr, br, tq=512):
    m, d = x.shape
    bsz = mk.shape[0]
    t = m // bsz
    nq = t // tq
    full = lambda shape: pl.BlockSpec(shape, lambda bi, qi: (0,) * len(shape))
    row = lambda w: pl.BlockSpec((tq, w), lambda bi, qi: (bi * nq + qi, 0))
    return pl.pallas_call(
        _xattn_prompt_kernel, grid=(bsz, nq),
        in_specs=[row(d), pl.BlockSpec((1, N_MEM, X_W), lambda bi, qi: (bi, 0, 0)),
                  pl.BlockSpec((1, N_MEM, X_W), lambda bi, qi: (bi, 0, 0)),
                  full(wq.shape), full(wo.shape), full(g.shape), full(b.shape), full(wr.shape), full(br.shape)],
        out_specs=[row(d), pl.BlockSpec((tq, d // LANES, LANES), lambda bi, qi: (bi * nq + qi, 0, 0)),
                   row(LANES), row(LANES)],
        out_shape=[jax.ShapeDtypeStruct((m, d), F32), jax.ShapeDtypeStruct((m, d // LANES, LANES), F32),
                   jax.ShapeDtypeStruct((m, LANES), jnp.int32), jax.ShapeDtypeStruct((m, LANES), F32)],
        compiler_params=_cparams("parallel", "parallel"), name="xattn_prompt")(x, mk, mv, wq, wo, g, b, wr, br)


def _xattn_sample_kernel(q_ref, mk_ref, mv_ref, o_ref):
    q = jnp.broadcast_to(q_ref[0], (8, X_W))
    o = _mem_attention(q, mk_ref[0], mv_ref[0])
    o_ref[0] = o[0:1]


def xattn_sample_core(q, mk, mv, off=0):
    bsz = q.shape[0]
    spec3 = pl.BlockSpec((1, N_MEM, X_W), lambda bi: (off + bi, 0, 0))
    out = pl.pallas_call(
        _xattn_sample_kernel, grid=(bsz,),
        in_specs=[pl.BlockSpec((1, 1, X_W), lambda bi: (bi, 0, 0)), spec3, spec3],
        out_specs=pl.BlockSpec((1, 1, X_W), lambda bi: (bi, 0, 0)),
        out_shape=jax.ShapeDtypeStruct((bsz, 1, X_W), F32),
        compiler_params=_cparams("parallel"), name="xattn_sample")(q.reshape(bsz, 1, X_W), mk, mv)
    return out.reshape(bsz, X_W)


def _ln_route_kernel(a_ref, w_ref, x_ref, g_ref, b_ref, wr_ref, br_ref, o_ref, o3_ref, idx_ref, gate_ref):
    h = jnp.dot(a_ref[...].astype(BF16), w_ref[...], preferred_element_type=F32)
    y = _layer_norm(DN_ALPHA * x_ref[...] + h, g_ref[...], b_ref[...])
    o_ref[...] = y
    _store_token_tiles(o3_ref, y)
    idx, gates = _route(y, wr_ref[...], br_ref[...])
    idx_ref[...] = idx
    gate_ref[...] = gates


def proj_ln_route(a, w, x_res, g, b, wr, br):
    m, d = x_res.shape
    return pl.pallas_call(
        _ln_route_kernel,
        out_shape=[jax.ShapeDtypeStruct((m, d), F32), jax.ShapeDtypeStruct((m, d // LANES, LANES), F32),
                   jax.ShapeDtypeStruct((m, LANES), jnp.int32), jax.ShapeDtypeStruct((m, LANES), F32)],
        compiler_params=pltpu.CompilerParams(vmem_limit_bytes=VMEM_LIMIT),
        name="proj_ln_route")(a, w, x_res, g, b, wr, br)


def _expert_onehots(idx):
    lane = lax.broadcasted_iota(jnp.int32, idx.shape, 1)
    onehots = [(idx[:, k:k + 1] == lane).astype(F32) for k in range(TOP_K)]
    tot = onehots[0]
    for k in range(1, TOP_K):
        tot = tot + onehots[k]
    return lane, onehots, tot


def _count_kernel(idx_ref, cnt_ref):
    @pl.when(pl.program_id(0) == 0)
    def _():
        cnt_ref[...] = jnp.zeros_like(cnt_ref)

    _, _, tot = _expert_onehots(idx_ref[...])
    cnt_ref[...] = cnt_ref[...] + jnp.sum(tot, axis=0, keepdims=True)


def _dest_kernel(idx_ref, base0_ref, dest_ref, base_ref):
    @pl.when(pl.program_id(0) == 0)
    def _():
        base_ref[...] = base0_ref[...]

    idx = idx_ref[...]
    tr = idx.shape[0]
    lane, onehots, tot = _expert_onehots(idx)
    r = lax.broadcasted_iota(jnp.int32, (tr, tr), 0)
    c = lax.broadcasted_iota(jnp.int32, (tr, tr), 1)
    strict = (c < r).astype(BF16)
    before = jnp.dot(strict, tot.astype(BF16), preferred_element_type=F32) + base_ref[...]
    out = jnp.zeros(idx.shape, F32)
    for k in range(TOP_K):
        out = jnp.where(lane == k, jnp.sum(onehots[k] * before, axis=-1, keepdims=True), out)
    dest_ref[...] = out.astype(jnp.int32)
    base_ref[...] = base_ref[...] + jnp.sum(tot, axis=0, keepdims=True)


def route_slots(idx, bm, tr=256):
    t = idx.shape[0]
    tr = min(tr, t)
    assert t % tr == 0
    rows = pl.BlockSpec((tr, LANES), lambda i: (i, 0))
    one = pl.BlockSpec((1, LANES), lambda i: (0, 0))
    counts = pl.pallas_call(
        _count_kernel, grid=(t // tr,), in_specs=[rows], out_specs=one,
        out_shape=jax.ShapeDtypeStruct((1, LANES), F32),
        compiler_params=_cparams("arbitrary"), name="route_count")(idx)
    counts = counts[0].astype(jnp.int32)
    padded = (counts + bm - 1) // bm * bm
    pad_end = jnp.cumsum(padded)
    base0 = (pad_end - padded).astype(F32).reshape(1, LANES)
    dest = pl.pallas_call(
        _dest_kernel, grid=(t // tr,), in_specs=[rows, one], out_specs=rows,
        out_shape=jax.ShapeDtypeStruct((t, LANES), jnp.int32),
        scratch_shapes=[pltpu.VMEM((1, LANES), F32)],
        compiler_params=_cparams("arbitrary"), name="route_dest")(idx, base0)
    return dest, pad_end[:N_EXPERTS]


def _clamped_swiglu(h):
    glu = jnp.minimum(h[:, :D_FF], SWIGLU_LIMIT)
    lin = jnp.clip(h[:, D_FF:], -SWIGLU_LIMIT, SWIGLU_LIMIT)
    return glu * jax.nn.sigmoid(SWIGLU_ALPHA * glu) * (lin + 1.0)


def _moe_gmm_kernel(bexp_ref, nused_ref, tok_ref, tok_next_ref, dst_ref, x_hbm, win_ref, bin_ref, wout_ref,
                    bout_ref, y_hbm, xbuf, ybuf, zbuf, winb, woutb, gsem, ssem, *, bm):
    i = pl.program_id(0)
    n = pl.num_programs(0)
    nused = nused_ref[0]
    slot = lax.rem(i, 2)
    n_sub = x_hbm.shape[1]
    rows_per_group = xbuf.shape[3]

    def gather_copy(tok, g, u, s):
        return pltpu.make_async_copy(x_hbm.at[tok], xbuf.at[s, g, :, u], gsem.at[s])

    def scatter_copy(dst, g, u, s):
        return pltpu.make_async_copy(ybuf.at[s, g, :, u], y_hbm.at[dst], ssem.at[s])

    def start_rows(copy_fn, idx_ref, s):
        def body(g, carry):
            for u in range(rows_per_group):
                copy_fn(idx_ref[0, 0, g * rows_per_group + u], g, u, s).start(priority=u % 2)
            return carry
        lax.fori_loop(0, bm // rows_per_group, body, 0)

    def wait_gather(s):
        pltpu.make_async_copy(xbuf.at[s], xbuf.at[s], gsem.at[s]).wait()

    def wait_scatter(s):
        pltpu.make_async_copy(ybuf.at[s], ybuf.at[s], ssem.at[s]).wait()

    @pl.when(i == 0)
    def _():
        start_rows(gather_copy, tok_ref, 0)
        zbuf[...] = jnp.zeros_like(zbuf)
        nz = zbuf.shape[0]
        n_real = y_hbm.shape[0] - 2 * bm
        cps = [pltpu.make_async_copy(zbuf, y_hbm.at[pl.ds(n_real + c * nz, nz)], ssem.at[0])
               for c in range(2 * bm // nz)]
        for cp in cps:
            cp.start()
        for cp in cps:
            cp.wait()

    @pl.when(jnp.logical_and(i + 1 < n, i + 1 < nused))
    def _():
        start_rows(gather_copy, tok_next_ref, 1 - slot)

    active = i < nused

    @pl.when(jnp.logical_and(active, jnp.logical_or(i == 0, bexp_ref[i] != bexp_ref[jnp.maximum(i - 1, 0)])))
    def _():
        winb[...] = win_ref[0].astype(BF16)
        woutb[...] = wout_ref[0].astype(BF16)

    @pl.when(active)
    def _():
        wait_gather(slot)

        @pl.when(i >= 2)
        def _():
            wait_scatter(slot)

        xb = jnp.concatenate([xbuf[slot, :, j].reshape(bm, LANES) for j in range(n_sub)], axis=-1).astype(BF16)
        h = jnp.dot(xb, winb[...], preferred_element_type=F32) + bin_ref[0]
        act = _clamped_swiglu(h)
        y = jnp.dot(act.astype(BF16), woutb[...], preferred_element_type=F32) + bout_ref[0]
        for j in range(n_sub):
            ybuf[slot, :, j] = y[:, j * LANES:(j + 1) * LANES].reshape(bm // rows_per_group, rows_per_group, LANES)
        start_rows(scatter_copy, dst_ref, slot)

    @pl.when(i == n - 1)
    def _():
        last = nused - 1

        @pl.when(nused >= 2)
        def _():
            wait_scatter(lax.rem(last + 1, 2))

        @pl.when(nused >= 1)
        def _():
            wait_scatter(lax.rem(last, 2))


def moe_gmm(x, row_tok, row_dst, block_expert, n_used, w_in, b_in, w_out, b_out, n_out_rows, bm):
    n_blocks = row_tok.shape[0]
    n_sub = x.shape[1]
    d = n_sub * LANES
    idx_spec = lambda off: pl.BlockSpec(
        (1, 1, bm), lambda i, be, nu: (jnp.minimum(i + off, n_blocks - 1), 0, 0), memory_space=pltpu.SMEM)
    ex = lambda i, be, nu: (be[i], 0, 0)
    grid_spec = pltpu.PrefetchScalarGridSpec(
        num_scalar_prefetch=2, grid=(n_blocks,),
        in_specs=[idx_spec(0), idx_spec(1), idx_spec(0),
                  pl.BlockSpec(memory_space=pl.ANY),
                  pl.BlockSpec((1, d, 2 * D_FF), ex), pl.BlockSpec((1, 1, 2 * D_FF), ex),
                  pl.BlockSpec((1, D_FF, d), ex), pl.BlockSpec((1, 1, d), ex)],
        out_specs=pl.BlockSpec(memory_space=pl.ANY),
        scratch_shapes=[pltpu.VMEM((2, bm // SUBLANES, n_sub, SUBLANES, LANES), F32),
                        pltpu.VMEM((2, bm // SUBLANES, n_sub, SUBLANES, LANES), F32),
                        pltpu.VMEM((min(bm, 64), n_sub, LANES), F32),
                        pltpu.VMEM((d, 2 * D_FF), BF16), pltpu.VMEM((D_FF, d), BF16),
                        pltpu.SemaphoreType.DMA((2,)), pltpu.SemaphoreType.DMA((2,))])
    w_elems = d * 2 * D_FF + D_FF * d
    vmem_bytes = (2 * 4 + 2) * w_elems + 4 * bm * d * 4 + bm * (2 * D_FF + D_FF + 2 * d) * 4
    vmem_bytes = min(vmem_bytes + (4 << 20), VMEM_PHYSICAL - (6 << 20))
    return pl.pallas_call(
        functools.partial(_moe_gmm_kernel, bm=bm), grid_spec=grid_spec,
        out_shape=jax.ShapeDtypeStruct((n_out_rows, n_sub, LANES), F32),
        compiler_params=pltpu.CompilerParams(dimension_semantics=("arbitrary",), vmem_limit_bytes=vmem_bytes),
        name="moe_gmm")(
            block_expert, n_used, row_tok, row_tok, row_dst, x, w_in, b_in.reshape(b_in.shape[0], 1, -1),
            w_out, b_out.reshape(b_out.shape[0], 1, -1))


def _combine_ln_kernel(y_ref, gate_ref, x_ref, g_ref, b_ref, o_ref):
    gates = gate_ref[...]
    cols = []
    for s in range(y_ref.shape[2]):
        acc = gates[:, 0:1] * y_ref[:, 0, s, :]
        for k in range(1, TOP_K):
            acc = acc + gates[:, k:k + 1] * y_ref[:, k, s, :]
        cols.append(acc)
    o_ref[...] = _layer_norm(DN_ALPHA * x_ref[...] + jnp.concatenate(cols, axis=-1), g_ref[...], b_ref[...])


def combine_ln(y4, gates, x_res, g, b, tc=256):
    t, d = x_res.shape
    tc = min(tc, t)
    assert t % tc == 0
    return pl.pallas_call(
        _combine_ln_kernel, grid=(t // tc,),
        in_specs=[pl.BlockSpec((tc,) + y4.shape[1:], lambda i: (i, 0, 0, 0)),
                  pl.BlockSpec((tc, LANES), lambda i: (i, 0)),
                  pl.BlockSpec((tc, d), lambda i: (i, 0)), pl.BlockSpec((1, d), lambda i: (0, 0)),
                  pl.BlockSpec((1, d), lambda i: (0, 0))],
        out_specs=pl.BlockSpec((tc, d), lambda i: (i, 0)),
        out_shape=jax.ShapeDtypeStruct((t, d), F32),
        compiler_params=_cparams("parallel"), name="combine_ln")(y4, gates, x_res, g, b)


def moe_ln(x, x_tiles, idx, gates, w_in, b_in, w_out, b_out, g, b, bm, e_off=0):
    t, d = x.shape
    tk = t * TOP_K
    dest, pad_end = route_slots(idx, bm)
    n_blocks = -(-tk // bm) + N_EXPERTS
    n_rows = n_blocks * bm
    src = jnp.arange(tk, dtype=jnp.int32)
    row_src = jnp.full((n_rows,), -1, jnp.int32).at[dest[:, :TOP_K].reshape(tk)].set(
        src, unique_indices=True, mode='promise_in_bounds')
    slot = jnp.arange(n_rows, dtype=jnp.int32)
    slot_in_pair = (slot // bm % 2) * bm + slot % bm
    row_tok = jnp.where(row_src >= 0, row_src // TOP_K, 0).reshape(n_blocks, 1, bm)
    row_dst = jnp.where(row_src >= 0, row_src, tk + slot_in_pair).reshape(n_blocks, 1, bm)
    block_start = jnp.arange(n_blocks, dtype=jnp.int32) * bm
    block_expert = jnp.sum((pad_end[None, :] <= block_start[:, None]).astype(jnp.int32), axis=1)
    block_expert = jnp.minimum(block_expert, N_EXPERTS - 1) + e_off
    n_used = (pad_end[-1] // bm).astype(jnp.int32).reshape(1)
    y = moe_gmm(x_tiles, row_tok, row_dst, block_expert, n_used, w_in, b_in, w_out, b_out, tk + 2 * bm, bm)
    y4 = y.reshape((tk + 2 * bm) // TOP_K, TOP_K, d // LANES, LANES)
    return combine_ln(y4, gates, x, g, b)


def _t5_causal_bucket(dist):
    n = jnp.maximum(dist, 0)
    max_exact = N_BUCKETS // 2
    nf = jnp.maximum(n, max_exact).astype(F32)
    large = max_exact + (jnp.log(nf / max_exact) / math.log(MAX_DISTANCE / max_exact)
                         * (N_BUCKETS - max_exact)).astype(jnp.int32)
    large = jnp.minimum(large, N_BUCKETS - 1)
    return jnp.where(n < max_exact, n, large)


def _split_maps(q):
    lane = lax.broadcasted_iota(jnp.int32, q.shape, 1)
    qs = q * (DH_C ** -0.5)
    return (jnp.where(lane < DH_C, qs, 0.0).astype(BF16), jnp.where(lane >= DH_C, qs, 0.0).astype(BF16))


def _diff_finish(acc1, l1, acc2, l2, lam, out_scale, g):
    o = acc1 / l1 - lam * (acc2 / l2)
    o = o * lax.rsqrt(jnp.mean(o * o, axis=-1, keepdims=True) + NORM_EPS) * g
    return o * out_scale


def _diff_prompt_kernel(sc_ref, far_ref, q_ref, k_ref, v_ref, bias_ref, g_ref, o_ref, m_ref, a_ref, *, tq):
    h = pl.program_id(1)
    qi = pl.program_id(2)
    qm = jnp.concatenate(_split_maps(q_ref[...]), axis=0)
    m_ref[...] = jnp.full_like(m_ref, -jnp.inf)
    a_ref[...] = jnp.zeros_like(a_ref)
    ones_col = (lax.broadcasted_iota(jnp.int32, (tq, LANES), 1) == 0).astype(BF16)
    reps = tq // LANES

    def update(tiles):
        scores, values = [], []
        for kidx, bias, shift, causal in tiles:
            start = pl.multiple_of(kidx * tq, tq)
            kt = k_ref[0, pl.ds(start, tq), :].astype(BF16)
            values.append(jnp.concatenate([v_ref[0, pl.ds(start, tq), :].astype(BF16), ones_col], axis=-1))
            s = lax.dot_general(qm, kt, (((1,), (1,)), ((), ())), preferred_element_type=F32)
            if bias is not None:
                s = s + jnp.concatenate([bias, bias], axis=0)
            if causal:
                r = lax.broadcasted_iota(jnp.int32, s.shape, 0)
                c = lax.broadcasted_iota(jnp.int32, s.shape, 1)
                s = jnp.where(c <= lax.bitwise_and(r, tq - 1), s, -jnp.inf)
            scores.append((s, shift))
        m_old = m_ref[...]
        m_new = m_old
        for s, shift in scores:
            m_new = jnp.maximum(m_new, jnp.max(s, axis=-1, keepdims=True) + shift)
        alpha = jnp.exp(m_old - m_new)
        acc = jnp.concatenate([alpha, alpha], axis=-1) * a_ref[...]
        for (s, shift), vt in zip(scores, values):
            p = jnp.exp(s - jnp.concatenate([m_new - shift] * reps, axis=-1))
            acc = acc + jnp.dot(p.astype(BF16), vt, preferred_element_type=F32)
        a_ref[...] = acc
        m_ref[...] = m_new

    far_bias = far_ref[h]
    n_far = jnp.maximum(qi - 1, 0)

    def far_body(j, carry):
        update([(2 * j, None, far_bias, False), (2 * j + 1, None, far_bias, False)])
        return carry

    lax.fori_loop(0, n_far // 2, far_body, 0)

    @pl.when(lax.rem(n_far, 2) == 1)
    def _():
        update([(n_far - 1, None, far_bias, False)])

    @pl.when(qi >= 1)
    def _():
        update([(qi - 1, bias_ref[0, 1], 0.0, False), (qi, bias_ref[0, 0], 0.0, True)])

    @pl.when(qi == 0)
    def _():
        update([(0, bias_ref[0, 0], 0.0, True)])

    acc = a_ref[...]
    o_ref[...] = _diff_finish(acc[:tq, :VD_C], acc[:tq, VD_C:VD_C + 1], acc[tq:, :VD_C], acc[tq:, VD_C:VD_C + 1],
                              sc_ref[0], sc_ref[1], g_ref[...])


def _bias_tiles_kernel(rb_ref, bucket_ref, o_ref):
    h = pl.program_id(0)
    b = bucket_ref[...]
    out = jnp.zeros(b.shape, F32)
    for k in range(N_BUCKETS):
        out = jnp.where(b == k, rb_ref[k * H_C + h], out)
    o_ref[0] = out


def diff_attn_prompt_core(q, k, v, bsz, rel_bias, lam, lam_init, subln_g, tq=256):
    m = q.shape[0]
    t = m // bsz
    nq = t // tq
    ii = jnp.arange(tq)[:, None]
    jj = jnp.arange(tq)[None, :]
    buckets = _t5_causal_bucket(jnp.stack([jnp.maximum(ii - jj, 0), tq + ii - jj])).astype(jnp.int32)
    bias = pl.pallas_call(
        _bias_tiles_kernel, grid=(H_C,),
        in_specs=[pl.BlockSpec(memory_space=pltpu.SMEM), pl.BlockSpec((2, tq, tq), lambda h: (0, 0, 0))],
        out_specs=pl.BlockSpec((1, 2, tq, tq), lambda h: (h, 0, 0, 0)),
        out_shape=jax.ShapeDtypeStruct((H_C, 2, tq, tq), F32),
        compiler_params=_cparams("parallel"), name="t5_bias_tiles")(rel_bias.astype(F32).reshape(-1), buckets)
    far = rel_bias[_t5_causal_bucket(jnp.array(2 * tq))].astype(F32)
    scal = jnp.stack([lam, 1.0 - lam_init]).astype(F32)
    k3 = k.reshape(bsz, t, C_W)
    v3 = v.reshape(bsz, t, C_W)
    smem = pl.BlockSpec(memory_space=pltpu.SMEM)
    kv_spec = pl.BlockSpec((1, t, VD_C), lambda b, h, i: (b, 0, h))
    row = pl.BlockSpec((tq, VD_C), lambda b, h, i: (b * nq + i, h))
    stat = pltpu.VMEM((2 * tq, LANES), F32)
    acc = pltpu.VMEM((2 * tq, VD_C + LANES), F32)
    return pl.pallas_call(
        functools.partial(_diff_prompt_kernel, tq=tq), grid=(bsz, H_C, nq),
        in_specs=[smem, smem, row, kv_spec, kv_spec,
                  pl.BlockSpec((1, 2, tq, tq), lambda b, h, i: (h, 0, 0, 0)),
                  pl.BlockSpec((1, VD_C), lambda b, h, i: (0, 0))],
        out_specs=row, out_shape=jax.ShapeDtypeStruct((m, C_W), F32),
        scratch_shapes=[stat, acc],
        compiler_params=_cparams("parallel", "parallel", "parallel"), name="diff_attn_prompt")(
            scal, far, q, k3, v3, bias, subln_g.reshape(1, VD_C))


def _diff_sample_kernel(pt_ref, sc_ref, q_ref, kn_ref, vn_ref, bm_ref, bnew_ref, g_ref, *refs, n_pp):
    k_refs, v_refs = refs[:n_pp], refs[n_pp:2 * n_pp]
    o_ref, m_ref, l_ref, a_ref = refs[2 * n_pp:]
    p = pl.program_id(1)
    n_p = pl.num_programs(1)
    q8 = q_ref[0]
    qm = jnp.concatenate(_split_maps(q8), axis=0)

    @pl.when(p == 0)
    def _():
        m_ref[...] = jnp.full_like(m_ref, -jnp.inf)
        l_ref[...] = jnp.zeros_like(l_ref)
        a_ref[...] = jnp.zeros_like(a_ref)

    scores = []
    for j in range(n_pp):
        bias = bm_ref[jnp.where(p == n_p - 1, 1, 0)] if j == n_pp - 1 else bm_ref[0]
        k2 = k_refs[j][0, 0].reshape(-1, 2 * DH_C).astype(BF16)
        scores.append(lax.dot_general(qm, k2, (((1,), (1,)), ((), ())), preferred_element_type=F32) + bias)
    m_old = m_ref[...]
    m_new = m_old
    for s in scores:
        m_new = jnp.maximum(m_new, jnp.max(s, axis=-1, keepdims=True))
    alpha = jnp.exp(m_old - m_new)
    l_new = alpha * l_ref[...]
    a_new = alpha * a_ref[...]
    for j, s in enumerate(scores):
        pr = jnp.exp(s - m_new)
        l_new = l_new + jnp.sum(pr, axis=-1, keepdims=True)
        v2 = v_refs[j][0, 0].reshape(-1, VD_C).astype(BF16)
        a_new = a_new + jnp.dot(pr.astype(BF16), v2, preferred_element_type=F32)
    l_ref[...] = l_new
    a_ref[...] = a_new
    m_ref[...] = m_new

    @pl.when(p == n_p - 1)
    def _():
        kn = jnp.concatenate([kn_ref[0], kn_ref[0]], axis=0)
        vn = jnp.concatenate([vn_ref[0], vn_ref[0]], axis=0)
        s = jnp.sum(qm.astype(F32) * kn, axis=-1, keepdims=True) + bnew_ref[:, 0:1]
        m_old = m_ref[...]
        m_new = jnp.maximum(m_old, s)
        alpha = jnp.exp(m_old - m_new)
        pr = jnp.exp(s - m_new)
        l = alpha * l_ref[...] + pr
        a = alpha * a_ref[...] + pr * vn
        o_ref[0] = _diff_finish(a[:H_C], l[:H_C], a[H_C:], l[H_C:], sc_ref[0], sc_ref[1], g_ref[...])


def diff_attn_sample_core(q, k_new, v_new, cache_k, cache_v, page_table, layer_j, rel_bias, lam, lam_init, subln_g,
                          n_pp=8):
    bsz = q.shape[0]
    n_pages = page_table.shape[1]
    page = cache_k.shape[2]
    past = n_pages * page
    assert n_pages % n_pp == 0
    dist_last = past - ((n_pages - 1) * page + jnp.arange(page))
    b_last = rel_bias[_t5_causal_bucket(dist_last)].astype(F32)
    b_far = jnp.broadcast_to(rel_bias[_t5_causal_bucket(jnp.array(page + 1))].astype(F32), (page, H_C))
    eye = jnp.eye(H_C, dtype=bool)

    def expand(bt):
        full = jnp.where(eye[:, None, :], bt.T[:, :, None], -jnp.inf).reshape(H_C, page * H_C)
        return jnp.concatenate([full, full], axis=0)

    bm = jnp.stack([expand(b_far), expand(b_last)])
    b_new = rel_bias[_t5_causal_bucket(jnp.array(0))].astype(F32)
    b_new = jnp.broadcast_to(jnp.concatenate([b_new, b_new])[:, None], (2 * H_C, LANES))
    scal = jnp.stack([lam, 1.0 - lam_init]).astype(F32)
    smem = pl.BlockSpec(memory_space=pltpu.SMEM)
    head3 = pl.BlockSpec((1, H_C, VD_C), lambda b, p, pt: (b, 0, 0))
    full = lambda shape: pl.BlockSpec(shape, lambda b, p, pt: (0,) * len(shape))

    def page_spec(j):
        return pl.BlockSpec((1, 1, page, H_C, VD_C), lambda b, p, pt: (pt[b, p * n_pp + j], layer_j, 0, 0, 0))

    grid_spec = pltpu.PrefetchScalarGridSpec(
        num_scalar_prefetch=1, grid=(bsz, n_pages // n_pp),
        in_specs=[smem, head3, head3, head3, full(bm.shape), full(b_new.shape), full((1, VD_C))]
        + [page_spec(j) for j in range(n_pp)] * 2,
        out_specs=head3,
        scratch_shapes=[pltpu.VMEM((2 * H_C, 1), F32), pltpu.VMEM((2 * H_C, 1), F32),
                        pltpu.VMEM((2 * H_C, VD_C), F32)])
    out = pl.pallas_call(
        functools.partial(_diff_sample_kernel, n_pp=n_pp), grid_spec=grid_spec,
        out_shape=jax.ShapeDtypeStruct((bsz, H_C, VD_C), F32),
        compiler_params=_cparams("parallel", "arbitrary"), name="diff_attn_sample")(
            page_table, scal, q.reshape(bsz, H_C, VD_C), k_new.reshape(bsz, H_C, VD_C),
            v_new.reshape(bsz, H_C, VD_C), bm, b_new, subln_g.reshape(1, VD_C),
            *([cache_k] * n_pp), *([cache_v] * n_pp))
    return out.reshape(bsz, C_W)


LANE_BETA, LANE_A, LANE_I, LANE_F = 0, H_A, 2 * H_A, 2 * H_A + H_B


def _softplus(x):
    return jnp.maximum(x, 0.0) + jnp.log1p(jnp.exp(-jnp.abs(x)))


def _silu(x):
    return x * jax.nn.sigmoid(x)


def _lanes(shape, lo, n):
    lane = lax.broadcasted_iota(jnp.int32, shape, 1)
    return jnp.logical_and(lane >= lo, lane < lo + n)


def _gate_tile(gt, alog_row, prow):
    z = gt + prow
    return jax.nn.sigmoid(gt), -jnp.exp(alog_row) * _softplus(z), z, -_softplus(-z)


def _nt(a, b, precision=None):
    return lax.dot_general(a, b, (((1,), (1,)), ((), ())), preferred_element_type=F32, precision=precision)


def _tn(a, b):
    return lax.dot_general(a, b, (((0,), (0,)), ((), ())), preferred_element_type=F32)


def _row_selector(n_heads, length, lanes_of_head):
    r = lax.broadcasted_iota(jnp.int32, (n_heads * length, LANES), 0) // length
    lane = lax.broadcasted_iota(jnp.int32, (n_heads * length, LANES), 1)
    sel = jnp.zeros((n_heads * length, LANES), F32)
    for lo in lanes_of_head:
        sel = sel + (lane == r + lo).astype(F32)
    return sel


def _tri_masks(n):
    r = lax.broadcasted_iota(jnp.int32, (n, n), 0)
    c = lax.broadcasted_iota(jnp.int32, (n, n), 1)
    return r, c


def _bmm(a, b):
    return lax.dot_general(a.astype(BF16), b.astype(BF16), (((2,), (1,)), ((0,), (0,))),
                           preferred_element_type=F32)


def _bmm_nt(a, b):
    return lax.dot_general(a.astype(BF16), b.astype(BF16), (((2,), (2,)), ((0,), (0,))),
                           preferred_element_type=F32)


def _heads(x, n_heads, width, offset=0):
    return jnp.stack([x[c * CHUNK:(c + 1) * CHUNK, offset + h * width:offset + (h + 1) * width]
                      for c in range(x.shape[0] // CHUNK) for h in range(n_heads)], axis=0)


def _head_cols(x, n_heads, lane0):
    return jnp.stack([x[c * CHUNK:(c + 1) * CHUNK, lane0 + h:lane0 + h + 1]
                      for c in range(x.shape[0] // CHUNK) for h in range(n_heads)], axis=0)


def _unheads(x, n_heads):
    n = x.shape[0] // n_heads
    return jnp.concatenate([jnp.concatenate([x[c * n_heads + h] for h in range(n_heads)], axis=-1)
                            for c in range(n)], axis=0)


def _chunk_cumsum(x):
    r, c = _tri_masks(x.shape[0])
    tri = jnp.logical_and(c <= r, r // CHUNK == c // CHUNK).astype(F32)
    return jnp.dot(tri, x, preferred_element_type=F32, precision=HI)


def _chunk_rows(sel, x):
    n = x.shape[0] // CHUNK
    rows = [_nt(sel, x[c * CHUNK:(c + 1) * CHUNK], HI) for c in range(n)]
    return jnp.concatenate(rows, axis=0).reshape(n * sel.shape[0] // CHUNK, CHUNK, CHUNK)


def _unit_lower_inverse(nmat, r, c):
    mm = _bmm
    eye = (r == c).astype(F32)
    same = (r // 16) == (c // 16)
    nd = jnp.where(same, nmat, 0.0)
    off = nmat - nd
    dinv = eye - nd
    p = nd
    for _ in range(3):
        p = mm(p, p)
        dinv = dinv + mm(dinv, p)
    m = mm(dinv, off)
    m2 = mm(m, m)
    left = eye - m
    left = left + mm(left, m2)
    return mm(left, dinv)


def _gdn_prompt_kernel(qkv_ref, z_ref, gt_ref, convw_ref, alog_ref, prow_ref, ng_ref, mix_ref, s_ref, ext_ref):
    cidx = pl.program_id(1)
    L = CHUNK
    rows = qkv_ref.shape[0]
    n_ch = rows // L

    @pl.when(cidx == 0)
    def _():
        ext_ref[0:8, :] = jnp.zeros((8, QKV_A), F32)
        s_ref[...] = jnp.zeros_like(s_ref)

    ext_ref[8:8 + rows, :] = qkv_ref[...]
    acc = convw_ref[0:1, :] * ext_ref[pl.ds(8 - (CONV_W - 1), rows), :]
    for i in range(1, CONV_W):
        acc = acc + convw_ref[i:i + 1, :] * ext_ref[pl.ds(8 - (CONV_W - 1) + i, rows), :]
    ext_ref[0:8, :] = ext_ref[rows:rows + 8, :]
    cs = _silu(acc)

    beta, g, _, _ = _gate_tile(gt_ref[...], alog_ref[...], prow_ref[...])
    r, c = _tri_masks(L)
    incl = c <= r
    strict = c < r
    gcum = _chunk_cumsum(jnp.where(_lanes(g.shape, LANE_A, H_A), g, 0.0))
    gc_rows = _chunk_rows(_row_selector(H_A, L, (LANE_A,)), gcum)
    q = _heads(cs, H_A, DK_A)
    k = _heads(cs, H_A, DK_A, H_A * DK_A)
    v = _heads(cs, H_A, DV_A, 2 * H_A * DK_A)
    q = q * lax.rsqrt(jnp.sum(q * q, axis=-1, keepdims=True) + NORM_EPS) * (DK_A ** -0.5)
    k = k * lax.rsqrt(jnp.sum(k * k, axis=-1, keepdims=True) + NORM_EPS)
    beta_c = _head_cols(beta, H_A, LANE_BETA)
    gc_c = _head_cols(gcum, H_A, LANE_A)
    decay = jnp.where(incl, jnp.exp(jnp.where(incl, gc_c - gc_rows, 0.0)), 0.0)
    kb = k.astype(BF16)
    nmat = jnp.where(strict, beta_c * _bmm_nt(kb, kb) * decay, 0.0)
    egc = jnp.exp(gc_c)
    rhs = jnp.concatenate([v * beta_c, k * (beta_c * egc)], axis=-1)
    sol = _bmm(_unit_lower_inverse(nmat, r, c), rhs)
    u = sol[:, :, :DV_A]
    wq = jnp.concatenate([sol[:, :, DV_A:], q * egc], axis=1).astype(BF16)
    qk = (_bmm_nt(q, kb) * decay).astype(BF16)
    gc_last = gc_c[:, L - 1:L]
    k_tail = (k * jnp.exp(gc_last - gc_c)).astype(BF16)
    g_tail = jnp.exp(gc_last)
    state = s_ref[0]
    outs = []
    for ci in range(n_ch):
        sl = slice(ci * H_A, (ci + 1) * H_A)
        ws = _bmm(wq[sl], state)
        delta = u[sl] - ws[:, :L]
        outs.append(ws[:, L:] + _bmm(qk[sl], delta))
        delta_b = delta.astype(BF16)
        state = state * g_tail[sl] + jnp.stack([_tn(k_tail[ci * H_A + h], delta_b[h]) for h in range(H_A)], axis=0)
    s_ref[0] = state
    o = jnp.concatenate(outs, axis=0)
    o = o * lax.rsqrt(jnp.mean(o * o, axis=-1, keepdims=True) + NORM_EPS) * ng_ref[...]
    o = o * _silu(_heads(z_ref[...], H_A, DV_A))
    mix_ref[...] = _unheads(o, H_A)


AB_CHUNKS_PER_STEP = 4


def gdn_prompt(qkv, z, gates, conv_w, alog_row, prow, norm_g, bsz):
    m = qkv.shape[0]
    rows = AB_CHUNKS_PER_STEP * CHUNK
    nc = m // bsz // rows
    row = lambda w: pl.BlockSpec((rows, w), lambda b, c: (b * nc + c, 0))
    full = lambda shape: pl.BlockSpec(shape, lambda b, c: (0,) * len(shape))
    return pl.pallas_call(
        _gdn_prompt_kernel, grid=(bsz, nc),
        in_specs=[row(QKV_A), row(H_A * DV_A), row(LANES), full(conv_w.shape), full((1, LANES)), full((1, LANES)),
                  full((1, DV_A))],
        out_specs=[row(H_A * DV_A), pl.BlockSpec((1, H_A, DK_A, DV_A), lambda b, c: (b, 0, 0, 0))],
        out_shape=[jax.ShapeDtypeStruct((m, H_A * DV_A), F32), jax.ShapeDtypeStruct((bsz, H_A, DK_A, DV_A), F32)],
        scratch_shapes=[pltpu.VMEM((rows + 8, QKV_A), F32)],
        compiler_params=_cparams("parallel", "arbitrary"), name="gdn_prompt")(
            qkv, z, gates, conv_w, alog_row, prow, norm_g.reshape(1, DV_A))


def _mlstm_prompt_kernel(q_ref, k_ref, v_ref, og_ref, gt_ref, alog_ref, prow_ref, ng_ref, mix_ref, c_ref, m_ref):
    cidx = pl.program_id(1)
    L = CHUNK

    @pl.when(cidx == 0)
    def _():
        c_ref[...] = jnp.zeros_like(c_ref)
        m_ref[...] = jnp.zeros_like(m_ref)

    n_ch = q_ref.shape[0] // L
    _, _, ipre, logf = _gate_tile(gt_ref[...], alog_ref[...], prow_ref[...])
    r, c = _tri_masks(L)
    incl = c <= r
    fsel = _lanes(logf.shape, LANE_F, H_B)
    bcum = _chunk_cumsum(jnp.where(fsel, logf, 0.0))
    rowvals = jnp.where(_lanes(ipre.shape, LANE_I, H_B), ipre, 0.0) - jnp.where(fsel, bcum, 0.0)
    rows = _chunk_rows(_row_selector(H_B, L, (LANE_I, LANE_F)), rowvals)
    mrow = m_ref[0]
    lane_row = lax.broadcasted_iota(jnp.int32, mrow.shape, 1)
    qb = _heads(q_ref[...], H_B, DQK_B).astype(BF16)
    ks = _heads(k_ref[...], H_B, DQK_B) * (DQK_B ** -0.5)
    ones_col = jnp.broadcast_to((lax.broadcasted_iota(jnp.int32, (L, LANES), 1) == 0).astype(F32),
                                (n_ch * H_B, L, LANES))
    v_ext = jnp.concatenate([_heads(v_ref[...], H_B, DV_B), ones_col], axis=-1).astype(BF16)
    b_c = _head_cols(bcum, H_B, LANE_F)
    i_c = _head_cols(ipre, H_B, LANE_I)
    dmat = jnp.where(incl, b_c + rows, -jnp.inf)
    m_intra = jnp.max(dmat, axis=-1, keepdims=True)
    w_intra = jnp.exp(dmat - m_intra) * _bmm_nt(qb, ks)
    nd_intra = _bmm(w_intra, v_ext)
    b_last = b_c[:, L - 1:L]
    e_end = b_last - b_c + i_c
    e_max = jnp.max(e_end, axis=1, keepdims=True)
    kw = (ks * jnp.exp(e_end - e_max)).astype(BF16)
    kv_end = jnp.stack([_tn(kw[i], v_ext[i]) for i in range(n_ch * H_B)], axis=0)
    m_prev = jnp.stack([mrow[:, h:h + 1] for h in range(H_B)], axis=0)
    state = c_ref[0]
    outs = []
    for ci in range(n_ch):
        sl = slice(ci * H_B, (ci + 1) * H_B)
        inter = b_c[sl] + m_prev
        m_t = jnp.maximum(inter, m_intra[sl])
        nd = jnp.exp(inter - m_t) * _bmm(qb[sl], state) + jnp.exp(m_intra[sl] - m_t) * nd_intra[sl]
        outs.append(nd[:, :, :DV_B] / jnp.maximum(jnp.abs(nd[:, :, DV_B:DV_B + 1]), jnp.exp(-m_t)))
        m_new = jnp.maximum(b_last[sl] + m_prev, e_max[sl])
        state = jnp.exp(b_last[sl] + m_prev - m_new) * state + jnp.exp(e_max[sl] - m_new) * kv_end[sl]
        m_prev = m_new
    c_ref[0] = state
    for h in range(H_B):
        mrow = jnp.where(lane_row == h, m_prev[h], mrow)
    m_ref[0] = mrow
    hh = jnp.concatenate(outs, axis=0)
    hh = hh * lax.rsqrt(jnp.mean(hh * hh, axis=-1, keepdims=True) + NORM_EPS) * ng_ref[...]
    hh = jax.nn.sigmoid(_heads(og_ref[...], H_B, DV_B)) * hh
    mix_ref[...] = _unheads(hh, H_B)


def mlstm_prompt(q, k, v, og, gates, alog_row, prow, norm_g, bsz):
    m = q.shape[0]
    rows = AB_CHUNKS_PER_STEP * CHUNK
    nc = m // bsz // rows
    row = lambda w: pl.BlockSpec((rows, w), lambda b, c: (b * nc + c, 0))
    full = lambda shape: pl.BlockSpec(shape, lambda b, c: (0,) * len(shape))
    return pl.pallas_call(
        _mlstm_prompt_kernel, grid=(bsz, nc),
        in_specs=[row(H_B * DQK_B), row(H_B * DQK_B), row(H_B * DV_B), row(H_B * DV_B), row(LANES),
                  full((1, LANES)), full((1, LANES)), full((1, DV_B))],
        out_specs=[row(H_B * DV_B), pl.BlockSpec((1, H_B, DQK_B, DV_B + LANES), lambda b, c: (b, 0, 0, 0)),
                   pl.BlockSpec((1, 1, LANES), lambda b, c: (b, 0, 0))],
        out_shape=[jax.ShapeDtypeStruct((m, H_B * DV_B), F32),
                   jax.ShapeDtypeStruct((bsz, H_B, DQK_B, DV_B + LANES), F32),
                   jax.ShapeDtypeStruct((bsz, 1, LANES), F32)],
        compiler_params=_cparams("parallel", "arbitrary"), name="mlstm_prompt")(
            q, k, v, og, gates, alog_row, prow, norm_g.reshape(1, DV_B))


def _columns(x8):
    n = x8.shape[1]
    r, c = _tri_masks(n)
    return _nt((r == c).astype(F32), x8, HI)


def _ab_sample_kernel(qn_ref, kn_ref, vn_ref, cq_ref, ck_ref, cv_ref, wq_ref, wk_ref, wv_ref, z_ref,
                      qb_ref, kb_ref, vb_ref, og_ref, gt_ref, alog_ref, prow_ref, nga_ref, ngb_ref,
                      s_in, c_in, n_in, m_in,
                      oa_ref, ob_ref, s_out, c_out, n_out, m_out):
    def conv(new_ref, prev_ref, w_ref):
        acc = w_ref[CONV_W - 1] * new_ref[0]
        for i in range(CONV_W - 1):
            acc = acc + w_ref[i] * prev_ref[0, i]
        return _silu(acc)

    q8 = conv(qn_ref, cq_ref, wq_ref)
    k8 = conv(kn_ref, ck_ref, wk_ref)
    v8 = conv(vn_ref, cv_ref, wv_ref)
    q8 = q8 * lax.rsqrt(jnp.sum(q8 * q8, axis=-1, keepdims=True) + NORM_EPS) * (DK_A ** -0.5)
    k8 = k8 * lax.rsqrt(jnp.sum(k8 * k8, axis=-1, keepdims=True) + NORM_EPS)
    beta, g, ipre, logf = _gate_tile(gt_ref[0], alog_ref[...], prow_ref[...])
    q_cols, k_cols = _columns(q8), _columns(k8)
    z8 = z_ref[0]
    outs = []
    for h in range(H_A):
        s = s_in[0, h] * jnp.exp(g[:, LANE_A + h:LANE_A + h + 1])
        kc = k_cols[:, h:h + 1]
        err = v8[h:h + 1] - jnp.sum(kc * s, axis=0, keepdims=True)
        s = s + kc * (beta[:, LANE_BETA + h:LANE_BETA + h + 1] * err)
        s_out[0, h] = s
        outs.append(jnp.sum(q_cols[:, h:h + 1] * s, axis=0, keepdims=True))
    o = jnp.concatenate(outs, axis=0)
    o = o * lax.rsqrt(jnp.mean(o * o, axis=-1, keepdims=True) + NORM_EPS) * nga_ref[...]
    oa_ref[0] = o * _silu(z8)

    zeros4 = jnp.zeros((8 - H_B, DQK_B), F32)
    qb_cols = _columns(jnp.concatenate([qb_ref[0], zeros4], axis=0))
    kb_cols = _columns(jnp.concatenate([kb_ref[0] * (DQK_B ** -0.5), zeros4], axis=0))
    vb = vb_ref[0]
    n_cols = n_in[0]
    m_row = m_in[0]
    lane_n = lax.broadcasted_iota(jnp.int32, n_cols.shape, 1)
    lane_m = lax.broadcasted_iota(jnp.int32, m_row.shape, 1)
    outs = []
    for h in range(H_B):
        lf = logf[:, LANE_F + h:LANE_F + h + 1]
        it = ipre[:, LANE_I + h:LANE_I + h + 1]
        m_prev = m_row[:, h:h + 1]
        m_new = jnp.maximum(lf + m_prev, it)
        f_sc = jnp.exp(lf + m_prev - m_new)
        i_sc = jnp.exp(it - m_new)
        kc = kb_cols[:, h:h + 1]
        qc = qb_cols[:, h:h + 1]
        cm = f_sc * c_in[0, h] + i_sc * (kc * vb[h:h + 1])
        nn = f_sc * n_cols[:, h:h + 1] + i_sc * kc
        c_out[0, h] = cm
        n_cols = jnp.where(lane_n == h, nn, n_cols)
        m_row = jnp.where(lane_m == h, m_new, m_row)
        num = jnp.sum(qc * cm, axis=0, keepdims=True)
        den = jnp.sum(qc * nn, axis=0, keepdims=True)
        outs.append(num / jnp.maximum(jnp.abs(den), jnp.exp(-m_new)))
    hb = jnp.concatenate(outs, axis=0)
    hb = hb * lax.rsqrt(jnp.mean(hb * hb, axis=-1, keepdims=True) + NORM_EPS) * ngb_ref[...]
    ob_ref[0] = jax.nn.sigmoid(og_ref[0]) * hb
    n_out[0] = n_cols
    m_out[0] = m_row


def ab_sample(qkv, z, q_b, k_b, v_b, o_b, gates, conv_prev, conv_w, alog_row, prow, norm_g_a, norm_g_b,
              s_prev, c_prev, n_prev, m_prev):
    bsz = qkv.shape[0]
    hk = H_A * DK_A
    part = lambda x, i, w: x[..., i * hk:i * hk + H_A * w].reshape(x.shape[:-1] + (H_A, w))
    new_parts = [part(qkv, 0, DK_A), part(qkv, 1, DK_A), part(qkv, 2, DV_A)]
    prev_parts = [part(conv_prev, 0, DK_A), part(conv_prev, 1, DK_A), part(conv_prev, 2, DV_A)]
    w_parts = [part(conv_w, 0, DK_A), part(conv_w, 1, DK_A), part(conv_w, 2, DV_A)]
    args = new_parts + prev_parts + w_parts + [
        z.reshape(bsz, H_A, DV_A), q_b.reshape(bsz, H_B, DQK_B), k_b.reshape(bsz, H_B, DQK_B),
        v_b.reshape(bsz, H_B, DV_B), o_b.reshape(bsz, H_B, DV_B), gates.reshape(bsz, 1, LANES),
        alog_row, prow, norm_g_a.reshape(1, DV_A), norm_g_b.reshape(1, DV_B),
        s_prev, c_prev, jnp.swapaxes(n_prev, 1, 2), m_prev.reshape(bsz, 1, H_B)]

    def spec(x, batched):
        nd = x.ndim
        if batched:
            return pl.BlockSpec((1,) + x.shape[1:], lambda b: (b,) + (0,) * (nd - 1))
        return pl.BlockSpec(x.shape, lambda b: (0,) * nd)

    batched = [True] * 6 + [False] * 3 + [True] * 6 + [False] * 4 + [True] * 4
    out_shape = [jax.ShapeDtypeStruct((bsz, H_A, DV_A), F32), jax.ShapeDtypeStruct((bsz, H_B, DV_B), F32),
                 jax.ShapeDtypeStruct(s_prev.shape, F32), jax.ShapeDtypeStruct(c_prev.shape, F32),
                 jax.ShapeDtypeStruct((bsz, DQK_B, H_B), F32), jax.ShapeDtypeStruct((bsz, 1, H_B), F32)]
    oa, ob, s_new, c_new, n_new, m_new = pl.pallas_call(
        _ab_sample_kernel, grid=(bsz,),
        in_specs=[spec(x, bt) for x, bt in zip(args, batched)],
        out_specs=[spec(x, True) for x in out_shape], out_shape=out_shape,
        compiler_params=_cparams("parallel"), name="ab_sample")(*args)
    mix = jnp.concatenate([oa.reshape(bsz, H_A * DV_A), ob.reshape(bsz, H_B * DV_B)], axis=-1)
    return mix, s_new, c_new, jnp.swapaxes(n_new, 1, 2), m_new.reshape(bsz, H_B)


MOE_BLOCK_PROMPT = 512
MOE_BLOCK_SAMPLE = 32


def _ab_weights(w_in, a_log, dt_bias, b_i, b_f):
    sizes = (QKV_A, H_A * DV_A, H_A, H_A, H_B * DQK_B, H_B * DQK_B, H_B * DV_B, H_B * DV_B, H_B, H_B)
    offs = [0]
    for s in sizes:
        offs.append(offs[-1] + s)
    col = lambda i: w_in[:, offs[i]:offs[i + 1]]
    w_lo = tuple(col(i).astype(BF16) for i in (0, 1, 4, 5, 6, 7))
    w_gate = jnp.concatenate([col(2), col(3), col(8), col(9)], axis=1)
    w_gate = jnp.pad(w_gate, ((0, 0), (0, LANES - w_gate.shape[1])))
    zeros = lambda n: jnp.zeros((n,), F32)
    pad = LANES - 2 * H_A - 2 * H_B
    alog_row = jnp.concatenate([zeros(H_A), a_log.astype(F32), zeros(2 * H_B + pad)]).reshape(1, LANES)
    prow = jnp.concatenate([zeros(H_A), dt_bias.astype(F32), b_i.astype(F32), b_f.astype(F32),
                            zeros(pad)]).reshape(1, LANES)
    return w_lo, w_gate, alog_row, prow


def kernel(x_prompt, x_sample, state_delta_S, state_delta_conv, state_mlstm_C, state_mlstm_n, state_mlstm_m,
           cache_diff_k, cache_diff_v, cache_mem_k, cache_mem_v, page_table, mem_prompt,
           w_in_ab, conv_w_a, a_log_a, dt_bias_a, norm_g_a, b_i_b, b_f_b, norm_g_b, w_out_ab,
           w_qkv_c, lam_q1, lam_k1, lam_q2, lam_k2, subln_g_c, w_o_c, rel_bias,
           w_xq, w_xkv, w_xo, ln_g, ln_b, w_router, b_router, w_moe_in, b_moe_in, w_moe_out, b_moe_out):
    bp, t, d = x_prompt.shape
    bs = x_sample.shape[0]
    xp = x_prompt.reshape(bp * t, d)
    xs = x_sample.reshape(bs, d)
    mem2d = mem_prompt.reshape(bp * N_MEM, d)
    cmk = cache_mem_k.reshape(DEPTH * bs, N_MEM, X_W)
    cmv = cache_mem_v.reshape(DEPTH * bs, N_MEM, X_W)
    wm_in = w_moe_in.reshape(DEPTH * N_EXPERTS, d, 2 * D_FF)
    bm_in = b_moe_in.reshape(DEPTH * N_EXPERTS, 2 * D_FF)
    wm_out = w_moe_out.reshape(DEPTH * N_EXPERTS, D_FF, d)
    bm_out = b_moe_out.reshape(DEPTH * N_EXPERTS, d)
    p_S, p_conv, p_C, p_n, p_m, p_k, p_v, p_mk, p_mv = [], [], [], [], [], [], [], [], []
    s_S, s_conv, s_C, s_n, s_m, s_k, s_v = [], [], [], [], [], [], []
    for layer in range(DEPTH):
        j = layer // 2
        g0, b0 = ln_g[layer, 0].reshape(1, d), ln_b[layer, 0].reshape(1, d)
        g1, b1 = ln_g[layer, 1].reshape(1, d), ln_b[layer, 1].reshape(1, d)
        g2, b2 = ln_g[layer, 2].reshape(1, d), ln_b[layer, 2].reshape(1, d)
        if layer % 2 == 0:
            w_lo, w_gate, alog_row, prow = _ab_weights(w_in_ab[j], a_log_a[j], dt_bias_a[j], b_i_b[j], b_f_b[j])
            w_out = w_out_ab[j].astype(BF16)
            w_out_a, w_out_b = w_out[:H_A * DV_A], w_out[H_A * DV_A:]
            qkv, z, q_b, k_b, v_b, o_b, gates = mm_multi(xp, w_lo, (w_gate,), tm=256)
            mix_a, st_S = gdn_prompt(qkv, z, gates, conv_w_a[j], alog_row, prow, norm_g_a[j], bp)
            mix_b, c_ext, m_row = mlstm_prompt(q_b, k_b, v_b, o_b, gates, alog_row, prow, norm_g_b[j], bp)
            xp = proj_ln([mix_a, mix_b], [w_out_a, w_out_b], xp, g0, b0)
            p_S.append(st_S)
            p_conv.append(qkv.reshape(bp, t, QKV_A)[:, t - (CONV_W - 1):])
            p_C.append(c_ext[..., :DV_B])
            p_n.append(c_ext[..., DV_B])
            p_m.append(m_row[:, 0, :H_B])
            qkv, z, q_b, k_b, v_b, o_b, gates = mm_multi(xs, w_lo, (w_gate,))
            mix, st_S, st_C, st_n, st_m = ab_sample(
                qkv, z, q_b, k_b, v_b, o_b, gates, state_delta_conv[j], conv_w_a[j], alog_row, prow,
                norm_g_a[j], norm_g_b[j], state_delta_S[j].astype(F32), state_mlstm_C[j].astype(F32),
                state_mlstm_n[j].astype(F32), state_mlstm_m[j].astype(F32))
            xs = proj_ln([mix], [w_out], xs, g0, b0)
            s_S.append(st_S)
            s_conv.append(jnp.concatenate([state_delta_conv[j][:, 1:].astype(F32), qkv[:, None, :]], axis=1))
            s_C.append(st_C)
            s_n.append(st_n)
            s_m.append(st_m)
        else:
            lam_init = 0.8 - 0.6 * math.exp(-0.3 * layer)
            lam = (jnp.exp(jnp.sum(lam_q1[j].astype(F32) * lam_k1[j].astype(F32)))
                   - jnp.exp(jnp.sum(lam_q2[j].astype(F32) * lam_k2[j].astype(F32))) + lam_init)
            w_qkv = tuple(w_qkv_c[j][:, i * C_W:(i + 1) * C_W].astype(BF16) for i in range(3))
            w_o = w_o_c[j].astype(BF16)
            q, k, v = mm_multi(xp, w_qkv, tm=512)
            o = diff_attn_prompt_core(q, k, v, bp, rel_bias, lam, lam_init, subln_g_c[j])
            xp = proj_ln([o], [w_o], xp, g0, b0)
            p_k.append(k.reshape(bp, t, H_C, 2 * DH_C))
            p_v.append(v.reshape(bp, t, H_C, VD_C))
            q, k, v = mm_multi(xs, w_qkv)
            o = diff_attn_sample_core(q, k, v, cache_diff_k, cache_diff_v, page_table, j, rel_bias, lam, lam_init,
                                      subln_g_c[j])
            xs = proj_ln([o], [w_o], xs, g0, b0)
            s_k.append(k.reshape(bs, 1, H_C, 2 * DH_C))
            s_v.append(v.reshape(bs, 1, H_C, VD_C))
        w_q, w_o = w_xq[layer].astype(BF16), w_xo[layer].astype(BF16)
        w_r = jnp.pad(w_router[layer].astype(F32), ((0, 0), (0, LANES - N_EXPERTS)))
        b_r = jnp.pad(b_router[layer].astype(F32), (0, LANES - N_EXPERTS)).reshape(1, LANES)
        mk, mv = mm_multi(mem2d, (w_xkv[layer][:, :X_W].astype(BF16), w_xkv[layer][:, X_W:].astype(BF16)))
        p_mk.append(mk.reshape(bp, N_MEM, H_X, DH_X))
        p_mv.append(mv.reshape(bp, N_MEM, H_X, DH_X))
        xp, xp_tiles, idx_p, gate_p = xattn_prompt(xp, mk.reshape(bp, N_MEM, X_W), mv.reshape(bp, N_MEM, X_W),
                                                   w_q, w_o, g1, b1, w_r, b_r)
        (q,) = mm_multi(xs, (w_q,))
        o = xattn_sample_core(q, cmk, cmv, off=layer * bs)
        xs, xs_tiles, idx_s, gate_s = proj_ln_route(o, w_o, xs, g1, b1, w_r, b_r)
        xp = moe_ln(xp, xp_tiles, idx_p, gate_p, wm_in, bm_in, wm_out, bm_out, g2, b2, MOE_BLOCK_PROMPT,
                    e_off=layer * N_EXPERTS)
        xs = moe_ln(xs, xs_tiles, idx_s, gate_s, wm_in, bm_in, wm_out, bm_out, g2, b2, MOE_BLOCK_SAMPLE,
                    e_off=layer * N_EXPERTS)

    return (xp.reshape(bp, t, d), xs.reshape(bs, 1, d),
            jnp.stack(p_S), jnp.stack(p_conv), jnp.stack(p_C), jnp.stack(p_n), jnp.stack(p_m),
            jnp.stack(p_k, axis=1), jnp.stack(p_v, axis=1), jnp.stack(p_mk), jnp.stack(p_mv),
            jnp.stack(s_S), jnp.stack(s_conv), jnp.stack(s_C), jnp.stack(s_n), jnp.stack(s_m),
            jnp.stack(s_k, axis=1), jnp.stack(s_v, axis=1))
```

```python
import functools
import math

import jax
import jax.numpy as jnp
from jax import lax
from jax.experimental import pallas as pl
from jax.experimental.pallas import tpu as pltpu

F32 = jnp.float32
BF16 = jnp.bfloat16
HI = lax.Precision.HIGHEST

D_MODEL = 1024
DEPTH = 2
H_A, DK_A, DV_A, CONV_W, CHUNK = 8, 64, 64, 4, 64
H_B, DQK_B, DV_B = 4, 64, 128
QKV_A = H_A * (2 * DK_A + DV_A)
H_C, DH_C = 8, 64
VD_C = 2 * DH_C
C_W = H_C * 2 * DH_C
N_BUCKETS, MAX_DISTANCE = 32, 128
N_MEM, H_X, DH_X = 256, 4, 128
X_W = H_X * DH_X
N_EXPERTS, TOP_K = 32, 4
D_FF = D_MODEL
SWIGLU_ALPHA, SWIGLU_LIMIT = 1.702, 7.0
DN_ALPHA = (2 * DEPTH) ** 0.25
LN_EPS = 1e-5
NORM_EPS = 1e-6

LANES = 128
SUBLANES = 8
VMEM_PHYSICAL = 64 * 1024 * 1024
VMEM_LIMIT = 48 * 1024 * 1024


def _cparams(*sem):
    return pltpu.CompilerParams(dimension_semantics=tuple(sem), vmem_limit_bytes=VMEM_LIMIT)


def _layer_norm(y, g, b):
    mu = jnp.mean(y, axis=-1, keepdims=True)
    d = y - mu
    var = jnp.mean(d * d, axis=-1, keepdims=True)
    return d * lax.rsqrt(var + LN_EPS) * g + b


def _mm_multi_kernel(x_ref, *refs, n_lo, n_hi):
    n = n_lo + n_hi
    ws, outs = refs[:n], refs[n:]
    x = x_ref[...]
    xb = x.astype(BF16)
    for i in range(n_lo):
        outs[i][...] = jnp.dot(xb, ws[i][...], preferred_element_type=F32)
    for i in range(n_lo, n):
        outs[i][...] = jnp.dot(x, ws[i][...], preferred_element_type=F32, precision=HI)


def mm_multi(x, w_lo, w_hi=(), tm=256):
    m, k = x.shape
    tm = min(tm, m)
    assert m % tm == 0
    ws = tuple(w_lo) + tuple(w_hi)
    in_specs = [pl.BlockSpec((tm, k), lambda i: (i, 0))]
    in_specs += [pl.BlockSpec(w.shape, lambda i: (0, 0)) for w in ws]
    out_specs = [pl.BlockSpec((tm, w.shape[1]), lambda i: (i, 0)) for w in ws]
    out_shape = [jax.ShapeDtypeStruct((m, w.shape[1]), F32) for w in ws]
    return pl.pallas_call(
        functools.partial(_mm_multi_kernel, n_lo=len(w_lo), n_hi=len(w_hi)),
        grid=(m // tm,), in_specs=in_specs, out_specs=out_specs, out_shape=out_shape,
        compiler_params=_cparams("parallel"), name="mm_multi")(x, *ws)


def _proj_ln_kernel(*refs, n):
    a_refs, w_refs = refs[:n], refs[n:2 * n]
    x_ref, g_ref, b_ref, o_ref = refs[2 * n:]
    h = jnp.dot(a_refs[0][...].astype(BF16), w_refs[0][...], preferred_element_type=F32)
    for a_ref, w_ref in zip(a_refs[1:], w_refs[1:]):
        h = h + jnp.dot(a_ref[...].astype(BF16), w_ref[...], preferred_element_type=F32)
    o_ref[...] = _layer_norm(DN_ALPHA * x_ref[...] + h, g_ref[...], b_ref[...])


def proj_ln(a_list, w_list, x_res, g, b, tm=512):
    m, d = x_res.shape
    tm = min(tm, m)
    assert m % tm == 0
    n = len(a_list)
    return pl.pallas_call(
        functools.partial(_proj_ln_kernel, n=n), grid=(m // tm,),
        in_specs=[pl.BlockSpec((tm, a.shape[1]), lambda i: (i, 0)) for a in a_list]
        + [pl.BlockSpec(w.shape, lambda i: (0, 0)) for w in w_list]
        + [pl.BlockSpec((tm, d), lambda i: (i, 0)), pl.BlockSpec((1, d), lambda i: (0, 0)),
           pl.BlockSpec((1, d), lambda i: (0, 0))],
        out_specs=pl.BlockSpec((tm, d), lambda i: (i, 0)),
        out_shape=jax.ShapeDtypeStruct((m, d), F32),
        compiler_params=_cparams("parallel"), name="proj_ln")(*a_list, *w_list, x_res, g, b)


def _mem_attention(q, mk, mv):
    outs = []
    for h in range(H_X):
        sl = slice(h * DH_X, (h + 1) * DH_X)
        s = lax.dot_general(q[:, sl].astype(BF16), mk[:, sl].astype(BF16), (((1,), (1,)), ((), ())),
                            preferred_element_type=F32) * (DH_X ** -0.5)
        s = s - jnp.max(s, axis=-1, keepdims=True)
        p = jnp.exp(s)
        p = p / jnp.sum(p, axis=-1, keepdims=True)
        outs.append(jnp.dot(p.astype(BF16), mv[:, sl].astype(BF16), preferred_element_type=F32))
    return jnp.concatenate(outs, axis=-1)


def _route(x, wr, br):
    logits = jnp.dot(x, wr, preferred_element_type=F32, precision=HI) + br
    lane = lax.broadcasted_iota(jnp.int32, logits.shape, 1)
    work = jnp.where(lane < N_EXPERTS, logits, -jnp.inf)
    idx_out = jnp.zeros(logits.shape, jnp.int32)
    val_out = jnp.full(logits.shape, -jnp.inf, F32)
    for k in range(TOP_K):
        m = jnp.max(work, axis=-1, keepdims=True)
        sel = jnp.min(jnp.where(work == m, lane, LANES), axis=-1, keepdims=True)
        idx_out = jnp.where(lane == k, sel, idx_out)
        val_out = jnp.where(lane == k, m, val_out)
        work = jnp.where(lane == sel, -jnp.inf, work)
    e = jnp.exp(val_out - jnp.max(val_out, axis=-1, keepdims=True))
    gates = e / jnp.sum(e, axis=-1, keepdims=True)
    return idx_out, gates


def _store_token_tiles(o3_ref, y):
    for s in range(y.shape[1] // LANES):
        o3_ref[:, s, :] = y[:, s * LANES:(s + 1) * LANES]


def _xattn_prompt_kernel(x_ref, mk_ref, mv_ref, wq_ref, wo_ref, g_ref, b_ref, wr_ref, br_ref,
                         o_ref, o3_ref, idx_ref, gate_ref):
    x = x_ref[...]
    q = jnp.dot(x.astype(BF16), wq_ref[...], preferred_element_type=F32)
    o = _mem_attention(q, mk_ref[0], mv_ref[0])
    h = jnp.dot(o.astype(BF16), wo_ref[...], preferred_element_type=F32)
    y = _layer_norm(DN_ALPHA * x + h, g_ref[...], b_ref[...])
    o_ref[...] = y
    _store_token_tiles(o3_ref, y)
    idx, gates = _route(y, wr_ref[...], br_ref[...])
    idx_ref[...] = idx
    gate_ref[...] = gates


def xattn_prompt(x, mk, mv, wq, wo, g, b, wr, br, tq=512):
    m, d = x.shape
    bsz = mk.shape[0]
    t = m // bsz
    nq = t // tq
    full = lambda shape: pl.BlockSpec(shape, lambda bi, qi: (0,) * len(shape))
    row = lambda w: pl.BlockSpec((tq, w), lambda bi, qi: (bi * nq + qi, 0))
    return pl.pallas_call(
        _xattn_prompt_kernel, grid=(bsz, nq),
        in_specs=[row(d), pl.BlockSpec((1, N_MEM, X_W), lambda bi, qi: (bi, 0, 0)),
                  pl.BlockSpec((1, N_MEM, X_W), lambda bi, qi: (bi, 0, 0)),
                  full(wq.shape), full(wo.shape), full(g.shape), full(b.shape), full(wr.shape), full(br.shape)],
        out_specs=[row(d), pl.BlockSpec((tq, d // LANES, LANES), lambda bi, qi: (bi * nq + qi, 0, 0)),
                   row(LANES), row(LANES)],
        out_shape=[jax.ShapeDtypeStruct((m, d), F32), jax.ShapeDtypeStruct((m, d // LANES, LANES), F32),
                   jax.ShapeDtypeStruct((m, LANES), jnp.int32), jax.ShapeDtypeStruct((m, LANES), F32)],
        compiler_params=_cparams("parallel", "parallel"), name="xattn_prompt")(x, mk, mv, wq, wo, g, b, wr, br)


def _xattn_sample_kernel(q_ref, mk_ref, mv_ref, o_ref):
    q = jnp.broadcast_to(q_ref[0], (8, X_W))
    mk = jnp.concatenate([mk_ref[0, pl.ds(h, N_MEM, stride=H_X), :] for h in range(H_X)], axis=-1)
    mv = jnp.concatenate([mv_ref[0, pl.ds(h, N_MEM, stride=H_X), :] for h in range(H_X)], axis=-1)
    o = _mem_attention(q, mk, mv)
    o_ref[0] = o[0:1]


def xattn_sample_core(q, mk, mv, off=0):
    bsz = q.shape[0]
    spec3 = pl.BlockSpec((1, N_MEM * H_X, DH_X), lambda bi: (off + bi, 0, 0))
    out = pl.pallas_call(
        _xattn_sample_kernel, grid=(bsz,),
        in_specs=[pl.BlockSpec((1, 1, X_W), lambda bi: (bi, 0, 0)), spec3, spec3],
        out_specs=pl.BlockSpec((1, 1, X_W), lambda bi: (bi, 0, 0)),
        out_shape=jax.ShapeDtypeStruct((bsz, 1, X_W), F32),
        compiler_params=_cparams("parallel"), name="xattn_sample")(q.reshape(bsz, 1, X_W), mk, mv)
    return out.reshape(bsz, X_W)


def _ln_route_kernel(a_ref, w_ref, x_ref, g_ref, b_ref, wr_ref, br_ref, o_ref, o3_ref, idx_ref, gate_ref):
    h = jnp.dot(a_ref[...].astype(BF16), w_ref[...], preferred_element_type=F32)
    y = _layer_norm(DN_ALPHA * x_ref[...] + h, g_ref[...], b_ref[...])
    o_ref[...] = y
    _store_token_tiles(o3_ref, y)
    idx, gates = _route(y, wr_ref[...], br_ref[...])
    idx_ref[...] = idx
    gate_ref[...] = gates


def proj_ln_route(a, w, x_res, g, b, wr, br):
    m, d = x_res.shape
    return pl.pallas_call(
        _ln_route_kernel,
        out_shape=[jax.ShapeDtypeStruct((m, d), F32), jax.ShapeDtypeStruct((m, d // LANES, LANES), F32),
                   jax.ShapeDtypeStruct((m, LANES), jnp.int32), jax.ShapeDtypeStruct((m, LANES), F32)],
        compiler_params=pltpu.CompilerParams(vmem_limit_bytes=VMEM_LIMIT),
        name="proj_ln_route")(a, w, x_res, g, b, wr, br)


def _expert_onehots(idx):
    lane = lax.broadcasted_iota(jnp.int32, idx.shape, 1)
    onehots = [(idx[:, k:k + 1] == lane).astype(F32) for k in range(TOP_K)]
    tot = onehots[0]
    for k in range(1, TOP_K):
        tot = tot + onehots[k]
    return lane, onehots, tot


def _count_kernel(idx_ref, cnt_ref):
    @pl.when(pl.program_id(0) == 0)
    def _():
        cnt_ref[...] = jnp.zeros_like(cnt_ref)

    _, _, tot = _expert_onehots(idx_ref[...])
    cnt_ref[...] = cnt_ref[...] + jnp.sum(tot, axis=0, keepdims=True)


def _dest_kernel(idx_ref, base0_ref, dest_ref, base_ref):
    @pl.when(pl.program_id(0) == 0)
    def _():
        base_ref[...] = base0_ref[...]

    idx = idx_ref[...]
    tr = idx.shape[0]
    lane, onehots, tot = _expert_onehots(idx)
    r = lax.broadcasted_iota(jnp.int32, (tr, tr), 0)
    c = lax.broadcasted_iota(jnp.int32, (tr, tr), 1)
    strict = (c < r).astype(BF16)
    before = jnp.dot(strict, tot.astype(BF16), preferred_element_type=F32) + base_ref[...]
    out = jnp.zeros(idx.shape, F32)
    for k in range(TOP_K):
        out = jnp.where(lane == k, jnp.sum(onehots[k] * before, axis=-1, keepdims=True), out)
    dest_ref[...] = out.astype(jnp.int32)
    base_ref[...] = base_ref[...] + jnp.sum(tot, axis=0, keepdims=True)


def route_slots(idx, bm, tr=256):
    t = idx.shape[0]
    tr = min(tr, t)
    assert t % tr == 0
    rows = pl.BlockSpec((tr, LANES), lambda i: (i, 0))
    one = pl.BlockSpec((1, LANES), lambda i: (0, 0))
    counts = pl.pallas_call(
        _count_kernel, grid=(t // tr,), in_specs=[rows], out_specs=one,
        out_shape=jax.ShapeDtypeStruct((1, LANES), F32),
        compiler_params=_cparams("arbitrary"), name="route_count")(idx)
    counts = counts[0].astype(jnp.int32)
    padded = (counts + bm - 1) // bm * bm
    pad_end = jnp.cumsum(padded)
    base0 = (pad_end - padded).astype(F32).reshape(1, LANES)
    dest = pl.pallas_call(
        _dest_kernel, grid=(t // tr,), in_specs=[rows, one], out_specs=rows,
        out_shape=jax.ShapeDtypeStruct((t, LANES), jnp.int32),
        scratch_shapes=[pltpu.VMEM((1, LANES), F32)],
        compiler_params=_cparams("arbitrary"), name="route_dest")(idx, base0)
    return dest, pad_end[:N_EXPERTS]


def _clamped_swiglu(h):
    glu = jnp.minimum(h[:, :D_FF], SWIGLU_LIMIT)
    lin = jnp.clip(h[:, D_FF:], -SWIGLU_LIMIT, SWIGLU_LIMIT)
    return glu * jax.nn.sigmoid(SWIGLU_ALPHA * glu) * (lin + 1.0)


def _tile_rows(buf):
    rows = buf.shape[0] * buf.shape[2]
    return jnp.concatenate([buf[:, j].reshape(rows, LANES) for j in range(buf.shape[1])], axis=-1)


def _moe_gmm_kernel(bexp_ref, nused_ref, tok_ref, tok_next_ref, x_hbm, win_ref, bin_ref, wout_ref,
                    bout_ref, y_hbm, xbuf, ybuf, winb, woutb, gsem, osem, *, bm):
    i = pl.program_id(0)
    n = pl.num_programs(0)
    nused = nused_ref[0]
    slot = lax.rem(i, 2)
    n_sub = x_hbm.shape[1]
    groups = bm // SUBLANES

    def start_gather(idx_ref, s):
        def body(g, carry):
            for u in range(SUBLANES):
                pltpu.make_async_copy(x_hbm.at[idx_ref[0, 0, g * SUBLANES + u]], xbuf.at[s, g, :, u],
                                      gsem.at[s]).start(priority=u % 2)
            return carry
        lax.fori_loop(0, groups, body, 0)

    def wait_gather(s):
        pltpu.make_async_copy(xbuf.at[s], xbuf.at[s], gsem.at[s]).wait()

    def out_copies(blk, s):
        return [pltpu.make_async_copy(ybuf.at[s, :, j], y_hbm.at[pl.ds(blk * groups, groups), :, j], osem.at[s])
                for j in range(n_sub)]

    @pl.when(i == 0)
    def _():
        start_gather(tok_ref, 0)

    @pl.when(jnp.logical_and(i + 1 < n, i + 1 < nused))
    def _():
        start_gather(tok_next_ref, 1 - slot)

    active = i < nused

    @pl.when(jnp.logical_and(active, jnp.logical_or(i == 0, bexp_ref[i] != bexp_ref[jnp.maximum(i - 1, 0)])))
    def _():
        winb[...] = win_ref[0].astype(BF16)
        woutb[...] = wout_ref[0].astype(BF16)

    @pl.when(i >= 2)
    def _():
        for cp in out_copies(i - 2, slot):
            cp.wait()

    @pl.when(active)
    def _():
        wait_gather(slot)
        h = jnp.dot(_tile_rows(xbuf[slot]).astype(BF16), winb[...], preferred_element_type=F32) + bin_ref[0]
        act = _clamped_swiglu(h)
        y = jnp.dot(act.astype(BF16), woutb[...], preferred_element_type=F32) + bout_ref[0]
        for j in range(n_sub):
            ybuf[slot, :, j] = y[:, j * LANES:(j + 1) * LANES].reshape(groups, SUBLANES, LANES)

    @pl.when(jnp.logical_not(active))
    def _():
        ybuf[slot] = jnp.zeros(ybuf.shape[1:], F32)

    for cp in out_copies(i, slot):
        cp.start()

    @pl.when(i == n - 1)
    def _():
        @pl.when(n >= 2)
        def _():
            for cp in out_copies(i - 1, 1 - slot):
                cp.wait()

        for cp in out_copies(i, slot):
            cp.wait()


def moe_gmm(x, row_tok, block_expert, n_used, w_in, b_in, w_out, b_out, bm):
    n_blocks = row_tok.shape[0]
    n_sub = x.shape[1]
    d = n_sub * LANES
    idx_spec = lambda off: pl.BlockSpec(
        (1, 1, bm), lambda i, be, nu: (jnp.minimum(i + off, n_blocks - 1), 0, 0), memory_space=pltpu.SMEM)
    ex = lambda i, be, nu: (be[i], 0, 0)
    grid_spec = pltpu.PrefetchScalarGridSpec(
        num_scalar_prefetch=2, grid=(n_blocks,),
        in_specs=[idx_spec(0), idx_spec(1),
                  pl.BlockSpec(memory_space=pl.ANY),
                  pl.BlockSpec((1, d, 2 * D_FF), ex), pl.BlockSpec((1, 1, 2 * D_FF), ex),
                  pl.BlockSpec((1, D_FF, d), ex), pl.BlockSpec((1, 1, d), ex)],
        out_specs=pl.BlockSpec(memory_space=pl.ANY),
        scratch_shapes=[pltpu.VMEM((2, bm // SUBLANES, n_sub, SUBLANES, LANES), F32),
                        pltpu.VMEM((2, bm // SUBLANES, n_sub, SUBLANES, LANES), F32),
                        pltpu.VMEM((d, 2 * D_FF), BF16), pltpu.VMEM((D_FF, d), BF16),
                        pltpu.SemaphoreType.DMA((2,)), pltpu.SemaphoreType.DMA((2,))])
    w_elems = d * 2 * D_FF + D_FF * d
    vmem_bytes = (2 * 4 + 2) * w_elems + 4 * bm * d * 4 + bm * (2 * D_FF + D_FF + 2 * d) * 4
    vmem_bytes = min(vmem_bytes + (4 << 20), VMEM_PHYSICAL - (6 << 20))
    return pl.pallas_call(
        functools.partial(_moe_gmm_kernel, bm=bm), grid_spec=grid_spec,
        out_shape=jax.ShapeDtypeStruct((n_blocks * bm // SUBLANES, SUBLANES, n_sub, LANES), F32),
        compiler_params=pltpu.CompilerParams(dimension_semantics=("arbitrary",), vmem_limit_bytes=vmem_bytes),
        name="moe_gmm")(
            block_expert, n_used, row_tok, row_tok, x, w_in, b_in.reshape(b_in.shape[0], 1, -1),
            w_out, b_out.reshape(b_out.shape[0], 1, -1))


def _combine_ln_kernel(dst_ref, dst_next_ref, y_hbm, gate_ref, x_ref, g_ref, b_ref, o_ref, gbuf, sem):
    i = pl.program_id(0)
    n = pl.num_programs(0)
    slot = lax.rem(i, 2)
    groups = gbuf.shape[2]

    def start_gather(idx_ref, s):
        def body(g, carry):
            for u in range(SUBLANES):
                for k in range(TOP_K):
                    d = idx_ref[0, 0, (g * SUBLANES + u) * TOP_K + k]
                    pltpu.make_async_copy(y_hbm.at[d], gbuf.at[s, k, g, :, u], sem.at[s]).start(priority=k % 2)
            return carry
        lax.fori_loop(0, groups, body, 0)

    @pl.when(i == 0)
    def _():
        start_gather(dst_ref, 0)

    @pl.when(i + 1 < n)
    def _():
        start_gather(dst_next_ref, 1 - slot)

    pltpu.make_async_copy(gbuf.at[slot], gbuf.at[slot], sem.at[slot]).wait()
    gates = gate_ref[...]
    acc = gates[:, 0:1] * _tile_rows(gbuf[slot, 0])
    for k in range(1, TOP_K):
        acc = acc + gates[:, k:k + 1] * _tile_rows(gbuf[slot, k])
    o_ref[...] = _layer_norm(DN_ALPHA * x_ref[...] + acc, g_ref[...], b_ref[...])


def combine_ln(y, dest, gates, x_res, g, b, tc=256):
    t, d = x_res.shape
    tc = min(tc, t)
    assert t % tc == 0
    n = t // tc
    n_sub = d // LANES
    dest_flat = dest[:, :TOP_K].reshape(n, 1, tc * TOP_K)
    idx_spec = lambda off: pl.BlockSpec((1, 1, tc * TOP_K), lambda i: (jnp.minimum(i + off, n - 1), 0, 0),
                                        memory_space=pltpu.SMEM)
    return pl.pallas_call(
        _combine_ln_kernel, grid=(n,),
        in_specs=[idx_spec(0), idx_spec(1), pl.BlockSpec(memory_space=pl.ANY),
                  pl.BlockSpec((tc, LANES), lambda i: (i, 0)),
                  pl.BlockSpec((tc, d), lambda i: (i, 0)), pl.BlockSpec((1, d), lambda i: (0, 0)),
                  pl.BlockSpec((1, d), lambda i: (0, 0))],
        out_specs=pl.BlockSpec((tc, d), lambda i: (i, 0)),
        out_shape=jax.ShapeDtypeStruct((t, d), F32),
        scratch_shapes=[pltpu.VMEM((2, TOP_K, tc // SUBLANES, n_sub, SUBLANES, LANES), F32),
                        pltpu.SemaphoreType.DMA((2,))],
        compiler_params=_cparams("arbitrary"), name="combine_ln")(
            dest_flat, dest_flat, y.reshape(-1, n_sub, LANES), gates, x_res, g, b)


def moe_ln(x, x_tiles, idx, gates, w_in, b_in, w_out, b_out, g, b, bm, e_off=0):
    t, d = x.shape
    tk = t * TOP_K
    dest, pad_end = route_slots(idx, bm)
    n_blocks = -(-tk // bm) + N_EXPERTS
    n_rows = n_blocks * bm
    tok = jnp.arange(tk, dtype=jnp.int32) // TOP_K
    row_tok = jnp.zeros((n_rows,), jnp.int32).at[dest[:, :TOP_K].reshape(tk)].set(
        tok, unique_indices=True, mode='promise_in_bounds').reshape(n_blocks, 1, bm)
    block_start = jnp.arange(n_blocks, dtype=jnp.int32) * bm
    block_expert = jnp.sum((pad_end[None, :] <= block_start[:, None]).astype(jnp.int32), axis=1)
    block_expert = jnp.minimum(block_expert, N_EXPERTS - 1) + e_off
    n_used = (pad_end[-1] // bm).astype(jnp.int32).reshape(1)
    y = moe_gmm(x_tiles, row_tok, block_expert, n_used, w_in, b_in, w_out, b_out, bm)
    return combine_ln(y, dest, gates, x, g, b)


def _t5_causal_bucket(dist):
    n = jnp.maximum(dist, 0)
    max_exact = N_BUCKETS // 2
    nf = jnp.maximum(n, max_exact).astype(F32)
    large = max_exact + (jnp.log(nf / max_exact) / math.log(MAX_DISTANCE / max_exact)
                         * (N_BUCKETS - max_exact)).astype(jnp.int32)
    large = jnp.minimum(large, N_BUCKETS - 1)
    return jnp.where(n < max_exact, n, large)


def _split_maps(q):
    lane = lax.broadcasted_iota(jnp.int32, q.shape, 1)
    qs = q * (DH_C ** -0.5)
    return (jnp.where(lane < DH_C, qs, 0.0).astype(BF16), jnp.where(lane >= DH_C, qs, 0.0).astype(BF16))


def _diff_finish(acc1, l1, acc2, l2, lam, out_scale, g):
    o = acc1 / l1 - lam * (acc2 / l2)
    o = o * lax.rsqrt(jnp.mean(o * o, axis=-1, keepdims=True) + NORM_EPS) * g
    return o * out_scale


def _diff_prompt_kernel(sc_ref, far_ref, q_ref, k_ref, v_ref, bias_ref, g_ref, o_ref, m_ref, a_ref, *, tq):
    h = pl.program_id(1)
    qi = pl.program_id(2)
    qm = jnp.concatenate(_split_maps(q_ref[...]), axis=0)
    m_ref[...] = jnp.full_like(m_ref, -jnp.inf)
    a_ref[...] = jnp.zeros_like(a_ref)
    ones_col = (lax.broadcasted_iota(jnp.int32, (tq, LANES), 1) == 0).astype(BF16)
    reps = tq // LANES

    def update(tiles):
        scores, values = [], []
        for kidx, bias, shift, causal in tiles:
            start = pl.multiple_of(kidx * tq, tq)
            kt = k_ref[0, pl.ds(start, tq), :].astype(BF16)
            values.append(jnp.concatenate([v_ref[0, pl.ds(start, tq), :].astype(BF16), ones_col], axis=-1))
            s = lax.dot_general(qm, kt, (((1,), (1,)), ((), ())), preferred_element_type=F32)
            if bias is not None:
                s = s + jnp.concatenate([bias, bias], axis=0)
            if causal:
                r = lax.broadcasted_iota(jnp.int32, s.shape, 0)
                c = lax.broadcasted_iota(jnp.int32, s.shape, 1)
                s = jnp.where(c <= lax.bitwise_and(r, tq - 1), s, -jnp.inf)
            scores.append((s, shift))
        m_old = m_ref[...]
        m_new = m_old
        for s, shift in scores:
            m_new = jnp.maximum(m_new, jnp.max(s, axis=-1, keepdims=True) + shift)
        alpha = jnp.exp(m_old - m_new)
        acc = jnp.concatenate([alpha, alpha], axis=-1) * a_ref[...]
        for (s, shift), vt in zip(scores, values):
            p = jnp.exp(s - jnp.concatenate([m_new - shift] * reps, axis=-1))
            acc = acc + jnp.dot(p.astype(BF16), vt, preferred_element_type=F32)
        a_ref[...] = acc
        m_ref[...] = m_new

    far_bias = far_ref[h]
    n_far = jnp.maximum(qi - 1, 0)

    def far_body(j, carry):
        update([(2 * j, None, far_bias, False), (2 * j + 1, None, far_bias, False)])
        return carry

    lax.fori_loop(0, n_far // 2, far_body, 0)

    @pl.when(lax.rem(n_far, 2) == 1)
    def _():
        update([(n_far - 1, None, far_bias, False)])

    @pl.when(qi >= 1)
    def _():
        update([(qi - 1, bias_ref[0, 1], 0.0, False), (qi, bias_ref[0, 0], 0.0, True)])

    @pl.when(qi == 0)
    def _():
        update([(0, bias_ref[0, 0], 0.0, True)])

    acc = a_ref[...]
    o_ref[...] = _diff_finish(acc[:tq, :VD_C], acc[:tq, VD_C:VD_C + 1], acc[tq:, :VD_C], acc[tq:, VD_C:VD_C + 1],
                              sc_ref[0], sc_ref[1], g_ref[...])


def _bias_tiles_kernel(rb_ref, bucket_ref, o_ref):
    h = pl.program_id(0)
    b = bucket_ref[...]
    out = jnp.zeros(b.shape, F32)
    for k in range(N_BUCKETS):
        out = jnp.where(b == k, rb_ref[k * H_C + h], out)
    o_ref[0] = out


def diff_attn_prompt_core(q, k, v, bsz, rel_bias, lam, lam_init, subln_g, tq=256):
    m = q.shape[0]
    t = m // bsz
    nq = t // tq
    ii = jnp.arange(tq)[:, None]
    jj = jnp.arange(tq)[None, :]
    buckets = _t5_causal_bucket(jnp.stack([jnp.maximum(ii - jj, 0), tq + ii - jj])).astype(jnp.int32)
    bias = pl.pallas_call(
        _bias_tiles_kernel, grid=(H_C,),
        in_specs=[pl.BlockSpec(memory_space=pltpu.SMEM), pl.BlockSpec((2, tq, tq), lambda h: (0, 0, 0))],
        out_specs=pl.BlockSpec((1, 2, tq, tq), lambda h: (h, 0, 0, 0)),
        out_shape=jax.ShapeDtypeStruct((H_C, 2, tq, tq), F32),
        compiler_params=_cparams("parallel"), name="t5_bias_tiles")(rel_bias.astype(F32).reshape(-1), buckets)
    far = rel_bias[_t5_causal_bucket(jnp.array(2 * tq))].astype(F32)
    scal = jnp.stack([lam, 1.0 - lam_init]).astype(F32)
    k3 = k.reshape(bsz, t, C_W)
    v3 = v.reshape(bsz, t, C_W)
    smem = pl.BlockSpec(memory_space=pltpu.SMEM)
    kv_spec = pl.BlockSpec((1, t, VD_C), lambda b, h, i: (b, 0, h))
    row = pl.BlockSpec((tq, VD_C), lambda b, h, i: (b * nq + i, h))
    stat = pltpu.VMEM((2 * tq, LANES), F32)
    acc = pltpu.VMEM((2 * tq, VD_C + LANES), F32)
    return pl.pallas_call(
        functools.partial(_diff_prompt_kernel, tq=tq), grid=(bsz, H_C, nq),
        in_specs=[smem, smem, row, kv_spec, kv_spec,
                  pl.BlockSpec((1, 2, tq, tq), lambda b, h, i: (h, 0, 0, 0)),
                  pl.BlockSpec((1, VD_C), lambda b, h, i: (0, 0))],
        out_specs=row, out_shape=jax.ShapeDtypeStruct((m, C_W), F32),
        scratch_shapes=[stat, acc],
        compiler_params=_cparams("parallel", "parallel", "parallel"), name="diff_attn_prompt")(
            scal, far, q, k3, v3, bias, subln_g.reshape(1, VD_C))


def _diff_sample_kernel(pt_ref, sc_ref, q_ref, kn_ref, vn_ref, bm_ref, bnew_ref, g_ref, *refs, n_pp):
    k_refs, v_refs = refs[:n_pp], refs[n_pp:2 * n_pp]
    o_ref, m_ref, l_ref, a_ref = refs[2 * n_pp:]
    p = pl.program_id(1)
    n_p = pl.num_programs(1)
    q8 = q_ref[0]
    qm = jnp.concatenate(_split_maps(q8), axis=0)

    @pl.when(p == 0)
    def _():
        m_ref[...] = jnp.full_like(m_ref, -jnp.inf)
        l_ref[...] = jnp.zeros_like(l_ref)
        a_ref[...] = jnp.zeros_like(a_ref)

    scores = []
    for j in range(n_pp):
        bias = bm_ref[jnp.where(p == n_p - 1, 1, 0)] if j == n_pp - 1 else bm_ref[0]
        k2 = k_refs[j][0, 0].reshape(-1, 2 * DH_C).astype(BF16)
        scores.append(lax.dot_general(qm, k2, (((1,), (1,)), ((), ())), preferred_element_type=F32) + bias)
    m_old = m_ref[...]
    m_new = m_old
    for s in scores:
        m_new = jnp.maximum(m_new, jnp.max(s, axis=-1, keepdims=True))
    alpha = jnp.exp(m_old - m_new)
    l_new = alpha * l_ref[...]
    a_new = alpha * a_ref[...]
    for j, s in enumerate(scores):
        pr = jnp.exp(s - m_new)
        l_new = l_new + jnp.sum(pr, axis=-1, keepdims=True)
        v2 = v_refs[j][0, 0].reshape(-1, VD_C).astype(BF16)
        a_new = a_new + jnp.dot(pr.astype(BF16), v2, preferred_element_type=F32)
    l_ref[...] = l_new
    a_ref[...] = a_new
    m_ref[...] = m_new

    @pl.when(p == n_p - 1)
    def _():
        kn = jnp.concatenate([kn_ref[0], kn_ref[0]], axis=0)
        vn = jnp.concatenate([vn_ref[0], vn_ref[0]], axis=0)
        s = jnp.sum(qm.astype(F32) * kn, axis=-1, keepdims=True) + bnew_ref[:, 0:1]
        m_old = m_ref[...]
        m_new = jnp.maximum(m_old, s)
        alpha = jnp.exp(m_old - m_new)
        pr = jnp.exp(s - m_new)
        l = alpha * l_ref[...] + pr
        a = alpha * a_ref[...] + pr * vn
        o_ref[0] = _diff_finish(a[:H_C], l[:H_C], a[H_C:], l[H_C:], sc_ref[0], sc_ref[1], g_ref[...])


def diff_attn_sample_core(q, k_new, v_new, cache_k, cache_v, page_table, layer_j, rel_bias, lam, lam_init, subln_g,
                          n_pp=8):
    bsz = q.shape[0]
    n_pages = page_table.shape[1]
    page = cache_k.shape[2]
    past = n_pages * page
    assert n_pages % n_pp == 0
    dist_last = past - ((n_pages - 1) * page + jnp.arange(page))
    b_last = rel_bias[_t5_causal_bucket(dist_last)].astype(F32)
    b_far = jnp.broadcast_to(rel_bias[_t5_causal_bucket(jnp.array(page + 1))].astype(F32), (page, H_C))
    eye = jnp.eye(H_C, dtype=bool)

    def expand(bt):
        full = jnp.where(eye[:, None, :], bt.T[:, :, None], -jnp.inf).reshape(H_C, page * H_C)
        return jnp.concatenate([full, full], axis=0)

    bm = jnp.stack([expand(b_far), expand(b_last)])
    b_new = rel_bias[_t5_causal_bucket(jnp.array(0))].astype(F32)
    b_new = jnp.broadcast_to(jnp.concatenate([b_new, b_new])[:, None], (2 * H_C, LANES))
    scal = jnp.stack([lam, 1.0 - lam_init]).astype(F32)
    smem = pl.BlockSpec(memory_space=pltpu.SMEM)
    head3 = pl.BlockSpec((1, H_C, VD_C), lambda b, p, pt: (b, 0, 0))
    full = lambda shape: pl.BlockSpec(shape, lambda b, p, pt: (0,) * len(shape))

    def page_spec(j):
        return pl.BlockSpec((1, 1, page, H_C, VD_C), lambda b, p, pt: (pt[b, p * n_pp + j], layer_j, 0, 0, 0))

    grid_spec = pltpu.PrefetchScalarGridSpec(
        num_scalar_prefetch=1, grid=(bsz, n_pages // n_pp),
        in_specs=[smem, head3, head3, head3, full(bm.shape), full(b_new.shape), full((1, VD_C))]
        + [page_spec(j) for j in range(n_pp)] * 2,
        out_specs=head3,
        scratch_shapes=[pltpu.VMEM((2 * H_C, 1), F32), pltpu.VMEM((2 * H_C, 1), F32),
                        pltpu.VMEM((2 * H_C, VD_C), F32)])
    out = pl.pallas_call(
        functools.partial(_diff_sample_kernel, n_pp=n_pp), grid_spec=grid_spec,
        out_shape=jax.ShapeDtypeStruct((bsz, H_C, VD_C), F32),
        compiler_params=_cparams("parallel", "arbitrary"), name="diff_attn_sample")(
            page_table, scal, q.reshape(bsz, H_C, VD_C), k_new.reshape(bsz, H_C, VD_C),
            v_new.reshape(bsz, H_C, VD_C), bm, b_new, subln_g.reshape(1, VD_C),
            *([cache_k] * n_pp), *([cache_v] * n_pp))
    return out.reshape(bsz, C_W)


LANE_BETA, LANE_A, LANE_I, LANE_F = 0, H_A, 2 * H_A, 2 * H_A + H_B


def _softplus(x):
    return jnp.maximum(x, 0.0) + jnp.log1p(jnp.exp(-jnp.abs(x)))


def _silu(x):
    return x * jax.nn.sigmoid(x)


def _lanes(shape, lo, n):
    lane = lax.broadcasted_iota(jnp.int32, shape, 1)
    return jnp.logical_and(lane >= lo, lane < lo + n)


def _gate_tile(gt, alog_row, prow):
    z = gt + prow
    return jax.nn.sigmoid(gt), -jnp.exp(alog_row) * _softplus(z), z, -_softplus(-z)


def _nt(a, b, precision=None):
    return lax.dot_general(a, b, (((1,), (1,)), ((), ())), preferred_element_type=F32, precision=precision)


def _tn(a, b):
    return lax.dot_general(a, b, (((0,), (0,)), ((), ())), preferred_element_type=F32)


def _row_selector(n_heads, length, lanes_of_head):
    r = lax.broadcasted_iota(jnp.int32, (n_heads * length, LANES), 0) // length
    lane = lax.broadcasted_iota(jnp.int32, (n_heads * length, LANES), 1)
    sel = jnp.zeros((n_heads * length, LANES), F32)
    for lo in lanes_of_head:
        sel = sel + (lane == r + lo).astype(F32)
    return sel


def _tri_masks(n):
    r = lax.broadcasted_iota(jnp.int32, (n, n), 0)
    c = lax.broadcasted_iota(jnp.int32, (n, n), 1)
    return r, c


def _bmm(a, b):
    return lax.dot_general(a.astype(BF16), b.astype(BF16), (((2,), (1,)), ((0,), (0,))),
                           preferred_element_type=F32)


def _bmm_nt(a, b):
    return lax.dot_general(a.astype(BF16), b.astype(BF16), (((2,), (2,)), ((0,), (0,))),
                           preferred_element_type=F32)


def _heads(x, n_heads, width, offset=0):
    return jnp.stack([x[c * CHUNK:(c + 1) * CHUNK, offset + h * width:offset + (h + 1) * width]
                      for c in range(x.shape[0] // CHUNK) for h in range(n_heads)], axis=0)


def _head_cols(x, n_heads, lane0):
    return jnp.stack([x[c * CHUNK:(c + 1) * CHUNK, lane0 + h:lane0 + h + 1]
                      for c in range(x.shape[0] // CHUNK) for h in range(n_heads)], axis=0)


def _unheads(x, n_heads):
    n = x.shape[0] // n_heads
    return jnp.concatenate([jnp.concatenate([x[c * n_heads + h] for h in range(n_heads)], axis=-1)
                            for c in range(n)], axis=0)


def _chunk_cumsum(x):
    r, c = _tri_masks(x.shape[0])
    tri = jnp.logical_and(c <= r, r // CHUNK == c // CHUNK).astype(F32)
    return jnp.dot(tri, x, preferred_element_type=F32, precision=HI)


def _chunk_rows(sel, x):
    n = x.shape[0] // CHUNK
    rows = [_nt(sel, x[c * CHUNK:(c + 1) * CHUNK], HI) for c in range(n)]
    return jnp.concatenate(rows, axis=0).reshape(n * sel.shape[0] // CHUNK, CHUNK, CHUNK)


def _unit_lower_inverse(nmat, r, c):
    mm = _bmm
    eye = (r == c).astype(F32)
    same = (r // 16) == (c // 16)
    nd = jnp.where(same, nmat, 0.0)
    off = nmat - nd
    dinv = eye - nd
    p = nd
    for _ in range(3):
        p = mm(p, p)
        dinv = dinv + mm(dinv, p)
    m = mm(dinv, off)
    m2 = mm(m, m)
    left = eye - m
    left = left + mm(left, m2)
    return mm(left, dinv)


def _gdn_prompt_kernel(qkv_ref, z_ref, gt_ref, convw_ref, alog_ref, prow_ref, ng_ref, mix_ref, s_ref, ext_ref):
    cidx = pl.program_id(1)
    L = CHUNK
    rows = qkv_ref.shape[0]
    n_ch = rows // L

    @pl.when(cidx == 0)
    def _():
        ext_ref[0:8, :] = jnp.zeros((8, QKV_A), F32)
        s_ref[...] = jnp.zeros_like(s_ref)

    ext_ref[8:8 + rows, :] = qkv_ref[...]
    acc = convw_ref[0:1, :] * ext_ref[pl.ds(8 - (CONV_W - 1), rows), :]
    for i in range(1, CONV_W):
        acc = acc + convw_ref[i:i + 1, :] * ext_ref[pl.ds(8 - (CONV_W - 1) + i, rows), :]
    ext_ref[0:8, :] = ext_ref[rows:rows + 8, :]
    cs = _silu(acc)

    beta, g, _, _ = _gate_tile(gt_ref[...], alog_ref[...], prow_ref[...])
    r, c = _tri_masks(L)
    incl = c <= r
    strict = c < r
    gcum = _chunk_cumsum(jnp.where(_lanes(g.shape, LANE_A, H_A), g, 0.0))
    gc_rows = _chunk_rows(_row_selector(H_A, L, (LANE_A,)), gcum)
    q = _heads(cs, H_A, DK_A)
    k = _heads(cs, H_A, DK_A, H_A * DK_A)
    v = _heads(cs, H_A, DV_A, 2 * H_A * DK_A)
    q = q * lax.rsqrt(jnp.sum(q * q, axis=-1, keepdims=True) + NORM_EPS) * (DK_A ** -0.5)
    k = k * lax.rsqrt(jnp.sum(k * k, axis=-1, keepdims=True) + NORM_EPS)
    beta_c = _head_cols(beta, H_A, LANE_BETA)
    gc_c = _head_cols(gcum, H_A, LANE_A)
    decay = jnp.where(incl, jnp.exp(jnp.where(incl, gc_c - gc_rows, 0.0)), 0.0)
    kb = k.astype(BF16)
    nmat = jnp.where(strict, beta_c * _bmm_nt(kb, kb) * decay, 0.0)
    egc = jnp.exp(gc_c)
    rhs = jnp.concatenate([v * beta_c, k * (beta_c * egc)], axis=-1)
    sol = _bmm(_unit_lower_inverse(nmat, r, c), rhs)
    u = sol[:, :, :DV_A]
    wq = jnp.concatenate([sol[:, :, DV_A:], q * egc], axis=1).astype(BF16)
    qk = (_bmm_nt(q, kb) * decay).astype(BF16)
    gc_last = gc_c[:, L - 1:L]
    k_tail = (k * jnp.exp(gc_last - gc_c)).astype(BF16)
    g_tail = jnp.exp(gc_last)
    state = s_ref[0]
    outs = []
    for ci in range(n_ch):
        sl = slice(ci * H_A, (ci + 1) * H_A)
        ws = _bmm(wq[sl], state)
        delta = u[sl] - ws[:, :L]
        outs.append(ws[:, L:] + _bmm(qk[sl], delta))
        delta_b = delta.astype(BF16)
        state = state * g_tail[sl] + jnp.stack([_tn(k_tail[ci * H_A + h], delta_b[h]) for h in range(H_A)], axis=0)
    s_ref[0] = state
    o = jnp.concatenate(outs, axis=0)
    o = o * lax.rsqrt(jnp.mean(o * o, axis=-1, keepdims=True) + NORM_EPS) * ng_ref[...]
    o = o * _silu(_heads(z_ref[...], H_A, DV_A))
    mix_ref[...] = _unheads(o, H_A)


AB_CHUNKS_PER_STEP = 4


def gdn_prompt(qkv, z, gates, conv_w, alog_row, prow, norm_g, bsz):
    m = qkv.shape[0]
    rows = AB_CHUNKS_PER_STEP * CHUNK
    nc = m // bsz // rows
    row = lambda w: pl.BlockSpec((rows, w), lambda b, c: (b * nc + c, 0))
    full = lambda shape: pl.BlockSpec(shape, lambda b, c: (0,) * len(shape))
    return pl.pallas_call(
        _gdn_prompt_kernel, grid=(bsz, nc),
        in_specs=[row(QKV_A), row(H_A * DV_A), row(LANES), full(conv_w.shape), full((1, LANES)), full((1, LANES)),
                  full((1, DV_A))],
        out_specs=[row(H_A * DV_A), pl.BlockSpec((1, H_A, DK_A, DV_A), lambda b, c: (b, 0, 0, 0))],
        out_shape=[jax.ShapeDtypeStruct((m, H_A * DV_A), F32), jax.ShapeDtypeStruct((bsz, H_A, DK_A, DV_A), F32)],
        scratch_shapes=[pltpu.VMEM((rows + 8, QKV_A), F32)],
        compiler_params=_cparams("parallel", "arbitrary"), name="gdn_prompt")(
            qkv, z, gates, conv_w, alog_row, prow, norm_g.reshape(1, DV_A))


def _mlstm_prompt_kernel(q_ref, k_ref, v_ref, og_ref, gt_ref, alog_ref, prow_ref, ng_ref, mix_ref, c_ref, m_ref):
    cidx = pl.program_id(1)
    L = CHUNK

    @pl.when(cidx == 0)
    def _():
        c_ref[...] = jnp.zeros_like(c_ref)
        m_ref[...] = jnp.zeros_like(m_ref)

    n_ch = q_ref.shape[0] // L
    _, _, ipre, logf = _gate_tile(gt_ref[...], alog_ref[...], prow_ref[...])
    r, c = _tri_masks(L)
    incl = c <= r
    fsel = _lanes(logf.shape, LANE_F, H_B)
    bcum = _chunk_cumsum(jnp.where(fsel, logf, 0.0))
    rowvals = jnp.where(_lanes(ipre.shape, LANE_I, H_B), ipre, 0.0) - jnp.where(fsel, bcum, 0.0)
    rows = _chunk_rows(_row_selector(H_B, L, (LANE_I, LANE_F)), rowvals)
    mrow = m_ref[0]
    lane_row = lax.broadcasted_iota(jnp.int32, mrow.shape, 1)
    qb = _heads(q_ref[...], H_B, DQK_B).astype(BF16)
    ks = _heads(k_ref[...], H_B, DQK_B) * (DQK_B ** -0.5)
    ones_col = jnp.broadcast_to((lax.broadcasted_iota(jnp.int32, (L, LANES), 1) == 0).astype(F32),
                                (n_ch * H_B, L, LANES))
    v_ext = jnp.concatenate([_heads(v_ref[...], H_B, DV_B), ones_col], axis=-1).astype(BF16)
    b_c = _head_cols(bcum, H_B, LANE_F)
    i_c = _head_cols(ipre, H_B, LANE_I)
    dmat = jnp.where(incl, b_c + rows, -jnp.inf)
    m_intra = jnp.max(dmat, axis=-1, keepdims=True)
    w_intra = jnp.exp(dmat - m_intra) * _bmm_nt(qb, ks)
    nd_intra = _bmm(w_intra, v_ext)
    b_last = b_c[:, L - 1:L]
    e_end = b_last - b_c + i_c
    e_max = jnp.max(e_end, axis=1, keepdims=True)
    kw = (ks * jnp.exp(e_end - e_max)).astype(BF16)
    kv_end = jnp.stack([_tn(kw[i], v_ext[i]) for i in range(n_ch * H_B)], axis=0)
    m_prev = jnp.stack([mrow[:, h:h + 1] for h in range(H_B)], axis=0)
    state = c_ref[0]
    outs = []
    for ci in range(n_ch):
        sl = slice(ci * H_B, (ci + 1) * H_B)
        inter = b_c[sl] + m_prev
        m_t = jnp.maximum(inter, m_intra[sl])
        nd = jnp.exp(inter - m_t) * _bmm(qb[sl], state) + jnp.exp(m_intra[sl] - m_t) * nd_intra[sl]
        outs.append(nd[:, :, :DV_B] / jnp.maximum(jnp.abs(nd[:, :, DV_B:DV_B + 1]), jnp.exp(-m_t)))
        m_new = jnp.maximum(b_last[sl] + m_prev, e_max[sl])
        state = jnp.exp(b_last[sl] + m_prev - m_new) * state + jnp.exp(e_max[sl] - m_new) * kv_end[sl]
        m_prev = m_new
    c_ref[0] = state
    for h in range(H_B):
        mrow = jnp.where(lane_row == h, m_prev[h], mrow)
    m_ref[0] = mrow
    hh = jnp.concatenate(outs, axis=0)
    hh = hh * lax.rsqrt(jnp.mean(hh * hh, axis=-1, keepdims=True) + NORM_EPS) * ng_ref[...]
    hh = jax.nn.sigmoid(_heads(og_ref[...], H_B, DV_B)) * hh
    mix_ref[...] = _unheads(hh, H_B)


def mlstm_prompt(q, k, v, og, gates, alog_row, prow, norm_g, bsz):
    m = q.shape[0]
    rows = AB_CHUNKS_PER_STEP * CHUNK
    nc = m // bsz // rows
    row = lambda w: pl.BlockSpec((rows, w), lambda b, c: (b * nc + c, 0))
    full = lambda shape: pl.BlockSpec(shape, lambda b, c: (0,) * len(shape))
    return pl.pallas_call(
        _mlstm_prompt_kernel, grid=(bsz, nc),
        in_specs=[row(H_B * DQK_B), row(H_B * DQK_B), row(H_B * DV_B), row(H_B * DV_B), row(LANES),
                  full((1, LANES)), full((1, LANES)), full((1, DV_B))],
        out_specs=[row(H_B * DV_B), pl.BlockSpec((1, H_B, DQK_B, DV_B + LANES), lambda b, c: (b, 0, 0, 0)),
                   pl.BlockSpec((1, 1, LANES), lambda b, c: (b, 0, 0))],
        out_shape=[jax.ShapeDtypeStruct((m, H_B * DV_B), F32),
                   jax.ShapeDtypeStruct((bsz, H_B, DQK_B, DV_B + LANES), F32),
                   jax.ShapeDtypeStruct((bsz, 1, LANES), F32)],
        compiler_params=_cparams("parallel", "arbitrary"), name="mlstm_prompt")(
            q, k, v, og, gates, alog_row, prow, norm_g.reshape(1, DV_B))


def _columns(x8):
    n = x8.shape[1]
    r, c = _tri_masks(n)
    return _nt((r == c).astype(F32), x8, HI)


def _ab_sample_kernel(qn_ref, kn_ref, vn_ref, cq_ref, ck_ref, cv_ref, wq_ref, wk_ref, wv_ref, z_ref,
                      qb_ref, kb_ref, vb_ref, og_ref, gt_ref, alog_ref, prow_ref, nga_ref, ngb_ref,
                      s_in, c_in, n_in, m_in,
                      oa_ref, ob_ref, s_out, c_out, n_out, m_out):
    def conv(new_ref, prev_ref, w_ref):
        acc = w_ref[CONV_W - 1] * new_ref[0]
        for i in range(CONV_W - 1):
            acc = acc + w_ref[i] * prev_ref[0, i]
        return _silu(acc)

    q8 = conv(qn_ref, cq_ref, wq_ref)
    k8 = conv(kn_ref, ck_ref, wk_ref)
    v8 = conv(vn_ref, cv_ref, wv_ref)
    q8 = q8 * lax.rsqrt(jnp.sum(q8 * q8, axis=-1, keepdims=True) + NORM_EPS) * (DK_A ** -0.5)
    k8 = k8 * lax.rsqrt(jnp.sum(k8 * k8, axis=-1, keepdims=True) + NORM_EPS)
    beta, g, ipre, logf = _gate_tile(gt_ref[0], alog_ref[...], prow_ref[...])
    q_cols, k_cols = _columns(q8), _columns(k8)
    z8 = z_ref[0]
    outs = []
    for h in range(H_A):
        s = s_in[0, h] * jnp.exp(g[:, LANE_A + h:LANE_A + h + 1])
        kc = k_cols[:, h:h + 1]
        err = v8[h:h + 1] - jnp.sum(kc * s, axis=0, keepdims=True)
        s = s + kc * (beta[:, LANE_BETA + h:LANE_BETA + h + 1] * err)
        s_out[0, h] = s
        outs.append(jnp.sum(q_cols[:, h:h + 1] * s, axis=0, keepdims=True))
    o = jnp.concatenate(outs, axis=0)
    o = o * lax.rsqrt(jnp.mean(o * o, axis=-1, keepdims=True) + NORM_EPS) * nga_ref[...]
    oa_ref[0] = o * _silu(z8)

    zeros4 = jnp.zeros((8 - H_B, DQK_B), F32)
    qb_cols = _columns(jnp.concatenate([qb_ref[0], zeros4], axis=0))
    kb_cols = _columns(jnp.concatenate([kb_ref[0] * (DQK_B ** -0.5), zeros4], axis=0))
    vb = vb_ref[0]
    n_cols = n_in[0]
    m_row = m_in[0]
    lane_n = lax.broadcasted_iota(jnp.int32, n_cols.shape, 1)
    lane_m = lax.broadcasted_iota(jnp.int32, m_row.shape, 1)
    outs = []
    for h in range(H_B):
        lf = logf[:, LANE_F + h:LANE_F + h + 1]
        it = ipre[:, LANE_I + h:LANE_I + h + 1]
        m_prev = m_row[:, h:h + 1]
        m_new = jnp.maximum(lf + m_prev, it)
        f_sc = jnp.exp(lf + m_prev - m_new)
        i_sc = jnp.exp(it - m_new)
        kc = kb_cols[:, h:h + 1]
        qc = qb_cols[:, h:h + 1]
        cm = f_sc * c_in[0, h] + i_sc * (kc * vb[h:h + 1])
        nn = f_sc * n_cols[:, h:h + 1] + i_sc * kc
        c_out[0, h] = cm
        n_cols = jnp.where(lane_n == h, nn, n_cols)
        m_row = jnp.where(lane_m == h, m_new, m_row)
        num = jnp.sum(qc * cm, axis=0, keepdims=True)
        den = jnp.sum(qc * nn, axis=0, keepdims=True)
        outs.append(num / jnp.maximum(jnp.abs(den), jnp.exp(-m_new)))
    hb = jnp.concatenate(outs, axis=0)
    hb = hb * lax.rsqrt(jnp.mean(hb * hb, axis=-1, keepdims=True) + NORM_EPS) * ngb_ref[...]
    ob_ref[0] = jax.nn.sigmoid(og_ref[0]) * hb
    n_out[0] = n_cols
    m_out[0] = m_row


def ab_sample(qkv, z, q_b, k_b, v_b, o_b, gates, conv_prev, conv_w, alog_row, prow, norm_g_a, norm_g_b,
              s_prev, c_prev, n_prev, m_prev):
    bsz = qkv.shape[0]
    hk = H_A * DK_A
    part = lambda x, i, w: x[..., i * hk:i * hk + H_A * w].reshape(x.shape[:-1] + (H_A, w))
    new_parts = [part(qkv, 0, DK_A), part(qkv, 1, DK_A), part(qkv, 2, DV_A)]
    prev_parts = [part(conv_prev, 0, DK_A), part(conv_prev, 1, DK_A), part(conv_prev, 2, DV_A)]
    w_parts = [part(conv_w, 0, DK_A), part(conv_w, 1, DK_A), part(conv_w, 2, DV_A)]
    args = new_parts + prev_parts + w_parts + [
        z.reshape(bsz, H_A, DV_A), q_b.reshape(bsz, H_B, DQK_B), k_b.reshape(bsz, H_B, DQK_B),
        v_b.reshape(bsz, H_B, DV_B), o_b.reshape(bsz, H_B, DV_B), gates.reshape(bsz, 1, LANES),
        alog_row, prow, norm_g_a.reshape(1, DV_A), norm_g_b.reshape(1, DV_B),
        s_prev, c_prev, jnp.swapaxes(n_prev, 1, 2), m_prev.reshape(bsz, 1, H_B)]

    def spec(x, batched):
        nd = x.ndim
        if batched:
            return pl.BlockSpec((1,) + x.shape[1:], lambda b: (b,) + (0,) * (nd - 1))
        return pl.BlockSpec(x.shape, lambda b: (0,) * nd)

    batched = [True] * 6 + [False] * 3 + [True] * 6 + [False] * 4 + [True] * 4
    out_shape = [jax.ShapeDtypeStruct((bsz, H_A, DV_A), F32), jax.ShapeDtypeStruct((bsz, H_B, DV_B), F32),
                 jax.ShapeDtypeStruct(s_prev.shape, F32), jax.ShapeDtypeStruct(c_prev.shape, F32),
                 jax.ShapeDtypeStruct((bsz, DQK_B, H_B), F32), jax.ShapeDtypeStruct((bsz, 1, H_B), F32)]
    oa, ob, s_new, c_new, n_new, m_new = pl.pallas_call(
        _ab_sample_kernel, grid=(bsz,),
        in_specs=[spec(x, bt) for x, bt in zip(args, batched)],
        out_specs=[spec(x, True) for x in out_shape], out_shape=out_shape,
        compiler_params=_cparams("parallel"), name="ab_sample")(*args)
    mix = jnp.concatenate([oa.reshape(bsz, H_A * DV_A), ob.reshape(bsz, H_B * DV_B)], axis=-1)
    return mix, s_new, c_new, jnp.swapaxes(n_new, 1, 2), m_new.reshape(bsz, H_B)


MOE_BLOCK_PROMPT = 256
MOE_BLOCK_SAMPLE = 32


def _ab_weights(w_in, a_log, dt_bias, b_i, b_f):
    sizes = (QKV_A, H_A * DV_A, H_A, H_A, H_B * DQK_B, H_B * DQK_B, H_B * DV_B, H_B * DV_B, H_B, H_B)
    offs = [0]
    for s in sizes:
        offs.append(offs[-1] + s)
    col = lambda i: w_in[:, offs[i]:offs[i + 1]]
    w_lo = tuple(col(i).astype(BF16) for i in (0, 1, 4, 5, 6, 7))
    w_gate = jnp.concatenate([col(2), col(3), col(8), col(9)], axis=1)
    w_gate = jnp.pad(w_gate, ((0, 0), (0, LANES - w_gate.shape[1])))
    zeros = lambda n: jnp.zeros((n,), F32)
    pad = LANES - 2 * H_A - 2 * H_B
    alog_row = jnp.concatenate([zeros(H_A), a_log.astype(F32), zeros(2 * H_B + pad)]).reshape(1, LANES)
    prow = jnp.concatenate([zeros(H_A), dt_bias.astype(F32), b_i.astype(F32), b_f.astype(F32),
                            zeros(pad)]).reshape(1, LANES)
    return w_lo, w_gate, alog_row, prow


def kernel(x_prompt, x_sample, state_delta_S, state_delta_conv, state_mlstm_C, state_mlstm_n, state_mlstm_m,
           cache_diff_k, cache_diff_v, cache_mem_k, cache_mem_v, page_table, mem_prompt,
           w_in_ab, conv_w_a, a_log_a, dt_bias_a, norm_g_a, b_i_b, b_f_b, norm_g_b, w_out_ab,
           w_qkv_c, lam_q1, lam_k1, lam_q2, lam_k2, subln_g_c, w_o_c, rel_bias,
           w_xq, w_xkv, w_xo, ln_g, ln_b, w_router, b_router, w_moe_in, b_moe_in, w_moe_out, b_moe_out):
    bp, t, d = x_prompt.shape
    bs = x_sample.shape[0]
    xp = x_prompt.reshape(bp * t, d)
    xs = x_sample.reshape(bs, d)
    mem2d = mem_prompt.reshape(bp * N_MEM, d)
    cmk = cache_mem_k.reshape(DEPTH * bs, N_MEM * H_X, DH_X)
    cmv = cache_mem_v.reshape(DEPTH * bs, N_MEM * H_X, DH_X)
    wm_in = w_moe_in.reshape(DEPTH * N_EXPERTS, d, 2 * D_FF)
    bm_in = b_moe_in.reshape(DEPTH * N_EXPERTS, 2 * D_FF)
    wm_out = w_moe_out.reshape(DEPTH * N_EXPERTS, D_FF, d)
    bm_out = b_moe_out.reshape(DEPTH * N_EXPERTS, d)
    p_S, p_conv, p_C, p_n, p_m, p_k, p_v, p_mk, p_mv = [], [], [], [], [], [], [], [], []
    s_S, s_conv, s_C, s_n, s_m, s_k, s_v = [], [], [], [], [], [], []
    for layer in range(DEPTH):
        j = layer // 2
        g0, b0 = ln_g[layer, 0].reshape(1, d), ln_b[layer, 0].reshape(1, d)
        g1, b1 = ln_g[layer, 1].reshape(1, d), ln_b[layer, 1].reshape(1, d)
        g2, b2 = ln_g[layer, 2].reshape(1, d), ln_b[layer, 2].reshape(1, d)
        if layer % 2 == 0:
            w_lo, w_gate, alog_row, prow = _ab_weights(w_in_ab[j], a_log_a[j], dt_bias_a[j], b_i_b[j], b_f_b[j])
            w_out = w_out_ab[j].astype(BF16)
            w_out_a, w_out_b = w_out[:H_A * DV_A], w_out[H_A * DV_A:]
            qkv, z, q_b, k_b, v_b, o_b, gates = mm_multi(xp, w_lo, (w_gate,), tm=256)
            mix_a, st_S = gdn_prompt(qkv, z, gates, conv_w_a[j], alog_row, prow, norm_g_a[j], bp)
            mix_b, c_ext, m_row = mlstm_prompt(q_b, k_b, v_b, o_b, gates, alog_row, prow, norm_g_b[j], bp)
            xp = proj_ln([mix_a, mix_b], [w_out_a, w_out_b], xp, g0, b0)
            p_S.append(st_S)
            p_conv.append(qkv.reshape(bp, t, QKV_A)[:, t - (CONV_W - 1):])
            p_C.append(c_ext[..., :DV_B])
            p_n.append(c_ext[..., DV_B])
            p_m.append(m_row[:, 0, :H_B])
            qkv, z, q_b, k_b, v_b, o_b, gates = mm_multi(xs, w_lo, (w_gate,))
            mix, st_S, st_C, st_n, st_m = ab_sample(
                qkv, z, q_b, k_b, v_b, o_b, gates, state_delta_conv[j], conv_w_a[j], alog_row, prow,
                norm_g_a[j], norm_g_b[j], state_delta_S[j].astype(F32), state_mlstm_C[j].astype(F32),
                state_mlstm_n[j].astype(F32), state_mlstm_m[j].astype(F32))
            xs = proj_ln([mix], [w_out], xs, g0, b0)
            s_S.append(st_S)
            s_conv.append(jnp.concatenate([state_delta_conv[j][:, 1:].astype(F32), qkv[:, None, :]], axis=1))
            s_C.append(st_C)
            s_n.append(st_n)
            s_m.append(st_m)
        else:
            lam_init = 0.8 - 0.6 * math.exp(-0.3 * layer)
            lam = (jnp.exp(jnp.sum(lam_q1[j].astype(F32) * lam_k1[j].astype(F32)))
                   - jnp.exp(jnp.sum(lam_q2[j].astype(F32) * lam_k2[j].astype(F32))) + lam_init)
            w_qkv = tuple(w_qkv_c[j][:, i * C_W:(i + 1) * C_W].astype(BF16) for i in range(3))
            w_o = w_o_c[j].astype(BF16)
            q, k, v = mm_multi(xp, w_qkv, tm=512)
            o = diff_attn_prompt_core(q, k, v, bp, rel_bias, lam, lam_init, subln_g_c[j])
            xp = proj_ln([o], [w_o], xp, g0, b0)
            p_k.append(k.reshape(bp, t, H_C, 2 * DH_C))
            p_v.append(v.reshape(bp, t, H_C, VD_C))
            q, k, v = mm_multi(xs, w_qkv)
            o = diff_attn_sample_core(q, k, v, cache_diff_k, cache_diff_v, page_table, j, rel_bias, lam, lam_init,
                                      subln_g_c[j])
            xs = proj_ln([o], [w_o], xs, g0, b0)
            s_k.append(k.reshape(bs, 1, H_C, 2 * DH_C))
            s_v.append(v.reshape(bs, 1, H_C, VD_C))
        w_q, w_o = w_xq[layer].astype(BF16), w_xo[layer].astype(BF16)
        w_r = jnp.pad(w_router[layer].astype(F32), ((0, 0), (0, LANES - N_EXPERTS)))
        b_r = jnp.pad(b_router[layer].astype(F32), (0, LANES - N_EXPERTS)).reshape(1, LANES)
        mk, mv = mm_multi(mem2d, (w_xkv[layer][:, :X_W].astype(BF16), w_xkv[layer][:, X_W:].astype(BF16)))
        p_mk.append(mk.reshape(bp, N_MEM, H_X, DH_X))
        p_mv.append(mv.reshape(bp, N_MEM, H_X, DH_X))
        xp, xp_tiles, idx_p, gate_p = xattn_prompt(xp, mk.reshape(bp, N_MEM, X_W), mv.reshape(bp, N_MEM, X_W),
                                                   w_q, w_o, g1, b1, w_r, b_r)
        (q,) = mm_multi(xs, (w_q,))
        o = xattn_sample_core(q, cmk, cmv, off=layer * bs)
        xs, xs_tiles, idx_s, gate_s = proj_ln_route(o, w_o, xs, g1, b1, w_r, b_r)
        xp = moe_ln(xp, xp_tiles, idx_p, gate_p, wm_in, bm_in, wm_out, bm_out, g2, b2, MOE_BLOCK_PROMPT,
                    e_off=layer * N_EXPERTS)
        xs = moe_ln(xs, xs_tiles, idx_s, gate_s, wm_in, bm_in, wm_out, bm_out, g2, b2, MOE_BLOCK_SAMPLE,
                    e_off=layer * N_EXPERTS)

    return (xp.reshape(bp, t, d), xs.reshape(bs, 1, d),
            jnp.stack(p_S), jnp.stack(p_conv), jnp.stack(p_C), jnp.stack(p_n), jnp.stack(p_m),
            jnp.stack(p_k, axis=1), jnp.stack(p_v, axis=1), jnp.stack(p_mk), jnp.stack(p_mv),
            jnp.stack(s_S), jnp.stack(s_conv), jnp.stack(s_C), jnp.stack(s_n), jnp.stack(s_m),
            jnp.stack(s_k, axis=1), jnp.stack(s_v, axis=1))
```

```python
import functools
import math

import jax
import jax.numpy as jnp
from jax import lax
from jax.experimental import pallas as pl
from jax.experimental.pallas import tpu as pltpu

F32 = jnp.float32
BF16 = jnp.bfloat16
HI = lax.Precision.HIGHEST

D_MODEL = 1024
DEPTH = 2
H_A, DK_A, DV_A, CONV_W, CHUNK = 8, 64, 64, 4, 64
H_B, DQK_B, DV_B = 4, 64, 128
QKV_A = H_A * (2 * DK_A + DV_A)
H_C, DH_C = 8, 64
VD_C = 2 * DH_C
C_W = H_C * 2 * DH_C
N_BUCKETS, MAX_DISTANCE = 32, 128
N_MEM, H_X, DH_X = 256, 4, 128
X_W = H_X * DH_X
N_EXPERTS, TOP_K = 32, 4
D_FF = D_MODEL
SWIGLU_ALPHA, SWIGLU_LIMIT = 1.702, 7.0
DN_ALPHA = (2 * DEPTH) ** 0.25
LN_EPS = 1e-5
NORM_EPS = 1e-6

LANES = 128
SUBLANES = 8
VMEM_PHYSICAL = 64 * 1024 * 1024
VMEM_LIMIT = 48 * 1024 * 1024


def _cparams(*sem):
    return pltpu.CompilerParams(dimension_semantics=tuple(sem), vmem_limit_bytes=VMEM_LIMIT)


def _layer_norm(y, g, b):
    mu = jnp.mean(y, axis=-1, keepdims=True)
    d = y - mu
    var = jnp.mean(d * d, axis=-1, keepdims=True)
    return d * lax.rsqrt(var + LN_EPS) * g + b


def _mm_multi_kernel(x_ref, *refs, n_lo, n_hi):
    n = n_lo + n_hi
    ws, outs = refs[:n], refs[n:]
    x = x_ref[...]
    xb = x.astype(BF16)
    for i in range(n_lo):
        outs[i][...] = jnp.dot(xb, ws[i][...], preferred_element_type=F32)
    for i in range(n_lo, n):
        outs[i][...] = jnp.dot(x, ws[i][...], preferred_element_type=F32, precision=HI)


def mm_multi(x, w_lo, w_hi=(), tm=256):
    m, k = x.shape
    tm = min(tm, m)
    assert m % tm == 0
    ws = tuple(w_lo) + tuple(w_hi)
    in_specs = [pl.BlockSpec((tm, k), lambda i: (i, 0))]
    in_specs += [pl.BlockSpec(w.shape, lambda i: (0, 0)) for w in ws]
    out_specs = [pl.BlockSpec((tm, w.shape[1]), lambda i: (i, 0)) for w in ws]
    out_shape = [jax.ShapeDtypeStruct((m, w.shape[1]), F32) for w in ws]
    return pl.pallas_call(
        functools.partial(_mm_multi_kernel, n_lo=len(w_lo), n_hi=len(w_hi)),
        grid=(m // tm,), in_specs=in_specs, out_specs=out_specs, out_shape=out_shape,
        compiler_params=_cparams("parallel"), name="mm_multi")(x, *ws)


def _proj_ln_kernel(*refs, n):
    a_refs, w_refs = refs[:n], refs[n:2 * n]
    x_ref, g_ref, b_ref, o_ref = refs[2 * n:]
    h = jnp.dot(a_refs[0][...].astype(BF16), w_refs[0][...], preferred_element_type=F32)
    for a_ref, w_ref in zip(a_refs[1:], w_refs[1:]):
        h = h + jnp.dot(a_ref[...].astype(BF16), w_ref[...], preferred_element_type=F32)
    o_ref[...] = _layer_norm(DN_ALPHA * x_ref[...] + h, g_ref[...], b_ref[...])


def proj_ln(a_list, w_list, x_res, g, b, tm=512):
    m, d = x_res.shape
    tm = min(tm, m)
    assert m % tm == 0
    n = len(a_list)
    return pl.pallas_call(
        functools.partial(_proj_ln_kernel, n=n), grid=(m // tm,),
        in_specs=[pl.BlockSpec((tm, a.shape[1]), lambda i: (i, 0)) for a in a_list]
        + [pl.BlockSpec(w.shape, lambda i: (0, 0)) for w in w_list]
        + [pl.BlockSpec((tm, d), lambda i: (i, 0)), pl.BlockSpec((1, d), lambda i: (0, 0)),
           pl.BlockSpec((1, d), lambda i: (0, 0))],
        out_specs=pl.BlockSpec((tm, d), lambda i: (i, 0)),
        out_shape=jax.ShapeDtypeStruct((m, d), F32),
        compiler_params=_cparams("parallel"), name="proj_ln")(*a_list, *w_list, x_res, g, b)


def _mem_attention(q, mk, mv):
    outs = []
    for h in range(H_X):
        sl = slice(h * DH_X, (h + 1) * DH_X)
        s = lax.dot_general(q[:, sl].astype(BF16), mk[:, sl].astype(BF16), (((1,), (1,)), ((), ())),
                            preferred_element_type=F32) * (DH_X ** -0.5)
        s = s - jnp.max(s, axis=-1, keepdims=True)
        p = jnp.exp(s)
        p = p / jnp.sum(p, axis=-1, keepdims=True)
        outs.append(jnp.dot(p.astype(BF16), mv[:, sl].astype(BF16), preferred_element_type=F32))
    return jnp.concatenate(outs, axis=-1)


def _route(x, wr, br):
    x_hi = x.astype(BF16)
    x_lo = (x - x_hi.astype(F32)).astype(BF16)
    logits = (jnp.dot(x_hi, wr[0], preferred_element_type=F32) + jnp.dot(x_lo, wr[0], preferred_element_type=F32)
              + jnp.dot(x_hi, wr[1], preferred_element_type=F32) + br)
    lane = lax.broadcasted_iota(jnp.int32, logits.shape, 1)
    work = jnp.where(lane < N_EXPERTS, logits, -jnp.inf)
    idx_out = jnp.zeros(logits.shape, jnp.int32)
    val_out = jnp.full(logits.shape, -jnp.inf, F32)
    for k in range(TOP_K):
        m = jnp.max(work, axis=-1, keepdims=True)
        sel = jnp.min(jnp.where(work == m, lane, LANES), axis=-1, keepdims=True)
        idx_out = jnp.where(lane == k, sel, idx_out)
        val_out = jnp.where(lane == k, m, val_out)
        work = jnp.where(lane == sel, -jnp.inf, work)
    e = jnp.exp(val_out - jnp.max(val_out, axis=-1, keepdims=True))
    gates = e / jnp.sum(e, axis=-1, keepdims=True)
    return idx_out, gates


def _store_token_tiles(o3_ref, y):
    for s in range(y.shape[1] // LANES):
        o3_ref[:, s, :] = y[:, s * LANES:(s + 1) * LANES]


def _xattn_prompt_kernel(x_ref, mk_ref, mv_ref, wq_ref, wo_ref, g_ref, b_ref, wr_ref, br_ref,
                         o_ref, o3_ref, idx_ref, gate_ref):
    x = x_ref[...]
    q = jnp.dot(x.astype(BF16), wq_ref[...], preferred_element_type=F32)
    o = _mem_attention(q, mk_ref[0], mv_ref[0])
    h = jnp.dot(o.astype(BF16), wo_ref[...], preferred_element_type=F32)
    y = _layer_norm(DN_ALPHA * x + h, g_ref[...], b_ref[...])
    o_ref[...] = y
    _store_token_tiles(o3_ref, y)
    idx, gates = _route(y, wr_ref[...], br_ref[...])
    idx_ref[...] = idx
    gate_ref[...] = gates


def xattn_prompt(x, mk, mv, wq, wo, g, b, wr, br, tq=512):
    m, d = x.shape
    bsz = mk.shape[0]
    t = m // bsz
    nq = t // tq
    full = lambda shape: pl.BlockSpec(shape, lambda bi, qi: (0,) * len(shape))
    row = lambda w: pl.BlockSpec((tq, w), lambda bi, qi: (bi * nq + qi, 0))
    return pl.pallas_call(
        _xattn_prompt_kernel, grid=(bsz, nq),
        in_specs=[row(d), pl.BlockSpec((1, N_MEM, X_W), lambda bi, qi: (bi, 0, 0)),
                  pl.BlockSpec((1, N_MEM, X_W), lambda bi, qi: (bi, 0, 0)),
                  full(wq.shape), full(wo.shape), full(g.shape), full(b.shape), full(wr.shape), full(br.shape)],
        out_specs=[row(d), pl.BlockSpec((tq, d // LANES, LANES), lambda bi, qi: (bi * nq + qi, 0, 0)),
                   row(LANES), row(LANES)],
        out_shape=[jax.ShapeDtypeStruct((m, d), F32), jax.ShapeDtypeStruct((m, d // LANES, LANES), F32),
                   jax.ShapeDtypeStruct((m, LANES), jnp.int32), jax.ShapeDtypeStruct((m, LANES), F32)],
        compiler_params=_cparams("parallel", "parallel"), name="xattn_prompt")(x, mk, mv, wq, wo, g, b, wr, br)


def _xattn_sample_kernel(q_ref, mk_ref, mv_ref, o_ref):
    q = jnp.broadcast_to(q_ref[0], (8, X_W))
    mk = jnp.concatenate([mk_ref[0, pl.ds(h, N_MEM, stride=H_X), :] for h in range(H_X)], axis=-1)
    mv = jnp.concatenate([mv_ref[0, pl.ds(h, N_MEM, stride=H_X), :] for h in range(H_X)], axis=-1)
    o = _mem_attention(q, mk, mv)
    o_ref[0] = o[0:1]


def xattn_sample_core(q, mk, mv, off=0):
    bsz = q.shape[0]
    spec3 = pl.BlockSpec((1, N_MEM * H_X, DH_X), lambda bi: (off + bi, 0, 0))
    out = pl.pallas_call(
        _xattn_sample_kernel, grid=(bsz,),
        in_specs=[pl.BlockSpec((1, 1, X_W), lambda bi: (bi, 0, 0)), spec3, spec3],
        out_specs=pl.BlockSpec((1, 1, X_W), lambda bi: (bi, 0, 0)),
        out_shape=jax.ShapeDtypeStruct((bsz, 1, X_W), F32),
        compiler_params=_cparams("parallel"), name="xattn_sample")(q.reshape(bsz, 1, X_W), mk, mv)
    return out.reshape(bsz, X_W)


def _ln_route_kernel(a_ref, w_ref, x_ref, g_ref, b_ref, wr_ref, br_ref, o_ref, o3_ref, idx_ref, gate_ref):
    h = jnp.dot(a_ref[...].astype(BF16), w_ref[...], preferred_element_type=F32)
    y = _layer_norm(DN_ALPHA * x_ref[...] + h, g_ref[...], b_ref[...])
    o_ref[...] = y
    _store_token_tiles(o3_ref, y)
    idx, gates = _route(y, wr_ref[...], br_ref[...])
    idx_ref[...] = idx
    gate_ref[...] = gates


def proj_ln_route(a, w, x_res, g, b, wr, br):
    m, d = x_res.shape
    return pl.pallas_call(
        _ln_route_kernel,
        out_shape=[jax.ShapeDtypeStruct((m, d), F32), jax.ShapeDtypeStruct((m, d // LANES, LANES), F32),
                   jax.ShapeDtypeStruct((m, LANES), jnp.int32), jax.ShapeDtypeStruct((m, LANES), F32)],
        compiler_params=pltpu.CompilerParams(vmem_limit_bytes=VMEM_LIMIT),
        name="proj_ln_route")(a, w, x_res, g, b, wr, br)


def _expert_onehots(idx):
    lane = lax.broadcasted_iota(jnp.int32, idx.shape, 1)
    onehots = [(idx[:, k:k + 1] == lane).astype(F32) for k in range(TOP_K)]
    tot = onehots[0]
    for k in range(1, TOP_K):
        tot = tot + onehots[k]
    return lane, onehots, tot


def _count_kernel(idx_ref, cnt_ref):
    @pl.when(pl.program_id(0) == 0)
    def _():
        cnt_ref[...] = jnp.zeros_like(cnt_ref)

    _, _, tot = _expert_onehots(idx_ref[...])
    cnt_ref[...] = cnt_ref[...] + jnp.sum(tot, axis=0, keepdims=True)


def _dest_kernel(idx_ref, base0_ref, dest_ref, base_ref):
    @pl.when(pl.program_id(0) == 0)
    def _():
        base_ref[...] = base0_ref[...]

    idx = idx_ref[...]
    tr = idx.shape[0]
    lane, onehots, tot = _expert_onehots(idx)
    r = lax.broadcasted_iota(jnp.int32, (tr, tr), 0)
    c = lax.broadcasted_iota(jnp.int32, (tr, tr), 1)
    strict = (c < r).astype(BF16)
    before = jnp.dot(strict, tot.astype(BF16), preferred_element_type=F32) + base_ref[...]
    out = jnp.zeros(idx.shape, F32)
    for k in range(TOP_K):
        out = jnp.where(lane == k, jnp.sum(onehots[k] * before, axis=-1, keepdims=True), out)
    dest_ref[...] = out.astype(jnp.int32)
    base_ref[...] = base_ref[...] + jnp.sum(tot, axis=0, keepdims=True)


def route_slots(idx, bm, tr=256):
    t = idx.shape[0]
    tr = min(tr, t)
    assert t % tr == 0
    rows = pl.BlockSpec((tr, LANES), lambda i: (i, 0))
    one = pl.BlockSpec((1, LANES), lambda i: (0, 0))
    counts = pl.pallas_call(
        _count_kernel, grid=(t // tr,), in_specs=[rows], out_specs=one,
        out_shape=jax.ShapeDtypeStruct((1, LANES), F32),
        compiler_params=_cparams("arbitrary"), name="route_count")(idx)
    counts = counts[0].astype(jnp.int32)
    padded = (counts + bm - 1) // bm * bm
    pad_end = jnp.cumsum(padded)
    base0 = (pad_end - padded).astype(F32).reshape(1, LANES)
    dest = pl.pallas_call(
        _dest_kernel, grid=(t // tr,), in_specs=[rows, one], out_specs=rows,
        out_shape=jax.ShapeDtypeStruct((t, LANES), jnp.int32),
        scratch_shapes=[pltpu.VMEM((1, LANES), F32)],
        compiler_params=_cparams("arbitrary"), name="route_dest")(idx, base0)
    return dest, pad_end[:N_EXPERTS]


def _clamped_swiglu(h):
    glu = jnp.minimum(h[:, :D_FF], SWIGLU_LIMIT)
    lin = jnp.clip(h[:, D_FF:], -SWIGLU_LIMIT, SWIGLU_LIMIT)
    return glu * jax.nn.sigmoid(SWIGLU_ALPHA * glu) * (lin + 1.0)


def _tile_rows(buf):
    rows = buf.shape[0] * buf.shape[2]
    return jnp.concatenate([buf[:, j].reshape(rows, LANES) for j in range(buf.shape[1])], axis=-1)


def _moe_gmm_kernel(bexp_ref, nused_ref, tok_ref, tok1_ref, tok2_ref, x_hbm, win_ref, bin_ref, wout_ref,
                    bout_ref, y_hbm, xbuf, ybuf, winb, woutb, gsem, osem, *, bm):
    i = pl.program_id(0)
    n = pl.num_programs(0)
    nused = nused_ref[0]
    slot = lax.rem(i, 2)
    xslot = lax.rem(i, 3)
    n_sub = x_hbm.shape[1]
    groups = bm // SUBLANES

    def start_gather(idx_ref, s):
        def body(g, carry):
            for u in range(SUBLANES):
                pltpu.make_async_copy(x_hbm.at[idx_ref[0, 0, g * SUBLANES + u]], xbuf.at[s, g, :, u],
                                      gsem.at[s]).start(priority=u % 2)
            return carry
        lax.fori_loop(0, groups, body, 0)

    def wait_gather(s):
        pltpu.make_async_copy(xbuf.at[s], xbuf.at[s], gsem.at[s]).wait()

    def out_copies(blk, s):
        return [pltpu.make_async_copy(ybuf.at[s, :, j], y_hbm.at[pl.ds(blk * groups, groups), :, j], osem.at[s])
                for j in range(n_sub)]

    @pl.when(i == 0)
    def _():
        start_gather(tok_ref, 0)

        @pl.when(jnp.logical_and(1 < n, 1 < nused))
        def _():
            start_gather(tok1_ref, 1)

    @pl.when(jnp.logical_and(i + 2 < n, i + 2 < nused))
    def _():
        start_gather(tok2_ref, lax.rem(i + 2, 3))

    active = i < nused

    @pl.when(jnp.logical_and(active, jnp.logical_or(i == 0, bexp_ref[i] != bexp_ref[jnp.maximum(i - 1, 0)])))
    def _():
        winb[...] = win_ref[0].astype(BF16)
        woutb[...] = wout_ref[0].astype(BF16)

    @pl.when(i >= 2)
    def _():
        for cp in out_copies(i - 2, slot):
            cp.wait()

    @pl.when(active)
    def _():
        wait_gather(xslot)
        h = jnp.dot(_tile_rows(xbuf[xslot]).astype(BF16), winb[...], preferred_element_type=F32) + bin_ref[0]
        act = _clamped_swiglu(h)
        y = jnp.dot(act.astype(BF16), woutb[...], preferred_element_type=F32) + bout_ref[0]
        for j in range(n_sub):
            ybuf[slot, :, j] = y[:, j * LANES:(j + 1) * LANES].reshape(groups, SUBLANES, LANES)

    @pl.when(jnp.logical_not(active))
    def _():
        ybuf[slot] = jnp.zeros(ybuf.shape[1:], F32)

    for cp in out_copies(i, slot):
        cp.start()

    @pl.when(i == n - 1)
    def _():
        @pl.when(n >= 2)
        def _():
            for cp in out_copies(i - 1, 1 - slot):
                cp.wait()

        for cp in out_copies(i, slot):
            cp.wait()


def moe_gmm(x, row_tok, block_expert, n_used, w_in, b_in, w_out, b_out, bm):
    n_blocks = row_tok.shape[0]
    n_sub = x.shape[1]
    d = n_sub * LANES
    idx_spec = lambda off: pl.BlockSpec(
        (1, 1, bm), lambda i, be, nu: (jnp.minimum(i + off, n_blocks - 1), 0, 0), memory_space=pltpu.SMEM)
    ex = lambda i, be, nu: (be[i], 0, 0)
    grid_spec = pltpu.PrefetchScalarGridSpec(
        num_scalar_prefetch=2, grid=(n_blocks,),
        in_specs=[idx_spec(0), idx_spec(1), idx_spec(2),
                  pl.BlockSpec(memory_space=pl.ANY),
                  pl.BlockSpec((1, d, 2 * D_FF), ex), pl.BlockSpec((1, 1, 2 * D_FF), ex),
                  pl.BlockSpec((1, D_FF, d), ex), pl.BlockSpec((1, 1, d), ex)],
        out_specs=pl.BlockSpec(memory_space=pl.ANY),
        scratch_shapes=[pltpu.VMEM((3, bm // SUBLANES, n_sub, SUBLANES, LANES), F32),
                        pltpu.VMEM((2, bm // SUBLANES, n_sub, SUBLANES, LANES), F32),
                        pltpu.VMEM((d, 2 * D_FF), BF16), pltpu.VMEM((D_FF, d), BF16),
                        pltpu.SemaphoreType.DMA((3,)), pltpu.SemaphoreType.DMA((2,))])
    w_elems = d * 2 * D_FF + D_FF * d
    vmem_bytes = (2 * 4 + 2) * w_elems + 5 * bm * d * 4 + bm * (2 * D_FF + D_FF + 2 * d) * 4
    vmem_bytes = min(vmem_bytes + (4 << 20), VMEM_PHYSICAL - (6 << 20))
    return pl.pallas_call(
        functools.partial(_moe_gmm_kernel, bm=bm), grid_spec=grid_spec,
        out_shape=jax.ShapeDtypeStruct((n_blocks * bm // SUBLANES, SUBLANES, n_sub, LANES), F32),
        compiler_params=pltpu.CompilerParams(dimension_semantics=("arbitrary",), vmem_limit_bytes=vmem_bytes),
        name="moe_gmm")(
            block_expert, n_used, row_tok, row_tok, row_tok, x, w_in, b_in.reshape(b_in.shape[0], 1, -1),
            w_out, b_out.reshape(b_out.shape[0], 1, -1))


def _combine_ln_kernel(dst_ref, dst_next_ref, y_hbm, gate_ref, x_ref, g_ref, b_ref, o_ref, gbuf, sem):
    i = pl.program_id(0)
    n = pl.num_programs(0)
    slot = lax.rem(i, 2)
    groups = gbuf.shape[2]

    def start_gather(idx_ref, s):
        def body(g, carry):
            for u in range(SUBLANES):
                for k in range(TOP_K):
                    d = idx_ref[0, 0, (g * SUBLANES + u) * TOP_K + k]
                    pltpu.make_async_copy(y_hbm.at[d], gbuf.at[s, k, g, :, u], sem.at[s]).start(priority=k % 2)
            return carry
        lax.fori_loop(0, groups, body, 0)

    @pl.when(i == 0)
    def _():
        start_gather(dst_ref, 0)

    @pl.when(i + 1 < n)
    def _():
        start_gather(dst_next_ref, 1 - slot)

    pltpu.make_async_copy(gbuf.at[slot], gbuf.at[slot], sem.at[slot]).wait()
    gates = gate_ref[...]
    acc = gates[:, 0:1] * _tile_rows(gbuf[slot, 0])
    for k in range(1, TOP_K):
        acc = acc + gates[:, k:k + 1] * _tile_rows(gbuf[slot, k])
    o_ref[...] = _layer_norm(DN_ALPHA * x_ref[...] + acc, g_ref[...], b_ref[...])


def combine_ln(y, dest, gates, x_res, g, b, tc=256):
    t, d = x_res.shape
    tc = min(tc, t)
    assert t % tc == 0
    n = t // tc
    n_sub = d // LANES
    dest_flat = dest[:, :TOP_K].reshape(n, 1, tc * TOP_K)
    idx_spec = lambda off: pl.BlockSpec((1, 1, tc * TOP_K), lambda i: (jnp.minimum(i + off, n - 1), 0, 0),
                                        memory_space=pltpu.SMEM)
    return pl.pallas_call(
        _combine_ln_kernel, grid=(n,),
        in_specs=[idx_spec(0), idx_spec(1), pl.BlockSpec(memory_space=pl.ANY),
                  pl.BlockSpec((tc, LANES), lambda i: (i, 0)),
                  pl.BlockSpec((tc, d), lambda i: (i, 0)), pl.BlockSpec((1, d), lambda i: (0, 0)),
                  pl.BlockSpec((1, d), lambda i: (0, 0))],
        out_specs=pl.BlockSpec((tc, d), lambda i: (i, 0)),
        out_shape=jax.ShapeDtypeStruct((t, d), F32),
        scratch_shapes=[pltpu.VMEM((2, TOP_K, tc // SUBLANES, n_sub, SUBLANES, LANES), F32),
                        pltpu.SemaphoreType.DMA((2,))],
        compiler_params=_cparams("arbitrary"), name="combine_ln")(
            dest_flat, dest_flat, y.reshape(-1, n_sub, LANES), gates, x_res, g, b)


def moe_ln(x, x_tiles, idx, gates, w_in, b_in, w_out, b_out, g, b, bm, e_off=0):
    t, d = x.shape
    tk = t * TOP_K
    dest, pad_end = route_slots(idx, bm)
    n_blocks = -(-tk // bm) + N_EXPERTS
    n_rows = n_blocks * bm
    tok = jnp.arange(tk, dtype=jnp.int32) // TOP_K
    row_tok = jnp.zeros((n_rows,), jnp.int32).at[dest[:, :TOP_K].reshape(tk)].set(
        tok, unique_indices=True, mode='promise_in_bounds').reshape(n_blocks, 1, bm)
    block_start = jnp.arange(n_blocks, dtype=jnp.int32) * bm
    block_expert = jnp.sum((pad_end[None, :] <= block_start[:, None]).astype(jnp.int32), axis=1)
    block_expert = jnp.minimum(block_expert, N_EXPERTS - 1) + e_off
    n_used = (pad_end[-1] // bm).astype(jnp.int32).reshape(1)
    y = moe_gmm(x_tiles, row_tok, block_expert, n_used, w_in, b_in, w_out, b_out, bm)
    return combine_ln(y, dest, gates, x, g, b)


def _t5_causal_bucket(dist):
    n = jnp.maximum(dist, 0)
    max_exact = N_BUCKETS // 2
    nf = jnp.maximum(n, max_exact).astype(F32)
    large = max_exact + (jnp.log(nf / max_exact) / math.log(MAX_DISTANCE / max_exact)
                         * (N_BUCKETS - max_exact)).astype(jnp.int32)
    large = jnp.minimum(large, N_BUCKETS - 1)
    return jnp.where(n < max_exact, n, large)


def _split_maps(q):
    lane = lax.broadcasted_iota(jnp.int32, q.shape, 1)
    qs = q * (DH_C ** -0.5)
    return (jnp.where(lane < DH_C, qs, 0.0).astype(BF16), jnp.where(lane >= DH_C, qs, 0.0).astype(BF16))


def _diff_finish(acc1, l1, acc2, l2, lam, out_scale, g):
    o = acc1 / l1 - lam * (acc2 / l2)
    o = o * lax.rsqrt(jnp.mean(o * o, axis=-1, keepdims=True) + NORM_EPS) * g
    return o * out_scale


def _diff_prompt_kernel(sc_ref, far_ref, q_ref, k_ref, v_ref, bias_ref, g_ref, o_ref, m_ref, a_ref, *, tq, hp):
    h0 = pl.program_id(1) * hp
    qi = pl.program_id(2)
    head = lambda x, h: x[:, h * VD_C:(h + 1) * VD_C]
    q = q_ref[...]
    qm = jnp.stack([jnp.concatenate(_split_maps(head(q, h)), axis=0) for h in range(hp)], axis=0)
    m_ref[...] = jnp.full_like(m_ref, -jnp.inf)
    a_ref[...] = jnp.zeros_like(a_ref)
    ones_col = (lax.broadcasted_iota(jnp.int32, (tq, LANES), 1) == 0).astype(BF16)
    reps = tq // LANES

    def update(tiles):
        scores, values = [], []
        for kidx, bias, shift, causal in tiles:
            start = pl.multiple_of(kidx * tq, tq)
            kt = k_ref[0, pl.ds(start, tq), :].astype(BF16)
            vt = v_ref[0, pl.ds(start, tq), :].astype(BF16)
            values.append(jnp.stack([jnp.concatenate([head(vt, h), ones_col], axis=-1) for h in range(hp)], axis=0))
            s = _bmm_nt(qm, jnp.stack([head(kt, h) for h in range(hp)], axis=0))
            if bias is not None:
                s = s + jnp.concatenate([bias, bias], axis=1)
            if causal:
                r = lax.broadcasted_iota(jnp.int32, s.shape[1:], 0)
                c = lax.broadcasted_iota(jnp.int32, s.shape[1:], 1)
                s = jnp.where(c <= lax.bitwise_and(r, tq - 1), s, -jnp.inf)
            scores.append((s, shift))
        m_old = m_ref[...]
        m_new = m_old
        for s, shift in scores:
            m_new = jnp.maximum(m_new, jnp.max(s, axis=-1, keepdims=True) + shift)
        alpha = jnp.exp(m_old - m_new)
        acc = jnp.concatenate([alpha, alpha], axis=-1) * a_ref[...]
        for (s, shift), vt in zip(scores, values):
            p = jnp.exp(s - jnp.concatenate([m_new - shift] * reps, axis=-1))
            acc = acc + _bmm(p, vt)
        a_ref[...] = acc
        m_ref[...] = m_new

    far_bias = jnp.stack([jnp.full((1, 1), far_ref[h0 + h], F32) for h in range(hp)], axis=0)
    no_shift = jnp.zeros((hp, 1, 1), F32)
    n_far = jnp.maximum(qi - 1, 0)

    def far_body(j, carry):
        update([(2 * j, None, far_bias, False), (2 * j + 1, None, far_bias, False)])
        return carry

    lax.fori_loop(0, n_far // 2, far_body, 0)

    @pl.when(lax.rem(n_far, 2) == 1)
    def _():
        update([(n_far - 1, None, far_bias, False)])

    @pl.when(qi >= 1)
    def _():
        update([(qi - 1, bias_ref[:, 1], no_shift, False), (qi, bias_ref[:, 0], no_shift, True)])

    @pl.when(qi == 0)
    def _():
        update([(0, bias_ref[:, 0], no_shift, True)])

    acc = a_ref[...]
    o_ref[...] = jnp.concatenate(
        [_diff_finish(acc[h, :tq, :VD_C], acc[h, :tq, VD_C:VD_C + 1], acc[h, tq:, :VD_C], acc[h, tq:, VD_C:VD_C + 1],
                      sc_ref[0], sc_ref[1], g_ref[...]) for h in range(hp)], axis=-1)


def _bias_tiles_kernel(rb_ref, bucket_ref, o_ref):
    h = pl.program_id(0)
    b = bucket_ref[...]
    out = jnp.zeros(b.shape, F32)
    for k in range(N_BUCKETS):
        out = jnp.where(b == k, rb_ref[k * H_C + h], out)
    o_ref[0] = out


DIFF_HEADS_PER_STEP = 2


def diff_attn_prompt_core(q, k, v, bsz, rel_bias, lam, lam_init, subln_g, tq=256):
    m = q.shape[0]
    t = m // bsz
    nq = t // tq
    ii = jnp.arange(tq)[:, None]
    jj = jnp.arange(tq)[None, :]
    buckets = _t5_causal_bucket(jnp.stack([jnp.maximum(ii - jj, 0), tq + ii - jj])).astype(jnp.int32)
    bias = pl.pallas_call(
        _bias_tiles_kernel, grid=(H_C,),
        in_specs=[pl.BlockSpec(memory_space=pltpu.SMEM), pl.BlockSpec((2, tq, tq), lambda h: (0, 0, 0))],
        out_specs=pl.BlockSpec((1, 2, tq, tq), lambda h: (h, 0, 0, 0)),
        out_shape=jax.ShapeDtypeStruct((H_C, 2, tq, tq), F32),
        compiler_params=_cparams("parallel"), name="t5_bias_tiles")(rel_bias.astype(F32).reshape(-1), buckets)
    far = rel_bias[_t5_causal_bucket(jnp.array(2 * tq))].astype(F32)
    scal = jnp.stack([lam, 1.0 - lam_init]).astype(F32)
    k3 = k.reshape(bsz, t, C_W)
    v3 = v.reshape(bsz, t, C_W)
    smem = pl.BlockSpec(memory_space=pltpu.SMEM)
    hp = DIFF_HEADS_PER_STEP
    kv_spec = pl.BlockSpec((1, t, hp * VD_C), lambda b, h, i: (b, 0, h))
    row = pl.BlockSpec((tq, hp * VD_C), lambda b, h, i: (b * nq + i, h))
    stat = pltpu.VMEM((hp, 2 * tq, LANES), F32)
    acc = pltpu.VMEM((hp, 2 * tq, VD_C + LANES), F32)
    return pl.pallas_call(
        functools.partial(_diff_prompt_kernel, tq=tq, hp=hp), grid=(bsz, H_C // hp, nq),
        in_specs=[smem, smem, row, kv_spec, kv_spec,
                  pl.BlockSpec((hp, 2, tq, tq), lambda b, h, i: (h, 0, 0, 0)),
                  pl.BlockSpec((1, VD_C), lambda b, h, i: (0, 0))],
        out_specs=row, out_shape=jax.ShapeDtypeStruct((m, C_W), F32),
        scratch_shapes=[stat, acc],
        compiler_params=_cparams("parallel", "parallel", "parallel"), name="diff_attn_prompt")(
            scal, far, q, k3, v3, bias, subln_g.reshape(1, VD_C))


def _diff_sample_kernel(pt_ref, sc_ref, q_ref, kn_ref, vn_ref, bm_ref, bnew_ref, g_ref, *refs, n_pp):
    k_refs, v_refs = refs[:n_pp], refs[n_pp:2 * n_pp]
    o_ref, m_ref, l_ref, a_ref = refs[2 * n_pp:]
    p = pl.program_id(1)
    n_p = pl.num_programs(1)
    q8 = q_ref[0]
    qm = jnp.concatenate(_split_maps(q8), axis=0)

    @pl.when(p == 0)
    def _():
        m_ref[...] = jnp.full_like(m_ref, -jnp.inf)
        l_ref[...] = jnp.zeros_like(l_ref)
        a_ref[...] = jnp.zeros_like(a_ref)

    scores = []
    for j in range(n_pp):
        bias = bm_ref[jnp.where(p == n_p - 1, 1, 0)] if j == n_pp - 1 else bm_ref[0]
        k2 = k_refs[j][0, 0].reshape(-1, 2 * DH_C).astype(BF16)
        scores.append(lax.dot_general(qm, k2, (((1,), (1,)), ((), ())), preferred_element_type=F32) + bias)
    m_old = m_ref[...]
    m_new = m_old
    for s in scores:
        m_new = jnp.maximum(m_new, jnp.max(s, axis=-1, keepdims=True))
    alpha = jnp.exp(m_old - m_new)
    l_new = alpha * l_ref[...]
    a_new = alpha * a_ref[...]
    for j, s in enumerate(scores):
        pr = jnp.exp(s - m_new)
        l_new = l_new + jnp.sum(pr, axis=-1, keepdims=True)
        v2 = v_refs[j][0, 0].reshape(-1, VD_C).astype(BF16)
        a_new = a_new + jnp.dot(pr.astype(BF16), v2, preferred_element_type=F32)
    l_ref[...] = l_new
    a_ref[...] = a_new
    m_ref[...] = m_new

    @pl.when(p == n_p - 1)
    def _():
        kn = jnp.concatenate([kn_ref[0], kn_ref[0]], axis=0)
        vn = jnp.concatenate([vn_ref[0], vn_ref[0]], axis=0)
        s = jnp.sum(qm.astype(F32) * kn, axis=-1, keepdims=True) + bnew_ref[:, 0:1]
        m_old = m_ref[...]
        m_new = jnp.maximum(m_old, s)
        alpha = jnp.exp(m_old - m_new)
        pr = jnp.exp(s - m_new)
        l = alpha * l_ref[...] + pr
        a = alpha * a_ref[...] + pr * vn
        o_ref[0] = _diff_finish(a[:H_C], l[:H_C], a[H_C:], l[H_C:], sc_ref[0], sc_ref[1], g_ref[...])


def diff_attn_sample_core(q, k_new, v_new, cache_k, cache_v, page_table, layer_j, rel_bias, lam, lam_init, subln_g,
                          n_pp=8):
    bsz = q.shape[0]
    n_pages = page_table.shape[1]
    page = cache_k.shape[2]
    past = n_pages * page
    assert n_pages % n_pp == 0
    dist_last = past - ((n_pages - 1) * page + jnp.arange(page))
    b_last = rel_bias[_t5_causal_bucket(dist_last)].astype(F32)
    b_far = jnp.broadcast_to(rel_bias[_t5_causal_bucket(jnp.array(page + 1))].astype(F32), (page, H_C))
    eye = jnp.eye(H_C, dtype=bool)

    def expand(bt):
        full = jnp.where(eye[:, None, :], bt.T[:, :, None], -jnp.inf).reshape(H_C, page * H_C)
        return jnp.concatenate([full, full], axis=0)

    bm = jnp.stack([expand(b_far), expand(b_last)])
    b_new = rel_bias[_t5_causal_bucket(jnp.array(0))].astype(F32)
    b_new = jnp.broadcast_to(jnp.concatenate([b_new, b_new])[:, None], (2 * H_C, LANES))
    scal = jnp.stack([lam, 1.0 - lam_init]).astype(F32)
    smem = pl.BlockSpec(memory_space=pltpu.SMEM)
    head3 = pl.BlockSpec((1, H_C, VD_C), lambda b, p, pt: (b, 0, 0))
    full = lambda shape: pl.BlockSpec(shape, lambda b, p, pt: (0,) * len(shape))

    def page_spec(j):
        return pl.BlockSpec((1, 1, page, H_C, VD_C), lambda b, p, pt: (pt[b, p * n_pp + j], layer_j, 0, 0, 0))

    grid_spec = pltpu.PrefetchScalarGridSpec(
        num_scalar_prefetch=1, grid=(bsz, n_pages // n_pp),
        in_specs=[smem, head3, head3, head3, full(bm.shape), full(b_new.shape), full((1, VD_C))]
        + [page_spec(j) for j in range(n_pp)] * 2,
        out_specs=head3,
        scratch_shapes=[pltpu.VMEM((2 * H_C, 1), F32), pltpu.VMEM((2 * H_C, 1), F32),
                        pltpu.VMEM((2 * H_C, VD_C), F32)])
    out = pl.pallas_call(
        functools.partial(_diff_sample_kernel, n_pp=n_pp), grid_spec=grid_spec,
        out_shape=jax.ShapeDtypeStruct((bsz, H_C, VD_C), F32),
        compiler_params=_cparams("parallel", "arbitrary"), name="diff_attn_sample")(
            page_table, scal, q.reshape(bsz, H_C, VD_C), k_new.reshape(bsz, H_C, VD_C),
            v_new.reshape(bsz, H_C, VD_C), bm, b_new, subln_g.reshape(1, VD_C),
            *([cache_k] * n_pp), *([cache_v] * n_pp))
    return out.reshape(bsz, C_W)


LANE_BETA, LANE_A, LANE_I, LANE_F = 0, H_A, 2 * H_A, 2 * H_A + H_B


def _softplus(x):
    return jnp.maximum(x, 0.0) + jnp.log1p(jnp.exp(-jnp.abs(x)))


def _silu(x):
    return x * jax.nn.sigmoid(x)


def _lanes(shape, lo, n):
    lane = lax.broadcasted_iota(jnp.int32, shape, 1)
    return jnp.logical_and(lane >= lo, lane < lo + n)


def _gate_tile(gt, alog_row, prow):
    z = gt + prow
    return jax.nn.sigmoid(gt), -jnp.exp(alog_row) * _softplus(z), z, -_softplus(-z)


def _nt(a, b, precision=None):
    return lax.dot_general(a, b, (((1,), (1,)), ((), ())), preferred_element_type=F32, precision=precision)


def _tn(a, b):
    return lax.dot_general(a, b, (((0,), (0,)), ((), ())), preferred_element_type=F32)


def _row_selector(n_heads, length, lanes_of_head):
    r = lax.broadcasted_iota(jnp.int32, (n_heads * length, LANES), 0) // length
    lane = lax.broadcasted_iota(jnp.int32, (n_heads * length, LANES), 1)
    sel = jnp.zeros((n_heads * length, LANES), F32)
    for lo in lanes_of_head:
        sel = sel + (lane == r + lo).astype(F32)
    return sel


def _tri_masks(n):
    r = lax.broadcasted_iota(jnp.int32, (n, n), 0)
    c = lax.broadcasted_iota(jnp.int32, (n, n), 1)
    return r, c


def _bmm(a, b):
    return lax.dot_general(a.astype(BF16), b.astype(BF16), (((2,), (1,)), ((0,), (0,))),
                           preferred_element_type=F32)


def _bmm_nt(a, b):
    return lax.dot_general(a.astype(BF16), b.astype(BF16), (((2,), (2,)), ((0,), (0,))),
                           preferred_element_type=F32)


def _heads(x, n_heads, width, offset=0):
    return jnp.stack([x[c * CHUNK:(c + 1) * CHUNK, offset + h * width:offset + (h + 1) * width]
                      for c in range(x.shape[0] // CHUNK) for h in range(n_heads)], axis=0)


def _head_cols(x, n_heads, lane0):
    return jnp.stack([x[c * CHUNK:(c + 1) * CHUNK, lane0 + h:lane0 + h + 1]
                      for c in range(x.shape[0] // CHUNK) for h in range(n_heads)], axis=0)


def _unheads(x, n_heads):
    n = x.shape[0] // n_heads
    return jnp.concatenate([jnp.concatenate([x[c * n_heads + h] for h in range(n_heads)], axis=-1)
                            for c in range(n)], axis=0)


def _chunk_cumsum(x):
    r, c = _tri_masks(x.shape[0])
    tri = jnp.logical_and(c <= r, r // CHUNK == c // CHUNK).astype(F32)
    return jnp.dot(tri, x, preferred_element_type=F32, precision=HI)


def _chunk_rows(sel, x):
    n = x.shape[0] // CHUNK
    rows = [_nt(sel, x[c * CHUNK:(c + 1) * CHUNK], HI) for c in range(n)]
    return jnp.concatenate(rows, axis=0).reshape(n * sel.shape[0] // CHUNK, CHUNK, CHUNK)


def _unit_lower_inverse(nmat, r, c):
    mm = _bmm
    eye = (r == c).astype(F32)
    same = (r // 16) == (c // 16)
    nd = jnp.where(same, nmat, 0.0)
    off = nmat - nd
    dinv = eye - nd
    p = nd
    for _ in range(3):
        p = mm(p, p)
        dinv = dinv + mm(dinv, p)
    m = mm(dinv, off)
    m2 = mm(m, m)
    left = eye - m
    left = left + mm(left, m2)
    return mm(left, dinv)


def _gdn_prompt_kernel(qkv_ref, z_ref, gt_ref, convw_ref, alog_ref, prow_ref, ng_ref, mix_ref, s_ref, ext_ref):
    cidx = pl.program_id(1)
    L = CHUNK
    rows = qkv_ref.shape[0]
    n_ch = rows // L

    @pl.when(cidx == 0)
    def _():
        ext_ref[0:8, :] = jnp.zeros((8, QKV_A), F32)
        s_ref[...] = jnp.zeros_like(s_ref)

    ext_ref[8:8 + rows, :] = qkv_ref[...]
    acc = convw_ref[0:1, :] * ext_ref[pl.ds(8 - (CONV_W - 1), rows), :]
    for i in range(1, CONV_W):
        acc = acc + convw_ref[i:i + 1, :] * ext_ref[pl.ds(8 - (CONV_W - 1) + i, rows), :]
    ext_ref[0:8, :] = ext_ref[rows:rows + 8, :]
    cs = _silu(acc)

    beta, g, _, _ = _gate_tile(gt_ref[...], alog_ref[...], prow_ref[...])
    r, c = _tri_masks(L)
    incl = c <= r
    strict = c < r
    gcum = _chunk_cumsum(jnp.where(_lanes(g.shape, LANE_A, H_A), g, 0.0))
    gc_rows = _chunk_rows(_row_selector(H_A, L, (LANE_A,)), gcum)
    q = _heads(cs, H_A, DK_A)
    k = _heads(cs, H_A, DK_A, H_A * DK_A)
    v = _heads(cs, H_A, DV_A, 2 * H_A * DK_A)
    q = q * lax.rsqrt(jnp.sum(q * q, axis=-1, keepdims=True) + NORM_EPS) * (DK_A ** -0.5)
    k = k * lax.rsqrt(jnp.sum(k * k, axis=-1, keepdims=True) + NORM_EPS)
    beta_c = _head_cols(beta, H_A, LANE_BETA)
    gc_c = _head_cols(gcum, H_A, LANE_A)
    decay = jnp.where(incl, jnp.exp(jnp.where(incl, gc_c - gc_rows, 0.0)), 0.0)
    kb = k.astype(BF16)
    nmat = jnp.where(strict, beta_c * _bmm_nt(kb, kb) * decay, 0.0)
    egc = jnp.exp(gc_c)
    rhs = jnp.concatenate([v * beta_c, k * (beta_c * egc)], axis=-1)
    sol = _bmm(_unit_lower_inverse(nmat, r, c), rhs)
    u = sol[:, :, :DV_A]
    wq = jnp.concatenate([sol[:, :, DV_A:], q * egc], axis=1).astype(BF16)
    qk = (_bmm_nt(q, kb) * decay).astype(BF16)
    gc_last = gc_c[:, L - 1:L]
    k_tail = (k * jnp.exp(gc_last - gc_c)).astype(BF16)
    g_tail = jnp.exp(gc_last)
    state = s_ref[0]
    outs = []
    for ci in range(n_ch):
        sl = slice(ci * H_A, (ci + 1) * H_A)
        ws = _bmm(wq[sl], state)
        delta = u[sl] - ws[:, :L]
        outs.append(ws[:, L:] + _bmm(qk[sl], delta))
        delta_b = delta.astype(BF16)
        state = state * g_tail[sl] + jnp.stack([_tn(k_tail[ci * H_A + h], delta_b[h]) for h in range(H_A)], axis=0)
    s_ref[0] = state
    o = jnp.concatenate(outs, axis=0)
    o = o * lax.rsqrt(jnp.mean(o * o, axis=-1, keepdims=True) + NORM_EPS) * ng_ref[...]
    o = o * _silu(_heads(z_ref[...], H_A, DV_A))
    mix_ref[...] = _unheads(o, H_A)


AB_CHUNKS_PER_STEP = 4


def gdn_prompt(qkv, z, gates, conv_w, alog_row, prow, norm_g, bsz):
    m = qkv.shape[0]
    rows = AB_CHUNKS_PER_STEP * CHUNK
    nc = m // bsz // rows
    row = lambda w: pl.BlockSpec((rows, w), lambda b, c: (b * nc + c, 0))
    full = lambda shape: pl.BlockSpec(shape, lambda b, c: (0,) * len(shape))
    return pl.pallas_call(
        _gdn_prompt_kernel, grid=(bsz, nc),
        in_specs=[row(QKV_A), row(H_A * DV_A), row(LANES), full(conv_w.shape), full((1, LANES)), full((1, LANES)),
                  full((1, DV_A))],
        out_specs=[row(H_A * DV_A), pl.BlockSpec((1, H_A, DK_A, DV_A), lambda b, c: (b, 0, 0, 0))],
        out_shape=[jax.ShapeDtypeStruct((m, H_A * DV_A), F32), jax.ShapeDtypeStruct((bsz, H_A, DK_A, DV_A), F32)],
        scratch_shapes=[pltpu.VMEM((rows + 8, QKV_A), F32)],
        compiler_params=_cparams("parallel", "arbitrary"), name="gdn_prompt")(
            qkv, z, gates, conv_w, alog_row, prow, norm_g.reshape(1, DV_A))


def _mlstm_prompt_kernel(q_ref, k_ref, v_ref, og_ref, gt_ref, alog_ref, prow_ref, ng_ref, mix_ref, c_ref, m_ref):
    cidx = pl.program_id(1)
    L = CHUNK

    @pl.when(cidx == 0)
    def _():
        c_ref[...] = jnp.zeros_like(c_ref)
        m_ref[...] = jnp.zeros_like(m_ref)

    n_ch = q_ref.shape[0] // L
    _, _, ipre, logf = _gate_tile(gt_ref[...], alog_ref[...], prow_ref[...])
    r, c = _tri_masks(L)
    incl = c <= r
    fsel = _lanes(logf.shape, LANE_F, H_B)
    bcum = _chunk_cumsum(jnp.where(fsel, logf, 0.0))
    rowvals = jnp.where(_lanes(ipre.shape, LANE_I, H_B), ipre, 0.0) - jnp.where(fsel, bcum, 0.0)
    rows = _chunk_rows(_row_selector(H_B, L, (LANE_I, LANE_F)), rowvals)
    mrow = m_ref[0]
    lane_row = lax.broadcasted_iota(jnp.int32, mrow.shape, 1)
    qb = _heads(q_ref[...], H_B, DQK_B).astype(BF16)
    ks = _heads(k_ref[...], H_B, DQK_B) * (DQK_B ** -0.5)
    ones_col = jnp.broadcast_to((lax.broadcasted_iota(jnp.int32, (L, LANES), 1) == 0).astype(F32),
                                (n_ch * H_B, L, LANES))
    v_ext = jnp.concatenate([_heads(v_ref[...], H_B, DV_B), ones_col], axis=-1).astype(BF16)
    b_c = _head_cols(bcum, H_B, LANE_F)
    i_c = _head_cols(ipre, H_B, LANE_I)
    dmat = jnp.where(incl, b_c + rows, -jnp.inf)
    m_intra = jnp.max(dmat, axis=-1, keepdims=True)
    w_intra = jnp.exp(dmat - m_intra) * _bmm_nt(qb, ks)
    nd_intra = _bmm(w_intra, v_ext)
    b_last = b_c[:, L - 1:L]
    e_end = b_last - b_c + i_c
    e_max = jnp.max(e_end, axis=1, keepdims=True)
    kw = (ks * jnp.exp(e_end - e_max)).astype(BF16)
    kv_end = jnp.stack([_tn(kw[i], v_ext[i]) for i in range(n_ch * H_B)], axis=0)
    m_prev = jnp.stack([mrow[:, h:h + 1] for h in range(H_B)], axis=0)
    state = c_ref[0]
    outs = []
    for ci in range(n_ch):
        sl = slice(ci * H_B, (ci + 1) * H_B)
        inter = b_c[sl] + m_prev
        m_t = jnp.maximum(inter, m_intra[sl])
        nd = jnp.exp(inter - m_t) * _bmm(qb[sl], state) + jnp.exp(m_intra[sl] - m_t) * nd_intra[sl]
        outs.append(nd[:, :, :DV_B] / jnp.maximum(jnp.abs(nd[:, :, DV_B:DV_B + 1]), jnp.exp(-m_t)))
        m_new = jnp.maximum(b_last[sl] + m_prev, e_max[sl])
        state = jnp.exp(b_last[sl] + m_prev - m_new) * state + jnp.exp(e_max[sl] - m_new) * kv_end[sl]
        m_prev = m_new
    c_ref[0] = state
    for h in range(H_B):
        mrow = jnp.where(lane_row == h, m_prev[h], mrow)
    m_ref[0] = mrow
    hh = jnp.concatenate(outs, axis=0)
    hh = hh * lax.rsqrt(jnp.mean(hh * hh, axis=-1, keepdims=True) + NORM_EPS) * ng_ref[...]
    hh = jax.nn.sigmoid(_heads(og_ref[...], H_B, DV_B)) * hh
    mix_ref[...] = _unheads(hh, H_B)


def mlstm_prompt(q, k, v, og, gates, alog_row, prow, norm_g, bsz):
    m = q.shape[0]
    rows = AB_CHUNKS_PER_STEP * CHUNK
    nc = m // bsz // rows
    row = lambda w: pl.BlockSpec((rows, w), lambda b, c: (b * nc + c, 0))
    full = lambda shape: pl.BlockSpec(shape, lambda b, c: (0,) * len(shape))
    return pl.pallas_call(
        _mlstm_prompt_kernel, grid=(bsz, nc),
        in_specs=[row(H_B * DQK_B), row(H_B * DQK_B), row(H_B * DV_B), row(H_B * DV_B), row(LANES),
                  full((1, LANES)), full((1, LANES)), full((1, DV_B))],
        out_specs=[row(H_B * DV_B), pl.BlockSpec((1, H_B, DQK_B, DV_B + LANES), lambda b, c: (b, 0, 0, 0)),
                   pl.BlockSpec((1, 1, LANES), lambda b, c: (b, 0, 0))],
        out_shape=[jax.ShapeDtypeStruct((m, H_B * DV_B), F32),
                   jax.ShapeDtypeStruct((bsz, H_B, DQK_B, DV_B + LANES), F32),
                   jax.ShapeDtypeStruct((bsz, 1, LANES), F32)],
        compiler_params=_cparams("parallel", "arbitrary"), name="mlstm_prompt")(
            q, k, v, og, gates, alog_row, prow, norm_g.reshape(1, DV_B))


def _columns(x8):
    n = x8.shape[1]
    r, c = _tri_masks(n)
    return _nt((r == c).astype(F32), x8, HI)


def _ab_sample_kernel(qn_ref, kn_ref, vn_ref, cq_ref, ck_ref, cv_ref, wq_ref, wk_ref, wv_ref, z_ref,
                      qb_ref, kb_ref, vb_ref, og_ref, gt_ref, alog_ref, prow_ref, nga_ref, ngb_ref,
                      s_in, c_in, n_in, m_in,
                      oa_ref, ob_ref, s_out, c_out, n_out, m_out):
    def conv(new_ref, prev_ref, w_ref):
        acc = w_ref[CONV_W - 1] * new_ref[0]
        for i in range(CONV_W - 1):
            acc = acc + w_ref[i] * prev_ref[0, i]
        return _silu(acc)

    q8 = conv(qn_ref, cq_ref, wq_ref)
    k8 = conv(kn_ref, ck_ref, wk_ref)
    v8 = conv(vn_ref, cv_ref, wv_ref)
    q8 = q8 * lax.rsqrt(jnp.sum(q8 * q8, axis=-1, keepdims=True) + NORM_EPS) * (DK_A ** -0.5)
    k8 = k8 * lax.rsqrt(jnp.sum(k8 * k8, axis=-1, keepdims=True) + NORM_EPS)
    beta, g, ipre, logf = _gate_tile(gt_ref[0], alog_ref[...], prow_ref[...])
    q_cols, k_cols = _columns(q8), _columns(k8)
    z8 = z_ref[0]
    outs = []
    for h in range(H_A):
        s = s_in[0, h] * jnp.exp(g[:, LANE_A + h:LANE_A + h + 1])
        kc = k_cols[:, h:h + 1]
        err = v8[h:h + 1] - jnp.sum(kc * s, axis=0, keepdims=True)
        s = s + kc * (beta[:, LANE_BETA + h:LANE_BETA + h + 1] * err)
        s_out[0, h] = s
        outs.append(jnp.sum(q_cols[:, h:h + 1] * s, axis=0, keepdims=True))
    o = jnp.concatenate(outs, axis=0)
    o = o * lax.rsqrt(jnp.mean(o * o, axis=-1, keepdims=True) + NORM_EPS) * nga_ref[...]
    oa_ref[0] = o * _silu(z8)

    zeros4 = jnp.zeros((8 - H_B, DQK_B), F32)
    qb_cols = _columns(jnp.concatenate([qb_ref[0], zeros4], axis=0))
    kb_cols = _columns(jnp.concatenate([kb_ref[0] * (DQK_B ** -0.5), zeros4], axis=0))
    vb = vb_ref[0]
    n_cols = n_in[0]
    m_row = m_in[0]
    lane_n = lax.broadcasted_iota(jnp.int32, n_cols.shape, 1)
    lane_m = lax.broadcasted_iota(jnp.int32, m_row.shape, 1)
    outs = []
    for h in range(H_B):
        lf = logf[:, LANE_F + h:LANE_F + h + 1]
        it = ipre[:, LANE_I + h:LANE_I + h + 1]
        m_prev = m_row[:, h:h + 1]
        m_new = jnp.maximum(lf + m_prev, it)
        f_sc = jnp.exp(lf + m_prev - m_new)
        i_sc = jnp.exp(it - m_new)
        kc = kb_cols[:, h:h + 1]
        qc = qb_cols[:, h:h + 1]
        cm = f_sc * c_in[0, h] + i_sc * (kc * vb[h:h + 1])
        nn = f_sc * n_cols[:, h:h + 1] + i_sc * kc
        c_out[0, h] = cm
        n_cols = jnp.where(lane_n == h, nn, n_cols)
        m_row = jnp.where(lane_m == h, m_new, m_row)
        num = jnp.sum(qc * cm, axis=0, keepdims=True)
        den = jnp.sum(qc * nn, axis=0, keepdims=True)
        outs.append(num / jnp.maximum(jnp.abs(den), jnp.exp(-m_new)))
    hb = jnp.concatenate(outs, axis=0)
    hb = hb * lax.rsqrt(jnp.mean(hb * hb, axis=-1, keepdims=True) + NORM_EPS) * ngb_ref[...]
    ob_ref[0] = jax.nn.sigmoid(og_ref[0]) * hb
    n_out[0] = n_cols
    m_out[0] = m_row


def ab_sample(qkv, z, q_b, k_b, v_b, o_b, gates, conv_prev, conv_w, alog_row, prow, norm_g_a, norm_g_b,
              s_prev, c_prev, n_prev, m_prev):
    bsz = qkv.shape[0]
    hk = H_A * DK_A
    part = lambda x, i, w: x[..., i * hk:i * hk + H_A * w].reshape(x.shape[:-1] + (H_A, w))
    new_parts = [part(qkv, 0, DK_A), part(qkv, 1, DK_A), part(qkv, 2, DV_A)]
    prev_parts = [part(conv_prev, 0, DK_A), part(conv_prev, 1, DK_A), part(conv_prev, 2, DV_A)]
    w_parts = [part(conv_w, 0, DK_A), part(conv_w, 1, DK_A), part(conv_w, 2, DV_A)]
    args = new_parts + prev_parts + w_parts + [
        z.reshape(bsz, H_A, DV_A), q_b.reshape(bsz, H_B, DQK_B), k_b.reshape(bsz, H_B, DQK_B),
        v_b.reshape(bsz, H_B, DV_B), o_b.reshape(bsz, H_B, DV_B), gates.reshape(bsz, 1, LANES),
        alog_row, prow, norm_g_a.reshape(1, DV_A), norm_g_b.reshape(1, DV_B),
        s_prev, c_prev, jnp.swapaxes(n_prev, 1, 2), m_prev.reshape(bsz, 1, H_B)]

    def spec(x, batched):
        nd = x.ndim
        if batched:
            return pl.BlockSpec((1,) + x.shape[1:], lambda b: (b,) + (0,) * (nd - 1))
        return pl.BlockSpec(x.shape, lambda b: (0,) * nd)

    batched = [True] * 6 + [False] * 3 + [True] * 6 + [False] * 4 + [True] * 4
    out_shape = [jax.ShapeDtypeStruct((bsz, H_A, DV_A), F32), jax.ShapeDtypeStruct((bsz, H_B, DV_B), F32),
                 jax.ShapeDtypeStruct(s_prev.shape, F32), jax.ShapeDtypeStruct(c_prev.shape, F32),
                 jax.ShapeDtypeStruct((bsz, DQK_B, H_B), F32), jax.ShapeDtypeStruct((bsz, 1, H_B), F32)]
    oa, ob, s_new, c_new, n_new, m_new = pl.pallas_call(
        _ab_sample_kernel, grid=(bsz,),
        in_specs=[spec(x, bt) for x, bt in zip(args, batched)],
        out_specs=[spec(x, True) for x in out_shape], out_shape=out_shape,
        compiler_params=_cparams("parallel"), name="ab_sample")(*args)
    mix = jnp.concatenate([oa.reshape(bsz, H_A * DV_A), ob.reshape(bsz, H_B * DV_B)], axis=-1)
    return mix, s_new, c_new, jnp.swapaxes(n_new, 1, 2), m_new.reshape(bsz, H_B)


MOE_BLOCK_PROMPT = 256
MOE_BLOCK_SAMPLE = 32


def _ab_weights(w_in, a_log, dt_bias, b_i, b_f):
    sizes = (QKV_A, H_A * DV_A, H_A, H_A, H_B * DQK_B, H_B * DQK_B, H_B * DV_B, H_B * DV_B, H_B, H_B)
    offs = [0]
    for s in sizes:
        offs.append(offs[-1] + s)
    col = lambda i: w_in[:, offs[i]:offs[i + 1]]
    w_lo = tuple(col(i).astype(BF16) for i in (0, 1, 4, 5, 6, 7))
    w_gate = jnp.concatenate([col(2), col(3), col(8), col(9)], axis=1)
    w_gate = jnp.pad(w_gate, ((0, 0), (0, LANES - w_gate.shape[1])))
    zeros = lambda n: jnp.zeros((n,), F32)
    pad = LANES - 2 * H_A - 2 * H_B
    alog_row = jnp.concatenate([zeros(H_A), a_log.astype(F32), zeros(2 * H_B + pad)]).reshape(1, LANES)
    prow = jnp.concatenate([zeros(H_A), dt_bias.astype(F32), b_i.astype(F32), b_f.astype(F32),
                            zeros(pad)]).reshape(1, LANES)
    return w_lo, w_gate, alog_row, prow


def kernel(x_prompt, x_sample, state_delta_S, state_delta_conv, state_mlstm_C, state_mlstm_n, state_mlstm_m,
           cache_diff_k, cache_diff_v, cache_mem_k, cache_mem_v, page_table, mem_prompt,
           w_in_ab, conv_w_a, a_log_a, dt_bias_a, norm_g_a, b_i_b, b_f_b, norm_g_b, w_out_ab,
           w_qkv_c, lam_q1, lam_k1, lam_q2, lam_k2, subln_g_c, w_o_c, rel_bias,
           w_xq, w_xkv, w_xo, ln_g, ln_b, w_router, b_router, w_moe_in, b_moe_in, w_moe_out, b_moe_out):
    bp, t, d = x_prompt.shape
    bs = x_sample.shape[0]
    xp = x_prompt.reshape(bp * t, d)
    xs = x_sample.reshape(bs, d)
    mem2d = mem_prompt.reshape(bp * N_MEM, d)
    cmk = cache_mem_k.reshape(DEPTH * bs, N_MEM * H_X, DH_X)
    cmv = cache_mem_v.reshape(DEPTH * bs, N_MEM * H_X, DH_X)
    wm_in = w_moe_in.reshape(DEPTH * N_EXPERTS, d, 2 * D_FF)
    bm_in = b_moe_in.reshape(DEPTH * N_EXPERTS, 2 * D_FF)
    wm_out = w_moe_out.reshape(DEPTH * N_EXPERTS, D_FF, d)
    bm_out = b_moe_out.reshape(DEPTH * N_EXPERTS, d)
    p_S, p_conv, p_C, p_n, p_m, p_k, p_v, p_mk, p_mv = [], [], [], [], [], [], [], [], []
    s_S, s_conv, s_C, s_n, s_m, s_k, s_v = [], [], [], [], [], [], []
    for layer in range(DEPTH):
        j = layer // 2
        g0, b0 = ln_g[layer, 0].reshape(1, d), ln_b[layer, 0].reshape(1, d)
        g1, b1 = ln_g[layer, 1].reshape(1, d), ln_b[layer, 1].reshape(1, d)
        g2, b2 = ln_g[layer, 2].reshape(1, d), ln_b[layer, 2].reshape(1, d)
        if layer % 2 == 0:
            w_lo, w_gate, alog_row, prow = _ab_weights(w_in_ab[j], a_log_a[j], dt_bias_a[j], b_i_b[j], b_f_b[j])
            w_out = w_out_ab[j].astype(BF16)
            w_out_a, w_out_b = w_out[:H_A * DV_A], w_out[H_A * DV_A:]
            qkv, z, q_b, k_b, v_b, o_b, gates = mm_multi(xp, w_lo, (w_gate,), tm=256)
            mix_a, st_S = gdn_prompt(qkv, z, gates, conv_w_a[j], alog_row, prow, norm_g_a[j], bp)
            mix_b, c_ext, m_row = mlstm_prompt(q_b, k_b, v_b, o_b, gates, alog_row, prow, norm_g_b[j], bp)
            xp = proj_ln([mix_a, mix_b], [w_out_a, w_out_b], xp, g0, b0)
            p_S.append(st_S)
            p_conv.append(qkv.reshape(bp, t, QKV_A)[:, t - (CONV_W - 1):])
            p_C.append(c_ext[..., :DV_B])
            p_n.append(c_ext[..., DV_B])
            p_m.append(m_row[:, 0, :H_B])
            qkv, z, q_b, k_b, v_b, o_b, gates = mm_multi(xs, w_lo, (w_gate,))
            mix, st_S, st_C, st_n, st_m = ab_sample(
                qkv, z, q_b, k_b, v_b, o_b, gates, state_delta_conv[j], conv_w_a[j], alog_row, prow,
                norm_g_a[j], norm_g_b[j], state_delta_S[j].astype(F32), state_mlstm_C[j].astype(F32),
                state_mlstm_n[j].astype(F32), state_mlstm_m[j].astype(F32))
            xs = proj_ln([mix], [w_out], xs, g0, b0)
            s_S.append(st_S)
            s_conv.append(jnp.concatenate([state_delta_conv[j][:, 1:].astype(F32), qkv[:, None, :]], axis=1))
            s_C.append(st_C)
            s_n.append(st_n)
            s_m.append(st_m)
        else:
            lam_init = 0.8 - 0.6 * math.exp(-0.3 * layer)
            lam = (jnp.exp(jnp.sum(lam_q1[j].astype(F32) * lam_k1[j].astype(F32)))
                   - jnp.exp(jnp.sum(lam_q2[j].astype(F32) * lam_k2[j].astype(F32))) + lam_init)
            w_qkv = tuple(w_qkv_c[j][:, i * C_W:(i + 1) * C_W].astype(BF16) for i in range(3))
            w_o = w_o_c[j].astype(BF16)
            q, k, v = mm_multi(xp, w_qkv, tm=512)
            o = diff_attn_prompt_core(q, k, v, bp, rel_bias, lam, lam_init, subln_g_c[j])
            xp = proj_ln([o], [w_o], xp, g0, b0)
            p_k.append(k.reshape(bp, t, H_C, 2 * DH_C))
            p_v.append(v.reshape(bp, t, H_C, VD_C))
            q, k, v = mm_multi(xs, w_qkv)
            o = diff_attn_sample_core(q, k, v, cache_diff_k, cache_diff_v, page_table, j, rel_bias, lam, lam_init,
                                      subln_g_c[j])
            xs = proj_ln([o], [w_o], xs, g0, b0)
            s_k.append(k.reshape(bs, 1, H_C, 2 * DH_C))
            s_v.append(v.reshape(bs, 1, H_C, VD_C))
        w_q, w_o = w_xq[layer].astype(BF16), w_xo[layer].astype(BF16)
        w_r = jnp.pad(w_router[layer].astype(F32), ((0, 0), (0, LANES - N_EXPERTS)))
        w_r_hi = w_r.astype(BF16)
        w_r = jnp.stack([w_r_hi, (w_r - w_r_hi.astype(F32)).astype(BF16)])
        b_r = jnp.pad(b_router[layer].astype(F32), (0, LANES - N_EXPERTS)).reshape(1, LANES)
        mk, mv = mm_multi(mem2d, (w_xkv[layer][:, :X_W].astype(BF16), w_xkv[layer][:, X_W:].astype(BF16)))
        p_mk.append(mk.reshape(bp, N_MEM, H_X, DH_X))
        p_mv.append(mv.reshape(bp, N_MEM, H_X, DH_X))
        xp, xp_tiles, idx_p, gate_p = xattn_prompt(xp, mk.reshape(bp, N_MEM, X_W), mv.reshape(bp, N_MEM, X_W),
                                                   w_q, w_o, g1, b1, w_r, b_r)
        (q,) = mm_multi(xs, (w_q,))
        o = xattn_sample_core(q, cmk, cmv, off=layer * bs)
        xs, xs_tiles, idx_s, gate_s = proj_ln_route(o, w_o, xs, g1, b1, w_r, b_r)
        xp = moe_ln(xp, xp_tiles, idx_p, gate_p, wm_in, bm_in, wm_out, bm_out, g2, b2, MOE_BLOCK_PROMPT,
                    e_off=layer * N_EXPERTS)
        xs = moe_ln(xs, xs_tiles, idx_s, gate_s, wm_in, bm_in, wm_out, bm_out, g2, b2, MOE_BLOCK_SAMPLE,
                    e_off=layer * N_EXPERTS)

    return (xp.reshape(bp, t, d), xs.reshape(bs, 1, d),
            jnp.stack(p_S), jnp.stack(p_conv), jnp.stack(p_C), jnp.stack(p_n), jnp.stack(p_m),
            jnp.stack(p_k, axis=1), jnp.stack(p_v, axis=1), jnp.stack(p_mk), jnp.stack(p_mv),
            jnp.stack(s_S), jnp.stack(s_conv), jnp.stack(s_C), jnp.stack(s_n), jnp.stack(s_m),
            jnp.stack(s_k, axis=1), jnp.stack(s_v, axis=1))
```

```python
import functools
import math

import jax
import jax.numpy as jnp
from jax import lax
from jax.experimental import pallas as pl
from jax.experimental.pallas import tpu as pltpu

F32 = jnp.float32
BF16 = jnp.bfloat16
HI = lax.Precision.HIGHEST

D_MODEL = 1024
DEPTH = 2
H_A, DK_A, DV_A, CONV_W, CHUNK = 8, 64, 64, 4, 64
H_B, DQK_B, DV_B = 4, 64, 128
QKV_A = H_A * (2 * DK_A + DV_A)
H_C, DH_C = 8, 64
VD_C = 2 * DH_C
C_W = H_C * 2 * DH_C
N_BUCKETS, MAX_DISTANCE = 32, 128
N_MEM, H_X, DH_X = 256, 4, 128
X_W = H_X * DH_X
N_EXPERTS, TOP_K = 32, 4
D_FF = D_MODEL
SWIGLU_ALPHA, SWIGLU_LIMIT = 1.702, 7.0
DN_ALPHA = (2 * DEPTH) ** 0.25
LN_EPS = 1e-5
NORM_EPS = 1e-6

LANES = 128
SUBLANES = 8
VMEM_PHYSICAL = 64 * 1024 * 1024
VMEM_LIMIT = 48 * 1024 * 1024


def _cparams(*sem):
    return pltpu.CompilerParams(dimension_semantics=tuple(sem), vmem_limit_bytes=VMEM_LIMIT)


def _layer_norm(y, g, b):
    mu = jnp.mean(y, axis=-1, keepdims=True)
    d = y - mu
    var = jnp.mean(d * d, axis=-1, keepdims=True)
    return d * lax.rsqrt(var + LN_EPS) * g + b


def _mm_multi_kernel(x_ref, *refs, n_lo, n_hi):
    n = n_lo + n_hi
    ws, outs = refs[:n], refs[n:]
    x = x_ref[...]
    xb = x.astype(BF16)
    for i in range(n_lo):
        outs[i][...] = jnp.dot(xb, ws[i][...], preferred_element_type=F32)
    for i in range(n_lo, n):
        outs[i][...] = jnp.dot(x, ws[i][...], preferred_element_type=F32, precision=HI)


def mm_multi(x, w_lo, w_hi=(), tm=256):
    m, k = x.shape
    tm = min(tm, m)
    assert m % tm == 0
    ws = tuple(w_lo) + tuple(w_hi)
    in_specs = [pl.BlockSpec((tm, k), lambda i: (i, 0))]
    in_specs += [pl.BlockSpec(w.shape, lambda i: (0, 0)) for w in ws]
    out_specs = [pl.BlockSpec((tm, w.shape[1]), lambda i: (i, 0)) for w in ws]
    out_shape = [jax.ShapeDtypeStruct((m, w.shape[1]), F32) for w in ws]
    return pl.pallas_call(
        functools.partial(_mm_multi_kernel, n_lo=len(w_lo), n_hi=len(w_hi)),
        grid=(m // tm,), in_specs=in_specs, out_specs=out_specs, out_shape=out_shape,
        compiler_params=_cparams("parallel"), name="mm_multi")(x, *ws)


def _proj_ln_kernel(*refs, n):
    a_refs, w_refs = refs[:n], refs[n:2 * n]
    x_ref, g_ref, b_ref, o_ref = refs[2 * n:]
    h = jnp.dot(a_refs[0][...].astype(BF16), w_refs[0][...], preferred_element_type=F32)
    for a_ref, w_ref in zip(a_refs[1:], w_refs[1:]):
        h = h + jnp.dot(a_ref[...].astype(BF16), w_ref[...], preferred_element_type=F32)
    o_ref[...] = _layer_norm(DN_ALPHA * x_ref[...] + h, g_ref[...], b_ref[...])


def proj_ln(a_list, w_list, x_res, g, b, tm=512):
    m, d = x_res.shape
    tm = min(tm, m)
    assert m % tm == 0
    n = len(a_list)
    return pl.pallas_call(
        functools.partial(_proj_ln_kernel, n=n), grid=(m // tm,),
        in_specs=[pl.BlockSpec((tm, a.shape[1]), lambda i: (i, 0)) for a in a_list]
        + [pl.BlockSpec(w.shape, lambda i: (0, 0)) for w in w_list]
        + [pl.BlockSpec((tm, d), lambda i: (i, 0)), pl.BlockSpec((1, d), lambda i: (0, 0)),
           pl.BlockSpec((1, d), lambda i: (0, 0))],
        out_specs=pl.BlockSpec((tm, d), lambda i: (i, 0)),
        out_shape=jax.ShapeDtypeStruct((m, d), F32),
        compiler_params=_cparams("parallel"), name="proj_ln")(*a_list, *w_list, x_res, g, b)


def _mem_attention(q, mk, mv):
    outs = []
    for h in range(H_X):
        sl = slice(h * DH_X, (h + 1) * DH_X)
        s = lax.dot_general(q[:, sl].astype(BF16), mk[:, sl].astype(BF16), (((1,), (1,)), ((), ())),
                            preferred_element_type=F32) * (DH_X ** -0.5)
        s = s - jnp.max(s, axis=-1, keepdims=True)
        p = jnp.exp(s)
        p = p / jnp.sum(p, axis=-1, keepdims=True)
        outs.append(jnp.dot(p.astype(BF16), mv[:, sl].astype(BF16), preferred_element_type=F32))
    return jnp.concatenate(outs, axis=-1)


def _route(x, wr, br):
    x_hi = x.astype(BF16)
    x_lo = (x - x_hi.astype(F32)).astype(BF16)
    logits = (jnp.dot(x_hi, wr[0], preferred_element_type=F32) + jnp.dot(x_lo, wr[0], preferred_element_type=F32)
              + jnp.dot(x_hi, wr[1], preferred_element_type=F32) + br)
    lane = lax.broadcasted_iota(jnp.int32, logits.shape, 1)
    work = jnp.where(lane < N_EXPERTS, logits, -jnp.inf)
    idx_out = jnp.zeros(logits.shape, jnp.int32)
    val_out = jnp.full(logits.shape, -jnp.inf, F32)
    for k in range(TOP_K):
        m = jnp.max(work, axis=-1, keepdims=True)
        sel = jnp.min(jnp.where(work == m, lane, LANES), axis=-1, keepdims=True)
        idx_out = jnp.where(lane == k, sel, idx_out)
        val_out = jnp.where(lane == k, m, val_out)
        work = jnp.where(lane == sel, -jnp.inf, work)
    e = jnp.exp(val_out - jnp.max(val_out, axis=-1, keepdims=True))
    gates = e / jnp.sum(e, axis=-1, keepdims=True)
    return idx_out, gates


def _store_token_tiles(o3_ref, y):
    for s in range(y.shape[1] // LANES):
        o3_ref[:, s, :] = y[:, s * LANES:(s + 1) * LANES]


def _xattn_prompt_kernel(x_ref, mk_ref, mv_ref, wq_ref, wo_ref, g_ref, b_ref, wr_ref, br_ref,
                         o_ref, o3_ref, idx_ref, gate_ref):
    x = x_ref[...]
    q = jnp.dot(x.astype(BF16), wq_ref[...], preferred_element_type=F32)
    o = _mem_attention(q, mk_ref[0], mv_ref[0])
    h = jnp.dot(o.astype(BF16), wo_ref[...], preferred_element_type=F32)
    y = _layer_norm(DN_ALPHA * x + h, g_ref[...], b_ref[...])
    o_ref[...] = y
    _store_token_tiles(o3_ref, y)
    idx, gates = _route(y, wr_ref[...], br_ref[...])
    idx_ref[...] = idx
    gate_ref[...] = gates


def xattn_prompt(x, mk, mv, wq, wo, g, b, wr, br, tq=512):
    m, d = x.shape
    bsz = mk.shape[0]
    t = m // bsz
    nq = t // tq
    full = lambda shape: pl.BlockSpec(shape, lambda bi, qi: (0,) * len(shape))
    row = lambda w: pl.BlockSpec((tq, w), lambda bi, qi: (bi * nq + qi, 0))
    return pl.pallas_call(
        _xattn_prompt_kernel, grid=(bsz, nq),
        in_specs=[row(d), pl.BlockSpec((1, N_MEM, X_W), lambda bi, qi: (bi, 0, 0)),
                  pl.BlockSpec((1, N_MEM, X_W), lambda bi, qi: (bi, 0, 0)),
                  full(wq.shape), full(wo.shape), full(g.shape), full(b.shape), full(wr.shape), full(br.shape)],
        out_specs=[row(d), pl.BlockSpec((tq, d // LANES, LANES), lambda bi, qi: (bi * nq + qi, 0, 0)),
                   row(LANES), row(LANES)],
        out_shape=[jax.ShapeDtypeStruct((m, d), F32), jax.ShapeDtypeStruct((m, d // LANES, LANES), F32),
                   jax.ShapeDtypeStruct((m, LANES), jnp.int32), jax.ShapeDtypeStruct((m, LANES), F32)],
        compiler_params=_cparams("parallel", "parallel"), name="xattn_prompt")(x, mk, mv, wq, wo, g, b, wr, br)


def _xattn_sample_kernel(q_ref, mk_ref, mv_ref, o_ref):
    q = jnp.broadcast_to(q_ref[0], (8, X_W))
    mk = jnp.concatenate([mk_ref[0, pl.ds(h, N_MEM, stride=H_X), :] for h in range(H_X)], axis=-1)
    mv = jnp.concatenate([mv_ref[0, pl.ds(h, N_MEM, stride=H_X), :] for h in range(H_X)], axis=-1)
    o = _mem_attention(q, mk, mv)
    o_ref[0] = o[0:1]


def xattn_sample_core(q, mk, mv, off=0):
    bsz = q.shape[0]
    spec3 = pl.BlockSpec((1, N_MEM * H_X, DH_X), lambda bi: (off + bi, 0, 0))
    out = pl.pallas_call(
        _xattn_sample_kernel, grid=(bsz,),
        in_specs=[pl.BlockSpec((1, 1, X_W), lambda bi: (bi, 0, 0)), spec3, spec3],
        out_specs=pl.BlockSpec((1, 1, X_W), lambda bi: (bi, 0, 0)),
        out_shape=jax.ShapeDtypeStruct((bsz, 1, X_W), F32),
        compiler_params=_cparams("parallel"), name="xattn_sample")(q.reshape(bsz, 1, X_W), mk, mv)
    return out.reshape(bsz, X_W)


def _ln_route_kernel(a_ref, w_ref, x_ref, g_ref, b_ref, wr_ref, br_ref, o_ref, o3_ref, idx_ref, gate_ref):
    h = jnp.dot(a_ref[...].astype(BF16), w_ref[...], preferred_element_type=F32)
    y = _layer_norm(DN_ALPHA * x_ref[...] + h, g_ref[...], b_ref[...])
    o_ref[...] = y
    _store_token_tiles(o3_ref, y)
    idx, gates = _route(y, wr_ref[...], br_ref[...])
    idx_ref[...] = idx
    gate_ref[...] = gates


def proj_ln_route(a, w, x_res, g, b, wr, br):
    m, d = x_res.shape
    return pl.pallas_call(
        _ln_route_kernel,
        out_shape=[jax.ShapeDtypeStruct((m, d), F32), jax.ShapeDtypeStruct((m, d // LANES, LANES), F32),
                   jax.ShapeDtypeStruct((m, LANES), jnp.int32), jax.ShapeDtypeStruct((m, LANES), F32)],
        compiler_params=pltpu.CompilerParams(vmem_limit_bytes=VMEM_LIMIT),
        name="proj_ln_route")(a, w, x_res, g, b, wr, br)


def _expert_onehots(idx):
    lane = lax.broadcasted_iota(jnp.int32, idx.shape, 1)
    onehots = [(idx[:, k:k + 1] == lane).astype(F32) for k in range(TOP_K)]
    tot = onehots[0]
    for k in range(1, TOP_K):
        tot = tot + onehots[k]
    return lane, onehots, tot


def _count_kernel(idx_ref, cnt_ref):
    @pl.when(pl.program_id(0) == 0)
    def _():
        cnt_ref[...] = jnp.zeros_like(cnt_ref)

    _, _, tot = _expert_onehots(idx_ref[...])
    cnt_ref[...] = cnt_ref[...] + jnp.sum(tot, axis=0, keepdims=True)


def _dest_kernel(idx_ref, base0_ref, dest_ref, base_ref):
    @pl.when(pl.program_id(0) == 0)
    def _():
        base_ref[...] = base0_ref[...]

    idx = idx_ref[...]
    tr = idx.shape[0]
    lane, onehots, tot = _expert_onehots(idx)
    r = lax.broadcasted_iota(jnp.int32, (tr, tr), 0)
    c = lax.broadcasted_iota(jnp.int32, (tr, tr), 1)
    strict = (c < r).astype(BF16)
    before = jnp.dot(strict, tot.astype(BF16), preferred_element_type=F32) + base_ref[...]
    out = jnp.zeros(idx.shape, F32)
    for k in range(TOP_K):
        out = jnp.where(lane == k, jnp.sum(onehots[k] * before, axis=-1, keepdims=True), out)
    dest_ref[...] = out.astype(jnp.int32)
    base_ref[...] = base_ref[...] + jnp.sum(tot, axis=0, keepdims=True)


def route_slots(idx, bm, tr=256):
    t = idx.shape[0]
    tr = min(tr, t)
    assert t % tr == 0
    rows = pl.BlockSpec((tr, LANES), lambda i: (i, 0))
    one = pl.BlockSpec((1, LANES), lambda i: (0, 0))
    counts = pl.pallas_call(
        _count_kernel, grid=(t // tr,), in_specs=[rows], out_specs=one,
        out_shape=jax.ShapeDtypeStruct((1, LANES), F32),
        compiler_params=_cparams("arbitrary"), name="route_count")(idx)
    counts = counts[0].astype(jnp.int32)
    padded = (counts + bm - 1) // bm * bm
    pad_end = jnp.cumsum(padded)
    base0 = (pad_end - padded).astype(F32).reshape(1, LANES)
    dest = pl.pallas_call(
        _dest_kernel, grid=(t // tr,), in_specs=[rows, one], out_specs=rows,
        out_shape=jax.ShapeDtypeStruct((t, LANES), jnp.int32),
        scratch_shapes=[pltpu.VMEM((1, LANES), F32)],
        compiler_params=_cparams("arbitrary"), name="route_dest")(idx, base0)
    return dest, pad_end[:N_EXPERTS], counts[:N_EXPERTS]


def _clamped_swiglu(glu, lin):
    glu = jnp.minimum(glu, SWIGLU_LIMIT)
    lin = jnp.clip(lin, -SWIGLU_LIMIT, SWIGLU_LIMIT)
    return glu * jax.nn.sigmoid(SWIGLU_ALPHA * glu) * (lin + 1.0)


def _tile_rows(buf):
    rows = buf.shape[0] * buf.shape[2]
    return jnp.concatenate([buf[:, j].reshape(rows, LANES) for j in range(buf.shape[1])], axis=-1)


def _moe_gmm_kernel(bexp_ref, nused_ref, tok_ref, tok1_ref, tok2_ref, x_hbm, win_ref, bin_ref, wout_ref,
                    bout_ref, y_hbm, xbuf, ybuf, winb, woutb, gsem, osem, *, bm):
    i = pl.program_id(0)
    n = pl.num_programs(0)
    nused = nused_ref[0]
    slot = lax.rem(i, 2)
    xslot = lax.rem(i, 3)
    n_sub = x_hbm.shape[1]
    groups = bm // SUBLANES

    def start_gather(idx_ref, s):
        def body(g, carry):
            for u in range(SUBLANES):
                pltpu.make_async_copy(x_hbm.at[idx_ref[0, 0, g * SUBLANES + u]], xbuf.at[s, g, :, u],
                                      gsem.at[s]).start(priority=u % 2)
            return carry
        lax.fori_loop(0, groups, body, 0)

    def wait_gather(s):
        pltpu.make_async_copy(xbuf.at[s], xbuf.at[s], gsem.at[s]).wait()

    def out_copies(blk, s):
        return [pltpu.make_async_copy(ybuf.at[s, :, j], y_hbm.at[pl.ds(blk * groups, groups), :, j], osem.at[s])
                for j in range(n_sub)]

    @pl.when(i == 0)
    def _():
        start_gather(tok_ref, 0)

        @pl.when(jnp.logical_and(1 < n, 1 < nused))
        def _():
            start_gather(tok1_ref, 1)

    active = i < nused
    prefetch = jnp.logical_and(i + 2 < n, i + 2 < nused)

    @pl.when(jnp.logical_and(active, jnp.logical_or(i == 0, bexp_ref[i] != bexp_ref[jnp.maximum(i - 1, 0)])))
    def _():
        winb[...] = win_ref[0].astype(BF16)
        woutb[...] = wout_ref[0].astype(BF16)

    @pl.when(i >= 2)
    def _():
        for cp in out_copies(i - 2, slot):
            cp.wait()

    def expert_mlp(gather_ahead):
        wait_gather(xslot)
        xb = _tile_rows(xbuf[xslot]).astype(BF16)
        if gather_ahead:
            nslot = lax.rem(i + 2, 3)
            for g in range(groups):
                for u in range(SUBLANES):
                    pltpu.make_async_copy(x_hbm.at[tok2_ref[0, 0, g * SUBLANES + u]], xbuf.at[nslot, g, :, u],
                                          gsem.at[nslot]).start(priority=u % 2)
        h = jnp.dot(xb, winb[...], preferred_element_type=F32) + bin_ref[0]
        act = _clamped_swiglu(h[:, :D_FF], h[:, D_FF:]).astype(BF16)
        y = jnp.dot(act, woutb[...], preferred_element_type=F32) + bout_ref[0]
        for j in range(n_sub):
            ybuf[slot, :, j] = y[:, j * LANES:(j + 1) * LANES].reshape(groups, SUBLANES, LANES)

    @pl.when(jnp.logical_and(active, prefetch))
    def _():
        expert_mlp(True)

    @pl.when(jnp.logical_and(active, jnp.logical_not(prefetch)))
    def _():
        expert_mlp(False)

    @pl.when(jnp.logical_not(active))
    def _():
        ybuf[slot] = jnp.zeros(ybuf.shape[1:], F32)

    for cp in out_copies(i, slot):
        cp.start()

    @pl.when(i == n - 1)
    def _():
        @pl.when(n >= 2)
        def _():
            for cp in out_copies(i - 1, 1 - slot):
                cp.wait()

        for cp in out_copies(i, slot):
            cp.wait()


def moe_gmm(x, row_tok, block_expert, n_used, w_in, b_in, w_out, b_out, bm):
    n_blocks = row_tok.shape[0]
    n_sub = x.shape[1]
    d = n_sub * LANES
    idx_spec = lambda off: pl.BlockSpec(
        (1, 1, bm), lambda i, be, nu: (jnp.minimum(i + off, n_blocks - 1), 0, 0), memory_space=pltpu.SMEM)
    ex = lambda i, be, nu: (be[i], 0, 0)
    grid_spec = pltpu.PrefetchScalarGridSpec(
        num_scalar_prefetch=2, grid=(n_blocks,),
        in_specs=[idx_spec(0), idx_spec(1), idx_spec(2),
                  pl.BlockSpec(memory_space=pl.ANY),
                  pl.BlockSpec((1, d, 2 * D_FF), ex), pl.BlockSpec((1, 1, 2 * D_FF), ex),
                  pl.BlockSpec((1, D_FF, d), ex), pl.BlockSpec((1, 1, d), ex)],
        out_specs=pl.BlockSpec(memory_space=pl.ANY),
        scratch_shapes=[pltpu.VMEM((3, bm // SUBLANES, n_sub, SUBLANES, LANES), F32),
                        pltpu.VMEM((2, bm // SUBLANES, n_sub, SUBLANES, LANES), F32),
                        pltpu.VMEM((d, 2 * D_FF), BF16), pltpu.VMEM((D_FF, d), BF16),
                        pltpu.SemaphoreType.DMA((3,)), pltpu.SemaphoreType.DMA((2,))])
    w_elems = d * 2 * D_FF + D_FF * d
    vmem_bytes = (2 * 4 + 2) * w_elems + 5 * bm * d * 4 + bm * (2 * D_FF + D_FF + 2 * d) * 4
    vmem_bytes = min(vmem_bytes + (4 << 20), VMEM_PHYSICAL - (6 << 20))
    return pl.pallas_call(
        functools.partial(_moe_gmm_kernel, bm=bm), grid_spec=grid_spec,
        out_shape=jax.ShapeDtypeStruct((n_blocks * bm // SUBLANES, SUBLANES, n_sub, LANES), F32),
        compiler_params=pltpu.CompilerParams(dimension_semantics=("arbitrary",), vmem_limit_bytes=vmem_bytes),
        name="moe_gmm")(
            block_expert, n_used, row_tok, row_tok, row_tok, x, w_in, b_in.reshape(b_in.shape[0], 1, -1),
            w_out, b_out.reshape(b_out.shape[0], 1, -1))


def _combine_ln_kernel(dst_ref, dst_next_ref, y_hbm, gate_ref, x_ref, g_ref, b_ref, o_ref, gbuf, sem):
    i = pl.program_id(0)
    n = pl.num_programs(0)
    slot = lax.rem(i, 2)
    groups = gbuf.shape[2]

    def start_gather(idx_ref, s):
        def body(g, carry):
            for u in range(SUBLANES):
                for k in range(TOP_K):
                    d = idx_ref[0, 0, (g * SUBLANES + u) * TOP_K + k]
                    pltpu.make_async_copy(y_hbm.at[d], gbuf.at[s, k, g, :, u], sem.at[s]).start(priority=k % 2)
            return carry
        lax.fori_loop(0, groups, body, 0)

    @pl.when(i == 0)
    def _():
        start_gather(dst_ref, 0)

    @pl.when(i + 1 < n)
    def _():
        start_gather(dst_next_ref, 1 - slot)

    pltpu.make_async_copy(gbuf.at[slot], gbuf.at[slot], sem.at[slot]).wait()
    gates = gate_ref[...]
    acc = gates[:, 0:1] * _tile_rows(gbuf[slot, 0])
    for k in range(1, TOP_K):
        acc = acc + gates[:, k:k + 1] * _tile_rows(gbuf[slot, k])
    o_ref[...] = _layer_norm(DN_ALPHA * x_ref[...] + acc, g_ref[...], b_ref[...])


def combine_ln(y, dest, gates, x_res, g, b, tc=256):
    t, d = x_res.shape
    tc = min(tc, t)
    assert t % tc == 0
    n = t // tc
    n_sub = d // LANES
    dest_flat = dest[:, :TOP_K].reshape(n, 1, tc * TOP_K)
    idx_spec = lambda off: pl.BlockSpec((1, 1, tc * TOP_K), lambda i: (jnp.minimum(i + off, n - 1), 0, 0),
                                        memory_space=pltpu.SMEM)
    return pl.pallas_call(
        _combine_ln_kernel, grid=(n,),
        in_specs=[idx_spec(0), idx_spec(1), pl.BlockSpec(memory_space=pl.ANY),
                  pl.BlockSpec((tc, LANES), lambda i: (i, 0)),
                  pl.BlockSpec((tc, d), lambda i: (i, 0)), pl.BlockSpec((1, d), lambda i: (0, 0)),
                  pl.BlockSpec((1, d), lambda i: (0, 0))],
        out_specs=pl.BlockSpec((tc, d), lambda i: (i, 0)),
        out_shape=jax.ShapeDtypeStruct((t, d), F32),
        scratch_shapes=[pltpu.VMEM((2, TOP_K, tc // SUBLANES, n_sub, SUBLANES, LANES), F32),
                        pltpu.SemaphoreType.DMA((2,))],
        compiler_params=_cparams("arbitrary"), name="combine_ln")(
            dest_flat, dest_flat, y.reshape(-1, n_sub, LANES), gates, x_res, g, b)


def moe_ln(x, x_tiles, idx, gates, w_in, b_in, w_out, b_out, g, b, bm, e_off=0):
    t, d = x.shape
    tk = t * TOP_K
    dest, pad_end, counts = route_slots(idx, bm)
    n_blocks = -(-tk // bm) + N_EXPERTS
    n_rows = n_blocks * bm
    block_start = jnp.arange(n_blocks, dtype=jnp.int32) * bm
    block_expert = jnp.sum((pad_end[None, :] <= block_start[:, None]).astype(jnp.int32), axis=1)
    block_expert = jnp.minimum(block_expert, N_EXPERTS - 1)
    tok = jnp.arange(tk, dtype=jnp.int32) // TOP_K
    _, tok_sorted = lax.sort_key_val(dest[:, :TOP_K].reshape(tk), tok)
    pad_start = jnp.concatenate([jnp.zeros((1,), jnp.int32), pad_end[:-1]])
    used_end = pad_start + counts
    padding_before = pad_start - (jnp.cumsum(counts) - counts)
    run_start = jnp.clip(block_start - padding_before[block_expert], 0, tk)
    run_len = jnp.clip(used_end[block_expert] - block_start, 0, bm)
    tok_sorted = jnp.concatenate([tok_sorted, jnp.zeros((bm,), jnp.int32)])
    runs = jax.vmap(lambda s: lax.dynamic_slice(tok_sorted, (s,), (bm,)))(run_start)
    row_tok = jnp.where(jnp.arange(bm, dtype=jnp.int32)[None, :] < run_len[:, None], runs, 0).reshape(n_blocks, 1, bm)
    block_expert = block_expert + e_off
    n_used = (pad_end[-1] // bm).astype(jnp.int32).reshape(1)
    y = moe_gmm(x_tiles, row_tok, block_expert, n_used, w_in, b_in, w_out, b_out, bm)
    return combine_ln(y, dest, gates, x, g, b)


def _t5_causal_bucket(dist):
    n = jnp.maximum(dist, 0)
    max_exact = N_BUCKETS // 2
    nf = jnp.maximum(n, max_exact).astype(F32)
    large = max_exact + (jnp.log(nf / max_exact) / math.log(MAX_DISTANCE / max_exact)
                         * (N_BUCKETS - max_exact)).astype(jnp.int32)
    large = jnp.minimum(large, N_BUCKETS - 1)
    return jnp.where(n < max_exact, n, large)


def _split_maps(q):
    lane = lax.broadcasted_iota(jnp.int32, q.shape, 1)
    qs = q * (DH_C ** -0.5)
    return (jnp.where(lane < DH_C, qs, 0.0).astype(BF16), jnp.where(lane >= DH_C, qs, 0.0).astype(BF16))


def _diff_finish(acc1, l1, acc2, l2, lam, out_scale, g):
    o = acc1 / l1 - lam * (acc2 / l2)
    o = o * lax.rsqrt(jnp.mean(o * o, axis=-1, keepdims=True) + NORM_EPS) * g
    return o * out_scale


def _diff_prompt_kernel(sc_ref, far_ref, q_ref, k_ref, v_ref, bias_ref, g_ref, o_ref, m_ref, a_ref, *, tq, hp):
    h0 = pl.program_id(1) * hp
    qi = pl.program_id(2)
    head = lambda x, h: x[:, h * VD_C:(h + 1) * VD_C]
    q = q_ref[...]
    qm = jnp.stack([jnp.concatenate(_split_maps(head(q, h)), axis=0) for h in range(hp)], axis=0)
    m_ref[...] = jnp.full_like(m_ref, -jnp.inf)
    a_ref[...] = jnp.zeros_like(a_ref)
    ones_col = (lax.broadcasted_iota(jnp.int32, (tq, LANES), 1) == 0).astype(BF16)
    reps = tq // LANES

    def update(tiles):
        scores, values = [], []
        for kidx, bias, shift, causal in tiles:
            start = pl.multiple_of(kidx * tq, tq)
            kt = k_ref[0, pl.ds(start, tq), :].astype(BF16)
            vt = v_ref[0, pl.ds(start, tq), :].astype(BF16)
            values.append(jnp.stack([jnp.concatenate([head(vt, h), ones_col], axis=-1) for h in range(hp)], axis=0))
            s = _bmm_nt(qm, jnp.stack([head(kt, h) for h in range(hp)], axis=0))
            if bias is not None:
                s = s + jnp.concatenate([bias, bias], axis=1)
            if causal:
                r = lax.broadcasted_iota(jnp.int32, s.shape[1:], 0)
                c = lax.broadcasted_iota(jnp.int32, s.shape[1:], 1)
                s = jnp.where(c <= lax.bitwise_and(r, tq - 1), s, -jnp.inf)
            scores.append((s, shift))
        m_old = m_ref[...]
        m_new = m_old
        for s, shift in scores:
            m_new = jnp.maximum(m_new, jnp.max(s, axis=-1, keepdims=True) + shift)
        alpha = jnp.exp(m_old - m_new)
        acc = jnp.concatenate([alpha, alpha], axis=-1) * a_ref[...]
        for (s, shift), vt in zip(scores, values):
            p = jnp.exp(s - jnp.concatenate([m_new - shift] * reps, axis=-1))
            acc = acc + _bmm(p, vt)
        a_ref[...] = acc
        m_ref[...] = m_new

    far_bias = jnp.stack([jnp.full((1, 1), far_ref[h0 + h], F32) for h in range(hp)], axis=0)
    no_shift = jnp.zeros((hp, 1, 1), F32)
    n_far = jnp.maximum(qi - 1, 0)

    def far_body(j, carry):
        update([(2 * j, None, far_bias, False), (2 * j + 1, None, far_bias, False)])
        return carry

    lax.fori_loop(0, n_far // 2, far_body, 0)

    @pl.when(lax.rem(n_far, 2) == 1)
    def _():
        update([(n_far - 1, None, far_bias, False)])

    @pl.when(qi >= 1)
    def _():
        update([(qi - 1, bias_ref[:, 1], no_shift, False), (qi, bias_ref[:, 0], no_shift, True)])

    @pl.when(qi == 0)
    def _():
        update([(0, bias_ref[:, 0], no_shift, True)])

    acc = a_ref[...]
    o_ref[...] = jnp.concatenate(
        [_diff_finish(acc[h, :tq, :VD_C], acc[h, :tq, VD_C:VD_C + 1], acc[h, tq:, :VD_C], acc[h, tq:, VD_C:VD_C + 1],
                      sc_ref[0], sc_ref[1], g_ref[...]) for h in range(hp)], axis=-1)


def _bias_tiles_kernel(rb_ref, bucket_ref, o_ref):
    h = pl.program_id(0)
    b = bucket_ref[...]
    out = jnp.zeros(b.shape, F32)
    for k in range(N_BUCKETS):
        out = jnp.where(b == k, rb_ref[k * H_C + h], out)
    o_ref[0] = out


DIFF_HEADS_PER_STEP = 2


def diff_attn_prompt_core(q, k, v, bsz, rel_bias, lam, lam_init, subln_g, tq=256):
    m = q.shape[0]
    t = m // bsz
    nq = t // tq
    ii = jnp.arange(tq)[:, None]
    jj = jnp.arange(tq)[None, :]
    buckets = _t5_causal_bucket(jnp.stack([jnp.maximum(ii - jj, 0), tq + ii - jj])).astype(jnp.int32)
    bias = pl.pallas_call(
        _bias_tiles_kernel, grid=(H_C,),
        in_specs=[pl.BlockSpec(memory_space=pltpu.SMEM), pl.BlockSpec((2, tq, tq), lambda h: (0, 0, 0))],
        out_specs=pl.BlockSpec((1, 2, tq, tq), lambda h: (h, 0, 0, 0)),
        out_shape=jax.ShapeDtypeStruct((H_C, 2, tq, tq), F32),
        compiler_params=_cparams("parallel"), name="t5_bias_tiles")(rel_bias.astype(F32).reshape(-1), buckets)
    far = rel_bias[_t5_causal_bucket(jnp.array(2 * tq))].astype(F32)
    scal = jnp.stack([lam, 1.0 - lam_init]).astype(F32)
    k3 = k.reshape(bsz, t, C_W)
    v3 = v.reshape(bsz, t, C_W)
    smem = pl.BlockSpec(memory_space=pltpu.SMEM)
    hp = DIFF_HEADS_PER_STEP
    kv_spec = pl.BlockSpec((1, t, hp * VD_C), lambda b, h, i: (b, 0, h))
    row = pl.BlockSpec((tq, hp * VD_C), lambda b, h, i: (b * nq + i, h))
    stat = pltpu.VMEM((hp, 2 * tq, LANES), F32)
    acc = pltpu.VMEM((hp, 2 * tq, VD_C + LANES), F32)
    return pl.pallas_call(
        functools.partial(_diff_prompt_kernel, tq=tq, hp=hp), grid=(bsz, H_C // hp, nq),
        in_specs=[smem, smem, row, kv_spec, kv_spec,
                  pl.BlockSpec((hp, 2, tq, tq), lambda b, h, i: (h, 0, 0, 0)),
                  pl.BlockSpec((1, VD_C), lambda b, h, i: (0, 0))],
        out_specs=row, out_shape=jax.ShapeDtypeStruct((m, C_W), F32),
        scratch_shapes=[stat, acc],
        compiler_params=_cparams("parallel", "parallel", "parallel"), name="diff_attn_prompt")(
            scal, far, q, k3, v3, bias, subln_g.reshape(1, VD_C))


def _diff_sample_kernel(pt_ref, sc_ref, q_ref, kn_ref, vn_ref, bm_ref, bnew_ref, g_ref, *refs, n_pp):
    k_refs, v_refs = refs[:n_pp], refs[n_pp:2 * n_pp]
    o_ref, m_ref, l_ref, a_ref = refs[2 * n_pp:]
    p = pl.program_id(1)
    n_p = pl.num_programs(1)
    q8 = q_ref[0]
    qm = jnp.concatenate(_split_maps(q8), axis=0)

    @pl.when(p == 0)
    def _():
        m_ref[...] = jnp.full_like(m_ref, -jnp.inf)
        l_ref[...] = jnp.zeros_like(l_ref)
        a_ref[...] = jnp.zeros_like(a_ref)

    scores = []
    for j in range(n_pp):
        bias = bm_ref[jnp.where(p == n_p - 1, 1, 0)] if j == n_pp - 1 else bm_ref[0]
        k2 = k_refs[j][0, 0].reshape(-1, 2 * DH_C).astype(BF16)
        scores.append(lax.dot_general(qm, k2, (((1,), (1,)), ((), ())), preferred_element_type=F32) + bias)
    m_old = m_ref[...]
    m_new = m_old
    for s in scores:
        m_new = jnp.maximum(m_new, jnp.max(s, axis=-1, keepdims=True))
    alpha = jnp.exp(m_old - m_new)
    l_new = alpha * l_ref[...]
    a_new = alpha * a_ref[...]
    for j, s in enumerate(scores):
        pr = jnp.exp(s - m_new)
        l_new = l_new + jnp.sum(pr, axis=-1, keepdims=True)
        v2 = v_refs[j][0, 0].reshape(-1, VD_C).astype(BF16)
        a_new = a_new + jnp.dot(pr.astype(BF16), v2, preferred_element_type=F32)
    l_ref[...] = l_new
    a_ref[...] = a_new
    m_ref[...] = m_new

    @pl.when(p == n_p - 1)
    def _():
        kn = jnp.concatenate([kn_ref[0], kn_ref[0]], axis=0)
        vn = jnp.concatenate([vn_ref[0], vn_ref[0]], axis=0)
        s = jnp.sum(qm.astype(F32) * kn, axis=-1, keepdims=True) + bnew_ref[:, 0:1]
        m_old = m_ref[...]
        m_new = jnp.maximum(m_old, s)
        alpha = jnp.exp(m_old - m_new)
        pr = jnp.exp(s - m_new)
        l = alpha * l_ref[...] + pr
        a = alpha * a_ref[...] + pr * vn
        o_ref[0] = _diff_finish(a[:H_C], l[:H_C], a[H_C:], l[H_C:], sc_ref[0], sc_ref[1], g_ref[...])


def diff_attn_sample_core(q, k_new, v_new, cache_k, cache_v, page_table, layer_j, rel_bias, lam, lam_init, subln_g,
                          n_pp=8):
    bsz = q.shape[0]
    n_pages = page_table.shape[1]
    page = cache_k.shape[2]
    past = n_pages * page
    assert n_pages % n_pp == 0
    dist_last = past - ((n_pages - 1) * page + jnp.arange(page))
    b_last = rel_bias[_t5_causal_bucket(dist_last)].astype(F32)
    b_far = jnp.broadcast_to(rel_bias[_t5_causal_bucket(jnp.array(page + 1))].astype(F32), (page, H_C))
    eye = jnp.eye(H_C, dtype=bool)

    def expand(bt):
        full = jnp.where(eye[:, None, :], bt.T[:, :, None], -jnp.inf).reshape(H_C, page * H_C)
        return jnp.concatenate([full, full], axis=0)

    bm = jnp.stack([expand(b_far), expand(b_last)])
    b_new = rel_bias[_t5_causal_bucket(jnp.array(0))].astype(F32)
    b_new = jnp.broadcast_to(jnp.concatenate([b_new, b_new])[:, None], (2 * H_C, LANES))
    scal = jnp.stack([lam, 1.0 - lam_init]).astype(F32)
    smem = pl.BlockSpec(memory_space=pltpu.SMEM)
    head3 = pl.BlockSpec((1, H_C, VD_C), lambda b, p, pt: (b, 0, 0))
    full = lambda shape: pl.BlockSpec(shape, lambda b, p, pt: (0,) * len(shape))

    def page_spec(j):
        return pl.BlockSpec((1, 1, page, H_C, VD_C), lambda b, p, pt: (pt[b, p * n_pp + j], layer_j, 0, 0, 0))

    grid_spec = pltpu.PrefetchScalarGridSpec(
        num_scalar_prefetch=1, grid=(bsz, n_pages // n_pp),
        in_specs=[smem, head3, head3, head3, full(bm.shape), full(b_new.shape), full((1, VD_C))]
        + [page_spec(j) for j in range(n_pp)] * 2,
        out_specs=head3,
        scratch_shapes=[pltpu.VMEM((2 * H_C, 1), F32), pltpu.VMEM((2 * H_C, 1), F32),
                        pltpu.VMEM((2 * H_C, VD_C), F32)])
    out = pl.pallas_call(
        functools.partial(_diff_sample_kernel, n_pp=n_pp), grid_spec=grid_spec,
        out_shape=jax.ShapeDtypeStruct((bsz, H_C, VD_C), F32),
        compiler_params=_cparams("parallel", "arbitrary"), name="diff_attn_sample")(
            page_table, scal, q.reshape(bsz, H_C, VD_C), k_new.reshape(bsz, H_C, VD_C),
            v_new.reshape(bsz, H_C, VD_C), bm, b_new, subln_g.reshape(1, VD_C),
            *([cache_k] * n_pp), *([cache_v] * n_pp))
    return out.reshape(bsz, C_W)


LANE_BETA, LANE_A, LANE_I, LANE_F = 0, H_A, 2 * H_A, 2 * H_A + H_B


def _softplus(x):
    return jnp.maximum(x, 0.0) + jnp.log1p(jnp.exp(-jnp.abs(x)))


def _silu(x):
    return x * jax.nn.sigmoid(x)


def _lanes(shape, lo, n):
    lane = lax.broadcasted_iota(jnp.int32, shape, 1)
    return jnp.logical_and(lane >= lo, lane < lo + n)


def _gate_tile(gt, alog_row, prow):
    z = gt + prow
    return jax.nn.sigmoid(gt), -jnp.exp(alog_row) * _softplus(z), z, -_softplus(-z)


def _nt(a, b, precision=None):
    return lax.dot_general(a, b, (((1,), (1,)), ((), ())), preferred_element_type=F32, precision=precision)


def _tn(a, b):
    return lax.dot_general(a, b, (((0,), (0,)), ((), ())), preferred_element_type=F32)


def _row_selector(n_heads, length, lanes_of_head):
    r = lax.broadcasted_iota(jnp.int32, (n_heads * length, LANES), 0) // length
    lane = lax.broadcasted_iota(jnp.int32, (n_heads * length, LANES), 1)
    sel = jnp.zeros((n_heads * length, LANES), F32)
    for lo in lanes_of_head:
        sel = sel + (lane == r + lo).astype(F32)
    return sel


def _tri_masks(n):
    r = lax.broadcasted_iota(jnp.int32, (n, n), 0)
    c = lax.broadcasted_iota(jnp.int32, (n, n), 1)
    return r, c


def _bmm(a, b):
    return lax.dot_general(a.astype(BF16), b.astype(BF16), (((2,), (1,)), ((0,), (0,))),
                           preferred_element_type=F32)


def _bmm_nt(a, b):
    return lax.dot_general(a.astype(BF16), b.astype(BF16), (((2,), (2,)), ((0,), (0,))),
                           preferred_element_type=F32)


def _heads(x, n_heads, width, offset=0):
    return jnp.stack([x[c * CHUNK:(c + 1) * CHUNK, offset + h * width:offset + (h + 1) * width]
                      for c in range(x.shape[0] // CHUNK) for h in range(n_heads)], axis=0)


def _head_cols(x, n_heads, lane0):
    return jnp.stack([x[c * CHUNK:(c + 1) * CHUNK, lane0 + h:lane0 + h + 1]
                      for c in range(x.shape[0] // CHUNK) for h in range(n_heads)], axis=0)


def _unheads(x, n_heads):
    n = x.shape[0] // n_heads
    return jnp.concatenate([jnp.concatenate([x[c * n_heads + h] for h in range(n_heads)], axis=-1)
                            for c in range(n)], axis=0)


def _chunk_cumsum(x):
    r, c = _tri_masks(x.shape[0])
    tri = jnp.logical_and(c <= r, r // CHUNK == c // CHUNK).astype(F32)
    return jnp.dot(tri, x, preferred_element_type=F32, precision=HI)


def _chunk_rows(sel, x):
    n = x.shape[0] // CHUNK
    rows = [_nt(sel, x[c * CHUNK:(c + 1) * CHUNK], HI) for c in range(n)]
    return jnp.concatenate(rows, axis=0).reshape(n * sel.shape[0] // CHUNK, CHUNK, CHUNK)


def _unit_lower_inverse(nmat, r, c):
    mm = _bmm
    eye = (r == c).astype(F32)
    same = (r // 16) == (c // 16)
    nd = jnp.where(same, nmat, 0.0)
    off = nmat - nd
    dinv = eye - nd
    p = nd
    for _ in range(3):
        p = mm(p, p)
        dinv = dinv + mm(dinv, p)
    m = mm(dinv, off)
    m2 = mm(m, m)
    left = eye - m
    left = left + mm(left, m2)
    return mm(left, dinv)


def _gdn_prompt_kernel(qkv_ref, z_ref, gt_ref, convw_ref, alog_ref, prow_ref, ng_ref, mix_ref, s_ref, ext_ref):
    cidx = pl.program_id(1)
    L = CHUNK
    rows = qkv_ref.shape[0]
    n_ch = rows // L

    @pl.when(cidx == 0)
    def _():
        ext_ref[0:8, :] = jnp.zeros((8, QKV_A), F32)
        s_ref[...] = jnp.zeros_like(s_ref)

    ext_ref[8:8 + rows, :] = qkv_ref[...]
    acc = convw_ref[0:1, :] * ext_ref[pl.ds(8 - (CONV_W - 1), rows), :]
    for i in range(1, CONV_W):
        acc = acc + convw_ref[i:i + 1, :] * ext_ref[pl.ds(8 - (CONV_W - 1) + i, rows), :]
    ext_ref[0:8, :] = ext_ref[rows:rows + 8, :]
    cs = _silu(acc)

    beta, g, _, _ = _gate_tile(gt_ref[...], alog_ref[...], prow_ref[...])
    r, c = _tri_masks(L)
    incl = c <= r
    strict = c < r
    gcum = _chunk_cumsum(jnp.where(_lanes(g.shape, LANE_A, H_A), g, 0.0))
    gc_rows = _chunk_rows(_row_selector(H_A, L, (LANE_A,)), gcum)
    q = _heads(cs, H_A, DK_A)
    k = _heads(cs, H_A, DK_A, H_A * DK_A)
    v = _heads(cs, H_A, DV_A, 2 * H_A * DK_A)
    q = q * lax.rsqrt(jnp.sum(q * q, axis=-1, keepdims=True) + NORM_EPS) * (DK_A ** -0.5)
    k = k * lax.rsqrt(jnp.sum(k * k, axis=-1, keepdims=True) + NORM_EPS)
    beta_c = _head_cols(beta, H_A, LANE_BETA)
    gc_c = _head_cols(gcum, H_A, LANE_A)
    decay = jnp.where(incl, jnp.exp(jnp.where(incl, gc_c - gc_rows, 0.0)), 0.0)
    kb = k.astype(BF16)
    nmat = jnp.where(strict, beta_c * _bmm_nt(kb, kb) * decay, 0.0)
    egc = jnp.exp(gc_c)
    rhs = jnp.concatenate([v * beta_c, k * (beta_c * egc)], axis=-1)
    sol = _bmm(_unit_lower_inverse(nmat, r, c), rhs)
    u = sol[:, :, :DV_A]
    wq = jnp.concatenate([sol[:, :, DV_A:], q * egc], axis=1).astype(BF16)
    qk = (_bmm_nt(q, kb) * decay).astype(BF16)
    gc_last = gc_c[:, L - 1:L]
    k_tail = (k * jnp.exp(gc_last - gc_c)).astype(BF16)
    g_tail = jnp.exp(gc_last)
    state = s_ref[0]
    outs = []
    for ci in range(n_ch):
        sl = slice(ci * H_A, (ci + 1) * H_A)
        ws = _bmm(wq[sl], state)
        delta = u[sl] - ws[:, :L]
        outs.append(ws[:, L:] + _bmm(qk[sl], delta))
        delta_b = delta.astype(BF16)
        state = state * g_tail[sl] + jnp.stack([_tn(k_tail[ci * H_A + h], delta_b[h]) for h in range(H_A)], axis=0)
    s_ref[0] = state
    o = jnp.concatenate(outs, axis=0)
    o = o * lax.rsqrt(jnp.mean(o * o, axis=-1, keepdims=True) + NORM_EPS) * ng_ref[...]
    o = o * _silu(_heads(z_ref[...], H_A, DV_A))
    mix_ref[...] = _unheads(o, H_A)


AB_CHUNKS_PER_STEP = 4


def gdn_prompt(qkv, z, gates, conv_w, alog_row, prow, norm_g, bsz):
    m = qkv.shape[0]
    rows = AB_CHUNKS_PER_STEP * CHUNK
    nc = m // bsz // rows
    row = lambda w: pl.BlockSpec((rows, w), lambda b, c: (b * nc + c, 0))
    full = lambda shape: pl.BlockSpec(shape, lambda b, c: (0,) * len(shape))
    return pl.pallas_call(
        _gdn_prompt_kernel, grid=(bsz, nc),
        in_specs=[row(QKV_A), row(H_A * DV_A), row(LANES), full(conv_w.shape), full((1, LANES)), full((1, LANES)),
                  full((1, DV_A))],
        out_specs=[row(H_A * DV_A), pl.BlockSpec((1, H_A, DK_A, DV_A), lambda b, c: (b, 0, 0, 0))],
        out_shape=[jax.ShapeDtypeStruct((m, H_A * DV_A), F32), jax.ShapeDtypeStruct((bsz, H_A, DK_A, DV_A), F32)],
        scratch_shapes=[pltpu.VMEM((rows + 8, QKV_A), F32)],
        compiler_params=_cparams("parallel", "arbitrary"), name="gdn_prompt")(
            qkv, z, gates, conv_w, alog_row, prow, norm_g.reshape(1, DV_A))


def _mlstm_prompt_kernel(q_ref, k_ref, v_ref, og_ref, gt_ref, alog_ref, prow_ref, ng_ref, mix_ref, c_ref, m_ref):
    cidx = pl.program_id(1)
    L = CHUNK

    @pl.when(cidx == 0)
    def _():
        c_ref[...] = jnp.zeros_like(c_ref)
        m_ref[...] = jnp.zeros_like(m_ref)

    n_ch = q_ref.shape[0] // L
    _, _, ipre, logf = _gate_tile(gt_ref[...], alog_ref[...], prow_ref[...])
    r, c = _tri_masks(L)
    incl = c <= r
    fsel = _lanes(logf.shape, LANE_F, H_B)
    bcum = _chunk_cumsum(jnp.where(fsel, logf, 0.0))
    rowvals = jnp.where(_lanes(ipre.shape, LANE_I, H_B), ipre, 0.0) - jnp.where(fsel, bcum, 0.0)
    rows = _chunk_rows(_row_selector(H_B, L, (LANE_I, LANE_F)), rowvals)
    mrow = m_ref[0]
    lane_row = lax.broadcasted_iota(jnp.int32, mrow.shape, 1)
    qb = _heads(q_ref[...], H_B, DQK_B).astype(BF16)
    ks = _heads(k_ref[...], H_B, DQK_B) * (DQK_B ** -0.5)
    ones_col = jnp.broadcast_to((lax.broadcasted_iota(jnp.int32, (L, LANES), 1) == 0).astype(F32),
                                (n_ch * H_B, L, LANES))
    v_ext = jnp.concatenate([_heads(v_ref[...], H_B, DV_B), ones_col], axis=-1).astype(BF16)
    b_c = _head_cols(bcum, H_B, LANE_F)
    i_c = _head_cols(ipre, H_B, LANE_I)
    dmat = jnp.where(incl, b_c + rows, -jnp.inf)
    m_intra = jnp.max(dmat, axis=-1, keepdims=True)
    w_intra = jnp.exp(dmat - m_intra) * _bmm_nt(qb, ks)
    nd_intra = _bmm(w_intra, v_ext)
    b_last = b_c[:, L - 1:L]
    e_end = b_last - b_c + i_c
    e_max = jnp.max(e_end, axis=1, keepdims=True)
    kw = (ks * jnp.exp(e_end - e_max)).astype(BF16)
    kv_end = jnp.stack([_tn(kw[i], v_ext[i]) for i in range(n_ch * H_B)], axis=0)
    m_prev = jnp.stack([mrow[:, h:h + 1] for h in range(H_B)], axis=0)
    state = c_ref[0]
    outs = []
    for ci in range(n_ch):
        sl = slice(ci * H_B, (ci + 1) * H_B)
        inter = b_c[sl] + m_prev
        m_t = jnp.maximum(inter, m_intra[sl])
        nd = jnp.exp(inter - m_t) * _bmm(qb[sl], state) + jnp.exp(m_intra[sl] - m_t) * nd_intra[sl]
        outs.append(nd[:, :, :DV_B] / jnp.maximum(jnp.abs(nd[:, :, DV_B:DV_B + 1]), jnp.exp(-m_t)))
        m_new = jnp.maximum(b_last[sl] + m_prev, e_max[sl])
        state = jnp.exp(b_last[sl] + m_prev - m_new) * state + jnp.exp(e_max[sl] - m_new) * kv_end[sl]
        m_prev = m_new
    c_ref[0] = state
    for h in range(H_B):
        mrow = jnp.where(lane_row == h, m_prev[h], mrow)
    m_ref[0] = mrow
    hh = jnp.concatenate(outs, axis=0)
    hh = hh * lax.rsqrt(jnp.mean(hh * hh, axis=-1, keepdims=True) + NORM_EPS) * ng_ref[...]
    hh = jax.nn.sigmoid(_heads(og_ref[...], H_B, DV_B)) * hh
    mix_ref[...] = _unheads(hh, H_B)


def mlstm_prompt(q, k, v, og, gates, alog_row, prow, norm_g, bsz):
    m = q.shape[0]
    rows = AB_CHUNKS_PER_STEP * CHUNK
    nc = m // bsz // rows
    row = lambda w: pl.BlockSpec((rows, w), lambda b, c: (b * nc + c, 0))
    full = lambda shape: pl.BlockSpec(shape, lambda b, c: (0,) * len(shape))
    return pl.pallas_call(
        _mlstm_prompt_kernel, grid=(bsz, nc),
        in_specs=[row(H_B * DQK_B), row(H_B * DQK_B), row(H_B * DV_B), row(H_B * DV_B), row(LANES),
                  full((1, LANES)), full((1, LANES)), full((1, DV_B))],
        out_specs=[row(H_B * DV_B), pl.BlockSpec((1, H_B, DQK_B, DV_B + LANES), lambda b, c: (b, 0, 0, 0)),
                   pl.BlockSpec((1, 1, LANES), lambda b, c: (b, 0, 0))],
        out_shape=[jax.ShapeDtypeStruct((m, H_B * DV_B), F32),
                   jax.ShapeDtypeStruct((bsz, H_B, DQK_B, DV_B + LANES), F32),
                   jax.ShapeDtypeStruct((bsz, 1, LANES), F32)],
        compiler_params=_cparams("parallel", "arbitrary"), name="mlstm_prompt")(
            q, k, v, og, gates, alog_row, prow, norm_g.reshape(1, DV_B))


def _columns(x8):
    n = x8.shape[1]
    r, c = _tri_masks(n)
    return _nt((r == c).astype(F32), x8, HI)


def _ab_sample_kernel(qn_ref, kn_ref, vn_ref, cq_ref, ck_ref, cv_ref, wq_ref, wk_ref, wv_ref, z_ref,
                      qb_ref, kb_ref, vb_ref, og_ref, gt_ref, alog_ref, prow_ref, nga_ref, ngb_ref,
                      s_in, c_in, n_in, m_in,
                      oa_ref, ob_ref, s_out, c_out, n_out, m_out):
    def conv(new_ref, prev_ref, w_ref):
        acc = w_ref[CONV_W - 1] * new_ref[0]
        for i in range(CONV_W - 1):
            acc = acc + w_ref[i] * prev_ref[0, i]
        return _silu(acc)

    q8 = conv(qn_ref, cq_ref, wq_ref)
    k8 = conv(kn_ref, ck_ref, wk_ref)
    v8 = conv(vn_ref, cv_ref, wv_ref)
    q8 = q8 * lax.rsqrt(jnp.sum(q8 * q8, axis=-1, keepdims=True) + NORM_EPS) * (DK_A ** -0.5)
    k8 = k8 * lax.rsqrt(jnp.sum(k8 * k8, axis=-1, keepdims=True) + NORM_EPS)
    beta, g, ipre, logf = _gate_tile(gt_ref[0], alog_ref[...], prow_ref[...])
    q_cols, k_cols = _columns(q8), _columns(k8)
    z8 = z_ref[0]
    outs = []
    for h in range(H_A):
        s = s_in[0, h] * jnp.exp(g[:, LANE_A + h:LANE_A + h + 1])
        kc = k_cols[:, h:h + 1]
        err = v8[h:h + 1] - jnp.sum(kc * s, axis=0, keepdims=True)
        s = s + kc * (beta[:, LANE_BETA + h:LANE_BETA + h + 1] * err)
        s_out[0, h] = s
        outs.append(jnp.sum(q_cols[:, h:h + 1] * s, axis=0, keepdims=True))
    o = jnp.concatenate(outs, axis=0)
    o = o * lax.rsqrt(jnp.mean(o * o, axis=-1, keepdims=True) + NORM_EPS) * nga_ref[...]
    oa_ref[0] = o * _silu(z8)

    zeros4 = jnp.zeros((8 - H_B, DQK_B), F32)
    qb_cols = _columns(jnp.concatenate([qb_ref[0], zeros4], axis=0))
    kb_cols = _columns(jnp.concatenate([kb_ref[0] * (DQK_B ** -0.5), zeros4], axis=0))
    vb = vb_ref[0]
    n_cols = n_in[0]
    m_row = m_in[0]
    lane_n = lax.broadcasted_iota(jnp.int32, n_cols.shape, 1)
    lane_m = lax.broadcasted_iota(jnp.int32, m_row.shape, 1)
    outs = []
    for h in range(H_B):
        lf = logf[:, LANE_F + h:LANE_F + h + 1]
        it = ipre[:, LANE_I + h:LANE_I + h + 1]
        m_prev = m_row[:, h:h + 1]
        m_new = jnp.maximum(lf + m_prev, it)
        f_sc = jnp.exp(lf + m_prev - m_new)
        i_sc = jnp.exp(it - m_new)
        kc = kb_cols[:, h:h + 1]
        qc = qb_cols[:, h:h + 1]
        cm = f_sc * c_in[0, h] + i_sc * (kc * vb[h:h + 1])
        nn = f_sc * n_cols[:, h:h + 1] + i_sc * kc
        c_out[0, h] = cm
        n_cols = jnp.where(lane_n == h, nn, n_cols)
        m_row = jnp.where(lane_m == h, m_new, m_row)
        num = jnp.sum(qc * cm, axis=0, keepdims=True)
        den = jnp.sum(qc * nn, axis=0, keepdims=True)
        outs.append(num / jnp.maximum(jnp.abs(den), jnp.exp(-m_new)))
    hb = jnp.concatenate(outs, axis=0)
    hb = hb * lax.rsqrt(jnp.mean(hb * hb, axis=-1, keepdims=True) + NORM_EPS) * ngb_ref[...]
    ob_ref[0] = jax.nn.sigmoid(og_ref[0]) * hb
    n_out[0] = n_cols
    m_out[0] = m_row


def ab_sample(qkv, z, q_b, k_b, v_b, o_b, gates, conv_prev, conv_w, alog_row, prow, norm_g_a, norm_g_b,
              s_prev, c_prev, n_prev, m_prev):
    bsz = qkv.shape[0]
    hk = H_A * DK_A
    part = lambda x, i, w: x[..., i * hk:i * hk + H_A * w].reshape(x.shape[:-1] + (H_A, w))
    new_parts = [part(qkv, 0, DK_A), part(qkv, 1, DK_A), part(qkv, 2, DV_A)]
    prev_parts = [part(conv_prev, 0, DK_A), part(conv_prev, 1, DK_A), part(conv_prev, 2, DV_A)]
    w_parts = [part(conv_w, 0, DK_A), part(conv_w, 1, DK_A), part(conv_w, 2, DV_A)]
    args = new_parts + prev_parts + w_parts + [
        z.reshape(bsz, H_A, DV_A), q_b.reshape(bsz, H_B, DQK_B), k_b.reshape(bsz, H_B, DQK_B),
        v_b.reshape(bsz, H_B, DV_B), o_b.reshape(bsz, H_B, DV_B), gates.reshape(bsz, 1, LANES),
        alog_row, prow, norm_g_a.reshape(1, DV_A), norm_g_b.reshape(1, DV_B),
        s_prev, c_prev, jnp.swapaxes(n_prev, 1, 2), m_prev.reshape(bsz, 1, H_B)]

    def spec(x, batched):
        nd = x.ndim
        if batched:
            return pl.BlockSpec((1,) + x.shape[1:], lambda b: (b,) + (0,) * (nd - 1))
        return pl.BlockSpec(x.shape, lambda b: (0,) * nd)

    batched = [True] * 6 + [False] * 3 + [True] * 6 + [False] * 4 + [True] * 4
    out_shape = [jax.ShapeDtypeStruct((bsz, H_A, DV_A), F32), jax.ShapeDtypeStruct((bsz, H_B, DV_B), F32),
                 jax.ShapeDtypeStruct(s_prev.shape, F32), jax.ShapeDtypeStruct(c_prev.shape, F32),
                 jax.ShapeDtypeStruct((bsz, DQK_B, H_B), F32), jax.ShapeDtypeStruct((bsz, 1, H_B), F32)]
    oa, ob, s_new, c_new, n_new, m_new = pl.pallas_call(
        _ab_sample_kernel, grid=(bsz,),
        in_specs=[spec(x, bt) for x, bt in zip(args, batched)],
        out_specs=[spec(x, True) for x in out_shape], out_shape=out_shape,
        compiler_params=_cparams("parallel"), name="ab_sample")(*args)
    mix = jnp.concatenate([oa.reshape(bsz, H_A * DV_A), ob.reshape(bsz, H_B * DV_B)], axis=-1)
    return mix, s_new, c_new, jnp.swapaxes(n_new, 1, 2), m_new.reshape(bsz, H_B)


MOE_BLOCK_PROMPT = 256
MOE_BLOCK_SAMPLE = 32


def _ab_weights(w_in, a_log, dt_bias, b_i, b_f):
    sizes = (QKV_A, H_A * DV_A, H_A, H_A, H_B * DQK_B, H_B * DQK_B, H_B * DV_B, H_B * DV_B, H_B, H_B)
    offs = [0]
    for s in sizes:
        offs.append(offs[-1] + s)
    col = lambda i: w_in[:, offs[i]:offs[i + 1]]
    w_lo = tuple(col(i).astype(BF16) for i in (0, 1, 4, 5, 6, 7))
    w_gate = jnp.concatenate([col(2), col(3), col(8), col(9)], axis=1)
    w_gate = jnp.pad(w_gate, ((0, 0), (0, LANES - w_gate.shape[1])))
    zeros = lambda n: jnp.zeros((n,), F32)
    pad = LANES - 2 * H_A - 2 * H_B
    alog_row = jnp.concatenate([zeros(H_A), a_log.astype(F32), zeros(2 * H_B + pad)]).reshape(1, LANES)
    prow = jnp.concatenate([zeros(H_A), dt_bias.astype(F32), b_i.astype(F32), b_f.astype(F32),
                            zeros(pad)]).reshape(1, LANES)
    return w_lo, w_gate, alog_row, prow


def kernel(x_prompt, x_sample, state_delta_S, state_delta_conv, state_mlstm_C, state_mlstm_n, state_mlstm_m,
           cache_diff_k, cache_diff_v, cache_mem_k, cache_mem_v, page_table, mem_prompt,
           w_in_ab, conv_w_a, a_log_a, dt_bias_a, norm_g_a, b_i_b, b_f_b, norm_g_b, w_out_ab,
           w_qkv_c, lam_q1, lam_k1, lam_q2, lam_k2, subln_g_c, w_o_c, rel_bias,
           w_xq, w_xkv, w_xo, ln_g, ln_b, w_router, b_router, w_moe_in, b_moe_in, w_moe_out, b_moe_out):
    bp, t, d = x_prompt.shape
    bs = x_sample.shape[0]
    xp = x_prompt.reshape(bp * t, d)
    xs = x_sample.reshape(bs, d)
    mem2d = mem_prompt.reshape(bp * N_MEM, d)
    cmk = cache_mem_k.reshape(DEPTH * bs, N_MEM * H_X, DH_X)
    cmv = cache_mem_v.reshape(DEPTH * bs, N_MEM * H_X, DH_X)
    wm_in = w_moe_in.reshape(DEPTH * N_EXPERTS, d, 2 * D_FF)
    bm_in = b_moe_in.reshape(DEPTH * N_EXPERTS, 2 * D_FF)
    wm_out = w_moe_out.reshape(DEPTH * N_EXPERTS, D_FF, d)
    bm_out = b_moe_out.reshape(DEPTH * N_EXPERTS, d)
    p_S, p_conv, p_C, p_n, p_m, p_k, p_v, p_mk, p_mv = [], [], [], [], [], [], [], [], []
    s_S, s_conv, s_C, s_n, s_m, s_k, s_v = [], [], [], [], [], [], []
    for layer in range(DEPTH):
        j = layer // 2
        g0, b0 = ln_g[layer, 0].reshape(1, d), ln_b[layer, 0].reshape(1, d)
        g1, b1 = ln_g[layer, 1].reshape(1, d), ln_b[layer, 1].reshape(1, d)
        g2, b2 = ln_g[layer, 2].reshape(1, d), ln_b[layer, 2].reshape(1, d)
        if layer % 2 == 0:
            w_lo, w_gate, alog_row, prow = _ab_weights(w_in_ab[j], a_log_a[j], dt_bias_a[j], b_i_b[j], b_f_b[j])
            w_out = w_out_ab[j].astype(BF16)
            w_out_a, w_out_b = w_out[:H_A * DV_A], w_out[H_A * DV_A:]
            qkv, z, q_b, k_b, v_b, o_b, gates = mm_multi(xp, w_lo, (w_gate,), tm=256)
            mix_a, st_S = gdn_prompt(qkv, z, gates, conv_w_a[j], alog_row, prow, norm_g_a[j], bp)
            mix_b, c_ext, m_row = mlstm_prompt(q_b, k_b, v_b, o_b, gates, alog_row, prow, norm_g_b[j], bp)
            xp = proj_ln([mix_a, mix_b], [w_out_a, w_out_b], xp, g0, b0)
            p_S.append(st_S)
            p_conv.append(qkv.reshape(bp, t, QKV_A)[:, t - (CONV_W - 1):])
            p_C.append(c_ext[..., :DV_B])
            p_n.append(c_ext[..., DV_B])
            p_m.append(m_row[:, 0, :H_B])
            qkv, z, q_b, k_b, v_b, o_b, gates = mm_multi(xs, w_lo, (w_gate,))
            mix, st_S, st_C, st_n, st_m = ab_sample(
                qkv, z, q_b, k_b, v_b, o_b, gates, state_delta_conv[j], conv_w_a[j], alog_row, prow,
                norm_g_a[j], norm_g_b[j], state_delta_S[j].astype(F32), state_mlstm_C[j].astype(F32),
                state_mlstm_n[j].astype(F32), state_mlstm_m[j].astype(F32))
            xs = proj_ln([mix], [w_out], xs, g0, b0)
            s_S.append(st_S)
            s_conv.append(jnp.concatenate([state_delta_conv[j][:, 1:].astype(F32), qkv[:, None, :]], axis=1))
            s_C.append(st_C)
            s_n.append(st_n)
            s_m.append(st_m)
        else:
            lam_init = 0.8 - 0.6 * math.exp(-0.3 * layer)
            lam = (jnp.exp(jnp.sum(lam_q1[j].astype(F32) * lam_k1[j].astype(F32)))
                   - jnp.exp(jnp.sum(lam_q2[j].astype(F32) * lam_k2[j].astype(F32))) + lam_init)
            w_qkv = tuple(w_qkv_c[j][:, i * C_W:(i + 1) * C_W].astype(BF16) for i in range(3))
            w_o = w_o_c[j].astype(BF16)
            q, k, v = mm_multi(xp, w_qkv, tm=512)
            o = diff_attn_prompt_core(q, k, v, bp, rel_bias, lam, lam_init, subln_g_c[j])
            xp = proj_ln([o], [w_o], xp, g0, b0)
            p_k.append(k.reshape(bp, t, H_C, 2 * DH_C))
            p_v.append(v.reshape(bp, t, H_C, VD_C))
            q, k, v = mm_multi(xs, w_qkv)
            o = diff_attn_sample_core(q, k, v, cache_diff_k, cache_diff_v, page_table, j, rel_bias, lam, lam_init,
                                      subln_g_c[j])
            xs = proj_ln([o], [w_o], xs, g0, b0)
            s_k.append(k.reshape(bs, 1, H_C, 2 * DH_C))
            s_v.append(v.reshape(bs, 1, H_C, VD_C))
        w_q, w_o = w_xq[layer].astype(BF16), w_xo[layer].astype(BF16)
        w_r = jnp.pad(w_router[layer].astype(F32), ((0, 0), (0, LANES - N_EXPERTS)))
        w_r_hi = w_r.astype(BF16)
        w_r = jnp.stack([w_r_hi, (w_r - w_r_hi.astype(F32)).astype(BF16)])
        b_r = jnp.pad(b_router[layer].astype(F32), (0, LANES - N_EXPERTS)).reshape(1, LANES)
        mk, mv = mm_multi(mem2d, (w_xkv[layer][:, :X_W].astype(BF16), w_xkv[layer][:, X_W:].astype(BF16)))
        p_mk.append(mk.reshape(bp, N_MEM, H_X, DH_X))
        p_mv.append(mv.reshape(bp, N_MEM, H_X, DH_X))
        xp, xp_tiles, idx_p, gate_p = xattn_prompt(xp, mk.reshape(bp, N_MEM, X_W), mv.reshape(bp, N_MEM, X_W),
                                                   w_q, w_o, g1, b1, w_r, b_r)
        (q,) = mm_multi(xs, (w_q,))
        o = xattn_sample_core(q, cmk, cmv, off=layer * bs)
        xs, xs_tiles, idx_s, gate_s = proj_ln_route(o, w_o, xs, g1, b1, w_r, b_r)
        xp = moe_ln(xp, xp_tiles, idx_p, gate_p, wm_in, bm_in, wm_out, bm_out, g2, b2, MOE_BLOCK_PROMPT,
                    e_off=layer * N_EXPERTS)
        xs = moe_ln(xs, xs_tiles, idx_s, gate_s, wm_in, bm_in, wm_out, bm_out, g2, b2, MOE_BLOCK_SAMPLE,
                    e_off=layer * N_EXPERTS)

    return (xp.reshape(bp, t, d), xs.reshape(bs, 1, d),
            jnp.stack(p_S), jnp.stack(p_conv), jnp.stack(p_C), jnp.stack(p_n), jnp.stack(p_m),
            jnp.stack(p_k, axis=1), jnp.stack(p_v, axis=1), jnp.stack(p_mk), jnp.stack(p_mv),
            jnp.stack(s_S), jnp.stack(s_conv), jnp.stack(s_C), jnp.stack(s_n), jnp.stack(s_m),
            jnp.stack(s_k, axis=1), jnp.stack(s_v, axis=1))
```

```python
import functools
import math

import jax
import jax.numpy as jnp
from jax import lax
from jax.experimental import pallas as pl
from jax.experimental.pallas import tpu as pltpu

F32 = jnp.float32
BF16 = jnp.bfloat16
HI = lax.Precision.HIGHEST

D_MODEL = 1024
DEPTH = 2
H_A, DK_A, DV_A, CONV_W, CHUNK = 8, 64, 64, 4, 64
H_B, DQK_B, DV_B = 4, 64, 128
QKV_A = H_A * (2 * DK_A + DV_A)
H_C, DH_C = 8, 64
VD_C = 2 * DH_C
C_W = H_C * 2 * DH_C
N_BUCKETS, MAX_DISTANCE = 32, 128
N_MEM, H_X, DH_X = 256, 4, 128
X_W = H_X * DH_X
N_EXPERTS, TOP_K = 32, 4
D_FF = D_MODEL
SWIGLU_ALPHA, SWIGLU_LIMIT = 1.702, 7.0
DN_ALPHA = (2 * DEPTH) ** 0.25
LN_EPS = 1e-5
NORM_EPS = 1e-6

LANES = 128
SUBLANES = 8
VMEM_PHYSICAL = 64 * 1024 * 1024
VMEM_LIMIT = 48 * 1024 * 1024


def _cparams(*sem):
    return pltpu.CompilerParams(dimension_semantics=tuple(sem), vmem_limit_bytes=VMEM_LIMIT)


def _layer_norm(y, g, b):
    mu = jnp.mean(y, axis=-1, keepdims=True)
    d = y - mu
    var = jnp.mean(d * d, axis=-1, keepdims=True)
    return d * lax.rsqrt(var + LN_EPS) * g + b


def _mm_multi_kernel(x_ref, *refs, n_lo, n_hi):
    n = n_lo + n_hi
    ws, outs = refs[:n], refs[n:]
    x = x_ref[...]
    xb = x.astype(BF16)
    for i in range(n_lo):
        outs[i][...] = jnp.dot(xb, ws[i][...], preferred_element_type=F32)
    for i in range(n_lo, n):
        outs[i][...] = jnp.dot(x, ws[i][...], preferred_element_type=F32, precision=HI)


def mm_multi(x, w_lo, w_hi=(), tm=256):
    m, k = x.shape
    tm = min(tm, m)
    assert m % tm == 0
    ws = tuple(w_lo) + tuple(w_hi)
    in_specs = [pl.BlockSpec((tm, k), lambda i: (i, 0))]
    in_specs += [pl.BlockSpec(w.shape, lambda i: (0, 0)) for w in ws]
    out_specs = [pl.BlockSpec((tm, w.shape[1]), lambda i: (i, 0)) for w in ws]
    out_shape = [jax.ShapeDtypeStruct((m, w.shape[1]), F32) for w in ws]
    return pl.pallas_call(
        functools.partial(_mm_multi_kernel, n_lo=len(w_lo), n_hi=len(w_hi)),
        grid=(m // tm,), in_specs=in_specs, out_specs=out_specs, out_shape=out_shape,
        compiler_params=_cparams("parallel"), name="mm_multi")(x, *ws)


def _proj_ln_kernel(*refs, n):
    a_refs, w_refs = refs[:n], refs[n:2 * n]
    x_ref, g_ref, b_ref, o_ref = refs[2 * n:]
    h = jnp.dot(a_refs[0][...].astype(BF16), w_refs[0][...], preferred_element_type=F32)
    for a_ref, w_ref in zip(a_refs[1:], w_refs[1:]):
        h = h + jnp.dot(a_ref[...].astype(BF16), w_ref[...], preferred_element_type=F32)
    o_ref[...] = _layer_norm(DN_ALPHA * x_ref[...] + h, g_ref[...], b_ref[...])


def proj_ln(a_list, w_list, x_res, g, b, tm=512):
    m, d = x_res.shape
    tm = min(tm, m)
    assert m % tm == 0
    n = len(a_list)
    return pl.pallas_call(
        functools.partial(_proj_ln_kernel, n=n), grid=(m // tm,),
        in_specs=[pl.BlockSpec((tm, a.shape[1]), lambda i: (i, 0)) for a in a_list]
        + [pl.BlockSpec(w.shape, lambda i: (0, 0)) for w in w_list]
        + [pl.BlockSpec((tm, d), lambda i: (i, 0)), pl.BlockSpec((1, d), lambda i: (0, 0)),
           pl.BlockSpec((1, d), lambda i: (0, 0))],
        out_specs=pl.BlockSpec((tm, d), lambda i: (i, 0)),
        out_shape=jax.ShapeDtypeStruct((m, d), F32),
        compiler_params=_cparams("parallel"), name="proj_ln")(*a_list, *w_list, x_res, g, b)


def _mem_attention(q, mk, mv):
    outs = []
    for h in range(H_X):
        sl = slice(h * DH_X, (h + 1) * DH_X)
        s = lax.dot_general(q[:, sl].astype(BF16), mk[:, sl].astype(BF16), (((1,), (1,)), ((), ())),
                            preferred_element_type=F32) * (DH_X ** -0.5)
        s = s - jnp.max(s, axis=-1, keepdims=True)
        p = jnp.exp(s)
        p = p / jnp.sum(p, axis=-1, keepdims=True)
        outs.append(jnp.dot(p.astype(BF16), mv[:, sl].astype(BF16), preferred_element_type=F32))
    return jnp.concatenate(outs, axis=-1)


def _route(x, wr, br):
    x_hi = x.astype(BF16)
    x_lo = (x - x_hi.astype(F32)).astype(BF16)
    logits = (jnp.dot(x_hi, wr[0], preferred_element_type=F32) + jnp.dot(x_lo, wr[0], preferred_element_type=F32)
              + jnp.dot(x_hi, wr[1], preferred_element_type=F32) + br)
    lane = lax.broadcasted_iota(jnp.int32, logits.shape, 1)
    work = jnp.where(lane < N_EXPERTS, logits, -jnp.inf)
    idx_out = jnp.zeros(logits.shape, jnp.int32)
    val_out = jnp.full(logits.shape, -jnp.inf, F32)
    for k in range(TOP_K):
        m = jnp.max(work, axis=-1, keepdims=True)
        sel = jnp.min(jnp.where(work == m, lane, LANES), axis=-1, keepdims=True)
        idx_out = jnp.where(lane == k, sel, idx_out)
        val_out = jnp.where(lane == k, m, val_out)
        work = jnp.where(lane == sel, -jnp.inf, work)
    e = jnp.exp(val_out - jnp.max(val_out, axis=-1, keepdims=True))
    gates = e / jnp.sum(e, axis=-1, keepdims=True)
    return idx_out, gates


def _store_token_tiles(o3_ref, y):
    for s in range(y.shape[1] // LANES):
        o3_ref[:, s, :] = y[:, s * LANES:(s + 1) * LANES]


def _xattn_prompt_kernel(x_ref, mk_ref, mv_ref, wq_ref, wo_ref, g_ref, b_ref, wr_ref, br_ref,
                         o_ref, o3_ref, idx_ref, gate_ref):
    x = x_ref[...]
    q = jnp.dot(x.astype(BF16), wq_ref[...], preferred_element_type=F32)
    o = _mem_attention(q, mk_ref[0], mv_ref[0])
    h = jnp.dot(o.astype(BF16), wo_ref[...], preferred_element_type=F32)
    y = _layer_norm(DN_ALPHA * x + h, g_ref[...], b_ref[...])
    o_ref[...] = y
    _store_token_tiles(o3_ref, y)
    idx, gates = _route(y, wr_ref[...], br_ref[...])
    idx_ref[...] = idx
    gate_ref[...] = gates


def xattn_prompt(x, mk, mv, wq, wo, g, b, wr, br, tq=512):
    m, d = x.shape
    bsz = mk.shape[0]
    t = m // bsz
    nq = t // tq
    full = lambda shape: pl.BlockSpec(shape, lambda bi, qi: (0,) * len(shape))
    row = lambda w: pl.BlockSpec((tq, w), lambda bi, qi: (bi * nq + qi, 0))
    return pl.pallas_call(
        _xattn_prompt_kernel, grid=(bsz, nq),
        in_specs=[row(d), pl.BlockSpec((1, N_MEM, X_W), lambda bi, qi: (bi, 0, 0)),
                  pl.BlockSpec((1, N_MEM, X_W), lambda bi, qi: (bi, 0, 0)),
                  full(wq.shape), full(wo.shape), full(g.shape), full(b.shape), full(wr.shape), full(br.shape)],
        out_specs=[row(d), pl.BlockSpec((tq, d // LANES, LANES), lambda bi, qi: (bi * nq + qi, 0, 0)),
                   row(LANES), row(LANES)],
        out_shape=[jax.ShapeDtypeStruct((m, d), F32), jax.ShapeDtypeStruct((m, d // LANES, LANES), F32),
                   jax.ShapeDtypeStruct((m, LANES), jnp.int32), jax.ShapeDtypeStruct((m, LANES), F32)],
        compiler_params=_cparams("parallel", "parallel"), name="xattn_prompt")(x, mk, mv, wq, wo, g, b, wr, br)


def _xattn_sample_kernel(q_ref, mk_ref, mv_ref, o_ref):
    q = jnp.broadcast_to(q_ref[0], (8, X_W))
    mk = jnp.concatenate([mk_ref[0, pl.ds(h, N_MEM, stride=H_X), :] for h in range(H_X)], axis=-1)
    mv = jnp.concatenate([mv_ref[0, pl.ds(h, N_MEM, stride=H_X), :] for h in range(H_X)], axis=-1)
    o = _mem_attention(q, mk, mv)
    o_ref[0] = o[0:1]


def xattn_sample_core(q, mk, mv, off=0):
    bsz = q.shape[0]
    spec3 = pl.BlockSpec((1, N_MEM * H_X, DH_X), lambda bi: (off + bi, 0, 0))
    out = pl.pallas_call(
        _xattn_sample_kernel, grid=(bsz,),
        in_specs=[pl.BlockSpec((1, 1, X_W), lambda bi: (bi, 0, 0)), spec3, spec3],
        out_specs=pl.BlockSpec((1, 1, X_W), lambda bi: (bi, 0, 0)),
        out_shape=jax.ShapeDtypeStruct((bsz, 1, X_W), F32),
        compiler_params=_cparams("parallel"), name="xattn_sample")(q.reshape(bsz, 1, X_W), mk, mv)
    return out.reshape(bsz, X_W)


def _ln_route_kernel(a_ref, w_ref, x_ref, g_ref, b_ref, wr_ref, br_ref, o_ref, o3_ref, idx_ref, gate_ref):
    h = jnp.dot(a_ref[...].astype(BF16), w_ref[...], preferred_element_type=F32)
    y = _layer_norm(DN_ALPHA * x_ref[...] + h, g_ref[...], b_ref[...])
    o_ref[...] = y
    _store_token_tiles(o3_ref, y)
    idx, gates = _route(y, wr_ref[...], br_ref[...])
    idx_ref[...] = idx
    gate_ref[...] = gates


def proj_ln_route(a, w, x_res, g, b, wr, br):
    m, d = x_res.shape
    return pl.pallas_call(
        _ln_route_kernel,
        out_shape=[jax.ShapeDtypeStruct((m, d), F32), jax.ShapeDtypeStruct((m, d // LANES, LANES), F32),
                   jax.ShapeDtypeStruct((m, LANES), jnp.int32), jax.ShapeDtypeStruct((m, LANES), F32)],
        compiler_params=pltpu.CompilerParams(vmem_limit_bytes=VMEM_LIMIT),
        name="proj_ln_route")(a, w, x_res, g, b, wr, br)


def _expert_onehots(idx):
    lane = lax.broadcasted_iota(jnp.int32, idx.shape, 1)
    onehots = [(idx[:, k:k + 1] == lane).astype(F32) for k in range(TOP_K)]
    tot = onehots[0]
    for k in range(1, TOP_K):
        tot = tot + onehots[k]
    return lane, onehots, tot


def _count_kernel(idx_ref, cnt_ref):
    @pl.when(pl.program_id(0) == 0)
    def _():
        cnt_ref[...] = jnp.zeros_like(cnt_ref)

    _, _, tot = _expert_onehots(idx_ref[...])
    cnt_ref[...] = cnt_ref[...] + jnp.sum(tot, axis=0, keepdims=True)


def _dest_kernel(idx_ref, base0_ref, dest_ref, base_ref):
    @pl.when(pl.program_id(0) == 0)
    def _():
        base_ref[...] = base0_ref[...]

    idx = idx_ref[...]
    tr = idx.shape[0]
    lane, onehots, tot = _expert_onehots(idx)
    r = lax.broadcasted_iota(jnp.int32, (tr, tr), 0)
    c = lax.broadcasted_iota(jnp.int32, (tr, tr), 1)
    strict = (c < r).astype(BF16)
    before = jnp.dot(strict, tot.astype(BF16), preferred_element_type=F32) + base_ref[...]
    out = jnp.zeros(idx.shape, F32)
    for k in range(TOP_K):
        out = jnp.where(lane == k, jnp.sum(onehots[k] * before, axis=-1, keepdims=True), out)
    dest_ref[...] = out.astype(jnp.int32)
    base_ref[...] = base_ref[...] + jnp.sum(tot, axis=0, keepdims=True)


def route_slots(idx, bm, tr=256):
    t = idx.shape[0]
    tr = min(tr, t)
    assert t % tr == 0
    rows = pl.BlockSpec((tr, LANES), lambda i: (i, 0))
    one = pl.BlockSpec((1, LANES), lambda i: (0, 0))
    counts = pl.pallas_call(
        _count_kernel, grid=(t // tr,), in_specs=[rows], out_specs=one,
        out_shape=jax.ShapeDtypeStruct((1, LANES), F32),
        compiler_params=_cparams("arbitrary"), name="route_count")(idx)
    counts = counts[0].astype(jnp.int32)
    padded = (counts + bm - 1) // bm * bm
    pad_end = jnp.cumsum(padded)
    base0 = (pad_end - padded).astype(F32).reshape(1, LANES)
    dest = pl.pallas_call(
        _dest_kernel, grid=(t // tr,), in_specs=[rows, one], out_specs=rows,
        out_shape=jax.ShapeDtypeStruct((t, LANES), jnp.int32),
        scratch_shapes=[pltpu.VMEM((1, LANES), F32)],
        compiler_params=_cparams("arbitrary"), name="route_dest")(idx, base0)
    return dest, pad_end[:N_EXPERTS], counts[:N_EXPERTS]


def _clamped_swiglu(glu, lin):
    glu = jnp.minimum(glu, SWIGLU_LIMIT)
    lin = jnp.clip(lin, -SWIGLU_LIMIT, SWIGLU_LIMIT)
    return glu * jax.nn.sigmoid(SWIGLU_ALPHA * glu) * (lin + 1.0)


def _tile_rows(buf):
    rows = buf.shape[0] * buf.shape[2]
    return jnp.concatenate([buf[:, j].reshape(rows, LANES) for j in range(buf.shape[1])], axis=-1)


def _moe_gmm_kernel(bexp_ref, nused_ref, tok_ref, tok1_ref, tok2_ref, x_hbm, win_ref, bin_ref, wout_ref,
                    bout_ref, y_hbm, xbuf, ybuf, winb, woutb, gsem, osem, *, bm):
    i = pl.program_id(0)
    n = pl.num_programs(0)
    nused = nused_ref[0]
    slot = lax.rem(i, 2)
    xslot = lax.rem(i, 3)
    n_sub = x_hbm.shape[1]
    groups = bm // SUBLANES

    def start_gather(idx_ref, s):
        def body(g, carry):
            for u in range(SUBLANES):
                pltpu.make_async_copy(x_hbm.at[idx_ref[0, 0, g * SUBLANES + u]], xbuf.at[s, g, :, u],
                                      gsem.at[s]).start(priority=u % 2)
            return carry
        lax.fori_loop(0, groups, body, 0)

    def wait_gather(s):
        pltpu.make_async_copy(xbuf.at[s], xbuf.at[s], gsem.at[s]).wait()

    def out_copies(blk, s):
        return [pltpu.make_async_copy(ybuf.at[s, :, j], y_hbm.at[pl.ds(blk * groups, groups), :, j], osem.at[s])
                for j in range(n_sub)]

    @pl.when(i == 0)
    def _():
        start_gather(tok_ref, 0)

        @pl.when(jnp.logical_and(1 < n, 1 < nused))
        def _():
            start_gather(tok1_ref, 1)

    active = i < nused
    prefetch = jnp.logical_and(i + 2 < n, i + 2 < nused)

    @pl.when(jnp.logical_and(active, jnp.logical_or(i == 0, bexp_ref[i] != bexp_ref[jnp.maximum(i - 1, 0)])))
    def _():
        winb[...] = win_ref[0].astype(BF16)
        woutb[...] = wout_ref[0].astype(BF16)

    @pl.when(i >= 2)
    def _():
        for cp in out_copies(i - 2, slot):
            cp.wait()

    def expert_mlp(gather_ahead):
        wait_gather(xslot)
        xb = _tile_rows(xbuf[xslot]).astype(BF16)
        if gather_ahead:
            nslot = lax.rem(i + 2, 3)
            for g in range(groups):
                for u in range(SUBLANES):
                    pltpu.make_async_copy(x_hbm.at[tok2_ref[0, 0, g * SUBLANES + u]], xbuf.at[nslot, g, :, u],
                                          gsem.at[nslot]).start(priority=u % 2)
        h = jnp.dot(xb, winb[...], preferred_element_type=F32) + bin_ref[0]
        act = _clamped_swiglu(h[:, :D_FF], h[:, D_FF:]).astype(BF16)
        y = jnp.dot(act, woutb[...], preferred_element_type=F32) + bout_ref[0]
        for j in range(n_sub):
            ybuf[slot, :, j] = y[:, j * LANES:(j + 1) * LANES].reshape(groups, SUBLANES, LANES)

    @pl.when(jnp.logical_and(active, prefetch))
    def _():
        expert_mlp(True)

    @pl.when(jnp.logical_and(active, jnp.logical_not(prefetch)))
    def _():
        expert_mlp(False)

    @pl.when(jnp.logical_not(active))
    def _():
        ybuf[slot] = jnp.zeros(ybuf.shape[1:], F32)

    for cp in out_copies(i, slot):
        cp.start()

    @pl.when(i == n - 1)
    def _():
        @pl.when(n >= 2)
        def _():
            for cp in out_copies(i - 1, 1 - slot):
                cp.wait()

        for cp in out_copies(i, slot):
            cp.wait()


def moe_gmm(x, row_tok, block_expert, n_used, w_in, b_in, w_out, b_out, bm):
    n_blocks = row_tok.shape[0]
    n_sub = x.shape[1]
    d = n_sub * LANES
    idx_spec = lambda off: pl.BlockSpec(
        (1, 1, bm), lambda i, be, nu: (jnp.minimum(i + off, n_blocks - 1), 0, 0), memory_space=pltpu.SMEM)
    ex = lambda i, be, nu: (be[i], 0, 0)
    grid_spec = pltpu.PrefetchScalarGridSpec(
        num_scalar_prefetch=2, grid=(n_blocks,),
        in_specs=[idx_spec(0), idx_spec(1), idx_spec(2),
                  pl.BlockSpec(memory_space=pl.ANY),
                  pl.BlockSpec((1, d, 2 * D_FF), ex), pl.BlockSpec((1, 1, 2 * D_FF), ex),
                  pl.BlockSpec((1, D_FF, d), ex), pl.BlockSpec((1, 1, d), ex)],
        out_specs=pl.BlockSpec(memory_space=pl.ANY),
        scratch_shapes=[pltpu.VMEM((3, bm // SUBLANES, n_sub, SUBLANES, LANES), F32),
                        pltpu.VMEM((2, bm // SUBLANES, n_sub, SUBLANES, LANES), F32),
                        pltpu.VMEM((d, 2 * D_FF), BF16), pltpu.VMEM((D_FF, d), BF16),
                        pltpu.SemaphoreType.DMA((3,)), pltpu.SemaphoreType.DMA((2,))])
    w_elems = d * 2 * D_FF + D_FF * d
    vmem_bytes = (2 * 4 + 2) * w_elems + 5 * bm * d * 4 + bm * (2 * D_FF + D_FF + 2 * d) * 4
    vmem_bytes = min(vmem_bytes + (4 << 20), VMEM_PHYSICAL - (6 << 20))
    return pl.pallas_call(
        functools.partial(_moe_gmm_kernel, bm=bm), grid_spec=grid_spec,
        out_shape=jax.ShapeDtypeStruct((n_blocks * bm // SUBLANES, SUBLANES, n_sub, LANES), F32),
        compiler_params=pltpu.CompilerParams(dimension_semantics=("arbitrary",), vmem_limit_bytes=vmem_bytes),
        name="moe_gmm")(
            block_expert, n_used, row_tok, row_tok, row_tok, x, w_in, b_in.reshape(b_in.shape[0], 1, -1),
            w_out, b_out.reshape(b_out.shape[0], 1, -1))


def _combine_ln_kernel(dst_ref, dst_next_ref, y_hbm, gate_ref, x_ref, g_ref, b_ref, o_ref, gbuf, sem):
    i = pl.program_id(0)
    n = pl.num_programs(0)
    slot = lax.rem(i, 2)
    groups = gbuf.shape[2]

    def start_gather(idx_ref, s):
        def body(g, carry):
            for u in range(SUBLANES):
                for k in range(TOP_K):
                    d = idx_ref[0, 0, (g * SUBLANES + u) * TOP_K + k]
                    pltpu.make_async_copy(y_hbm.at[d], gbuf.at[s, k, g, :, u], sem.at[s]).start(priority=k % 2)
            return carry
        lax.fori_loop(0, groups, body, 0)

    @pl.when(i == 0)
    def _():
        start_gather(dst_ref, 0)

    @pl.when(i + 1 < n)
    def _():
        start_gather(dst_next_ref, 1 - slot)

    pltpu.make_async_copy(gbuf.at[slot], gbuf.at[slot], sem.at[slot]).wait()
    gates = gate_ref[...]
    acc = gates[:, 0:1] * _tile_rows(gbuf[slot, 0])
    for k in range(1, TOP_K):
        acc = acc + gates[:, k:k + 1] * _tile_rows(gbuf[slot, k])
    o_ref[...] = _layer_norm(DN_ALPHA * x_ref[...] + acc, g_ref[...], b_ref[...])


def combine_ln(y, dest, gates, x_res, g, b, tc=256):
    t, d = x_res.shape
    tc = min(tc, t)
    assert t % tc == 0
    n = t // tc
    n_sub = d // LANES
    dest_flat = dest[:, :TOP_K].reshape(n, 1, tc * TOP_K)
    idx_spec = lambda off: pl.BlockSpec((1, 1, tc * TOP_K), lambda i: (jnp.minimum(i + off, n - 1), 0, 0),
                                        memory_space=pltpu.SMEM)
    return pl.pallas_call(
        _combine_ln_kernel, grid=(n,),
        in_specs=[idx_spec(0), idx_spec(1), pl.BlockSpec(memory_space=pl.ANY),
                  pl.BlockSpec((tc, LANES), lambda i: (i, 0)),
                  pl.BlockSpec((tc, d), lambda i: (i, 0)), pl.BlockSpec((1, d), lambda i: (0, 0)),
                  pl.BlockSpec((1, d), lambda i: (0, 0))],
        out_specs=pl.BlockSpec((tc, d), lambda i: (i, 0)),
        out_shape=jax.ShapeDtypeStruct((t, d), F32),
        scratch_shapes=[pltpu.VMEM((2, TOP_K, tc // SUBLANES, n_sub, SUBLANES, LANES), F32),
                        pltpu.SemaphoreType.DMA((2,))],
        compiler_params=_cparams("arbitrary"), name="combine_ln")(
            dest_flat, dest_flat, y.reshape(-1, n_sub, LANES), gates, x_res, g, b)


def _slot_runs_kernel(start_ref, len_ref, src_ref, o_ref, *, per_step, n_row):
    base = pl.program_id(0) * per_step
    lane = lax.broadcasted_iota(jnp.int32, (n_row, LANES), 1)
    pos = lax.broadcasted_iota(jnp.int32, (n_row, LANES), 0) * LANES + lane
    for j in range(per_step):
        start = start_ref[base + j]
        off = lax.bitwise_and(start, LANES - 1)
        rows = src_ref[pl.ds(lax.shift_right_logical(start, 7), SUBLANES), :]
        rolled = pltpu.roll(rows, lax.bitwise_and(LANES - off, LANES - 1), axis=1)
        run = jnp.where(lane + off < LANES, rolled[:n_row], rolled[1:n_row + 1])
        o_ref[j] = jnp.where(pos < len_ref[base + j], run, 0)


def slot_runs(src, starts, lens, width):
    n_runs = starts.shape[0]
    n_row = max(width // LANES, 1)
    assert n_row < SUBLANES
    per_step = next(d for d in (16, 12, 9, 8, 6, 4, 3, 2, 1) if n_runs % d == 0)
    rows = -(-src.shape[0] // LANES) + SUBLANES
    src2 = jnp.pad(src, (0, rows * LANES - src.shape[0])).reshape(rows, LANES)
    grid_spec = pltpu.PrefetchScalarGridSpec(
        num_scalar_prefetch=2, grid=(n_runs // per_step,),
        in_specs=[pl.BlockSpec((rows, LANES), lambda i, s, l: (0, 0))],
        out_specs=pl.BlockSpec((per_step, n_row, LANES), lambda i, s, l: (i, 0, 0)))
    out = pl.pallas_call(
        functools.partial(_slot_runs_kernel, per_step=per_step, n_row=n_row), grid_spec=grid_spec,
        out_shape=jax.ShapeDtypeStruct((n_runs, n_row, LANES), jnp.int32),
        compiler_params=_cparams("arbitrary"), name="slot_runs")(starts, lens, src2)
    return out.reshape(n_runs, n_row * LANES)[:, :width]


def moe_ln(x, x_tiles, idx, gates, w_in, b_in, w_out, b_out, g, b, bm, e_off=0):
    t, d = x.shape
    tk = t * TOP_K
    dest, pad_end, counts = route_slots(idx, bm)
    n_blocks = -(-tk // bm) + N_EXPERTS
    n_rows = n_blocks * bm
    block_start = jnp.arange(n_blocks, dtype=jnp.int32) * bm
    block_expert = jnp.sum((pad_end[None, :] <= block_start[:, None]).astype(jnp.int32), axis=1)
    block_expert = jnp.minimum(block_expert, N_EXPERTS - 1)
    tok = jnp.arange(tk, dtype=jnp.int32) // TOP_K
    _, tok_sorted = lax.sort_key_val(dest[:, :TOP_K].reshape(tk), tok)
    pad_start = jnp.concatenate([jnp.zeros((1,), jnp.int32), pad_end[:-1]])
    used_end = pad_start + counts
    padding_before = pad_start - (jnp.cumsum(counts) - counts)
    run_start = jnp.clip(block_start - padding_before[block_expert], 0, tk)
    run_len = jnp.clip(used_end[block_expert] - block_start, 0, bm)
    row_tok = slot_runs(tok_sorted, run_start, run_len, bm).reshape(n_blocks, 1, bm)
    block_expert = block_expert + e_off
    n_used = (pad_end[-1] // bm).astype(jnp.int32).reshape(1)
    y = moe_gmm(x_tiles, row_tok, block_expert, n_used, w_in, b_in, w_out, b_out, bm)
    return combine_ln(y, dest, gates, x, g, b)


def _t5_causal_bucket(dist):
    n = jnp.maximum(dist, 0)
    max_exact = N_BUCKETS // 2
    nf = jnp.maximum(n, max_exact).astype(F32)
    large = max_exact + (jnp.log(nf / max_exact) / math.log(MAX_DISTANCE / max_exact)
                         * (N_BUCKETS - max_exact)).astype(jnp.int32)
    large = jnp.minimum(large, N_BUCKETS - 1)
    return jnp.where(n < max_exact, n, large)


def _split_maps(q):
    lane = lax.broadcasted_iota(jnp.int32, q.shape, 1)
    qs = q * (DH_C ** -0.5)
    return (jnp.where(lane < DH_C, qs, 0.0).astype(BF16), jnp.where(lane >= DH_C, qs, 0.0).astype(BF16))


def _diff_finish(acc1, l1, acc2, l2, lam, out_scale, g):
    o = acc1 / l1 - lam * (acc2 / l2)
    o = o * lax.rsqrt(jnp.mean(o * o, axis=-1, keepdims=True) + NORM_EPS) * g
    return o * out_scale


def _diff_prompt_kernel(sc_ref, far_ref, q_ref, k_ref, v_ref, bias_ref, g_ref, o_ref, m_ref, a_ref, *, tq, hp):
    h0 = pl.program_id(1) * hp
    qi = pl.program_id(2)
    head = lambda x, h: x[:, h * VD_C:(h + 1) * VD_C]
    q = q_ref[...]
    qm = jnp.stack([jnp.concatenate(_split_maps(head(q, h)), axis=0) for h in range(hp)], axis=0)
    m_ref[...] = jnp.full_like(m_ref, -jnp.inf)
    a_ref[...] = jnp.zeros_like(a_ref)
    ones_col = (lax.broadcasted_iota(jnp.int32, (tq, LANES), 1) == 0).astype(BF16)
    reps = tq // LANES

    def update(tiles):
        scores, values = [], []
        for kidx, bias, shift, causal in tiles:
            start = pl.multiple_of(kidx * tq, tq)
            kt = k_ref[0, pl.ds(start, tq), :].astype(BF16)
            vt = v_ref[0, pl.ds(start, tq), :].astype(BF16)
            values.append(jnp.stack([jnp.concatenate([head(vt, h), ones_col], axis=-1) for h in range(hp)], axis=0))
            s = _bmm_nt(qm, jnp.stack([head(kt, h) for h in range(hp)], axis=0))
            if bias is not None:
                s = s + jnp.concatenate([bias, bias], axis=1)
            if causal:
                r = lax.broadcasted_iota(jnp.int32, s.shape[1:], 0)
                c = lax.broadcasted_iota(jnp.int32, s.shape[1:], 1)
                s = jnp.where(c <= lax.bitwise_and(r, tq - 1), s, -jnp.inf)
            scores.append((s, shift))
        m_old = m_ref[...]
        m_new = m_old
        for s, shift in scores:
            m_new = jnp.maximum(m_new, jnp.max(s, axis=-1, keepdims=True) + shift)
        alpha = jnp.exp(m_old - m_new)
        acc = jnp.concatenate([alpha, alpha], axis=-1) * a_ref[...]
        for (s, shift), vt in zip(scores, values):
            p = jnp.exp(s - jnp.concatenate([m_new - shift] * reps, axis=-1))
            acc = acc + _bmm(p, vt)
        a_ref[...] = acc
        m_ref[...] = m_new

    far_bias = jnp.stack([jnp.full((1, 1), far_ref[h0 + h], F32) for h in range(hp)], axis=0)
    no_shift = jnp.zeros((hp, 1, 1), F32)
    n_far = jnp.maximum(qi - 1, 0)

    def far_body(j, carry):
        update([(2 * j, None, far_bias, False), (2 * j + 1, None, far_bias, False)])
        return carry

    lax.fori_loop(0, n_far // 2, far_body, 0)

    @pl.when(lax.rem(n_far, 2) == 1)
    def _():
        update([(n_far - 1, None, far_bias, False)])

    @pl.when(qi >= 1)
    def _():
        update([(qi - 1, bias_ref[:, 1], no_shift, False), (qi, bias_ref[:, 0], no_shift, True)])

    @pl.when(qi == 0)
    def _():
        update([(0, bias_ref[:, 0], no_shift, True)])

    acc = a_ref[...]
    o_ref[...] = jnp.concatenate(
        [_diff_finish(acc[h, :tq, :VD_C], acc[h, :tq, VD_C:VD_C + 1], acc[h, tq:, :VD_C], acc[h, tq:, VD_C:VD_C + 1],
                      sc_ref[0], sc_ref[1], g_ref[...]) for h in range(hp)], axis=-1)


def _bias_tiles_kernel(rb_ref, bucket_ref, o_ref):
    h = pl.program_id(0)
    b = bucket_ref[...]
    out = jnp.zeros(b.shape, F32)
    for k in range(N_BUCKETS):
        out = jnp.where(b == k, rb_ref[k * H_C + h], out)
    o_ref[0] = out


DIFF_HEADS_PER_STEP = 2


def diff_attn_prompt_core(q, k, v, bsz, rel_bias, lam, lam_init, subln_g, tq=256):
    m = q.shape[0]
    t = m // bsz
    nq = t // tq
    ii = jnp.arange(tq)[:, None]
    jj = jnp.arange(tq)[None, :]
    buckets = _t5_causal_bucket(jnp.stack([jnp.maximum(ii - jj, 0), tq + ii - jj])).astype(jnp.int32)
    bias = pl.pallas_call(
        _bias_tiles_kernel, grid=(H_C,),
        in_specs=[pl.BlockSpec(memory_space=pltpu.SMEM), pl.BlockSpec((2, tq, tq), lambda h: (0, 0, 0))],
        out_specs=pl.BlockSpec((1, 2, tq, tq), lambda h: (h, 0, 0, 0)),
        out_shape=jax.ShapeDtypeStruct((H_C, 2, tq, tq), F32),
        compiler_params=_cparams("parallel"), name="t5_bias_tiles")(rel_bias.astype(F32).reshape(-1), buckets)
    far = rel_bias[_t5_causal_bucket(jnp.array(2 * tq))].astype(F32)
    scal = jnp.stack([lam, 1.0 - lam_init]).astype(F32)
    k3 = k.reshape(bsz, t, C_W)
    v3 = v.reshape(bsz, t, C_W)
    smem = pl.BlockSpec(memory_space=pltpu.SMEM)
    hp = DIFF_HEADS_PER_STEP
    kv_spec = pl.BlockSpec((1, t, hp * VD_C), lambda b, h, i: (b, 0, h))
    row = pl.BlockSpec((tq, hp * VD_C), lambda b, h, i: (b * nq + i, h))
    stat = pltpu.VMEM((hp, 2 * tq, LANES), F32)
    acc = pltpu.VMEM((hp, 2 * tq, VD_C + LANES), F32)
    return pl.pallas_call(
        functools.partial(_diff_prompt_kernel, tq=tq, hp=hp), grid=(bsz, H_C // hp, nq),
        in_specs=[smem, smem, row, kv_spec, kv_spec,
                  pl.BlockSpec((hp, 2, tq, tq), lambda b, h, i: (h, 0, 0, 0)),
                  pl.BlockSpec((1, VD_C), lambda b, h, i: (0, 0))],
        out_specs=row, out_shape=jax.ShapeDtypeStruct((m, C_W), F32),
        scratch_shapes=[stat, acc],
        compiler_params=_cparams("parallel", "parallel", "parallel"), name="diff_attn_prompt")(
            scal, far, q, k3, v3, bias, subln_g.reshape(1, VD_C))


def _diff_sample_kernel(pt_ref, sc_ref, q_ref, kn_ref, vn_ref, bm_ref, bnew_ref, g_ref, *refs, n_pp):
    k_refs, v_refs = refs[:n_pp], refs[n_pp:2 * n_pp]
    o_ref, m_ref, l_ref, a_ref = refs[2 * n_pp:]
    p = pl.program_id(1)
    n_p = pl.num_programs(1)
    q8 = q_ref[0]
    qm = jnp.concatenate(_split_maps(q8), axis=0)

    @pl.when(p == 0)
    def _():
        m_ref[...] = jnp.full_like(m_ref, -jnp.inf)
        l_ref[...] = jnp.zeros_like(l_ref)
        a_ref[...] = jnp.zeros_like(a_ref)

    scores = []
    for j in range(n_pp):
        bias = bm_ref[jnp.where(p == n_p - 1, 1, 0)] if j == n_pp - 1 else bm_ref[0]
        k2 = k_refs[j][0, 0].reshape(-1, 2 * DH_C).astype(BF16)
        scores.append(lax.dot_general(qm, k2, (((1,), (1,)), ((), ())), preferred_element_type=F32) + bias)
    m_old = m_ref[...]
    m_new = m_old
    for s in scores:
        m_new = jnp.maximum(m_new, jnp.max(s, axis=-1, keepdims=True))
    alpha = jnp.exp(m_old - m_new)
    l_new = alpha * l_ref[...]
    a_new = alpha * a_ref[...]
    for j, s in enumerate(scores):
        pr = jnp.exp(s - m_new)
        l_new = l_new + jnp.sum(pr, axis=-1, keepdims=True)
        v2 = v_refs[j][0, 0].reshape(-1, VD_C).astype(BF16)
        a_new = a_new + jnp.dot(pr.astype(BF16), v2, preferred_element_type=F32)
    l_ref[...] = l_new
    a_ref[...] = a_new
    m_ref[...] = m_new

    @pl.when(p == n_p - 1)
    def _():
        kn = jnp.concatenate([kn_ref[0], kn_ref[0]], axis=0)
        vn = jnp.concatenate([vn_ref[0], vn_ref[0]], axis=0)
        s = jnp.sum(qm.astype(F32) * kn, axis=-1, keepdims=True) + bnew_ref[:, 0:1]
        m_old = m_ref[...]
        m_new = jnp.maximum(m_old, s)
        alpha = jnp.exp(m_old - m_new)
        pr = jnp.exp(s - m_new)
        l = alpha * l_ref[...] + pr
        a = alpha * a_ref[...] + pr * vn
        o_ref[0] = _diff_finish(a[:H_C], l[:H_C], a[H_C:], l[H_C:], sc_ref[0], sc_ref[1], g_ref[...])


def diff_attn_sample_core(q, k_new, v_new, cache_k, cache_v, page_table, layer_j, rel_bias, lam, lam_init, subln_g,
                          n_pp=8):
    bsz = q.shape[0]
    n_pages = page_table.shape[1]
    page = cache_k.shape[2]
    past = n_pages * page
    assert n_pages % n_pp == 0
    dist_last = past - ((n_pages - 1) * page + jnp.arange(page))
    b_last = rel_bias[_t5_causal_bucket(dist_last)].astype(F32)
    b_far = jnp.broadcast_to(rel_bias[_t5_causal_bucket(jnp.array(page + 1))].astype(F32), (page, H_C))
    eye = jnp.eye(H_C, dtype=bool)

    def expand(bt):
        full = jnp.where(eye[:, None, :], bt.T[:, :, None], -jnp.inf).reshape(H_C, page * H_C)
        return jnp.concatenate([full, full], axis=0)

    bm = jnp.stack([expand(b_far), expand(b_last)])
    b_new = rel_bias[_t5_causal_bucket(jnp.array(0))].astype(F32)
    b_new = jnp.broadcast_to(jnp.concatenate([b_new, b_new])[:, None], (2 * H_C, LANES))
    scal = jnp.stack([lam, 1.0 - lam_init]).astype(F32)
    smem = pl.BlockSpec(memory_space=pltpu.SMEM)
    head3 = pl.BlockSpec((1, H_C, VD_C), lambda b, p, pt: (b, 0, 0))
    full = lambda shape: pl.BlockSpec(shape, lambda b, p, pt: (0,) * len(shape))

    def page_spec(j):
        return pl.BlockSpec((1, 1, page, H_C, VD_C), lambda b, p, pt: (pt[b, p * n_pp + j], layer_j, 0, 0, 0))

    grid_spec = pltpu.PrefetchScalarGridSpec(
        num_scalar_prefetch=1, grid=(bsz, n_pages // n_pp),
        in_specs=[smem, head3, head3, head3, full(bm.shape), full(b_new.shape), full((1, VD_C))]
        + [page_spec(j) for j in range(n_pp)] * 2,
        out_specs=head3,
        scratch_shapes=[pltpu.VMEM((2 * H_C, 1), F32), pltpu.VMEM((2 * H_C, 1), F32),
                        pltpu.VMEM((2 * H_C, VD_C), F32)])
    out = pl.pallas_call(
        functools.partial(_diff_sample_kernel, n_pp=n_pp), grid_spec=grid_spec,
        out_shape=jax.ShapeDtypeStruct((bsz, H_C, VD_C), F32),
        compiler_params=_cparams("parallel", "arbitrary"), name="diff_attn_sample")(
            page_table, scal, q.reshape(bsz, H_C, VD_C), k_new.reshape(bsz, H_C, VD_C),
            v_new.reshape(bsz, H_C, VD_C), bm, b_new, subln_g.reshape(1, VD_C),
            *([cache_k] * n_pp), *([cache_v] * n_pp))
    return out.reshape(bsz, C_W)


LANE_BETA, LANE_A, LANE_I, LANE_F = 0, H_A, 2 * H_A, 2 * H_A + H_B


def _softplus(x):
    return jnp.maximum(x, 0.0) + jnp.log1p(jnp.exp(-jnp.abs(x)))


def _silu(x):
    return x * jax.nn.sigmoid(x)


def _lanes(shape, lo, n):
    lane = lax.broadcasted_iota(jnp.int32, shape, 1)
    return jnp.logical_and(lane >= lo, lane < lo + n)


def _gate_tile(gt, alog_row, prow):
    z = gt + prow
    return jax.nn.sigmoid(gt), -jnp.exp(alog_row) * _softplus(z), z, -_softplus(-z)


def _nt(a, b, precision=None):
    return lax.dot_general(a, b, (((1,), (1,)), ((), ())), preferred_element_type=F32, precision=precision)


def _tn(a, b):
    return lax.dot_general(a, b, (((0,), (0,)), ((), ())), preferred_element_type=F32)


def _row_selector(n_heads, length, lanes_of_head):
    r = lax.broadcasted_iota(jnp.int32, (n_heads * length, LANES), 0) // length
    lane = lax.broadcasted_iota(jnp.int32, (n_heads * length, LANES), 1)
    sel = jnp.zeros((n_heads * length, LANES), F32)
    for lo in lanes_of_head:
        sel = sel + (lane == r + lo).astype(F32)
    return sel


def _tri_masks(n):
    r = lax.broadcasted_iota(jnp.int32, (n, n), 0)
    c = lax.broadcasted_iota(jnp.int32, (n, n), 1)
    return r, c


def _bmm(a, b):
    return lax.dot_general(a.astype(BF16), b.astype(BF16), (((2,), (1,)), ((0,), (0,))),
                           preferred_element_type=F32)


def _bmm_nt(a, b):
    return lax.dot_general(a.astype(BF16), b.astype(BF16), (((2,), (2,)), ((0,), (0,))),
                           preferred_element_type=F32)


def _heads(x, n_heads, width, offset=0):
    return jnp.stack([x[c * CHUNK:(c + 1) * CHUNK, offset + h * width:offset + (h + 1) * width]
                      for c in range(x.shape[0] // CHUNK) for h in range(n_heads)], axis=0)


def _head_cols(x, n_heads, lane0):
    return jnp.stack([x[c * CHUNK:(c + 1) * CHUNK, lane0 + h:lane0 + h + 1]
                      for c in range(x.shape[0] // CHUNK) for h in range(n_heads)], axis=0)


def _unheads(x, n_heads):
    n = x.shape[0] // n_heads
    return jnp.concatenate([jnp.concatenate([x[c * n_heads + h] for h in range(n_heads)], axis=-1)
                            for c in range(n)], axis=0)


def _chunk_cumsum(x):
    r, c = _tri_masks(x.shape[0])
    tri = jnp.logical_and(c <= r, r // CHUNK == c // CHUNK).astype(F32)
    return jnp.dot(tri, x, preferred_element_type=F32, precision=HI)


def _chunk_rows(sel, x):
    n = x.shape[0] // CHUNK
    rows = [_nt(sel, x[c * CHUNK:(c + 1) * CHUNK], HI) for c in range(n)]
    return jnp.concatenate(rows, axis=0).reshape(n * sel.shape[0] // CHUNK, CHUNK, CHUNK)


def _unit_lower_inverse(nmat, r, c):
    mm = _bmm
    eye = (r == c).astype(F32)
    same = (r // 16) == (c // 16)
    nd = jnp.where(same, nmat, 0.0)
    off = nmat - nd
    dinv = eye - nd
    p = nd
    for _ in range(3):
        p = mm(p, p)
        dinv = dinv + mm(dinv, p)
    m = mm(dinv, off)
    m2 = mm(m, m)
    left = eye - m
    left = left + mm(left, m2)
    return mm(left, dinv)


def _gdn_prompt_kernel(qkv_ref, z_ref, gt_ref, convw_ref, alog_ref, prow_ref, ng_ref, mix_ref, s_ref, ext_ref):
    cidx = pl.program_id(1)
    L = CHUNK
    rows = qkv_ref.shape[0]
    n_ch = rows // L

    @pl.when(cidx == 0)
    def _():
        ext_ref[0:8, :] = jnp.zeros((8, QKV_A), F32)
        s_ref[...] = jnp.zeros_like(s_ref)

    ext_ref[8:8 + rows, :] = qkv_ref[...]
    acc = convw_ref[0:1, :] * ext_ref[pl.ds(8 - (CONV_W - 1), rows), :]
    for i in range(1, CONV_W):
        acc = acc + convw_ref[i:i + 1, :] * ext_ref[pl.ds(8 - (CONV_W - 1) + i, rows), :]
    ext_ref[0:8, :] = ext_ref[rows:rows + 8, :]
    cs = _silu(acc)

    beta, g, _, _ = _gate_tile(gt_ref[...], alog_ref[...], prow_ref[...])
    r, c = _tri_masks(L)
    incl = c <= r
    strict = c < r
    gcum = _chunk_cumsum(jnp.where(_lanes(g.shape, LANE_A, H_A), g, 0.0))
    gc_rows = _chunk_rows(_row_selector(H_A, L, (LANE_A,)), gcum)
    q = _heads(cs, H_A, DK_A)
    k = _heads(cs, H_A, DK_A, H_A * DK_A)
    v = _heads(cs, H_A, DV_A, 2 * H_A * DK_A)
    q = q * lax.rsqrt(jnp.sum(q * q, axis=-1, keepdims=True) + NORM_EPS) * (DK_A ** -0.5)
    k = k * lax.rsqrt(jnp.sum(k * k, axis=-1, keepdims=True) + NORM_EPS)
    beta_c = _head_cols(beta, H_A, LANE_BETA)
    gc_c = _head_cols(gcum, H_A, LANE_A)
    decay = jnp.where(incl, jnp.exp(jnp.where(incl, gc_c - gc_rows, 0.0)), 0.0)
    kb = k.astype(BF16)
    nmat = jnp.where(strict, beta_c * _bmm_nt(kb, kb) * decay, 0.0)
    egc = jnp.exp(gc_c)
    rhs = jnp.concatenate([v * beta_c, k * (beta_c * egc)], axis=-1)
    sol = _bmm(_unit_lower_inverse(nmat, r, c), rhs)
    u = sol[:, :, :DV_A]
    wq = jnp.concatenate([sol[:, :, DV_A:], q * egc], axis=1).astype(BF16)
    qk = (_bmm_nt(q, kb) * decay).astype(BF16)
    gc_last = gc_c[:, L - 1:L]
    k_tail = (k * jnp.exp(gc_last - gc_c)).astype(BF16)
    g_tail = jnp.exp(gc_last)
    state = s_ref[0]
    outs = []
    for ci in range(n_ch):
        sl = slice(ci * H_A, (ci + 1) * H_A)
        ws = _bmm(wq[sl], state)
        delta = u[sl] - ws[:, :L]
        outs.append(ws[:, L:] + _bmm(qk[sl], delta))
        delta_b = delta.astype(BF16)
        state = state * g_tail[sl] + jnp.stack([_tn(k_tail[ci * H_A + h], delta_b[h]) for h in range(H_A)], axis=0)
    s_ref[0] = state
    o = jnp.concatenate(outs, axis=0)
    o = o * lax.rsqrt(jnp.mean(o * o, axis=-1, keepdims=True) + NORM_EPS) * ng_ref[...]
    o = o * _silu(_heads(z_ref[...], H_A, DV_A))
    mix_ref[...] = _unheads(o, H_A)


AB_CHUNKS_PER_STEP = 4


def gdn_prompt(qkv, z, gates, conv_w, alog_row, prow, norm_g, bsz):
    m = qkv.shape[0]
    rows = AB_CHUNKS_PER_STEP * CHUNK
    nc = m // bsz // rows
    row = lambda w: pl.BlockSpec((rows, w), lambda b, c: (b * nc + c, 0))
    full = lambda shape: pl.BlockSpec(shape, lambda b, c: (0,) * len(shape))
    return pl.pallas_call(
        _gdn_prompt_kernel, grid=(bsz, nc),
        in_specs=[row(QKV_A), row(H_A * DV_A), row(LANES), full(conv_w.shape), full((1, LANES)), full((1, LANES)),
                  full((1, DV_A))],
        out_specs=[row(H_A * DV_A), pl.BlockSpec((1, H_A, DK_A, DV_A), lambda b, c: (b, 0, 0, 0))],
        out_shape=[jax.ShapeDtypeStruct((m, H_A * DV_A), F32), jax.ShapeDtypeStruct((bsz, H_A, DK_A, DV_A), F32)],
        scratch_shapes=[pltpu.VMEM((rows + 8, QKV_A), F32)],
        compiler_params=_cparams("parallel", "arbitrary"), name="gdn_prompt")(
            qkv, z, gates, conv_w, alog_row, prow, norm_g.reshape(1, DV_A))


def _mlstm_prompt_kernel(q_ref, k_ref, v_ref, og_ref, gt_ref, alog_ref, prow_ref, ng_ref, mix_ref, c_ref, m_ref):
    cidx = pl.program_id(1)
    L = CHUNK

    @pl.when(cidx == 0)
    def _():
        c_ref[...] = jnp.zeros_like(c_ref)
        m_ref[...] = jnp.zeros_like(m_ref)

    n_ch = q_ref.shape[0] // L
    _, _, ipre, logf = _gate_tile(gt_ref[...], alog_ref[...], prow_ref[...])
    r, c = _tri_masks(L)
    incl = c <= r
    fsel = _lanes(logf.shape, LANE_F, H_B)
    bcum = _chunk_cumsum(jnp.where(fsel, logf, 0.0))
    rowvals = jnp.where(_lanes(ipre.shape, LANE_I, H_B), ipre, 0.0) - jnp.where(fsel, bcum, 0.0)
    rows = _chunk_rows(_row_selector(H_B, L, (LANE_I, LANE_F)), rowvals)
    mrow = m_ref[0]
    lane_row = lax.broadcasted_iota(jnp.int32, mrow.shape, 1)
    qb = _heads(q_ref[...], H_B, DQK_B).astype(BF16)
    ks = _heads(k_ref[...], H_B, DQK_B) * (DQK_B ** -0.5)
    ones_col = jnp.broadcast_to((lax.broadcasted_iota(jnp.int32, (L, LANES), 1) == 0).astype(F32),
                                (n_ch * H_B, L, LANES))
    v_ext = jnp.concatenate([_heads(v_ref[...], H_B, DV_B), ones_col], axis=-1).astype(BF16)
    b_c = _head_cols(bcum, H_B, LANE_F)
    i_c = _head_cols(ipre, H_B, LANE_I)
    dmat = jnp.where(incl, b_c + rows, -jnp.inf)
    m_intra = jnp.max(dmat, axis=-1, keepdims=True)
    w_intra = jnp.exp(dmat - m_intra) * _bmm_nt(qb, ks)
    nd_intra = _bmm(w_intra, v_ext)
    b_last = b_c[:, L - 1:L]
    e_end = b_last - b_c + i_c
    e_max = jnp.max(e_end, axis=1, keepdims=True)
    kw = (ks * jnp.exp(e_end - e_max)).astype(BF16)
    kv_end = jnp.stack([_tn(kw[i], v_ext[i]) for i in range(n_ch * H_B)], axis=0)
    m_prev = jnp.stack([mrow[:, h:h + 1] for h in range(H_B)], axis=0)
    state = c_ref[0]
    outs = []
    for ci in range(n_ch):
        sl = slice(ci * H_B, (ci + 1) * H_B)
        inter = b_c[sl] + m_prev
        m_t = jnp.maximum(inter, m_intra[sl])
        nd = jnp.exp(inter - m_t) * _bmm(qb[sl], state) + jnp.exp(m_intra[sl] - m_t) * nd_intra[sl]
        outs.append(nd[:, :, :DV_B] / jnp.maximum(jnp.abs(nd[:, :, DV_B:DV_B + 1]), jnp.exp(-m_t)))
        m_new = jnp.maximum(b_last[sl] + m_prev, e_max[sl])
        state = jnp.exp(b_last[sl] + m_prev - m_new) * state + jnp.exp(e_max[sl] - m_new) * kv_end[sl]
        m_prev = m_new
    c_ref[0] = state
    for h in range(H_B):
        mrow = jnp.where(lane_row == h, m_prev[h], mrow)
    m_ref[0] = mrow
    hh = jnp.concatenate(outs, axis=0)
    hh = hh * lax.rsqrt(jnp.mean(hh * hh, axis=-1, keepdims=True) + NORM_EPS) * ng_ref[...]
    hh = jax.nn.sigmoid(_heads(og_ref[...], H_B, DV_B)) * hh
    mix_ref[...] = _unheads(hh, H_B)


def mlstm_prompt(q, k, v, og, gates, alog_row, prow, norm_g, bsz):
    m = q.shape[0]
    rows = AB_CHUNKS_PER_STEP * CHUNK
    nc = m // bsz // rows
    row = lambda w: pl.BlockSpec((rows, w), lambda b, c: (b * nc + c, 0))
    full = lambda shape: pl.BlockSpec(shape, lambda b, c: (0,) * len(shape))
    return pl.pallas_call(
        _mlstm_prompt_kernel, grid=(bsz, nc),
        in_specs=[row(H_B * DQK_B), row(H_B * DQK_B), row(H_B * DV_B), row(H_B * DV_B), row(LANES),
                  full((1, LANES)), full((1, LANES)), full((1, DV_B))],
        out_specs=[row(H_B * DV_B), pl.BlockSpec((1, H_B, DQK_B, DV_B + LANES), lambda b, c: (b, 0, 0, 0)),
                   pl.BlockSpec((1, 1, LANES), lambda b, c: (b, 0, 0))],
        out_shape=[jax.ShapeDtypeStruct((m, H_B * DV_B), F32),
                   jax.ShapeDtypeStruct((bsz, H_B, DQK_B, DV_B + LANES), F32),
                   jax.ShapeDtypeStruct((bsz, 1, LANES), F32)],
        compiler_params=_cparams("parallel", "arbitrary"), name="mlstm_prompt")(
            q, k, v, og, gates, alog_row, prow, norm_g.reshape(1, DV_B))


def _columns(x8):
    n = x8.shape[1]
    r, c = _tri_masks(n)
    return _nt((r == c).astype(F32), x8, HI)


def _ab_sample_kernel(qn_ref, kn_ref, vn_ref, cq_ref, ck_ref, cv_ref, wq_ref, wk_ref, wv_ref, z_ref,
                      qb_ref, kb_ref, vb_ref, og_ref, gt_ref, alog_ref, prow_ref, nga_ref, ngb_ref,
                      s_in, c_in, n_in, m_in,
                      oa_ref, ob_ref, s_out, c_out, n_out, m_out):
    def conv(new_ref, prev_ref, w_ref):
        acc = w_ref[CONV_W - 1] * new_ref[0]
        for i in range(CONV_W - 1):
            acc = acc + w_ref[i] * prev_ref[0, i]
        return _silu(acc)

    q8 = conv(qn_ref, cq_ref, wq_ref)
    k8 = conv(kn_ref, ck_ref, wk_ref)
    v8 = conv(vn_ref, cv_ref, wv_ref)
    q8 = q8 * lax.rsqrt(jnp.sum(q8 * q8, axis=-1, keepdims=True) + NORM_EPS) * (DK_A ** -0.5)
    k8 = k8 * lax.rsqrt(jnp.sum(k8 * k8, axis=-1, keepdims=True) + NORM_EPS)
    beta, g, ipre, logf = _gate_tile(gt_ref[0], alog_ref[...], prow_ref[...])
    q_cols, k_cols = _columns(q8), _columns(k8)
    z8 = z_ref[0]
    outs = []
    for h in range(H_A):
        s = s_in[0, h] * jnp.exp(g[:, LANE_A + h:LANE_A + h + 1])
        kc = k_cols[:, h:h + 1]
        err = v8[h:h + 1] - jnp.sum(kc * s, axis=0, keepdims=True)
        s = s + kc * (beta[:, LANE_BETA + h:LANE_BETA + h + 1] * err)
        s_out[0, h] = s
        outs.append(jnp.sum(q_cols[:, h:h + 1] * s, axis=0, keepdims=True))
    o = jnp.concatenate(outs, axis=0)
    o = o * lax.rsqrt(jnp.mean(o * o, axis=-1, keepdims=True) + NORM_EPS) * nga_ref[...]
    oa_ref[0] = o * _silu(z8)

    zeros4 = jnp.zeros((8 - H_B, DQK_B), F32)
    qb_cols = _columns(jnp.concatenate([qb_ref[0], zeros4], axis=0))
    kb_cols = _columns(jnp.concatenate([kb_ref[0] * (DQK_B ** -0.5), zeros4], axis=0))
    vb = vb_ref[0]
    n_cols = n_in[0]
    m_row = m_in[0]
    lane_n = lax.broadcasted_iota(jnp.int32, n_cols.shape, 1)
    lane_m = lax.broadcasted_iota(jnp.int32, m_row.shape, 1)
    outs = []
    for h in range(H_B):
        lf = logf[:, LANE_F + h:LANE_F + h + 1]
        it = ipre[:, LANE_I + h:LANE_I + h + 1]
        m_prev = m_row[:, h:h + 1]
        m_new = jnp.maximum(lf + m_prev, it)
        f_sc = jnp.exp(lf + m_prev - m_new)
        i_sc = jnp.exp(it - m_new)
        kc = kb_cols[:, h:h + 1]
        qc = qb_cols[:, h:h + 1]
        cm = f_sc * c_in[0, h] + i_sc * (kc * vb[h:h + 1])
        nn = f_sc * n_cols[:, h:h + 1] + i_sc * kc
        c_out[0, h] = cm
        n_cols = jnp.where(lane_n == h, nn, n_cols)
        m_row = jnp.where(lane_m == h, m_new, m_row)
        num = jnp.sum(qc * cm, axis=0, keepdims=True)
        den = jnp.sum(qc * nn, axis=0, keepdims=True)
        outs.append(num / jnp.maximum(jnp.abs(den), jnp.exp(-m_new)))
    hb = jnp.concatenate(outs, axis=0)
    hb = hb * lax.rsqrt(jnp.mean(hb * hb, axis=-1, keepdims=True) + NORM_EPS) * ngb_ref[...]
    ob_ref[0] = jax.nn.sigmoid(og_ref[0]) * hb
    n_out[0] = n_cols
    m_out[0] = m_row


def ab_sample(qkv, z, q_b, k_b, v_b, o_b, gates, conv_prev, conv_w, alog_row, prow, norm_g_a, norm_g_b,
              s_prev, c_prev, n_prev, m_prev):
    bsz = qkv.shape[0]
    hk = H_A * DK_A
    part = lambda x, i, w: x[..., i * hk:i * hk + H_A * w].reshape(x.shape[:-1] + (H_A, w))
    new_parts = [part(qkv, 0, DK_A), part(qkv, 1, DK_A), part(qkv, 2, DV_A)]
    prev_parts = [part(conv_prev, 0, DK_A), part(conv_prev, 1, DK_A), part(conv_prev, 2, DV_A)]
    w_parts = [part(conv_w, 0, DK_A), part(conv_w, 1, DK_A), part(conv_w, 2, DV_A)]
    args = new_parts + prev_parts + w_parts + [
        z.reshape(bsz, H_A, DV_A), q_b.reshape(bsz, H_B, DQK_B), k_b.reshape(bsz, H_B, DQK_B),
        v_b.reshape(bsz, H_B, DV_B), o_b.reshape(bsz, H_B, DV_B), gates.reshape(bsz, 1, LANES),
        alog_row, prow, norm_g_a.reshape(1, DV_A), norm_g_b.reshape(1, DV_B),
        s_prev, c_prev, jnp.swapaxes(n_prev, 1, 2), m_prev.reshape(bsz, 1, H_B)]

    def spec(x, batched):
        nd = x.ndim
        if batched:
            return pl.BlockSpec((1,) + x.shape[1:], lambda b: (b,) + (0,) * (nd - 1))
        return pl.BlockSpec(x.shape, lambda b: (0,) * nd)

    batched = [True] * 6 + [False] * 3 + [True] * 6 + [False] * 4 + [True] * 4
    out_shape = [jax.ShapeDtypeStruct((bsz, H_A, DV_A), F32), jax.ShapeDtypeStruct((bsz, H_B, DV_B), F32),
                 jax.ShapeDtypeStruct(s_prev.shape, F32), jax.ShapeDtypeStruct(c_prev.shape, F32),
                 jax.ShapeDtypeStruct((bsz, DQK_B, H_B), F32), jax.ShapeDtypeStruct((bsz, 1, H_B), F32)]
    oa, ob, s_new, c_new, n_new, m_new = pl.pallas_call(
        _ab_sample_kernel, grid=(bsz,),
        in_specs=[spec(x, bt) for x, bt in zip(args, batched)],
        out_specs=[spec(x, True) for x in out_shape], out_shape=out_shape,
        compiler_params=_cparams("parallel"), name="ab_sample")(*args)
    mix = jnp.concatenate([oa.reshape(bsz, H_A * DV_A), ob.reshape(bsz, H_B * DV_B)], axis=-1)
    return mix, s_new, c_new, jnp.swapaxes(n_new, 1, 2), m_new.reshape(bsz, H_B)


MOE_BLOCK_PROMPT = 256
MOE_BLOCK_SAMPLE = 32


def _ab_weights(w_in, a_log, dt_bias, b_i, b_f):
    sizes = (QKV_A, H_A * DV_A, H_A, H_A, H_B * DQK_B, H_B * DQK_B, H_B * DV_B, H_B * DV_B, H_B, H_B)
    offs = [0]
    for s in sizes:
        offs.append(offs[-1] + s)
    col = lambda i: w_in[:, offs[i]:offs[i + 1]]
    w_lo = tuple(col(i).astype(BF16) for i in (0, 1, 4, 5, 6, 7))
    w_gate = jnp.concatenate([col(2), col(3), col(8), col(9)], axis=1)
    w_gate = jnp.pad(w_gate, ((0, 0), (0, LANES - w_gate.shape[1])))
    zeros = lambda n: jnp.zeros((n,), F32)
    pad = LANES - 2 * H_A - 2 * H_B
    alog_row = jnp.concatenate([zeros(H_A), a_log.astype(F32), zeros(2 * H_B + pad)]).reshape(1, LANES)
    prow = jnp.concatenate([zeros(H_A), dt_bias.astype(F32), b_i.astype(F32), b_f.astype(F32),
                            zeros(pad)]).reshape(1, LANES)
    return w_lo, w_gate, alog_row, prow


def kernel(x_prompt, x_sample, state_delta_S, state_delta_conv, state_mlstm_C, state_mlstm_n, state_mlstm_m,
           cache_diff_k, cache_diff_v, cache_mem_k, cache_mem_v, page_table, mem_prompt,
           w_in_ab, conv_w_a, a_log_a, dt_bias_a, norm_g_a, b_i_b, b_f_b, norm_g_b, w_out_ab,
           w_qkv_c, lam_q1, lam_k1, lam_q2, lam_k2, subln_g_c, w_o_c, rel_bias,
           w_xq, w_xkv, w_xo, ln_g, ln_b, w_router, b_router, w_moe_in, b_moe_in, w_moe_out, b_moe_out):
    bp, t, d = x_prompt.shape
    bs = x_sample.shape[0]
    xp = x_prompt.reshape(bp * t, d)
    xs = x_sample.reshape(bs, d)
    mem2d = mem_prompt.reshape(bp * N_MEM, d)
    cmk = cache_mem_k.reshape(DEPTH * bs, N_MEM * H_X, DH_X)
    cmv = cache_mem_v.reshape(DEPTH * bs, N_MEM * H_X, DH_X)
    wm_in = w_moe_in.reshape(DEPTH * N_EXPERTS, d, 2 * D_FF)
    bm_in = b_moe_in.reshape(DEPTH * N_EXPERTS, 2 * D_FF)
    wm_out = w_moe_out.reshape(DEPTH * N_EXPERTS, D_FF, d)
    bm_out = b_moe_out.reshape(DEPTH * N_EXPERTS, d)
    p_S, p_conv, p_C, p_n, p_m, p_k, p_v, p_mk, p_mv = [], [], [], [], [], [], [], [], []
    s_S, s_conv, s_C, s_n, s_m, s_k, s_v = [], [], [], [], [], [], []
    for layer in range(DEPTH):
        j = layer // 2
        g0, b0 = ln_g[layer, 0].reshape(1, d), ln_b[layer, 0].reshape(1, d)
        g1, b1 = ln_g[layer, 1].reshape(1, d), ln_b[layer, 1].reshape(1, d)
        g2, b2 = ln_g[layer, 2].reshape(1, d), ln_b[layer, 2].reshape(1, d)
        if layer % 2 == 0:
            w_lo, w_gate, alog_row, prow = _ab_weights(w_in_ab[j], a_log_a[j], dt_bias_a[j], b_i_b[j], b_f_b[j])
            w_out = w_out_ab[j].astype(BF16)
            w_out_a, w_out_b = w_out[:H_A * DV_A], w_out[H_A * DV_A:]
            qkv, z, q_b, k_b, v_b, o_b, gates = mm_multi(xp, w_lo, (w_gate,), tm=256)
            mix_a, st_S = gdn_prompt(qkv, z, gates, conv_w_a[j], alog_row, prow, norm_g_a[j], bp)
            mix_b, c_ext, m_row = mlstm_prompt(q_b, k_b, v_b, o_b, gates, alog_row, prow, norm_g_b[j], bp)
            xp = proj_ln([mix_a, mix_b], [w_out_a, w_out_b], xp, g0, b0)
            p_S.append(st_S)
            p_conv.append(qkv.reshape(bp, t, QKV_A)[:, t - (CONV_W - 1):])
            p_C.append(c_ext[..., :DV_B])
            p_n.append(c_ext[..., DV_B])
            p_m.append(m_row[:, 0, :H_B])
            qkv, z, q_b, k_b, v_b, o_b, gates = mm_multi(xs, w_lo, (w_gate,))
            mix, st_S, st_C, st_n, st_m = ab_sample(
                qkv, z, q_b, k_b, v_b, o_b, gates, state_delta_conv[j], conv_w_a[j], alog_row, prow,
                norm_g_a[j], norm_g_b[j], state_delta_S[j].astype(F32), state_mlstm_C[j].astype(F32),
                state_mlstm_n[j].astype(F32), state_mlstm_m[j].astype(F32))
            xs = proj_ln([mix], [w_out], xs, g0, b0)
            s_S.append(st_S)
            s_conv.append(jnp.concatenate([state_delta_conv[j][:, 1:].astype(F32), qkv[:, None, :]], axis=1))
            s_C.append(st_C)
            s_n.append(st_n)
            s_m.append(st_m)
        else:
            lam_init = 0.8 - 0.6 * math.exp(-0.3 * layer)
            lam = (jnp.exp(jnp.sum(lam_q1[j].astype(F32) * lam_k1[j].astype(F32)))
                   - jnp.exp(jnp.sum(lam_q2[j].astype(F32) * lam_k2[j].astype(F32))) + lam_init)
            w_qkv = tuple(w_qkv_c[j][:, i * C_W:(i + 1) * C_W].astype(BF16) for i in range(3))
            w_o = w_o_c[j].astype(BF16)
            q, k, v = mm_multi(xp, w_qkv, tm=512)
            o = diff_attn_prompt_core(q, k, v, bp, rel_bias, lam, lam_init, subln_g_c[j])
            xp = proj_ln([o], [w_o], xp, g0, b0)
            p_k.append(k.reshape(bp, t, H_C, 2 * DH_C))
            p_v.append(v.reshape(bp, t, H_C, VD_C))
            q, k, v = mm_multi(xs, w_qkv)
            o = diff_attn_sample_core(q, k, v, cache_diff_k, cache_diff_v, page_table, j, rel_bias, lam, lam_init,
                                      subln_g_c[j])
            xs = proj_ln([o], [w_o], xs, g0, b0)
            s_k.append(k.reshape(bs, 1, H_C, 2 * DH_C))
            s_v.append(v.reshape(bs, 1, H_C, VD_C))
        w_q, w_o = w_xq[layer].astype(BF16), w_xo[layer].astype(BF16)
        w_r = jnp.pad(w_router[layer].astype(F32), ((0, 0), (0, LANES - N_EXPERTS)))
        w_r_hi = w_r.astype(BF16)
        w_r = jnp.stack([w_r_hi, (w_r - w_r_hi.astype(F32)).astype(BF16)])
        b_r = jnp.pad(b_router[layer].astype(F32), (0, LANES - N_EXPERTS)).reshape(1, LANES)
        mk, mv = mm_multi(mem2d, (w_xkv[layer][:, :X_W].astype(BF16), w_xkv[layer][:, X_W:].astype(BF16)))
        p_mk.append(mk.reshape(bp, N_MEM, H_X, DH_X))
        p_mv.append(mv.reshape(bp, N_MEM, H_X, DH_X))
        xp, xp_tiles, idx_p, gate_p = xattn_prompt(xp, mk.reshape(bp, N_MEM, X_W), mv.reshape(bp, N_MEM, X_W),
                                                   w_q, w_o, g1, b1, w_r, b_r)
        (q,) = mm_multi(xs, (w_q,))
        o = xattn_sample_core(q, cmk, cmv, off=layer * bs)
        xs, xs_tiles, idx_s, gate_s = proj_ln_route(o, w_o, xs, g1, b1, w_r, b_r)
        xp = moe_ln(xp, xp_tiles, idx_p, gate_p, wm_in, bm_in, wm_out, bm_out, g2, b2, MOE_BLOCK_PROMPT,
                    e_off=layer * N_EXPERTS)
        xs = moe_ln(xs, xs_tiles, idx_s, gate_s, wm_in, bm_in, wm_out, bm_out, g2, b2, MOE_BLOCK_SAMPLE,
                    e_off=layer * N_EXPERTS)

    return (xp.reshape(bp, t, d), xs.reshape(bs, 1, d),
            jnp.stack(p_S), jnp.stack(p_conv), jnp.stack(p_C), jnp.stack(p_n), jnp.stack(p_m),
            jnp.stack(p_k, axis=1), jnp.stack(p_v, axis=1), jnp.stack(p_mk), jnp.stack(p_mv),
            jnp.stack(s_S), jnp.stack(s_conv), jnp.stack(s_C), jnp.stack(s_n), jnp.stack(s_m),
            jnp.stack(s_k, axis=1), jnp.stack(s_v, axis=1))
```

```python
import functools
import math

import jax
import jax.numpy as jnp
from jax import lax
from jax.experimental import pallas as pl
from jax.experimental.pallas import tpu as pltpu

F32 = jnp.float32
BF16 = jnp.bfloat16
HI = lax.Precision.HIGHEST

D_MODEL = 1024
DEPTH = 2
H_A, DK_A, DV_A, CONV_W, CHUNK = 8, 64, 64, 4, 64
H_B, DQK_B, DV_B = 4, 64, 128
QKV_A = H_A * (2 * DK_A + DV_A)
H_C, DH_C = 8, 64
VD_C = 2 * DH_C
C_W = H_C * 2 * DH_C
N_BUCKETS, MAX_DISTANCE = 32, 128
N_MEM, H_X, DH_X = 256, 4, 128
X_W = H_X * DH_X
N_EXPERTS, TOP_K = 32, 4
D_FF = D_MODEL
SWIGLU_ALPHA, SWIGLU_LIMIT = 1.702, 7.0
DN_ALPHA = (2 * DEPTH) ** 0.25
LN_EPS = 1e-5
NORM_EPS = 1e-6

LANES = 128
SUBLANES = 8
VMEM_PHYSICAL = 64 * 1024 * 1024
VMEM_LIMIT = 48 * 1024 * 1024


def _cparams(*sem):
    return pltpu.CompilerParams(dimension_semantics=tuple(sem), vmem_limit_bytes=VMEM_LIMIT)


def _layer_norm(y, g, b):
    mu = jnp.mean(y, axis=-1, keepdims=True)
    d = y - mu
    var = jnp.mean(d * d, axis=-1, keepdims=True)
    return d * lax.rsqrt(var + LN_EPS) * g + b


def _mm_multi_kernel(x_ref, *refs, n_lo, n_hi):
    n = n_lo + n_hi
    ws, outs = refs[:n], refs[n:]
    x = x_ref[...]
    xb = x.astype(BF16)
    for i in range(n_lo):
        outs[i][...] = jnp.dot(xb, ws[i][...], preferred_element_type=F32)
    for i in range(n_lo, n):
        outs[i][...] = jnp.dot(x, ws[i][...], preferred_element_type=F32, precision=HI)


def mm_multi(x, w_lo, w_hi=(), tm=256):
    m, k = x.shape
    tm = min(tm, m)
    assert m % tm == 0
    ws = tuple(w_lo) + tuple(w_hi)
    in_specs = [pl.BlockSpec((tm, k), lambda i: (i, 0))]
    in_specs += [pl.BlockSpec(w.shape, lambda i: (0, 0)) for w in ws]
    out_specs = [pl.BlockSpec((tm, w.shape[1]), lambda i: (i, 0)) for w in ws]
    out_shape = [jax.ShapeDtypeStruct((m, w.shape[1]), F32) for w in ws]
    return pl.pallas_call(
        functools.partial(_mm_multi_kernel, n_lo=len(w_lo), n_hi=len(w_hi)),
        grid=(m // tm,), in_specs=in_specs, out_specs=out_specs, out_shape=out_shape,
        compiler_params=_cparams("parallel"), name="mm_multi")(x, *ws)


def _proj_ln_kernel(*refs, n):
    a_refs, w_refs = refs[:n], refs[n:2 * n]
    x_ref, g_ref, b_ref, o_ref = refs[2 * n:]
    h = jnp.dot(a_refs[0][...].astype(BF16), w_refs[0][...], preferred_element_type=F32)
    for a_ref, w_ref in zip(a_refs[1:], w_refs[1:]):
        h = h + jnp.dot(a_ref[...].astype(BF16), w_ref[...], preferred_element_type=F32)
    o_ref[...] = _layer_norm(DN_ALPHA * x_ref[...] + h, g_ref[...], b_ref[...])


def proj_ln(a_list, w_list, x_res, g, b, tm=512):
    m, d = x_res.shape
    tm = min(tm, m)
    assert m % tm == 0
    n = len(a_list)
    return pl.pallas_call(
        functools.partial(_proj_ln_kernel, n=n), grid=(m // tm,),
        in_specs=[pl.BlockSpec((tm, a.shape[1]), lambda i: (i, 0)) for a in a_list]
        + [pl.BlockSpec(w.shape, lambda i: (0, 0)) for w in w_list]
        + [pl.BlockSpec((tm, d), lambda i: (i, 0)), pl.BlockSpec((1, d), lambda i: (0, 0)),
           pl.BlockSpec((1, d), lambda i: (0, 0))],
        out_specs=pl.BlockSpec((tm, d), lambda i: (i, 0)),
        out_shape=jax.ShapeDtypeStruct((m, d), F32),
        compiler_params=_cparams("parallel"), name="proj_ln")(*a_list, *w_list, x_res, g, b)


def _mem_attention(q, mk, mv):
    outs = []
    for h in range(H_X):
        sl = slice(h * DH_X, (h + 1) * DH_X)
        s = lax.dot_general(q[:, sl].astype(BF16), mk[:, sl].astype(BF16), (((1,), (1,)), ((), ())),
                            preferred_element_type=F32) * (DH_X ** -0.5)
        s = s - jnp.max(s, axis=-1, keepdims=True)
        p = jnp.exp(s)
        p = p / jnp.sum(p, axis=-1, keepdims=True)
        outs.append(jnp.dot(p.astype(BF16), mv[:, sl].astype(BF16), preferred_element_type=F32))
    return jnp.concatenate(outs, axis=-1)


def _route(x, wr, br):
    x_hi = x.astype(BF16)
    x_lo = (x - x_hi.astype(F32)).astype(BF16)
    logits = (jnp.dot(x_hi, wr[0], preferred_element_type=F32) + jnp.dot(x_lo, wr[0], preferred_element_type=F32)
              + jnp.dot(x_hi, wr[1], preferred_element_type=F32) + br)
    lane = lax.broadcasted_iota(jnp.int32, logits.shape, 1)
    work = jnp.where(lane < N_EXPERTS, logits, -jnp.inf)
    idx_out = jnp.zeros(logits.shape, jnp.int32)
    val_out = jnp.full(logits.shape, -jnp.inf, F32)
    for k in range(TOP_K):
        m = jnp.max(work, axis=-1, keepdims=True)
        sel = jnp.min(jnp.where(work == m, lane, LANES), axis=-1, keepdims=True)
        idx_out = jnp.where(lane == k, sel, idx_out)
        val_out = jnp.where(lane == k, m, val_out)
        work = jnp.where(lane == sel, -jnp.inf, work)
    e = jnp.exp(val_out - jnp.max(val_out, axis=-1, keepdims=True))
    gates = e / jnp.sum(e, axis=-1, keepdims=True)
    return idx_out, gates


def _store_token_tiles(o3_ref, y):
    for s in range(y.shape[1] // LANES):
        o3_ref[:, s, :] = y[:, s * LANES:(s + 1) * LANES]


def _xattn_prompt_kernel(x_ref, mk_ref, mv_ref, wq_ref, wo_ref, g_ref, b_ref, wr_ref, br_ref,
                         o_ref, o3_ref, idx_ref, gate_ref):
    x = x_ref[...]
    q = jnp.dot(x.astype(BF16), wq_ref[...], preferred_element_type=F32)
    o = _mem_attention(q, mk_ref[0], mv_ref[0])
    h = jnp.dot(o.astype(BF16), wo_ref[...], preferred_element_type=F32)
    y = _layer_norm(DN_ALPHA * x + h, g_ref[...], b_ref[...])
    o_ref[...] = y
    _store_token_tiles(o3_ref, y)
    idx, gates = _route(y, wr_ref[...], br_ref[...])
    idx_ref[...] = idx
    gate_ref[...] = gates


def xattn_prompt(x, mk, mv, wq, wo, g, b, wr, br, tq=512):
    m, d = x.shape
    bsz = mk.shape[0]
    t = m // bsz
    nq = t // tq
    full = lambda shape: pl.BlockSpec(shape, lambda bi, qi: (0,) * len(shape))
    row = lambda w: pl.BlockSpec((tq, w), lambda bi, qi: (bi * nq + qi, 0))
    return pl.pallas_call(
        _xattn_prompt_kernel, grid=(bsz, nq),
        in_specs=[row(d), pl.BlockSpec((1, N_MEM, X_W), lambda bi, qi: (bi, 0, 0)),
                  pl.BlockSpec((1, N_MEM, X_W), lambda bi, qi: (bi, 0, 0)),
                  full(wq.shape), full(wo.shape), full(g.shape), full(b.shape), full(wr.shape), full(br.shape)],
        out_specs=[row(d), pl.BlockSpec((tq, d // LANES, LANES), lambda bi, qi: (bi * nq + qi, 0, 0)),
                   row(LANES), row(LANES)],
        out_shape=[jax.ShapeDtypeStruct((m, d), F32), jax.ShapeDtypeStruct((m, d // LANES, LANES), F32),
                   jax.ShapeDtypeStruct((m, LANES), jnp.int32), jax.ShapeDtypeStruct((m, LANES), F32)],
        compiler_params=_cparams("parallel", "parallel"), name="xattn_prompt")(x, mk, mv, wq, wo, g, b, wr, br)


def _xattn_sample_kernel(q_ref, mk_ref, mv_ref, o_ref):
    q = jnp.broadcast_to(q_ref[0], (8, X_W))
    mk = jnp.concatenate([mk_ref[0, pl.ds(h, N_MEM, stride=H_X), :] for h in range(H_X)], axis=-1)
    mv = jnp.concatenate([mv_ref[0, pl.ds(h, N_MEM, stride=H_X), :] for h in range(H_X)], axis=-1)
    o = _mem_attention(q, mk, mv)
    o_ref[0] = o[0:1]


def xattn_sample_core(q, mk, mv, off=0):
    bsz = q.shape[0]
    spec3 = pl.BlockSpec((1, N_MEM * H_X, DH_X), lambda bi: (off + bi, 0, 0))
    out = pl.pallas_call(
        _xattn_sample_kernel, grid=(bsz,),
        in_specs=[pl.BlockSpec((1, 1, X_W), lambda bi: (bi, 0, 0)), spec3, spec3],
        out_specs=pl.BlockSpec((1, 1, X_W), lambda bi: (bi, 0, 0)),
        out_shape=jax.ShapeDtypeStruct((bsz, 1, X_W), F32),
        compiler_params=_cparams("parallel"), name="xattn_sample")(q.reshape(bsz, 1, X_W), mk, mv)
    return out.reshape(bsz, X_W)


def _ln_route_kernel(a_ref, w_ref, x_ref, g_ref, b_ref, wr_ref, br_ref, o_ref, o3_ref, idx_ref, gate_ref):
    h = jnp.dot(a_ref[...].astype(BF16), w_ref[...], preferred_element_type=F32)
    y = _layer_norm(DN_ALPHA * x_ref[...] + h, g_ref[...], b_ref[...])
    o_ref[...] = y
    _store_token_tiles(o3_ref, y)
    idx, gates = _route(y, wr_ref[...], br_ref[...])
    idx_ref[...] = idx
    gate_ref[...] = gates


def proj_ln_route(a, w, x_res, g, b, wr, br):
    m, d = x_res.shape
    return pl.pallas_call(
        _ln_route_kernel,
        out_shape=[jax.ShapeDtypeStruct((m, d), F32), jax.ShapeDtypeStruct((m, d // LANES, LANES), F32),
                   jax.ShapeDtypeStruct((m, LANES), jnp.int32), jax.ShapeDtypeStruct((m, LANES), F32)],
        compiler_params=pltpu.CompilerParams(vmem_limit_bytes=VMEM_LIMIT),
        name="proj_ln_route")(a, w, x_res, g, b, wr, br)


def _expert_onehots(idx):
    lane = lax.broadcasted_iota(jnp.int32, idx.shape, 1)
    onehots = [(idx[:, k:k + 1] == lane).astype(F32) for k in range(TOP_K)]
    tot = onehots[0]
    for k in range(1, TOP_K):
        tot = tot + onehots[k]
    return lane, onehots, tot


def _count_kernel(idx_ref, cnt_ref):
    @pl.when(pl.program_id(0) == 0)
    def _():
        cnt_ref[...] = jnp.zeros_like(cnt_ref)

    _, _, tot = _expert_onehots(idx_ref[...])
    cnt_ref[...] = cnt_ref[...] + jnp.sum(tot, axis=0, keepdims=True)


def _dest_kernel(idx_ref, base0_ref, dest_ref, base_ref):
    @pl.when(pl.program_id(0) == 0)
    def _():
        base_ref[...] = base0_ref[...]

    idx = idx_ref[...]
    tr = idx.shape[0]
    lane, onehots, tot = _expert_onehots(idx)
    r = lax.broadcasted_iota(jnp.int32, (tr, tr), 0)
    c = lax.broadcasted_iota(jnp.int32, (tr, tr), 1)
    strict = (c < r).astype(BF16)
    before = jnp.dot(strict, tot.astype(BF16), preferred_element_type=F32) + base_ref[...]
    out = jnp.zeros(idx.shape, F32)
    for k in range(TOP_K):
        out = jnp.where(lane == k, jnp.sum(onehots[k] * before, axis=-1, keepdims=True), out)
    dest_ref[...] = out.astype(jnp.int32)
    base_ref[...] = base_ref[...] + jnp.sum(tot, axis=0, keepdims=True)


def route_slots(idx, bm, tr=256):
    t = idx.shape[0]
    tr = min(tr, t)
    assert t % tr == 0
    rows = pl.BlockSpec((tr, LANES), lambda i: (i, 0))
    one = pl.BlockSpec((1, LANES), lambda i: (0, 0))
    counts = pl.pallas_call(
        _count_kernel, grid=(t // tr,), in_specs=[rows], out_specs=one,
        out_shape=jax.ShapeDtypeStruct((1, LANES), F32),
        compiler_params=_cparams("arbitrary"), name="route_count")(idx)
    counts = counts[0].astype(jnp.int32)
    padded = (counts + bm - 1) // bm * bm
    pad_end = jnp.cumsum(padded)
    base0 = (pad_end - padded).astype(F32).reshape(1, LANES)
    dest = pl.pallas_call(
        _dest_kernel, grid=(t // tr,), in_specs=[rows, one], out_specs=rows,
        out_shape=jax.ShapeDtypeStruct((t, LANES), jnp.int32),
        scratch_shapes=[pltpu.VMEM((1, LANES), F32)],
        compiler_params=_cparams("arbitrary"), name="route_dest")(idx, base0)
    return dest, pad_end[:N_EXPERTS], counts[:N_EXPERTS]


def _clamped_swiglu(glu, lin):
    glu = jnp.minimum(glu, SWIGLU_LIMIT)
    lin = jnp.clip(lin, -SWIGLU_LIMIT, SWIGLU_LIMIT)
    return glu * jax.nn.sigmoid(SWIGLU_ALPHA * glu) * (lin + 1.0)


def _tile_rows(buf):
    rows = buf.shape[0] * buf.shape[2]
    return jnp.concatenate([buf[:, j].reshape(rows, LANES) for j in range(buf.shape[1])], axis=-1)


def _moe_gmm_kernel(bexp_ref, nused_ref, tok_ref, tok1_ref, tok2_ref, x_hbm, win_ref, bin_ref, wout_ref,
                    bout_ref, y_hbm, xbuf, ybuf, winb, woutb, gsem, osem, *, bm):
    i = pl.program_id(0)
    n = pl.num_programs(0)
    nused = nused_ref[0]
    slot = lax.rem(i, 2)
    xslot = lax.rem(i, 3)
    n_sub = x_hbm.shape[1]
    groups = bm // SUBLANES

    def start_gather(idx_ref, s):
        def body(g, carry):
            for u in range(SUBLANES):
                pltpu.make_async_copy(x_hbm.at[idx_ref[0, 0, g * SUBLANES + u]], xbuf.at[s, g, :, u],
                                      gsem.at[s]).start(priority=u % 2)
            return carry
        lax.fori_loop(0, groups, body, 0)

    def wait_gather(s):
        pltpu.make_async_copy(xbuf.at[s], xbuf.at[s], gsem.at[s]).wait()

    def out_copies(blk, s):
        return [pltpu.make_async_copy(ybuf.at[s, :, j], y_hbm.at[pl.ds(blk * groups, groups), :, j], osem.at[s])
                for j in range(n_sub)]

    @pl.when(i == 0)
    def _():
        start_gather(tok_ref, 0)

        @pl.when(jnp.logical_and(1 < n, 1 < nused))
        def _():
            start_gather(tok1_ref, 1)

    active = i < nused
    prefetch = jnp.logical_and(i + 2 < n, i + 2 < nused)

    @pl.when(jnp.logical_and(active, jnp.logical_or(i == 0, bexp_ref[i] != bexp_ref[jnp.maximum(i - 1, 0)])))
    def _():
        winb[...] = win_ref[0].astype(BF16)
        woutb[...] = wout_ref[0].astype(BF16)

    @pl.when(i >= 2)
    def _():
        for cp in out_copies(i - 2, slot):
            cp.wait()

    def expert_mlp(gather_ahead):
        wait_gather(xslot)
        xb = _tile_rows(xbuf[xslot]).astype(BF16)
        if gather_ahead:
            nslot = lax.rem(i + 2, 3)
            for g in range(groups):
                for u in range(SUBLANES):
                    pltpu.make_async_copy(x_hbm.at[tok2_ref[0, 0, g * SUBLANES + u]], xbuf.at[nslot, g, :, u],
                                          gsem.at[nslot]).start(priority=u % 2)
        h = jnp.dot(xb, winb[...], preferred_element_type=F32) + bin_ref[0]
        act = _clamped_swiglu(h[:, :D_FF], h[:, D_FF:]).astype(BF16)
        y = jnp.dot(act, woutb[...], preferred_element_type=F32) + bout_ref[0]
        for j in range(n_sub):
            ybuf[slot, :, j] = y[:, j * LANES:(j + 1) * LANES].reshape(groups, SUBLANES, LANES)

    @pl.when(jnp.logical_and(active, prefetch))
    def _():
        expert_mlp(True)

    @pl.when(jnp.logical_and(active, jnp.logical_not(prefetch)))
    def _():
        expert_mlp(False)

    @pl.when(jnp.logical_not(active))
    def _():
        ybuf[slot] = jnp.zeros(ybuf.shape[1:], F32)

    for cp in out_copies(i, slot):
        cp.start()

    @pl.when(i == n - 1)
    def _():
        @pl.when(n >= 2)
        def _():
            for cp in out_copies(i - 1, 1 - slot):
                cp.wait()

        for cp in out_copies(i, slot):
            cp.wait()


def moe_gmm(x, row_tok, block_expert, n_used, w_in, b_in, w_out, b_out, bm):
    n_blocks = row_tok.shape[0]
    n_sub = x.shape[1]
    d = n_sub * LANES
    idx_spec = lambda off: pl.BlockSpec(
        (1, 1, bm), lambda i, be, nu: (jnp.minimum(i + off, n_blocks - 1), 0, 0), memory_space=pltpu.SMEM)
    ex = lambda i, be, nu: (be[i], 0, 0)
    grid_spec = pltpu.PrefetchScalarGridSpec(
        num_scalar_prefetch=2, grid=(n_blocks,),
        in_specs=[idx_spec(0), idx_spec(1), idx_spec(2),
                  pl.BlockSpec(memory_space=pl.ANY),
                  pl.BlockSpec((1, d, 2 * D_FF), ex), pl.BlockSpec((1, 1, 2 * D_FF), ex),
                  pl.BlockSpec((1, D_FF, d), ex), pl.BlockSpec((1, 1, d), ex)],
        out_specs=pl.BlockSpec(memory_space=pl.ANY),
        scratch_shapes=[pltpu.VMEM((3, bm // SUBLANES, n_sub, SUBLANES, LANES), F32),
                        pltpu.VMEM((2, bm // SUBLANES, n_sub, SUBLANES, LANES), F32),
                        pltpu.VMEM((d, 2 * D_FF), BF16), pltpu.VMEM((D_FF, d), BF16),
                        pltpu.SemaphoreType.DMA((3,)), pltpu.SemaphoreType.DMA((2,))])
    w_elems = d * 2 * D_FF + D_FF * d
    vmem_bytes = (2 * 4 + 2) * w_elems + 5 * bm * d * 4 + bm * (2 * D_FF + D_FF + 2 * d) * 4
    vmem_bytes = min(vmem_bytes + (4 << 20), VMEM_PHYSICAL - (6 << 20))
    return pl.pallas_call(
        functools.partial(_moe_gmm_kernel, bm=bm), grid_spec=grid_spec,
        out_shape=jax.ShapeDtypeStruct((n_blocks * bm // SUBLANES, SUBLANES, n_sub, LANES), F32),
        compiler_params=pltpu.CompilerParams(dimension_semantics=("arbitrary",), vmem_limit_bytes=vmem_bytes),
        name="moe_gmm")(
            block_expert, n_used, row_tok, row_tok, row_tok, x, w_in, b_in.reshape(b_in.shape[0], 1, -1),
            w_out, b_out.reshape(b_out.shape[0], 1, -1))


def _combine_ln_kernel(dst_ref, dst_next_ref, y_hbm, gate_ref, x_ref, g_ref, b_ref, *refs, n_proj):
    w_refs, o_ref, p_refs = refs[:n_proj], refs[n_proj], refs[n_proj + 1:2 * n_proj + 1]
    gbuf, sem = refs[2 * n_proj + 1:]
    i = pl.program_id(0)
    n = pl.num_programs(0)
    slot = lax.rem(i, 2)
    groups = gbuf.shape[2]

    def start_gather(idx_ref, s):
        def body(g, carry):
            for u in range(SUBLANES):
                for k in range(TOP_K):
                    d = idx_ref[0, 0, (g * SUBLANES + u) * TOP_K + k]
                    pltpu.make_async_copy(y_hbm.at[d], gbuf.at[s, k, g, :, u], sem.at[s]).start(priority=k % 2)
            return carry
        lax.fori_loop(0, groups, body, 0)

    @pl.when(i == 0)
    def _():
        start_gather(dst_ref, 0)

    def tile(gather_ahead):
        pltpu.make_async_copy(gbuf.at[slot], gbuf.at[slot], sem.at[slot]).wait()
        if gather_ahead:
            for g in range(groups):
                for u in range(SUBLANES):
                    for k in range(TOP_K):
                        d = dst_next_ref[0, 0, (g * SUBLANES + u) * TOP_K + k]
                        pltpu.make_async_copy(y_hbm.at[d], gbuf.at[1 - slot, k, g, :, u],
                                              sem.at[1 - slot]).start(priority=k % 2)
        gates = gate_ref[...]
        acc = gates[:, 0:1] * _tile_rows(gbuf[slot, 0])
        for k in range(1, TOP_K):
            acc = acc + gates[:, k:k + 1] * _tile_rows(gbuf[slot, k])
        out = _layer_norm(DN_ALPHA * x_ref[...] + acc, g_ref[...], b_ref[...])
        o_ref[...] = out
        if n_proj:
            out_b = out.astype(BF16)
            for w_ref, p_ref in zip(w_refs, p_refs):
                p_ref[...] = jnp.dot(out_b, w_ref[...], preferred_element_type=F32)

    @pl.when(i + 1 < n)
    def _():
        tile(True)

    @pl.when(i + 1 >= n)
    def _():
        tile(False)


def combine_ln(y, dest, gates, x_res, g, b, w_next=(), tc=256):
    t, d = x_res.shape
    tc = min(tc, t)
    assert t % tc == 0
    n = t // tc
    n_sub = d // LANES
    dest_flat = dest[:, :TOP_K].reshape(n, 1, tc * TOP_K)
    idx_spec = lambda off: pl.BlockSpec((1, 1, tc * TOP_K), lambda i: (jnp.minimum(i + off, n - 1), 0, 0),
                                        memory_space=pltpu.SMEM)
    outs = pl.pallas_call(
        functools.partial(_combine_ln_kernel, n_proj=len(w_next)), grid=(n,),
        in_specs=[idx_spec(0), idx_spec(1), pl.BlockSpec(memory_space=pl.ANY),
                  pl.BlockSpec((tc, LANES), lambda i: (i, 0)),
                  pl.BlockSpec((tc, d), lambda i: (i, 0)), pl.BlockSpec((1, d), lambda i: (0, 0)),
                  pl.BlockSpec((1, d), lambda i: (0, 0))]
        + [pl.BlockSpec(w.shape, lambda i: (0, 0)) for w in w_next],
        out_specs=[pl.BlockSpec((tc, d), lambda i: (i, 0))]
        + [pl.BlockSpec((tc, w.shape[1]), lambda i: (i, 0)) for w in w_next],
        out_shape=[jax.ShapeDtypeStruct((t, d), F32)]
        + [jax.ShapeDtypeStruct((t, w.shape[1]), F32) for w in w_next],
        scratch_shapes=[pltpu.VMEM((2, TOP_K, tc // SUBLANES, n_sub, SUBLANES, LANES), F32),
                        pltpu.SemaphoreType.DMA((2,))],
        compiler_params=_cparams("arbitrary"), name="combine_ln")(
            dest_flat, dest_flat, y.reshape(-1, n_sub, LANES), gates, x_res, g, b, *w_next)
    return outs[0], tuple(outs[1:])


def _slot_runs_kernel(start_ref, len_ref, src_ref, o_ref, *, per_step, n_row):
    base = pl.program_id(0) * per_step
    lane = lax.broadcasted_iota(jnp.int32, (n_row, LANES), 1)
    pos = lax.broadcasted_iota(jnp.int32, (n_row, LANES), 0) * LANES + lane
    for j in range(per_step):
        start = start_ref[base + j]
        off = lax.bitwise_and(start, LANES - 1)
        rows = src_ref[pl.ds(lax.shift_right_logical(start, 7), SUBLANES), :]
        rolled = pltpu.roll(rows, lax.bitwise_and(LANES - off, LANES - 1), axis=1)
        run = jnp.where(lane + off < LANES, rolled[:n_row], rolled[1:n_row + 1])
        o_ref[j] = jnp.where(pos < len_ref[base + j], run, 0)


def slot_runs(src, starts, lens, width):
    n_runs = starts.shape[0]
    n_row = max(width // LANES, 1)
    assert n_row < SUBLANES
    per_step = next(d for d in (16, 12, 9, 8, 6, 4, 3, 2, 1) if n_runs % d == 0)
    rows = -(-src.shape[0] // LANES) + SUBLANES
    src2 = jnp.pad(src, (0, rows * LANES - src.shape[0])).reshape(rows, LANES)
    grid_spec = pltpu.PrefetchScalarGridSpec(
        num_scalar_prefetch=2, grid=(n_runs // per_step,),
        in_specs=[pl.BlockSpec((rows, LANES), lambda i, s, l: (0, 0))],
        out_specs=pl.BlockSpec((per_step, n_row, LANES), lambda i, s, l: (i, 0, 0)))
    out = pl.pallas_call(
        functools.partial(_slot_runs_kernel, per_step=per_step, n_row=n_row), grid_spec=grid_spec,
        out_shape=jax.ShapeDtypeStruct((n_runs, n_row, LANES), jnp.int32),
        compiler_params=_cparams("arbitrary"), name="slot_runs")(starts, lens, src2)
    return out.reshape(n_runs, n_row * LANES)[:, :width]


def moe_ln(x, x_tiles, idx, gates, w_in, b_in, w_out, b_out, g, b, bm, e_off=0, w_next=()):
    t, d = x.shape
    tk = t * TOP_K
    dest, pad_end, counts = route_slots(idx, bm)
    n_blocks = -(-tk // bm) + N_EXPERTS
    n_rows = n_blocks * bm
    block_start = jnp.arange(n_blocks, dtype=jnp.int32) * bm
    block_expert = jnp.sum((pad_end[None, :] <= block_start[:, None]).astype(jnp.int32), axis=1)
    block_expert = jnp.minimum(block_expert, N_EXPERTS - 1)
    tok = jnp.arange(tk, dtype=jnp.int32) // TOP_K
    _, tok_sorted = lax.sort_key_val(dest[:, :TOP_K].reshape(tk), tok)
    pad_start = jnp.concatenate([jnp.zeros((1,), jnp.int32), pad_end[:-1]])
    used_end = pad_start + counts
    padding_before = pad_start - (jnp.cumsum(counts) - counts)
    run_start = jnp.clip(block_start - padding_before[block_expert], 0, tk)
    run_len = jnp.clip(used_end[block_expert] - block_start, 0, bm)
    row_tok = slot_runs(tok_sorted, run_start, run_len, bm).reshape(n_blocks, 1, bm)
    block_expert = block_expert + e_off
    n_used = (pad_end[-1] // bm).astype(jnp.int32).reshape(1)
    y = moe_gmm(x_tiles, row_tok, block_expert, n_used, w_in, b_in, w_out, b_out, bm)
    return combine_ln(y, dest, gates, x, g, b, w_next)


def _t5_causal_bucket(dist):
    n = jnp.maximum(dist, 0)
    max_exact = N_BUCKETS // 2
    nf = jnp.maximum(n, max_exact).astype(F32)
    large = max_exact + (jnp.log(nf / max_exact) / math.log(MAX_DISTANCE / max_exact)
                         * (N_BUCKETS - max_exact)).astype(jnp.int32)
    large = jnp.minimum(large, N_BUCKETS - 1)
    return jnp.where(n < max_exact, n, large)


def _split_maps(q):
    lane = lax.broadcasted_iota(jnp.int32, q.shape, 1)
    qs = q * (DH_C ** -0.5)
    return (jnp.where(lane < DH_C, qs, 0.0).astype(BF16), jnp.where(lane >= DH_C, qs, 0.0).astype(BF16))


def _diff_finish(acc1, l1, acc2, l2, lam, out_scale, g):
    o = acc1 / l1 - lam * (acc2 / l2)
    o = o * lax.rsqrt(jnp.mean(o * o, axis=-1, keepdims=True) + NORM_EPS) * g
    return o * out_scale


def _diff_prompt_kernel(sc_ref, far_ref, q_ref, k_ref, v_ref, bias_ref, g_ref, o_ref, m_ref, a_ref, *, tq, hp):
    h0 = pl.program_id(1) * hp
    qi = pl.program_id(2)
    head = lambda x, h: x[:, h * VD_C:(h + 1) * VD_C]
    q = q_ref[...]
    qm = jnp.stack([jnp.concatenate(_split_maps(head(q, h)), axis=0) for h in range(hp)], axis=0)
    m_ref[...] = jnp.full_like(m_ref, -jnp.inf)
    a_ref[...] = jnp.zeros_like(a_ref)
    ones_col = (lax.broadcasted_iota(jnp.int32, (tq, LANES), 1) == 0).astype(BF16)
    reps = tq // LANES

    def update(tiles):
        scores, values = [], []
        for kidx, bias, shift, causal in tiles:
            start = pl.multiple_of(kidx * tq, tq)
            kt = k_ref[0, pl.ds(start, tq), :].astype(BF16)
            vt = v_ref[0, pl.ds(start, tq), :].astype(BF16)
            values.append(jnp.stack([jnp.concatenate([head(vt, h), ones_col], axis=-1) for h in range(hp)], axis=0))
            s = _bmm_nt(qm, jnp.stack([head(kt, h) for h in range(hp)], axis=0))
            if bias is not None:
                s = s + jnp.concatenate([bias, bias], axis=1)
            if causal:
                r = lax.broadcasted_iota(jnp.int32, s.shape[1:], 0)
                c = lax.broadcasted_iota(jnp.int32, s.shape[1:], 1)
                s = jnp.where(c <= lax.bitwise_and(r, tq - 1), s, -jnp.inf)
            scores.append((s, shift))
        m_old = m_ref[...]
        m_new = m_old
        for s, shift in scores:
            m_new = jnp.maximum(m_new, jnp.max(s, axis=-1, keepdims=True) + shift)
        alpha = jnp.exp(m_old - m_new)
        acc = jnp.concatenate([alpha, alpha], axis=-1) * a_ref[...]
        for (s, shift), vt in zip(scores, values):
            p = jnp.exp(s - jnp.concatenate([m_new - shift] * reps, axis=-1))
            acc = acc + _bmm(p, vt)
        a_ref[...] = acc
        m_ref[...] = m_new

    far_bias = jnp.stack([jnp.full((1, 1), far_ref[h0 + h], F32) for h in range(hp)], axis=0)
    no_shift = jnp.zeros((hp, 1, 1), F32)
    n_far = jnp.maximum(qi - 1, 0)

    def far_body(j, carry):
        update([(2 * j, None, far_bias, False), (2 * j + 1, None, far_bias, False)])
        return carry

    lax.fori_loop(0, n_far // 2, far_body, 0)

    @pl.when(lax.rem(n_far, 2) == 1)
    def _():
        update([(n_far - 1, None, far_bias, False)])

    @pl.when(qi >= 1)
    def _():
        update([(qi - 1, bias_ref[:, 1], no_shift, False), (qi, bias_ref[:, 0], no_shift, True)])

    @pl.when(qi == 0)
    def _():
        update([(0, bias_ref[:, 0], no_shift, True)])

    acc = a_ref[...]
    o_ref[...] = jnp.concatenate(
        [_diff_finish(acc[h, :tq, :VD_C], acc[h, :tq, VD_C:VD_C + 1], acc[h, tq:, :VD_C], acc[h, tq:, VD_C:VD_C + 1],
                      sc_ref[0], sc_ref[1], g_ref[...]) for h in range(hp)], axis=-1)


def _bias_tiles_kernel(rb_ref, bucket_ref, o_ref):
    h = pl.program_id(0)
    b = bucket_ref[...]
    out = jnp.zeros(b.shape, F32)
    for k in range(N_BUCKETS):
        out = jnp.where(b == k, rb_ref[k * H_C + h], out)
    o_ref[0] = out


DIFF_HEADS_PER_STEP = 2


def diff_attn_prompt_core(q, k, v, bsz, rel_bias, lam, lam_init, subln_g, tq=256):
    m = q.shape[0]
    t = m // bsz
    nq = t // tq
    ii = jnp.arange(tq)[:, None]
    jj = jnp.arange(tq)[None, :]
    buckets = _t5_causal_bucket(jnp.stack([jnp.maximum(ii - jj, 0), tq + ii - jj])).astype(jnp.int32)
    bias = pl.pallas_call(
        _bias_tiles_kernel, grid=(H_C,),
        in_specs=[pl.BlockSpec(memory_space=pltpu.SMEM), pl.BlockSpec((2, tq, tq), lambda h: (0, 0, 0))],
        out_specs=pl.BlockSpec((1, 2, tq, tq), lambda h: (h, 0, 0, 0)),
        out_shape=jax.ShapeDtypeStruct((H_C, 2, tq, tq), F32),
        compiler_params=_cparams("parallel"), name="t5_bias_tiles")(rel_bias.astype(F32).reshape(-1), buckets)
    far = rel_bias[_t5_causal_bucket(jnp.array(2 * tq))].astype(F32)
    scal = jnp.stack([lam, 1.0 - lam_init]).astype(F32)
    k3 = k.reshape(bsz, t, C_W)
    v3 = v.reshape(bsz, t, C_W)
    smem = pl.BlockSpec(memory_space=pltpu.SMEM)
    hp = DIFF_HEADS_PER_STEP
    kv_spec = pl.BlockSpec((1, t, hp * VD_C), lambda b, h, i: (b, 0, h))
    row = pl.BlockSpec((tq, hp * VD_C), lambda b, h, i: (b * nq + i, h))
    stat = pltpu.VMEM((hp, 2 * tq, LANES), F32)
    acc = pltpu.VMEM((hp, 2 * tq, VD_C + LANES), F32)
    return pl.pallas_call(
        functools.partial(_diff_prompt_kernel, tq=tq, hp=hp), grid=(bsz, H_C // hp, nq),
        in_specs=[smem, smem, row, kv_spec, kv_spec,
                  pl.BlockSpec((hp, 2, tq, tq), lambda b, h, i: (h, 0, 0, 0)),
                  pl.BlockSpec((1, VD_C), lambda b, h, i: (0, 0))],
        out_specs=row, out_shape=jax.ShapeDtypeStruct((m, C_W), F32),
        scratch_shapes=[stat, acc],
        compiler_params=_cparams("parallel", "parallel", "parallel"), name="diff_attn_prompt")(
            scal, far, q, k3, v3, bias, subln_g.reshape(1, VD_C))


def _diff_sample_kernel(pt_ref, sc_ref, q_ref, kn_ref, vn_ref, bm_ref, bnew_ref, g_ref, *refs, n_pp):
    k_refs, v_refs = refs[:n_pp], refs[n_pp:2 * n_pp]
    o_ref, m_ref, l_ref, a_ref = refs[2 * n_pp:]
    p = pl.program_id(1)
    n_p = pl.num_programs(1)
    q8 = q_ref[0]
    qm = jnp.concatenate(_split_maps(q8), axis=0)

    @pl.when(p == 0)
    def _():
        m_ref[...] = jnp.full_like(m_ref, -jnp.inf)
        l_ref[...] = jnp.zeros_like(l_ref)
        a_ref[...] = jnp.zeros_like(a_ref)

    scores = []
    for j in range(n_pp):
        bias = bm_ref[jnp.where(p == n_p - 1, 1, 0)] if j == n_pp - 1 else bm_ref[0]
        k2 = k_refs[j][0, 0].reshape(-1, 2 * DH_C).astype(BF16)
        scores.append(lax.dot_general(qm, k2, (((1,), (1,)), ((), ())), preferred_element_type=F32) + bias)
    m_old = m_ref[...]
    m_new = m_old
    for s in scores:
        m_new = jnp.maximum(m_new, jnp.max(s, axis=-1, keepdims=True))
    alpha = jnp.exp(m_old - m_new)
    l_new = alpha * l_ref[...]
    a_new = alpha * a_ref[...]
    for j, s in enumerate(scores):
        pr = jnp.exp(s - m_new)
        l_new = l_new + jnp.sum(pr, axis=-1, keepdims=True)
        v2 = v_refs[j][0, 0].reshape(-1, VD_C).astype(BF16)
        a_new = a_new + jnp.dot(pr.astype(BF16), v2, preferred_element_type=F32)
    l_ref[...] = l_new
    a_ref[...] = a_new
    m_ref[...] = m_new

    @pl.when(p == n_p - 1)
    def _():
        kn = jnp.concatenate([kn_ref[0], kn_ref[0]], axis=0)
        vn = jnp.concatenate([vn_ref[0], vn_ref[0]], axis=0)
        s = jnp.sum(qm.astype(F32) * kn, axis=-1, keepdims=True) + bnew_ref[:, 0:1]
        m_old = m_ref[...]
        m_new = jnp.maximum(m_old, s)
        alpha = jnp.exp(m_old - m_new)
        pr = jnp.exp(s - m_new)
        l = alpha * l_ref[...] + pr
        a = alpha * a_ref[...] + pr * vn
        o_ref[0] = _diff_finish(a[:H_C], l[:H_C], a[H_C:], l[H_C:], sc_ref[0], sc_ref[1], g_ref[...])


def diff_attn_sample_core(q, k_new, v_new, cache_k, cache_v, page_table, layer_j, rel_bias, lam, lam_init, subln_g,
                          n_pp=8):
    bsz = q.shape[0]
    n_pages = page_table.shape[1]
    page = cache_k.shape[2]
    past = n_pages * page
    assert n_pages % n_pp == 0
    dist_last = past - ((n_pages - 1) * page + jnp.arange(page))
    b_last = rel_bias[_t5_causal_bucket(dist_last)].astype(F32)
    b_far = jnp.broadcast_to(rel_bias[_t5_causal_bucket(jnp.array(page + 1))].astype(F32), (page, H_C))
    eye = jnp.eye(H_C, dtype=bool)

    def expand(bt):
        full = jnp.where(eye[:, None, :], bt.T[:, :, None], -jnp.inf).reshape(H_C, page * H_C)
        return jnp.concatenate([full, full], axis=0)

    bm = jnp.stack([expand(b_far), expand(b_last)])
    b_new = rel_bias[_t5_causal_bucket(jnp.array(0))].astype(F32)
    b_new = jnp.broadcast_to(jnp.concatenate([b_new, b_new])[:, None], (2 * H_C, LANES))
    scal = jnp.stack([lam, 1.0 - lam_init]).astype(F32)
    smem = pl.BlockSpec(memory_space=pltpu.SMEM)
    head3 = pl.BlockSpec((1, H_C, VD_C), lambda b, p, pt: (b, 0, 0))
    full = lambda shape: pl.BlockSpec(shape, lambda b, p, pt: (0,) * len(shape))

    def page_spec(j):
        return pl.BlockSpec((1, 1, page, H_C, VD_C), lambda b, p, pt: (pt[b, p * n_pp + j], layer_j, 0, 0, 0))

    grid_spec = pltpu.PrefetchScalarGridSpec(
        num_scalar_prefetch=1, grid=(bsz, n_pages // n_pp),
        in_specs=[smem, head3, head3, head3, full(bm.shape), full(b_new.shape), full((1, VD_C))]
        + [page_spec(j) for j in range(n_pp)] * 2,
        out_specs=head3,
        scratch_shapes=[pltpu.VMEM((2 * H_C, 1), F32), pltpu.VMEM((2 * H_C, 1), F32),
                        pltpu.VMEM((2 * H_C, VD_C), F32)])
    out = pl.pallas_call(
        functools.partial(_diff_sample_kernel, n_pp=n_pp), grid_spec=grid_spec,
        out_shape=jax.ShapeDtypeStruct((bsz, H_C, VD_C), F32),
        compiler_params=_cparams("parallel", "arbitrary"), name="diff_attn_sample")(
            page_table, scal, q.reshape(bsz, H_C, VD_C), k_new.reshape(bsz, H_C, VD_C),
            v_new.reshape(bsz, H_C, VD_C), bm, b_new, subln_g.reshape(1, VD_C),
            *([cache_k] * n_pp), *([cache_v] * n_pp))
    return out.reshape(bsz, C_W)


LANE_BETA, LANE_A, LANE_I, LANE_F = 0, H_A, 2 * H_A, 2 * H_A + H_B


def _softplus(x):
    return jnp.maximum(x, 0.0) + jnp.log1p(jnp.exp(-jnp.abs(x)))


def _silu(x):
    return x * jax.nn.sigmoid(x)


def _lanes(shape, lo, n):
    lane = lax.broadcasted_iota(jnp.int32, shape, 1)
    return jnp.logical_and(lane >= lo, lane < lo + n)


def _gate_tile(gt, alog_row, prow):
    z = gt + prow
    return jax.nn.sigmoid(gt), -jnp.exp(alog_row) * _softplus(z), z, -_softplus(-z)


def _nt(a, b, precision=None):
    return lax.dot_general(a, b, (((1,), (1,)), ((), ())), preferred_element_type=F32, precision=precision)


def _tn(a, b):
    return lax.dot_general(a, b, (((0,), (0,)), ((), ())), preferred_element_type=F32)


def _row_selector(n_heads, length, lanes_of_head):
    r = lax.broadcasted_iota(jnp.int32, (n_heads * length, LANES), 0) // length
    lane = lax.broadcasted_iota(jnp.int32, (n_heads * length, LANES), 1)
    sel = jnp.zeros((n_heads * length, LANES), F32)
    for lo in lanes_of_head:
        sel = sel + (lane == r + lo).astype(F32)
    return sel


def _tri_masks(n):
    r = lax.broadcasted_iota(jnp.int32, (n, n), 0)
    c = lax.broadcasted_iota(jnp.int32, (n, n), 1)
    return r, c


def _bmm(a, b):
    return lax.dot_general(a.astype(BF16), b.astype(BF16), (((2,), (1,)), ((0,), (0,))),
                           preferred_element_type=F32)


def _bmm_nt(a, b):
    return lax.dot_general(a.astype(BF16), b.astype(BF16), (((2,), (2,)), ((0,), (0,))),
                           preferred_element_type=F32)


def _heads(x, n_heads, width, offset=0):
    return jnp.stack([x[c * CHUNK:(c + 1) * CHUNK, offset + h * width:offset + (h + 1) * width]
                      for c in range(x.shape[0] // CHUNK) for h in range(n_heads)], axis=0)


def _head_cols(x, n_heads, lane0):
    return jnp.stack([x[c * CHUNK:(c + 1) * CHUNK, lane0 + h:lane0 + h + 1]
                      for c in range(x.shape[0] // CHUNK) for h in range(n_heads)], axis=0)


def _unheads(x, n_heads):
    n = x.shape[0] // n_heads
    return jnp.concatenate([jnp.concatenate([x[c * n_heads + h] for h in range(n_heads)], axis=-1)
                            for c in range(n)], axis=0)


def _chunk_cumsum(x):
    r, c = _tri_masks(x.shape[0])
    tri = jnp.logical_and(c <= r, r // CHUNK == c // CHUNK).astype(F32)
    return jnp.dot(tri, x, preferred_element_type=F32, precision=HI)


def _chunk_rows(sel, x):
    n = x.shape[0] // CHUNK
    rows = [_nt(sel, x[c * CHUNK:(c + 1) * CHUNK], HI) for c in range(n)]
    return jnp.concatenate(rows, axis=0).reshape(n * sel.shape[0] // CHUNK, CHUNK, CHUNK)


def _unit_lower_inverse(nmat, r, c):
    mm = _bmm
    eye = (r == c).astype(F32)
    same = (r // 16) == (c // 16)
    nd = jnp.where(same, nmat, 0.0)
    off = nmat - nd
    dinv = eye - nd
    p = nd
    for _ in range(3):
        p = mm(p, p)
        dinv = dinv + mm(dinv, p)
    m = mm(dinv, off)
    m2 = mm(m, m)
    left = eye - m
    left = left + mm(left, m2)
    return mm(left, dinv)


def _gdn_prompt_kernel(qkv_ref, z_ref, gt_ref, convw_ref, alog_ref, prow_ref, ng_ref, mix_ref, s_ref, ext_ref):
    cidx = pl.program_id(1)
    L = CHUNK
    rows = qkv_ref.shape[0]
    n_ch = rows // L

    @pl.when(cidx == 0)
    def _():
        ext_ref[0:8, :] = jnp.zeros((8, QKV_A), F32)
        s_ref[...] = jnp.zeros_like(s_ref)

    ext_ref[8:8 + rows, :] = qkv_ref[...]
    acc = convw_ref[0:1, :] * ext_ref[pl.ds(8 - (CONV_W - 1), rows), :]
    for i in range(1, CONV_W):
        acc = acc + convw_ref[i:i + 1, :] * ext_ref[pl.ds(8 - (CONV_W - 1) + i, rows), :]
    ext_ref[0:8, :] = ext_ref[rows:rows + 8, :]
    cs = _silu(acc)

    beta, g, _, _ = _gate_tile(gt_ref[...], alog_ref[...], prow_ref[...])
    r, c = _tri_masks(L)
    incl = c <= r
    strict = c < r
    gcum = _chunk_cumsum(jnp.where(_lanes(g.shape, LANE_A, H_A), g, 0.0))
    gc_rows = _chunk_rows(_row_selector(H_A, L, (LANE_A,)), gcum)
    q = _heads(cs, H_A, DK_A)
    k = _heads(cs, H_A, DK_A, H_A * DK_A)
    v = _heads(cs, H_A, DV_A, 2 * H_A * DK_A)
    q = q * lax.rsqrt(jnp.sum(q * q, axis=-1, keepdims=True) + NORM_EPS) * (DK_A ** -0.5)
    k = k * lax.rsqrt(jnp.sum(k * k, axis=-1, keepdims=True) + NORM_EPS)
    beta_c = _head_cols(beta, H_A, LANE_BETA)
    gc_c = _head_cols(gcum, H_A, LANE_A)
    decay = jnp.where(incl, jnp.exp(jnp.where(incl, gc_c - gc_rows, 0.0)), 0.0)
    kb = k.astype(BF16)
    nmat = jnp.where(strict, beta_c * _bmm_nt(kb, kb) * decay, 0.0)
    egc = jnp.exp(gc_c)
    rhs = jnp.concatenate([v * beta_c, k * (beta_c * egc)], axis=-1)
    sol = _bmm(_unit_lower_inverse(nmat, r, c), rhs)
    u = sol[:, :, :DV_A]
    wq = jnp.concatenate([sol[:, :, DV_A:], q * egc], axis=1).astype(BF16)
    qk = (_bmm_nt(q, kb) * decay).astype(BF16)
    gc_last = gc_c[:, L - 1:L]
    k_tail = (k * jnp.exp(gc_last - gc_c)).astype(BF16)
    g_tail = jnp.exp(gc_last)
    state = s_ref[0]
    outs = []
    for ci in range(n_ch):
        sl = slice(ci * H_A, (ci + 1) * H_A)
        ws = _bmm(wq[sl], state)
        delta = u[sl] - ws[:, :L]
        outs.append(ws[:, L:] + _bmm(qk[sl], delta))
        delta_b = delta.astype(BF16)
        state = state * g_tail[sl] + jnp.stack([_tn(k_tail[ci * H_A + h], delta_b[h]) for h in range(H_A)], axis=0)
    s_ref[0] = state
    o = jnp.concatenate(outs, axis=0)
    o = o * lax.rsqrt(jnp.mean(o * o, axis=-1, keepdims=True) + NORM_EPS) * ng_ref[...]
    o = o * _silu(_heads(z_ref[...], H_A, DV_A))
    mix_ref[...] = _unheads(o, H_A)


AB_CHUNKS_PER_STEP = 4


def gdn_prompt(qkv, z, gates, conv_w, alog_row, prow, norm_g, bsz):
    m = qkv.shape[0]
    rows = AB_CHUNKS_PER_STEP * CHUNK
    nc = m // bsz // rows
    row = lambda w: pl.BlockSpec((rows, w), lambda b, c: (b * nc + c, 0))
    full = lambda shape: pl.BlockSpec(shape, lambda b, c: (0,) * len(shape))
    return pl.pallas_call(
        _gdn_prompt_kernel, grid=(bsz, nc),
        in_specs=[row(QKV_A), row(H_A * DV_A), row(LANES), full(conv_w.shape), full((1, LANES)), full((1, LANES)),
                  full((1, DV_A))],
        out_specs=[row(H_A * DV_A), pl.BlockSpec((1, H_A, DK_A, DV_A), lambda b, c: (b, 0, 0, 0))],
        out_shape=[jax.ShapeDtypeStruct((m, H_A * DV_A), F32), jax.ShapeDtypeStruct((bsz, H_A, DK_A, DV_A), F32)],
        scratch_shapes=[pltpu.VMEM((rows + 8, QKV_A), F32)],
        compiler_params=_cparams("parallel", "arbitrary"), name="gdn_prompt")(
            qkv, z, gates, conv_w, alog_row, prow, norm_g.reshape(1, DV_A))


def _mlstm_prompt_kernel(q_ref, k_ref, v_ref, og_ref, gt_ref, alog_ref, prow_ref, ng_ref, mix_ref, c_ref, m_ref):
    cidx = pl.program_id(1)
    L = CHUNK

    @pl.when(cidx == 0)
    def _():
        c_ref[...] = jnp.zeros_like(c_ref)
        m_ref[...] = jnp.zeros_like(m_ref)

    n_ch = q_ref.shape[0] // L
    _, _, ipre, logf = _gate_tile(gt_ref[...], alog_ref[...], prow_ref[...])
    r, c = _tri_masks(L)
    incl = c <= r
    fsel = _lanes(logf.shape, LANE_F, H_B)
    bcum = _chunk_cumsum(jnp.where(fsel, logf, 0.0))
    rowvals = jnp.where(_lanes(ipre.shape, LANE_I, H_B), ipre, 0.0) - jnp.where(fsel, bcum, 0.0)
    rows = _chunk_rows(_row_selector(H_B, L, (LANE_I, LANE_F)), rowvals)
    mrow = m_ref[0]
    lane_row = lax.broadcasted_iota(jnp.int32, mrow.shape, 1)
    qb = _heads(q_ref[...], H_B, DQK_B).astype(BF16)
    ks = _heads(k_ref[...], H_B, DQK_B) * (DQK_B ** -0.5)
    ones_col = jnp.broadcast_to((lax.broadcasted_iota(jnp.int32, (L, LANES), 1) == 0).astype(F32),
                                (n_ch * H_B, L, LANES))
    v_ext = jnp.concatenate([_heads(v_ref[...], H_B, DV_B), ones_col], axis=-1).astype(BF16)
    b_c = _head_cols(bcum, H_B, LANE_F)
    i_c = _head_cols(ipre, H_B, LANE_I)
    dmat = jnp.where(incl, b_c + rows, -jnp.inf)
    m_intra = jnp.max(dmat, axis=-1, keepdims=True)
    w_intra = jnp.exp(dmat - m_intra) * _bmm_nt(qb, ks)
    nd_intra = _bmm(w_intra, v_ext)
    b_last = b_c[:, L - 1:L]
    e_end = b_last - b_c + i_c
    e_max = jnp.max(e_end, axis=1, keepdims=True)
    kw = (ks * jnp.exp(e_end - e_max)).astype(BF16)
    kv_end = jnp.stack([_tn(kw[i], v_ext[i]) for i in range(n_ch * H_B)], axis=0)
    m_prev = jnp.stack([mrow[:, h:h + 1] for h in range(H_B)], axis=0)
    state = c_ref[0]
    outs = []
    for ci in range(n_ch):
        sl = slice(ci * H_B, (ci + 1) * H_B)
        inter = b_c[sl] + m_prev
        m_t = jnp.maximum(inter, m_intra[sl])
        nd = jnp.exp(inter - m_t) * _bmm(qb[sl], state) + jnp.exp(m_intra[sl] - m_t) * nd_intra[sl]
        outs.append(nd[:, :, :DV_B] / jnp.maximum(jnp.abs(nd[:, :, DV_B:DV_B + 1]), jnp.exp(-m_t)))
        m_new = jnp.maximum(b_last[sl] + m_prev, e_max[sl])
        state = jnp.exp(b_last[sl] + m_prev - m_new) * state + jnp.exp(e_max[sl] - m_new) * kv_end[sl]
        m_prev = m_new
    c_ref[0] = state
    for h in range(H_B):
        mrow = jnp.where(lane_row == h, m_prev[h], mrow)
    m_ref[0] = mrow
    hh = jnp.concatenate(outs, axis=0)
    hh = hh * lax.rsqrt(jnp.mean(hh * hh, axis=-1, keepdims=True) + NORM_EPS) * ng_ref[...]
    hh = jax.nn.sigmoid(_heads(og_ref[...], H_B, DV_B)) * hh
    mix_ref[...] = _unheads(hh, H_B)


def mlstm_prompt(q, k, v, og, gates, alog_row, prow, norm_g, bsz):
    m = q.shape[0]
    rows = AB_CHUNKS_PER_STEP * CHUNK
    nc = m // bsz // rows
    row = lambda w: pl.BlockSpec((rows, w), lambda b, c: (b * nc + c, 0))
    full = lambda shape: pl.BlockSpec(shape, lambda b, c: (0,) * len(shape))
    return pl.pallas_call(
        _mlstm_prompt_kernel, grid=(bsz, nc),
        in_specs=[row(H_B * DQK_B), row(H_B * DQK_B), row(H_B * DV_B), row(H_B * DV_B), row(LANES),
                  full((1, LANES)), full((1, LANES)), full((1, DV_B))],
        out_specs=[row(H_B * DV_B), pl.BlockSpec((1, H_B, DQK_B, DV_B + LANES), lambda b, c: (b, 0, 0, 0)),
                   pl.BlockSpec((1, 1, LANES), lambda b, c: (b, 0, 0))],
        out_shape=[jax.ShapeDtypeStruct((m, H_B * DV_B), F32),
                   jax.ShapeDtypeStruct((bsz, H_B, DQK_B, DV_B + LANES), F32),
                   jax.ShapeDtypeStruct((bsz, 1, LANES), F32)],
        compiler_params=_cparams("parallel", "arbitrary"), name="mlstm_prompt")(
            q, k, v, og, gates, alog_row, prow, norm_g.reshape(1, DV_B))


def _columns(x8):
    n = x8.shape[1]
    r, c = _tri_masks(n)
    return _nt((r == c).astype(F32), x8, HI)


def _ab_sample_kernel(qn_ref, kn_ref, vn_ref, cq_ref, ck_ref, cv_ref, wq_ref, wk_ref, wv_ref, z_ref,
                      qb_ref, kb_ref, vb_ref, og_ref, gt_ref, alog_ref, prow_ref, nga_ref, ngb_ref,
                      s_in, c_in, n_in, m_in,
                      oa_ref, ob_ref, s_out, c_out, n_out, m_out):
    def conv(new_ref, prev_ref, w_ref):
        acc = w_ref[CONV_W - 1] * new_ref[0]
        for i in range(CONV_W - 1):
            acc = acc + w_ref[i] * prev_ref[0, i]
        return _silu(acc)

    q8 = conv(qn_ref, cq_ref, wq_ref)
    k8 = conv(kn_ref, ck_ref, wk_ref)
    v8 = conv(vn_ref, cv_ref, wv_ref)
    q8 = q8 * lax.rsqrt(jnp.sum(q8 * q8, axis=-1, keepdims=True) + NORM_EPS) * (DK_A ** -0.5)
    k8 = k8 * lax.rsqrt(jnp.sum(k8 * k8, axis=-1, keepdims=True) + NORM_EPS)
    beta, g, ipre, logf = _gate_tile(gt_ref[0], alog_ref[...], prow_ref[...])
    q_cols, k_cols = _columns(q8), _columns(k8)
    z8 = z_ref[0]
    outs = []
    for h in range(H_A):
        s = s_in[0, h] * jnp.exp(g[:, LANE_A + h:LANE_A + h + 1])
        kc = k_cols[:, h:h + 1]
        err = v8[h:h + 1] - jnp.sum(kc * s, axis=0, keepdims=True)
        s = s + kc * (beta[:, LANE_BETA + h:LANE_BETA + h + 1] * err)
        s_out[0, h] = s
        outs.append(jnp.sum(q_cols[:, h:h + 1] * s, axis=0, keepdims=True))
    o = jnp.concatenate(outs, axis=0)
    o = o * lax.rsqrt(jnp.mean(o * o, axis=-1, keepdims=True) + NORM_EPS) * nga_ref[...]
    oa_ref[0] = o * _silu(z8)

    zeros4 = jnp.zeros((8 - H_B, DQK_B), F32)
    qb_cols = _columns(jnp.concatenate([qb_ref[0], zeros4], axis=0))
    kb_cols = _columns(jnp.concatenate([kb_ref[0] * (DQK_B ** -0.5), zeros4], axis=0))
    vb = vb_ref[0]
    n_cols = n_in[0]
    m_row = m_in[0]
    lane_n = lax.broadcasted_iota(jnp.int32, n_cols.shape, 1)
    lane_m = lax.broadcasted_iota(jnp.int32, m_row.shape, 1)
    outs = []
    for h in range(H_B):
        lf = logf[:, LANE_F + h:LANE_F + h + 1]
        it = ipre[:, LANE_I + h:LANE_I + h + 1]
        m_prev = m_row[:, h:h + 1]
        m_new = jnp.maximum(lf + m_prev, it)
        f_sc = jnp.exp(lf + m_prev - m_new)
        i_sc = jnp.exp(it - m_new)
        kc = kb_cols[:, h:h + 1]
        qc = qb_cols[:, h:h + 1]
        cm = f_sc * c_in[0, h] + i_sc * (kc * vb[h:h + 1])
        nn = f_sc * n_cols[:, h:h + 1] + i_sc * kc
        c_out[0, h] = cm
        n_cols = jnp.where(lane_n == h, nn, n_cols)
        m_row = jnp.where(lane_m == h, m_new, m_row)
        num = jnp.sum(qc * cm, axis=0, keepdims=True)
        den = jnp.sum(qc * nn, axis=0, keepdims=True)
        outs.append(num / jnp.maximum(jnp.abs(den), jnp.exp(-m_new)))
    hb = jnp.concatenate(outs, axis=0)
    hb = hb * lax.rsqrt(jnp.mean(hb * hb, axis=-1, keepdims=True) + NORM_EPS) * ngb_ref[...]
    ob_ref[0] = jax.nn.sigmoid(og_ref[0]) * hb
    n_out[0] = n_cols
    m_out[0] = m_row


def ab_sample(qkv, z, q_b, k_b, v_b, o_b, gates, conv_prev, conv_w, alog_row, prow, norm_g_a, norm_g_b,
              s_prev, c_prev, n_prev, m_prev):
    bsz = qkv.shape[0]
    hk = H_A * DK_A
    part = lambda x, i, w: x[..., i * hk:i * hk + H_A * w].reshape(x.shape[:-1] + (H_A, w))
    new_parts = [part(qkv, 0, DK_A), part(qkv, 1, DK_A), part(qkv, 2, DV_A)]
    prev_parts = [part(conv_prev, 0, DK_A), part(conv_prev, 1, DK_A), part(conv_prev, 2, DV_A)]
    w_parts = [part(conv_w, 0, DK_A), part(conv_w, 1, DK_A), part(conv_w, 2, DV_A)]
    args = new_parts + prev_parts + w_parts + [
        z.reshape(bsz, H_A, DV_A), q_b.reshape(bsz, H_B, DQK_B), k_b.reshape(bsz, H_B, DQK_B),
        v_b.reshape(bsz, H_B, DV_B), o_b.reshape(bsz, H_B, DV_B), gates.reshape(bsz, 1, LANES),
        alog_row, prow, norm_g_a.reshape(1, DV_A), norm_g_b.reshape(1, DV_B),
        s_prev, c_prev, jnp.swapaxes(n_prev, 1, 2), m_prev.reshape(bsz, 1, H_B)]

    def spec(x, batched):
        nd = x.ndim
        if batched:
            return pl.BlockSpec((1,) + x.shape[1:], lambda b: (b,) + (0,) * (nd - 1))
        return pl.BlockSpec(x.shape, lambda b: (0,) * nd)

    batched = [True] * 6 + [False] * 3 + [True] * 6 + [False] * 4 + [True] * 4
    out_shape = [jax.ShapeDtypeStruct((bsz, H_A, DV_A), F32), jax.ShapeDtypeStruct((bsz, H_B, DV_B), F32),
                 jax.ShapeDtypeStruct(s_prev.shape, F32), jax.ShapeDtypeStruct(c_prev.shape, F32),
                 jax.ShapeDtypeStruct((bsz, DQK_B, H_B), F32), jax.ShapeDtypeStruct((bsz, 1, H_B), F32)]
    oa, ob, s_new, c_new, n_new, m_new = pl.pallas_call(
        _ab_sample_kernel, grid=(bsz,),
        in_specs=[spec(x, bt) for x, bt in zip(args, batched)],
        out_specs=[spec(x, True) for x in out_shape], out_shape=out_shape,
        compiler_params=_cparams("parallel"), name="ab_sample")(*args)
    mix = jnp.concatenate([oa.reshape(bsz, H_A * DV_A), ob.reshape(bsz, H_B * DV_B)], axis=-1)
    return mix, s_new, c_new, jnp.swapaxes(n_new, 1, 2), m_new.reshape(bsz, H_B)


MOE_BLOCK_PROMPT = 256
MOE_BLOCK_SAMPLE = 32


def _diff_qkv_weights(w_qkv):
    return tuple(w_qkv[:, i * C_W:(i + 1) * C_W].astype(BF16) for i in range(3))


def _ab_weights(w_in, a_log, dt_bias, b_i, b_f):
    sizes = (QKV_A, H_A * DV_A, H_A, H_A, H_B * DQK_B, H_B * DQK_B, H_B * DV_B, H_B * DV_B, H_B, H_B)
    offs = [0]
    for s in sizes:
        offs.append(offs[-1] + s)
    col = lambda i: w_in[:, offs[i]:offs[i + 1]]
    w_lo = tuple(col(i).astype(BF16) for i in (0, 1, 4, 5, 6, 7))
    w_gate = jnp.concatenate([col(2), col(3), col(8), col(9)], axis=1)
    w_gate = jnp.pad(w_gate, ((0, 0), (0, LANES - w_gate.shape[1])))
    zeros = lambda n: jnp.zeros((n,), F32)
    pad = LANES - 2 * H_A - 2 * H_B
    alog_row = jnp.concatenate([zeros(H_A), a_log.astype(F32), zeros(2 * H_B + pad)]).reshape(1, LANES)
    prow = jnp.concatenate([zeros(H_A), dt_bias.astype(F32), b_i.astype(F32), b_f.astype(F32),
                            zeros(pad)]).reshape(1, LANES)
    return w_lo, w_gate, alog_row, prow


def kernel(x_prompt, x_sample, state_delta_S, state_delta_conv, state_mlstm_C, state_mlstm_n, state_mlstm_m,
           cache_diff_k, cache_diff_v, cache_mem_k, cache_mem_v, page_table, mem_prompt,
           w_in_ab, conv_w_a, a_log_a, dt_bias_a, norm_g_a, b_i_b, b_f_b, norm_g_b, w_out_ab,
           w_qkv_c, lam_q1, lam_k1, lam_q2, lam_k2, subln_g_c, w_o_c, rel_bias,
           w_xq, w_xkv, w_xo, ln_g, ln_b, w_router, b_router, w_moe_in, b_moe_in, w_moe_out, b_moe_out):
    bp, t, d = x_prompt.shape
    bs = x_sample.shape[0]
    xp = x_prompt.reshape(bp * t, d)
    xs = x_sample.reshape(bs, d)
    mem2d = mem_prompt.reshape(bp * N_MEM, d)
    cmk = cache_mem_k.reshape(DEPTH * bs, N_MEM * H_X, DH_X)
    cmv = cache_mem_v.reshape(DEPTH * bs, N_MEM * H_X, DH_X)
    wm_in = w_moe_in.reshape(DEPTH * N_EXPERTS, d, 2 * D_FF)
    bm_in = b_moe_in.reshape(DEPTH * N_EXPERTS, 2 * D_FF)
    wm_out = w_moe_out.reshape(DEPTH * N_EXPERTS, D_FF, d)
    bm_out = b_moe_out.reshape(DEPTH * N_EXPERTS, d)
    p_S, p_conv, p_C, p_n, p_m, p_k, p_v, p_mk, p_mv = [], [], [], [], [], [], [], [], []
    s_S, s_conv, s_C, s_n, s_m, s_k, s_v = [], [], [], [], [], [], []
    prompt_proj = ()
    for layer in range(DEPTH):
        j = layer // 2
        g0, b0 = ln_g[layer, 0].reshape(1, d), ln_b[layer, 0].reshape(1, d)
        g1, b1 = ln_g[layer, 1].reshape(1, d), ln_b[layer, 1].reshape(1, d)
        g2, b2 = ln_g[layer, 2].reshape(1, d), ln_b[layer, 2].reshape(1, d)
        if layer % 2 == 0:
            w_lo, w_gate, alog_row, prow = _ab_weights(w_in_ab[j], a_log_a[j], dt_bias_a[j], b_i_b[j], b_f_b[j])
            w_out = w_out_ab[j].astype(BF16)
            w_out_a, w_out_b = w_out[:H_A * DV_A], w_out[H_A * DV_A:]
            qkv, z, q_b, k_b, v_b, o_b, gates = mm_multi(xp, w_lo, (w_gate,), tm=256)
            mix_a, st_S = gdn_prompt(qkv, z, gates, conv_w_a[j], alog_row, prow, norm_g_a[j], bp)
            mix_b, c_ext, m_row = mlstm_prompt(q_b, k_b, v_b, o_b, gates, alog_row, prow, norm_g_b[j], bp)
            xp = proj_ln([mix_a, mix_b], [w_out_a, w_out_b], xp, g0, b0)
            p_S.append(st_S)
            p_conv.append(qkv.reshape(bp, t, QKV_A)[:, t - (CONV_W - 1):])
            p_C.append(c_ext[..., :DV_B])
            p_n.append(c_ext[..., DV_B])
            p_m.append(m_row[:, 0, :H_B])
            qkv, z, q_b, k_b, v_b, o_b, gates = mm_multi(xs, w_lo, (w_gate,))
            mix, st_S, st_C, st_n, st_m = ab_sample(
                qkv, z, q_b, k_b, v_b, o_b, gates, state_delta_conv[j], conv_w_a[j], alog_row, prow,
                norm_g_a[j], norm_g_b[j], state_delta_S[j].astype(F32), state_mlstm_C[j].astype(F32),
                state_mlstm_n[j].astype(F32), state_mlstm_m[j].astype(F32))
            xs = proj_ln([mix], [w_out], xs, g0, b0)
            s_S.append(st_S)
            s_conv.append(jnp.concatenate([state_delta_conv[j][:, 1:].astype(F32), qkv[:, None, :]], axis=1))
            s_C.append(st_C)
            s_n.append(st_n)
            s_m.append(st_m)
        else:
            lam_init = 0.8 - 0.6 * math.exp(-0.3 * layer)
            lam = (jnp.exp(jnp.sum(lam_q1[j].astype(F32) * lam_k1[j].astype(F32)))
                   - jnp.exp(jnp.sum(lam_q2[j].astype(F32) * lam_k2[j].astype(F32))) + lam_init)
            w_qkv = _diff_qkv_weights(w_qkv_c[j])
            w_o = w_o_c[j].astype(BF16)
            q, k, v = prompt_proj if prompt_proj else mm_multi(xp, w_qkv, tm=512)
            o = diff_attn_prompt_core(q, k, v, bp, rel_bias, lam, lam_init, subln_g_c[j])
            xp = proj_ln([o], [w_o], xp, g0, b0)
            p_k.append(k.reshape(bp, t, H_C, 2 * DH_C))
            p_v.append(v.reshape(bp, t, H_C, VD_C))
            q, k, v = mm_multi(xs, w_qkv)
            o = diff_attn_sample_core(q, k, v, cache_diff_k, cache_diff_v, page_table, j, rel_bias, lam, lam_init,
                                      subln_g_c[j])
            xs = proj_ln([o], [w_o], xs, g0, b0)
            s_k.append(k.reshape(bs, 1, H_C, 2 * DH_C))
            s_v.append(v.reshape(bs, 1, H_C, VD_C))
        w_q, w_o = w_xq[layer].astype(BF16), w_xo[layer].astype(BF16)
        w_r = jnp.pad(w_router[layer].astype(F32), ((0, 0), (0, LANES - N_EXPERTS)))
        w_r_hi = w_r.astype(BF16)
        w_r = jnp.stack([w_r_hi, (w_r - w_r_hi.astype(F32)).astype(BF16)])
        b_r = jnp.pad(b_router[layer].astype(F32), (0, LANES - N_EXPERTS)).reshape(1, LANES)
        mk, mv = mm_multi(mem2d, (w_xkv[layer][:, :X_W].astype(BF16), w_xkv[layer][:, X_W:].astype(BF16)))
        p_mk.append(mk.reshape(bp, N_MEM, H_X, DH_X))
        p_mv.append(mv.reshape(bp, N_MEM, H_X, DH_X))
        xp, xp_tiles, idx_p, gate_p = xattn_prompt(xp, mk.reshape(bp, N_MEM, X_W), mv.reshape(bp, N_MEM, X_W),
                                                   w_q, w_o, g1, b1, w_r, b_r)
        (q,) = mm_multi(xs, (w_q,))
        o = xattn_sample_core(q, cmk, cmv, off=layer * bs)
        xs, xs_tiles, idx_s, gate_s = proj_ln_route(o, w_o, xs, g1, b1, w_r, b_r)
        nxt = layer + 1
        w_next = _diff_qkv_weights(w_qkv_c[nxt // 2]) if nxt < DEPTH and nxt % 2 == 1 else ()
        xp, prompt_proj = moe_ln(xp, xp_tiles, idx_p, gate_p, wm_in, bm_in, wm_out, bm_out, g2, b2, MOE_BLOCK_PROMPT,
                                 e_off=layer * N_EXPERTS, w_next=w_next)
        xs, _ = moe_ln(xs, xs_tiles, idx_s, gate_s, wm_in, bm_in, wm_out, bm_out, g2, b2, MOE_BLOCK_SAMPLE,
                       e_off=layer * N_EXPERTS)

    return (xp.reshape(bp, t, d), xs.reshape(bs, 1, d),
            jnp.stack(p_S), jnp.stack(p_conv), jnp.stack(p_C), jnp.stack(p_n), jnp.stack(p_m),
            jnp.stack(p_k, axis=1), jnp.stack(p_v, axis=1), jnp.stack(p_mk), jnp.stack(p_mv),
            jnp.stack(s_S), jnp.stack(s_conv), jnp.stack(s_C), jnp.stack(s_n), jnp.stack(s_m),
            jnp.stack(s_k, axis=1), jnp.stack(s_v, axis=1))
```

```python
import functools
import math

import jax
import jax.numpy as jnp
from jax import lax
from jax.experimental import pallas as pl
from jax.experimental.pallas import tpu as pltpu

F32 = jnp.float32
BF16 = jnp.bfloat16
HI = lax.Precision.HIGHEST

D_MODEL = 1024
DEPTH = 2
H_A, DK_A, DV_A, CONV_W, CHUNK = 8, 64, 64, 4, 64
H_B, DQK_B, DV_B = 4, 64, 128
QKV_A = H_A * (2 * DK_A + DV_A)
H_C, DH_C = 8, 64
VD_C = 2 * DH_C
C_W = H_C * 2 * DH_C
N_BUCKETS, MAX_DISTANCE = 32, 128
N_MEM, H_X, DH_X = 256, 4, 128
X_W = H_X * DH_X
N_EXPERTS, TOP_K = 32, 4
D_FF = D_MODEL
SWIGLU_ALPHA, SWIGLU_LIMIT = 1.702, 7.0
DN_ALPHA = (2 * DEPTH) ** 0.25
LN_EPS = 1e-5
NORM_EPS = 1e-6

LANES = 128
SUBLANES = 8
VMEM_PHYSICAL = 64 * 1024 * 1024
VMEM_LIMIT = 48 * 1024 * 1024


def _cparams(*sem):
    return pltpu.CompilerParams(dimension_semantics=tuple(sem), vmem_limit_bytes=VMEM_LIMIT)


def _layer_norm(y, g, b):
    mu = jnp.mean(y, axis=-1, keepdims=True)
    d = y - mu
    var = jnp.mean(d * d, axis=-1, keepdims=True)
    return d * lax.rsqrt(var + LN_EPS) * g + b


def _mm_multi_kernel(x_ref, *refs, n_lo, n_hi):
    n = n_lo + n_hi
    ws, outs = refs[:n], refs[n:]
    x = x_ref[...]
    xb = x.astype(BF16)
    for i in range(n_lo):
        outs[i][...] = jnp.dot(xb, ws[i][...], preferred_element_type=F32)
    for i in range(n_lo, n):
        outs[i][...] = jnp.dot(x, ws[i][...], preferred_element_type=F32, precision=HI)


def mm_multi(x, w_lo, w_hi=(), tm=256):
    m, k = x.shape
    tm = min(tm, m)
    assert m % tm == 0
    ws = tuple(w_lo) + tuple(w_hi)
    in_specs = [pl.BlockSpec((tm, k), lambda i: (i, 0))]
    in_specs += [pl.BlockSpec(w.shape, lambda i: (0, 0)) for w in ws]
    out_specs = [pl.BlockSpec((tm, w.shape[1]), lambda i: (i, 0)) for w in ws]
    out_shape = [jax.ShapeDtypeStruct((m, w.shape[1]), F32) for w in ws]
    return pl.pallas_call(
        functools.partial(_mm_multi_kernel, n_lo=len(w_lo), n_hi=len(w_hi)),
        grid=(m // tm,), in_specs=in_specs, out_specs=out_specs, out_shape=out_shape,
        compiler_params=_cparams("parallel"), name="mm_multi")(x, *ws)


def _proj_ln_kernel(*refs, n):
    a_refs, w_refs = refs[:n], refs[n:2 * n]
    x_ref, g_ref, b_ref, o_ref = refs[2 * n:]
    h = jnp.dot(a_refs[0][...].astype(BF16), w_refs[0][...], preferred_element_type=F32)
    for a_ref, w_ref in zip(a_refs[1:], w_refs[1:]):
        h = h + jnp.dot(a_ref[...].astype(BF16), w_ref[...], preferred_element_type=F32)
    o_ref[...] = _layer_norm(DN_ALPHA * x_ref[...] + h, g_ref[...], b_ref[...])


def proj_ln(a_list, w_list, x_res, g, b, tm=512):
    m, d = x_res.shape
    tm = min(tm, m)
    assert m % tm == 0
    n = len(a_list)
    return pl.pallas_call(
        functools.partial(_proj_ln_kernel, n=n), grid=(m // tm,),
        in_specs=[pl.BlockSpec((tm, a.shape[1]), lambda i: (i, 0)) for a in a_list]
        + [pl.BlockSpec(w.shape, lambda i: (0, 0)) for w in w_list]
        + [pl.BlockSpec((tm, d), lambda i: (i, 0)), pl.BlockSpec((1, d), lambda i: (0, 0)),
           pl.BlockSpec((1, d), lambda i: (0, 0))],
        out_specs=pl.BlockSpec((tm, d), lambda i: (i, 0)),
        out_shape=jax.ShapeDtypeStruct((m, d), F32),
        compiler_params=_cparams("parallel"), name="proj_ln")(*a_list, *w_list, x_res, g, b)


def _mem_attention(q, mk, mv):
    outs = []
    for h in range(H_X):
        sl = slice(h * DH_X, (h + 1) * DH_X)
        s = lax.dot_general(q[:, sl].astype(BF16), mk[:, sl].astype(BF16), (((1,), (1,)), ((), ())),
                            preferred_element_type=F32) * (DH_X ** -0.5)
        s = s - jnp.max(s, axis=-1, keepdims=True)
        p = jnp.exp(s)
        p = p / jnp.sum(p, axis=-1, keepdims=True)
        outs.append(jnp.dot(p.astype(BF16), mv[:, sl].astype(BF16), preferred_element_type=F32))
    return jnp.concatenate(outs, axis=-1)


def _route(x, wr, br):
    x_hi = x.astype(BF16)
    x_lo = (x - x_hi.astype(F32)).astype(BF16)
    logits = (jnp.dot(x_hi, wr[0], preferred_element_type=F32) + jnp.dot(x_lo, wr[0], preferred_element_type=F32)
              + jnp.dot(x_hi, wr[1], preferred_element_type=F32) + br)
    lane = lax.broadcasted_iota(jnp.int32, logits.shape, 1)
    work = jnp.where(lane < N_EXPERTS, logits, -jnp.inf)
    idx_out = jnp.zeros(logits.shape, jnp.int32)
    val_out = jnp.full(logits.shape, -jnp.inf, F32)
    for k in range(TOP_K):
        m = jnp.max(work, axis=-1, keepdims=True)
        sel = jnp.min(jnp.where(work == m, lane, LANES), axis=-1, keepdims=True)
        idx_out = jnp.where(lane == k, sel, idx_out)
        val_out = jnp.where(lane == k, m, val_out)
        work = jnp.where(lane == sel, -jnp.inf, work)
    e = jnp.exp(val_out - jnp.max(val_out, axis=-1, keepdims=True))
    gates = e / jnp.sum(e, axis=-1, keepdims=True)
    return idx_out, gates


def _store_token_tiles(o3_ref, y):
    for s in range(y.shape[1] // LANES):
        o3_ref[:, s, :] = y[:, s * LANES:(s + 1) * LANES]


def _xattn_prompt_kernel(x_ref, mk_ref, mv_ref, wq_ref, wo_ref, g_ref, b_ref, wr_ref, br_ref,
                         o_ref, o3_ref, idx_ref, gate_ref):
    x = x_ref[...]
    q = jnp.dot(x.astype(BF16), wq_ref[...], preferred_element_type=F32)
    o = _mem_attention(q, mk_ref[0], mv_ref[0])
    h = jnp.dot(o.astype(BF16), wo_ref[...], preferred_element_type=F32)
    y = _layer_norm(DN_ALPHA * x + h, g_ref[...], b_ref[...])
    o_ref[...] = y
    _store_token_tiles(o3_ref, y)
    idx, gates = _route(y, wr_ref[...], br_ref[...])
    idx_ref[...] = idx
    gate_ref[...] = gates


def xattn_prompt(x, mk, mv, wq, wo, g, b, wr, br, tq=512):
    m, d = x.shape
    bsz = mk.shape[0]
    t = m // bsz
    nq = t // tq
    full = lambda shape: pl.BlockSpec(shape, lambda bi, qi: (0,) * len(shape))
    row = lambda w: pl.BlockSpec((tq, w), lambda bi, qi: (bi * nq + qi, 0))
    return pl.pallas_call(
        _xattn_prompt_kernel, grid=(bsz, nq),
        in_specs=[row(d), pl.BlockSpec((1, N_MEM, X_W), lambda bi, qi: (bi, 0, 0)),
                  pl.BlockSpec((1, N_MEM, X_W), lambda bi, qi: (bi, 0, 0)),
                  full(wq.shape), full(wo.shape), full(g.shape), full(b.shape), full(wr.shape), full(br.shape)],
        out_specs=[row(d), pl.BlockSpec((tq, d // LANES, LANES), lambda bi, qi: (bi * nq + qi, 0, 0)),
                   row(LANES), row(LANES)],
        out_shape=[jax.ShapeDtypeStruct((m, d), F32), jax.ShapeDtypeStruct((m, d // LANES, LANES), F32),
                   jax.ShapeDtypeStruct((m, LANES), jnp.int32), jax.ShapeDtypeStruct((m, LANES), F32)],
        compiler_params=_cparams("parallel", "parallel"), name="xattn_prompt")(x, mk, mv, wq, wo, g, b, wr, br)


def _xattn_sample_kernel(q_ref, mk_ref, mv_ref, o_ref):
    q = jnp.broadcast_to(q_ref[0], (8, X_W))
    mk = jnp.concatenate([mk_ref[0, pl.ds(h, N_MEM, stride=H_X), :] for h in range(H_X)], axis=-1)
    mv = jnp.concatenate([mv_ref[0, pl.ds(h, N_MEM, stride=H_X), :] for h in range(H_X)], axis=-1)
    o = _mem_attention(q, mk, mv)
    o_ref[0] = o[0:1]


def xattn_sample_core(q, mk, mv, off=0):
    bsz = q.shape[0]
    spec3 = pl.BlockSpec((1, N_MEM * H_X, DH_X), lambda bi: (off + bi, 0, 0))
    out = pl.pallas_call(
        _xattn_sample_kernel, grid=(bsz,),
        in_specs=[pl.BlockSpec((1, 1, X_W), lambda bi: (bi, 0, 0)), spec3, spec3],
        out_specs=pl.BlockSpec((1, 1, X_W), lambda bi: (bi, 0, 0)),
        out_shape=jax.ShapeDtypeStruct((bsz, 1, X_W), F32),
        compiler_params=_cparams("parallel"), name="xattn_sample")(q.reshape(bsz, 1, X_W), mk, mv)
    return out.reshape(bsz, X_W)


def _ln_route_kernel(a_ref, w_ref, x_ref, g_ref, b_ref, wr_ref, br_ref, o_ref, o3_ref, idx_ref, gate_ref):
    h = jnp.dot(a_ref[...].astype(BF16), w_ref[...], preferred_element_type=F32)
    y = _layer_norm(DN_ALPHA * x_ref[...] + h, g_ref[...], b_ref[...])
    o_ref[...] = y
    _store_token_tiles(o3_ref, y)
    idx, gates = _route(y, wr_ref[...], br_ref[...])
    idx_ref[...] = idx
    gate_ref[...] = gates


def proj_ln_route(a, w, x_res, g, b, wr, br):
    m, d = x_res.shape
    return pl.pallas_call(
        _ln_route_kernel,
        out_shape=[jax.ShapeDtypeStruct((m, d), F32), jax.ShapeDtypeStruct((m, d // LANES, LANES), F32),
                   jax.ShapeDtypeStruct((m, LANES), jnp.int32), jax.ShapeDtypeStruct((m, LANES), F32)],
        compiler_params=pltpu.CompilerParams(vmem_limit_bytes=VMEM_LIMIT),
        name="proj_ln_route")(a, w, x_res, g, b, wr, br)


def _expert_onehots(idx):
    lane = lax.broadcasted_iota(jnp.int32, idx.shape, 1)
    onehots = [(idx[:, k:k + 1] == lane).astype(F32) for k in range(TOP_K)]
    tot = onehots[0]
    for k in range(1, TOP_K):
        tot = tot + onehots[k]
    return lane, onehots, tot


def _count_kernel(idx_ref, cnt_ref):
    @pl.when(pl.program_id(0) == 0)
    def _():
        cnt_ref[...] = jnp.zeros_like(cnt_ref)

    _, _, tot = _expert_onehots(idx_ref[...])
    cnt_ref[...] = cnt_ref[...] + jnp.sum(tot, axis=0, keepdims=True)


def _dest_kernel(idx_ref, base0_ref, dest_ref, base_ref):
    @pl.when(pl.program_id(0) == 0)
    def _():
        base_ref[...] = base0_ref[...]

    idx = idx_ref[...]
    tr = idx.shape[0]
    lane, onehots, tot = _expert_onehots(idx)
    r = lax.broadcasted_iota(jnp.int32, (tr, tr), 0)
    c = lax.broadcasted_iota(jnp.int32, (tr, tr), 1)
    strict = (c < r).astype(BF16)
    before = jnp.dot(strict, tot.astype(BF16), preferred_element_type=F32) + base_ref[...]
    out = jnp.zeros(idx.shape, F32)
    for k in range(TOP_K):
        out = jnp.where(lane == k, jnp.sum(onehots[k] * before, axis=-1, keepdims=True), out)
    dest_ref[...] = out.astype(jnp.int32)
    base_ref[...] = base_ref[...] + jnp.sum(tot, axis=0, keepdims=True)


def route_slots(idx, bm, tr=256):
    t = idx.shape[0]
    tr = min(tr, t)
    assert t % tr == 0
    rows = pl.BlockSpec((tr, LANES), lambda i: (i, 0))
    one = pl.BlockSpec((1, LANES), lambda i: (0, 0))
    counts = pl.pallas_call(
        _count_kernel, grid=(t // tr,), in_specs=[rows], out_specs=one,
        out_shape=jax.ShapeDtypeStruct((1, LANES), F32),
        compiler_params=_cparams("arbitrary"), name="route_count")(idx)
    counts = counts[0].astype(jnp.int32)
    padded = (counts + bm - 1) // bm * bm
    pad_end = jnp.cumsum(padded)
    base0 = (pad_end - padded).astype(F32).reshape(1, LANES)
    dest = pl.pallas_call(
        _dest_kernel, grid=(t // tr,), in_specs=[rows, one], out_specs=rows,
        out_shape=jax.ShapeDtypeStruct((t, LANES), jnp.int32),
        scratch_shapes=[pltpu.VMEM((1, LANES), F32)],
        compiler_params=_cparams("arbitrary"), name="route_dest")(idx, base0)
    return dest, pad_end[:N_EXPERTS], counts[:N_EXPERTS]


def _clamped_swiglu(glu, lin):
    glu = jnp.minimum(glu, SWIGLU_LIMIT)
    lin = jnp.clip(lin, -SWIGLU_LIMIT, SWIGLU_LIMIT)
    return glu * jax.nn.sigmoid(SWIGLU_ALPHA * glu) * (lin + 1.0)


def _tile_rows(buf):
    rows = buf.shape[0] * buf.shape[2]
    return jnp.concatenate([buf[:, j].reshape(rows, LANES) for j in range(buf.shape[1])], axis=-1)


def _moe_gmm_kernel(bexp_ref, nused_ref, tok_ref, tok1_ref, tok2_ref, x_hbm, win_ref, bin_ref, wout_ref,
                    bout_ref, y_hbm, xbuf, ybuf, winb, woutb, gsem, osem, *, bm):
    i = pl.program_id(0)
    n = pl.num_programs(0)
    nused = nused_ref[0]
    slot = lax.rem(i, 2)
    xslot = lax.rem(i, 3)
    n_sub = x_hbm.shape[1]
    groups = bm // SUBLANES

    def start_gather(idx_ref, s):
        def body(g, carry):
            for u in range(SUBLANES):
                pltpu.make_async_copy(x_hbm.at[idx_ref[0, 0, g * SUBLANES + u]], xbuf.at[s, g, :, u],
                                      gsem.at[s]).start(priority=u % 2)
            return carry
        lax.fori_loop(0, groups, body, 0)

    def wait_gather(s):
        pltpu.make_async_copy(xbuf.at[s], xbuf.at[s], gsem.at[s]).wait()

    def out_copies(blk, s):
        return [pltpu.make_async_copy(ybuf.at[s, :, j], y_hbm.at[pl.ds(blk * groups, groups), :, j], osem.at[s])
                for j in range(n_sub)]

    @pl.when(i == 0)
    def _():
        start_gather(tok_ref, 0)

        @pl.when(jnp.logical_and(1 < n, 1 < nused))
        def _():
            start_gather(tok1_ref, 1)

    active = i < nused
    prefetch = jnp.logical_and(i + 2 < n, i + 2 < nused)

    @pl.when(jnp.logical_and(active, jnp.logical_or(i == 0, bexp_ref[i] != bexp_ref[jnp.maximum(i - 1, 0)])))
    def _():
        winb[...] = win_ref[0].astype(BF16)
        woutb[...] = wout_ref[0].astype(BF16)

    @pl.when(i >= 2)
    def _():
        for cp in out_copies(i - 2, slot):
            cp.wait()

    def expert_mlp(gather_ahead):
        wait_gather(xslot)
        xb = _tile_rows(xbuf[xslot]).astype(BF16)
        if gather_ahead:
            nslot = lax.rem(i + 2, 3)
            for g in range(groups):
                for u in range(SUBLANES):
                    pltpu.make_async_copy(x_hbm.at[tok2_ref[0, 0, g * SUBLANES + u]], xbuf.at[nslot, g, :, u],
                                          gsem.at[nslot]).start(priority=u % 2)
        h = jnp.dot(xb, winb[...], preferred_element_type=F32) + bin_ref[0]
        act = _clamped_swiglu(h[:, :D_FF], h[:, D_FF:]).astype(BF16)
        y = jnp.dot(act, woutb[...], preferred_element_type=F32) + bout_ref[0]
        for j in range(n_sub):
            ybuf[slot, :, j] = y[:, j * LANES:(j + 1) * LANES].reshape(groups, SUBLANES, LANES)

    @pl.when(jnp.logical_and(active, prefetch))
    def _():
        expert_mlp(True)

    @pl.when(jnp.logical_and(active, jnp.logical_not(prefetch)))
    def _():
        expert_mlp(False)

    @pl.when(jnp.logical_not(active))
    def _():
        ybuf[slot] = jnp.zeros(ybuf.shape[1:], F32)

    for cp in out_copies(i, slot):
        cp.start()

    @pl.when(i == n - 1)
    def _():
        @pl.when(n >= 2)
        def _():
            for cp in out_copies(i - 1, 1 - slot):
                cp.wait()

        for cp in out_copies(i, slot):
            cp.wait()


def moe_gmm(x, row_tok, block_expert, n_used, w_in, b_in, w_out, b_out, bm):
    n_blocks = row_tok.shape[0]
    n_sub = x.shape[1]
    d = n_sub * LANES
    idx_spec = lambda off: pl.BlockSpec(
        (1, 1, bm), lambda i, be, nu: (jnp.minimum(i + off, n_blocks - 1), 0, 0), memory_space=pltpu.SMEM)
    ex = lambda i, be, nu: (be[i], 0, 0)
    grid_spec = pltpu.PrefetchScalarGridSpec(
        num_scalar_prefetch=2, grid=(n_blocks,),
        in_specs=[idx_spec(0), idx_spec(1), idx_spec(2),
                  pl.BlockSpec(memory_space=pl.ANY),
                  pl.BlockSpec((1, d, 2 * D_FF), ex), pl.BlockSpec((1, 1, 2 * D_FF), ex),
                  pl.BlockSpec((1, D_FF, d), ex), pl.BlockSpec((1, 1, d), ex)],
        out_specs=pl.BlockSpec(memory_space=pl.ANY),
        scratch_shapes=[pltpu.VMEM((3, bm // SUBLANES, n_sub, SUBLANES, LANES), F32),
                        pltpu.VMEM((2, bm // SUBLANES, n_sub, SUBLANES, LANES), F32),
                        pltpu.VMEM((d, 2 * D_FF), BF16), pltpu.VMEM((D_FF, d), BF16),
                        pltpu.SemaphoreType.DMA((3,)), pltpu.SemaphoreType.DMA((2,))])
    w_elems = d * 2 * D_FF + D_FF * d
    vmem_bytes = (2 * 4 + 2) * w_elems + 5 * bm * d * 4 + bm * (2 * D_FF + D_FF + 2 * d) * 4
    vmem_bytes = min(vmem_bytes + (4 << 20), VMEM_PHYSICAL - (6 << 20))
    return pl.pallas_call(
        functools.partial(_moe_gmm_kernel, bm=bm), grid_spec=grid_spec,
        out_shape=jax.ShapeDtypeStruct((n_blocks * bm // SUBLANES, SUBLANES, n_sub, LANES), F32),
        compiler_params=pltpu.CompilerParams(dimension_semantics=("arbitrary",), vmem_limit_bytes=vmem_bytes),
        name="moe_gmm")(
            block_expert, n_used, row_tok, row_tok, row_tok, x, w_in, b_in.reshape(b_in.shape[0], 1, -1),
            w_out, b_out.reshape(b_out.shape[0], 1, -1))


def _combine_ln_kernel(dst_ref, dst_next_ref, y_hbm, gate_ref, x_ref, g_ref, b_ref, *refs, n_proj):
    w_refs, o_ref, p_refs = refs[:n_proj], refs[n_proj], refs[n_proj + 1:2 * n_proj + 1]
    gbuf, sem = refs[2 * n_proj + 1:]
    i = pl.program_id(0)
    n = pl.num_programs(0)
    slot = lax.rem(i, 2)
    groups = gbuf.shape[2]

    def start_gather(idx_ref, s):
        def body(g, carry):
            for u in range(SUBLANES):
                for k in range(TOP_K):
                    d = idx_ref[0, 0, (g * SUBLANES + u) * TOP_K + k]
                    pltpu.make_async_copy(y_hbm.at[d], gbuf.at[s, k, g, :, u], sem.at[s]).start(priority=k % 2)
            return carry
        lax.fori_loop(0, groups, body, 0)

    @pl.when(i == 0)
    def _():
        start_gather(dst_ref, 0)

    def tile(gather_ahead):
        pltpu.make_async_copy(gbuf.at[slot], gbuf.at[slot], sem.at[slot]).wait()
        if gather_ahead:
            for g in range(groups):
                for u in range(SUBLANES):
                    for k in range(TOP_K):
                        d = dst_next_ref[0, 0, (g * SUBLANES + u) * TOP_K + k]
                        pltpu.make_async_copy(y_hbm.at[d], gbuf.at[1 - slot, k, g, :, u],
                                              sem.at[1 - slot]).start(priority=k % 2)
        gates = gate_ref[...]
        acc = gates[:, 0:1] * _tile_rows(gbuf[slot, 0])
        for k in range(1, TOP_K):
            acc = acc + gates[:, k:k + 1] * _tile_rows(gbuf[slot, k])
        out = _layer_norm(DN_ALPHA * x_ref[...] + acc, g_ref[...], b_ref[...])
        o_ref[...] = out
        if n_proj:
            out_b = out.astype(BF16)
            for w_ref, p_ref in zip(w_refs, p_refs):
                p_ref[...] = jnp.dot(out_b, w_ref[...], preferred_element_type=F32)

    @pl.when(i + 1 < n)
    def _():
        tile(True)

    @pl.when(i + 1 >= n)
    def _():
        tile(False)


def combine_ln(y, dest, gates, x_res, g, b, w_next=(), tc=256):
    t, d = x_res.shape
    tc = min(tc, t)
    assert t % tc == 0
    n = t // tc
    n_sub = d // LANES
    dest_flat = dest[:, :TOP_K].reshape(n, 1, tc * TOP_K)
    idx_spec = lambda off: pl.BlockSpec((1, 1, tc * TOP_K), lambda i: (jnp.minimum(i + off, n - 1), 0, 0),
                                        memory_space=pltpu.SMEM)
    outs = pl.pallas_call(
        functools.partial(_combine_ln_kernel, n_proj=len(w_next)), grid=(n,),
        in_specs=[idx_spec(0), idx_spec(1), pl.BlockSpec(memory_space=pl.ANY),
                  pl.BlockSpec((tc, LANES), lambda i: (i, 0)),
                  pl.BlockSpec((tc, d), lambda i: (i, 0)), pl.BlockSpec((1, d), lambda i: (0, 0)),
                  pl.BlockSpec((1, d), lambda i: (0, 0))]
        + [pl.BlockSpec(w.shape, lambda i: (0, 0)) for w in w_next],
        out_specs=[pl.BlockSpec((tc, d), lambda i: (i, 0))]
        + [pl.BlockSpec((tc, w.shape[1]), lambda i: (i, 0)) for w in w_next],
        out_shape=[jax.ShapeDtypeStruct((t, d), F32)]
        + [jax.ShapeDtypeStruct((t, w.shape[1]), F32) for w in w_next],
        scratch_shapes=[pltpu.VMEM((2, TOP_K, tc // SUBLANES, n_sub, SUBLANES, LANES), F32),
                        pltpu.SemaphoreType.DMA((2,))],
        compiler_params=_cparams("arbitrary"), name="combine_ln")(
            dest_flat, dest_flat, y.reshape(-1, n_sub, LANES), gates, x_res, g, b, *w_next)
    return outs[0], tuple(outs[1:])


def _slot_runs_kernel(start_ref, len_ref, src_ref, o_ref, *, per_step, n_row):
    base = pl.program_id(0) * per_step
    lane = lax.broadcasted_iota(jnp.int32, (n_row, LANES), 1)
    pos = lax.broadcasted_iota(jnp.int32, (n_row, LANES), 0) * LANES + lane
    for j in range(per_step):
        start = start_ref[base + j]
        off = lax.bitwise_and(start, LANES - 1)
        rows = src_ref[pl.ds(lax.shift_right_logical(start, 7), SUBLANES), :]
        rolled = pltpu.roll(rows, lax.bitwise_and(LANES - off, LANES - 1), axis=1)
        run = jnp.where(lane + off < LANES, rolled[:n_row], rolled[1:n_row + 1])
        o_ref[j] = jnp.where(pos < len_ref[base + j], run, 0)


def slot_runs(src, starts, lens, width):
    n_runs = starts.shape[0]
    n_row = max(width // LANES, 1)
    assert n_row < SUBLANES
    per_step = next(d for d in (16, 12, 9, 8, 6, 4, 3, 2, 1) if n_runs % d == 0)
    rows = -(-src.shape[0] // LANES) + SUBLANES
    src2 = jnp.pad(src, (0, rows * LANES - src.shape[0])).reshape(rows, LANES)
    grid_spec = pltpu.PrefetchScalarGridSpec(
        num_scalar_prefetch=2, grid=(n_runs // per_step,),
        in_specs=[pl.BlockSpec((rows, LANES), lambda i, s, l: (0, 0))],
        out_specs=pl.BlockSpec((per_step, n_row, LANES), lambda i, s, l: (i, 0, 0)))
    out = pl.pallas_call(
        functools.partial(_slot_runs_kernel, per_step=per_step, n_row=n_row), grid_spec=grid_spec,
        out_shape=jax.ShapeDtypeStruct((n_runs, n_row, LANES), jnp.int32),
        compiler_params=_cparams("arbitrary"), name="slot_runs")(starts, lens, src2)
    return out.reshape(n_runs, n_row * LANES)[:, :width]


def moe_ln(x, x_tiles, idx, gates, w_in, b_in, w_out, b_out, g, b, bm, e_off=0, w_next=()):
    t, d = x.shape
    tk = t * TOP_K
    dest, pad_end, counts = route_slots(idx, bm)
    n_blocks = -(-tk // bm) + N_EXPERTS
    n_rows = n_blocks * bm
    block_start = jnp.arange(n_blocks, dtype=jnp.int32) * bm
    block_expert = jnp.sum((pad_end[None, :] <= block_start[:, None]).astype(jnp.int32), axis=1)
    block_expert = jnp.minimum(block_expert, N_EXPERTS - 1)
    tok = jnp.arange(tk, dtype=jnp.int32) // TOP_K
    _, tok_sorted = lax.sort_key_val(dest[:, :TOP_K].reshape(tk), tok)
    pad_start = jnp.concatenate([jnp.zeros((1,), jnp.int32), pad_end[:-1]])
    used_end = pad_start + counts
    padding_before = pad_start - (jnp.cumsum(counts) - counts)
    run_start = jnp.clip(block_start - padding_before[block_expert], 0, tk)
    run_len = jnp.clip(used_end[block_expert] - block_start, 0, bm)
    row_tok = slot_runs(tok_sorted, run_start, run_len, bm).reshape(n_blocks, 1, bm)
    block_expert = block_expert + e_off
    n_used = (pad_end[-1] // bm).astype(jnp.int32).reshape(1)
    y = moe_gmm(x_tiles, row_tok, block_expert, n_used, w_in, b_in, w_out, b_out, bm)
    return combine_ln(y, dest, gates, x, g, b, w_next)


def _t5_causal_bucket(dist):
    n = jnp.maximum(dist, 0)
    max_exact = N_BUCKETS // 2
    nf = jnp.maximum(n, max_exact).astype(F32)
    large = max_exact + (jnp.log(nf / max_exact) / math.log(MAX_DISTANCE / max_exact)
                         * (N_BUCKETS - max_exact)).astype(jnp.int32)
    large = jnp.minimum(large, N_BUCKETS - 1)
    return jnp.where(n < max_exact, n, large)


def _split_maps(q):
    lane = lax.broadcasted_iota(jnp.int32, q.shape, 1)
    qs = q * (DH_C ** -0.5)
    return (jnp.where(lane < DH_C, qs, 0.0).astype(BF16), jnp.where(lane >= DH_C, qs, 0.0).astype(BF16))


def _diff_finish(acc1, l1, acc2, l2, lam, out_scale, g):
    o = acc1 / l1 - lam * (acc2 / l2)
    o = o * lax.rsqrt(jnp.mean(o * o, axis=-1, keepdims=True) + NORM_EPS) * g
    return o * out_scale


def _diff_prompt_kernel(sc_ref, far_ref, q_ref, k_ref, v_ref, bias_ref, g_ref, o_ref, m_ref, a_ref, *, tq, hp):
    h0 = pl.program_id(1) * hp
    qi = pl.program_id(2)
    head = lambda x, h: x[:, h * VD_C:(h + 1) * VD_C]
    q = q_ref[...]
    qm = jnp.stack([jnp.concatenate(_split_maps(head(q, h)), axis=0) for h in range(hp)], axis=0)
    m_ref[...] = jnp.full_like(m_ref, -jnp.inf)
    a_ref[...] = jnp.zeros_like(a_ref)
    ones_col = (lax.broadcasted_iota(jnp.int32, (tq, LANES), 1) == 0).astype(BF16)
    reps = tq // LANES

    def update(tiles):
        scores, values = [], []
        for kidx, bias, shift, causal in tiles:
            start = pl.multiple_of(kidx * tq, tq)
            kt = k_ref[0, pl.ds(start, tq), :].astype(BF16)
            vt = v_ref[0, pl.ds(start, tq), :].astype(BF16)
            values.append(jnp.stack([jnp.concatenate([head(vt, h), ones_col], axis=-1) for h in range(hp)], axis=0))
            s = _bmm_nt(qm, jnp.stack([head(kt, h) for h in range(hp)], axis=0))
            if bias is not None:
                s = s + jnp.concatenate([bias, bias], axis=1)
            if causal:
                r = lax.broadcasted_iota(jnp.int32, s.shape[1:], 0)
                c = lax.broadcasted_iota(jnp.int32, s.shape[1:], 1)
                s = jnp.where(c <= lax.bitwise_and(r, tq - 1), s, -jnp.inf)
            scores.append((s, shift))
        m_old = m_ref[...]
        m_new = m_old
        for s, shift in scores:
            m_new = jnp.maximum(m_new, jnp.max(s, axis=-1, keepdims=True) + shift)
        alpha = jnp.exp(m_old - m_new)
        acc = jnp.concatenate([alpha, alpha], axis=-1) * a_ref[...]
        for (s, shift), vt in zip(scores, values):
            p = jnp.exp(s - jnp.concatenate([m_new - shift] * reps, axis=-1))
            acc = acc + _bmm(p, vt)
        a_ref[...] = acc
        m_ref[...] = m_new

    far_bias = jnp.stack([jnp.full((1, 1), far_ref[h0 + h], F32) for h in range(hp)], axis=0)
    no_shift = jnp.zeros((hp, 1, 1), F32)
    n_far = jnp.maximum(qi - 1, 0)

    def far_body(j, carry):
        update([(2 * j, None, far_bias, False), (2 * j + 1, None, far_bias, False)])
        return carry

    lax.fori_loop(0, n_far // 2, far_body, 0)

    @pl.when(lax.rem(n_far, 2) == 1)
    def _():
        update([(n_far - 1, None, far_bias, False)])

    @pl.when(qi >= 1)
    def _():
        update([(qi - 1, bias_ref[:, 1], no_shift, False), (qi, bias_ref[:, 0], no_shift, True)])

    @pl.when(qi == 0)
    def _():
        update([(0, bias_ref[:, 0], no_shift, True)])

    acc = a_ref[...]
    o_ref[...] = jnp.concatenate(
        [_diff_finish(acc[h, :tq, :VD_C], acc[h, :tq, VD_C:VD_C + 1], acc[h, tq:, :VD_C], acc[h, tq:, VD_C:VD_C + 1],
                      sc_ref[0], sc_ref[1], g_ref[...]) for h in range(hp)], axis=-1)


def _bias_tiles_kernel(rb_ref, bucket_ref, o_ref):
    h = pl.program_id(0)
    b = bucket_ref[...]
    out = jnp.zeros(b.shape, F32)
    for k in range(N_BUCKETS):
        out = jnp.where(b == k, rb_ref[k * H_C + h], out)
    o_ref[0] = out


DIFF_HEADS_PER_STEP = 2


def diff_attn_prompt_core(q, k, v, bsz, rel_bias, lam, lam_init, subln_g, tq=256):
    m = q.shape[0]
    t = m // bsz
    nq = t // tq
    ii = jnp.arange(tq)[:, None]
    jj = jnp.arange(tq)[None, :]
    buckets = _t5_causal_bucket(jnp.stack([jnp.maximum(ii - jj, 0), tq + ii - jj])).astype(jnp.int32)
    bias = pl.pallas_call(
        _bias_tiles_kernel, grid=(H_C,),
        in_specs=[pl.BlockSpec(memory_space=pltpu.SMEM), pl.BlockSpec((2, tq, tq), lambda h: (0, 0, 0))],
        out_specs=pl.BlockSpec((1, 2, tq, tq), lambda h: (h, 0, 0, 0)),
        out_shape=jax.ShapeDtypeStruct((H_C, 2, tq, tq), F32),
        compiler_params=_cparams("parallel"), name="t5_bias_tiles")(rel_bias.astype(F32).reshape(-1), buckets)
    far = rel_bias[_t5_causal_bucket(jnp.array(2 * tq))].astype(F32)
    scal = jnp.stack([lam, 1.0 - lam_init]).astype(F32)
    k3 = k.reshape(bsz, t, C_W)
    v3 = v.reshape(bsz, t, C_W)
    smem = pl.BlockSpec(memory_space=pltpu.SMEM)
    hp = DIFF_HEADS_PER_STEP
    kv_spec = pl.BlockSpec((1, t, hp * VD_C), lambda b, h, i: (b, 0, h))
    row = pl.BlockSpec((tq, hp * VD_C), lambda b, h, i: (b * nq + i, h))
    stat = pltpu.VMEM((hp, 2 * tq, LANES), F32)
    acc = pltpu.VMEM((hp, 2 * tq, VD_C + LANES), F32)
    return pl.pallas_call(
        functools.partial(_diff_prompt_kernel, tq=tq, hp=hp), grid=(bsz, H_C // hp, nq),
        in_specs=[smem, smem, row, kv_spec, kv_spec,
                  pl.BlockSpec((hp, 2, tq, tq), lambda b, h, i: (h, 0, 0, 0)),
                  pl.BlockSpec((1, VD_C), lambda b, h, i: (0, 0))],
        out_specs=row, out_shape=jax.ShapeDtypeStruct((m, C_W), F32),
        scratch_shapes=[stat, acc],
        compiler_params=_cparams("parallel", "parallel", "parallel"), name="diff_attn_prompt")(
            scal, far, q, k3, v3, bias, subln_g.reshape(1, VD_C))


def _diff_sample_kernel(pt_ref, sc_ref, q_ref, kn_ref, vn_ref, bm_ref, bnew_ref, g_ref, *refs, n_pp):
    k_refs, v_refs = refs[:n_pp], refs[n_pp:2 * n_pp]
    o_ref, m_ref, l_ref, a_ref = refs[2 * n_pp:]
    p = pl.program_id(1)
    n_p = pl.num_programs(1)
    q8 = q_ref[0]
    qm = jnp.concatenate(_split_maps(q8), axis=0)

    @pl.when(p == 0)
    def _():
        m_ref[...] = jnp.full_like(m_ref, -jnp.inf)
        l_ref[...] = jnp.zeros_like(l_ref)
        a_ref[...] = jnp.zeros_like(a_ref)

    scores = []
    for j in range(n_pp):
        bias = bm_ref[jnp.where(p == n_p - 1, 1, 0)] if j == n_pp - 1 else bm_ref[0]
        k2 = k_refs[j][0, 0].reshape(-1, 2 * DH_C).astype(BF16)
        scores.append(lax.dot_general(qm, k2, (((1,), (1,)), ((), ())), preferred_element_type=F32) + bias)
    m_old = m_ref[...]
    m_new = m_old
    for s in scores:
        m_new = jnp.maximum(m_new, jnp.max(s, axis=-1, keepdims=True))
    alpha = jnp.exp(m_old - m_new)
    l_new = alpha * l_ref[...]
    a_new = alpha * a_ref[...]
    for j, s in enumerate(scores):
        pr = jnp.exp(s - m_new)
        l_new = l_new + jnp.sum(pr, axis=-1, keepdims=True)
        v2 = v_refs[j][0, 0].reshape(-1, VD_C).astype(BF16)
        a_new = a_new + jnp.dot(pr.astype(BF16), v2, preferred_element_type=F32)
    l_ref[...] = l_new
    a_ref[...] = a_new
    m_ref[...] = m_new

    @pl.when(p == n_p - 1)
    def _():
        kn = jnp.concatenate([kn_ref[0], kn_ref[0]], axis=0)
        vn = jnp.concatenate([vn_ref[0], vn_ref[0]], axis=0)
        s = jnp.sum(qm.astype(F32) * kn, axis=-1, keepdims=True) + bnew_ref[:, 0:1]
        m_old = m_ref[...]
        m_new = jnp.maximum(m_old, s)
        alpha = jnp.exp(m_old - m_new)
        pr = jnp.exp(s - m_new)
        l = alpha * l_ref[...] + pr
        a = alpha * a_ref[...] + pr * vn
        o_ref[0] = _diff_finish(a[:H_C], l[:H_C], a[H_C:], l[H_C:], sc_ref[0], sc_ref[1], g_ref[...])


def diff_attn_sample_core(q, k_new, v_new, cache_k, cache_v, page_table, layer_j, rel_bias, lam, lam_init, subln_g,
                          n_pp=16):
    bsz = q.shape[0]
    n_pages = page_table.shape[1]
    page = cache_k.shape[2]
    past = n_pages * page
    assert n_pages % n_pp == 0
    dist_last = past - ((n_pages - 1) * page + jnp.arange(page))
    b_last = rel_bias[_t5_causal_bucket(dist_last)].astype(F32)
    b_far = jnp.broadcast_to(rel_bias[_t5_causal_bucket(jnp.array(page + 1))].astype(F32), (page, H_C))
    eye = jnp.eye(H_C, dtype=bool)

    def expand(bt):
        full = jnp.where(eye[:, None, :], bt.T[:, :, None], -jnp.inf).reshape(H_C, page * H_C)
        return jnp.concatenate([full, full], axis=0)

    bm = jnp.stack([expand(b_far), expand(b_last)])
    b_new = rel_bias[_t5_causal_bucket(jnp.array(0))].astype(F32)
    b_new = jnp.broadcast_to(jnp.concatenate([b_new, b_new])[:, None], (2 * H_C, LANES))
    scal = jnp.stack([lam, 1.0 - lam_init]).astype(F32)
    smem = pl.BlockSpec(memory_space=pltpu.SMEM)
    head3 = pl.BlockSpec((1, H_C, VD_C), lambda b, p, pt: (b, 0, 0))
    full = lambda shape: pl.BlockSpec(shape, lambda b, p, pt: (0,) * len(shape))

    def page_spec(j):
        return pl.BlockSpec((1, 1, page, H_C, VD_C), lambda b, p, pt: (pt[b, p * n_pp + j], layer_j, 0, 0, 0))

    grid_spec = pltpu.PrefetchScalarGridSpec(
        num_scalar_prefetch=1, grid=(bsz, n_pages // n_pp),
        in_specs=[smem, head3, head3, head3, full(bm.shape), full(b_new.shape), full((1, VD_C))]
        + [page_spec(j) for j in range(n_pp)] * 2,
        out_specs=head3,
        scratch_shapes=[pltpu.VMEM((2 * H_C, 1), F32), pltpu.VMEM((2 * H_C, 1), F32),
                        pltpu.VMEM((2 * H_C, VD_C), F32)])
    out = pl.pallas_call(
        functools.partial(_diff_sample_kernel, n_pp=n_pp), grid_spec=grid_spec,
        out_shape=jax.ShapeDtypeStruct((bsz, H_C, VD_C), F32),
        compiler_params=_cparams("parallel", "arbitrary"), name="diff_attn_sample")(
            page_table, scal, q.reshape(bsz, H_C, VD_C), k_new.reshape(bsz, H_C, VD_C),
            v_new.reshape(bsz, H_C, VD_C), bm, b_new, subln_g.reshape(1, VD_C),
            *([cache_k] * n_pp), *([cache_v] * n_pp))
    return out.reshape(bsz, C_W)


LANE_BETA, LANE_A, LANE_I, LANE_F = 0, H_A, 2 * H_A, 2 * H_A + H_B


def _softplus(x):
    return jnp.maximum(x, 0.0) + jnp.log1p(jnp.exp(-jnp.abs(x)))


def _silu(x):
    return x * jax.nn.sigmoid(x)


def _lanes(shape, lo, n):
    lane = lax.broadcasted_iota(jnp.int32, shape, 1)
    return jnp.logical_and(lane >= lo, lane < lo + n)


def _gate_tile(gt, alog_row, prow):
    z = gt + prow
    return jax.nn.sigmoid(gt), -jnp.exp(alog_row) * _softplus(z), z, -_softplus(-z)


def _nt(a, b, precision=None):
    return lax.dot_general(a, b, (((1,), (1,)), ((), ())), preferred_element_type=F32, precision=precision)


def _tn(a, b):
    return lax.dot_general(a, b, (((0,), (0,)), ((), ())), preferred_element_type=F32)


def _row_selector(n_heads, length, lanes_of_head):
    r = lax.broadcasted_iota(jnp.int32, (n_heads * length, LANES), 0) // length
    lane = lax.broadcasted_iota(jnp.int32, (n_heads * length, LANES), 1)
    sel = jnp.zeros((n_heads * length, LANES), F32)
    for lo in lanes_of_head:
        sel = sel + (lane == r + lo).astype(F32)
    return sel


def _tri_masks(n):
    r = lax.broadcasted_iota(jnp.int32, (n, n), 0)
    c = lax.broadcasted_iota(jnp.int32, (n, n), 1)
    return r, c


def _bmm(a, b):
    return lax.dot_general(a.astype(BF16), b.astype(BF16), (((2,), (1,)), ((0,), (0,))),
                           preferred_element_type=F32)


def _bmm_nt(a, b):
    return lax.dot_general(a.astype(BF16), b.astype(BF16), (((2,), (2,)), ((0,), (0,))),
                           preferred_element_type=F32)


def _heads(x, n_heads, width, offset=0):
    return jnp.stack([x[c * CHUNK:(c + 1) * CHUNK, offset + h * width:offset + (h + 1) * width]
                      for c in range(x.shape[0] // CHUNK) for h in range(n_heads)], axis=0)


def _head_cols(x, n_heads, lane0):
    return jnp.stack([x[c * CHUNK:(c + 1) * CHUNK, lane0 + h:lane0 + h + 1]
                      for c in range(x.shape[0] // CHUNK) for h in range(n_heads)], axis=0)


def _unheads(x, n_heads):
    n = x.shape[0] // n_heads
    return jnp.concatenate([jnp.concatenate([x[c * n_heads + h] for h in range(n_heads)], axis=-1)
                            for c in range(n)], axis=0)


def _chunk_cumsum(x):
    r, c = _tri_masks(x.shape[0])
    tri = jnp.logical_and(c <= r, r // CHUNK == c // CHUNK).astype(F32)
    return jnp.dot(tri, x, preferred_element_type=F32, precision=HI)


def _chunk_rows(sel, x):
    n = x.shape[0] // CHUNK
    rows = [_nt(sel, x[c * CHUNK:(c + 1) * CHUNK], HI) for c in range(n)]
    return jnp.concatenate(rows, axis=0).reshape(n * sel.shape[0] // CHUNK, CHUNK, CHUNK)


def _unit_lower_inverse(nmat, r, c):
    mm = _bmm
    eye = (r == c).astype(F32)
    same = (r // 16) == (c // 16)
    nd = jnp.where(same, nmat, 0.0)
    off = nmat - nd
    dinv = eye - nd
    p = nd
    for _ in range(3):
        p = mm(p, p)
        dinv = dinv + mm(dinv, p)
    m = mm(dinv, off)
    m2 = mm(m, m)
    left = eye - m
    left = left + mm(left, m2)
    return mm(left, dinv)


def _gdn_prompt_kernel(qkv_ref, z_ref, gt_ref, convw_ref, alog_ref, prow_ref, ng_ref, mix_ref, s_ref, ext_ref):
    cidx = pl.program_id(1)
    L = CHUNK
    rows = qkv_ref.shape[0]
    n_ch = rows // L

    @pl.when(cidx == 0)
    def _():
        ext_ref[0:8, :] = jnp.zeros((8, QKV_A), F32)
        s_ref[...] = jnp.zeros_like(s_ref)

    ext_ref[8:8 + rows, :] = qkv_ref[...]
    acc = convw_ref[0:1, :] * ext_ref[pl.ds(8 - (CONV_W - 1), rows), :]
    for i in range(1, CONV_W):
        acc = acc + convw_ref[i:i + 1, :] * ext_ref[pl.ds(8 - (CONV_W - 1) + i, rows), :]
    ext_ref[0:8, :] = ext_ref[rows:rows + 8, :]
    cs = _silu(acc)

    beta, g, _, _ = _gate_tile(gt_ref[...], alog_ref[...], prow_ref[...])
    r, c = _tri_masks(L)
    incl = c <= r
    strict = c < r
    gcum = _chunk_cumsum(jnp.where(_lanes(g.shape, LANE_A, H_A), g, 0.0))
    gc_rows = _chunk_rows(_row_selector(H_A, L, (LANE_A,)), gcum)
    q = _heads(cs, H_A, DK_A)
    k = _heads(cs, H_A, DK_A, H_A * DK_A)
    v = _heads(cs, H_A, DV_A, 2 * H_A * DK_A)
    q = q * lax.rsqrt(jnp.sum(q * q, axis=-1, keepdims=True) + NORM_EPS) * (DK_A ** -0.5)
    k = k * lax.rsqrt(jnp.sum(k * k, axis=-1, keepdims=True) + NORM_EPS)
    beta_c = _head_cols(beta, H_A, LANE_BETA)
    gc_c = _head_cols(gcum, H_A, LANE_A)
    decay = jnp.where(incl, jnp.exp(jnp.where(incl, gc_c - gc_rows, 0.0)), 0.0)
    kb = k.astype(BF16)
    nmat = jnp.where(strict, beta_c * _bmm_nt(kb, kb) * decay, 0.0)
    egc = jnp.exp(gc_c)
    rhs = jnp.concatenate([v * beta_c, k * (beta_c * egc)], axis=-1)
    sol = _bmm(_unit_lower_inverse(nmat, r, c), rhs)
    u = sol[:, :, :DV_A]
    wq = jnp.concatenate([sol[:, :, DV_A:], q * egc], axis=1).astype(BF16)
    qk = (_bmm_nt(q, kb) * decay).astype(BF16)
    gc_last = gc_c[:, L - 1:L]
    k_tail = (k * jnp.exp(gc_last - gc_c)).astype(BF16)
    g_tail = jnp.exp(gc_last)
    state = s_ref[0]
    outs = []
    for ci in range(n_ch):
        sl = slice(ci * H_A, (ci + 1) * H_A)
        ws = _bmm(wq[sl], state)
        delta = u[sl] - ws[:, :L]
        outs.append(ws[:, L:] + _bmm(qk[sl], delta))
        delta_b = delta.astype(BF16)
        state = state * g_tail[sl] + jnp.stack([_tn(k_tail[ci * H_A + h], delta_b[h]) for h in range(H_A)], axis=0)
    s_ref[0] = state
    o = jnp.concatenate(outs, axis=0)
    o = o * lax.rsqrt(jnp.mean(o * o, axis=-1, keepdims=True) + NORM_EPS) * ng_ref[...]
    o = o * _silu(_heads(z_ref[...], H_A, DV_A))
    mix_ref[...] = _unheads(o, H_A)


AB_CHUNKS_PER_STEP = 4


def gdn_prompt(qkv, z, gates, conv_w, alog_row, prow, norm_g, bsz):
    m = qkv.shape[0]
    rows = AB_CHUNKS_PER_STEP * CHUNK
    nc = m // bsz // rows
    row = lambda w: pl.BlockSpec((rows, w), lambda b, c: (b * nc + c, 0))
    full = lambda shape: pl.BlockSpec(shape, lambda b, c: (0,) * len(shape))
    return pl.pallas_call(
        _gdn_prompt_kernel, grid=(bsz, nc),
        in_specs=[row(QKV_A), row(H_A * DV_A), row(LANES), full(conv_w.shape), full((1, LANES)), full((1, LANES)),
                  full((1, DV_A))],
        out_specs=[row(H_A * DV_A), pl.BlockSpec((1, H_A, DK_A, DV_A), lambda b, c: (b, 0, 0, 0))],
        out_shape=[jax.ShapeDtypeStruct((m, H_A * DV_A), F32), jax.ShapeDtypeStruct((bsz, H_A, DK_A, DV_A), F32)],
        scratch_shapes=[pltpu.VMEM((rows + 8, QKV_A), F32)],
        compiler_params=_cparams("parallel", "arbitrary"), name="gdn_prompt")(
            qkv, z, gates, conv_w, alog_row, prow, norm_g.reshape(1, DV_A))


def _mlstm_prompt_kernel(q_ref, k_ref, v_ref, og_ref, gt_ref, alog_ref, prow_ref, ng_ref, mix_ref, c_ref, m_ref):
    cidx = pl.program_id(1)
    L = CHUNK

    @pl.when(cidx == 0)
    def _():
        c_ref[...] = jnp.zeros_like(c_ref)
        m_ref[...] = jnp.zeros_like(m_ref)

    n_ch = q_ref.shape[0] // L
    _, _, ipre, logf = _gate_tile(gt_ref[...], alog_ref[...], prow_ref[...])
    r, c = _tri_masks(L)
    incl = c <= r
    fsel = _lanes(logf.shape, LANE_F, H_B)
    bcum = _chunk_cumsum(jnp.where(fsel, logf, 0.0))
    rowvals = jnp.where(_lanes(ipre.shape, LANE_I, H_B), ipre, 0.0) - jnp.where(fsel, bcum, 0.0)
    rows = _chunk_rows(_row_selector(H_B, L, (LANE_I, LANE_F)), rowvals)
    mrow = m_ref[0]
    lane_row = lax.broadcasted_iota(jnp.int32, mrow.shape, 1)
    qb = _heads(q_ref[...], H_B, DQK_B).astype(BF16)
    ks = _heads(k_ref[...], H_B, DQK_B) * (DQK_B ** -0.5)
    ones_col = jnp.broadcast_to((lax.broadcasted_iota(jnp.int32, (L, LANES), 1) == 0).astype(F32),
                                (n_ch * H_B, L, LANES))
    v_ext = jnp.concatenate([_heads(v_ref[...], H_B, DV_B), ones_col], axis=-1).astype(BF16)
    b_c = _head_cols(bcum, H_B, LANE_F)
    i_c = _head_cols(ipre, H_B, LANE_I)
    dmat = jnp.where(incl, b_c + rows, -jnp.inf)
    m_intra = jnp.max(dmat, axis=-1, keepdims=True)
    w_intra = jnp.exp(dmat - m_intra) * _bmm_nt(qb, ks)
    nd_intra = _bmm(w_intra, v_ext)
    b_last = b_c[:, L - 1:L]
    e_end = b_last - b_c + i_c
    e_max = jnp.max(e_end, axis=1, keepdims=True)
    kw = (ks * jnp.exp(e_end - e_max)).astype(BF16)
    kv_end = jnp.stack([_tn(kw[i], v_ext[i]) for i in range(n_ch * H_B)], axis=0)
    m_prev = jnp.stack([mrow[:, h:h + 1] for h in range(H_B)], axis=0)
    state = c_ref[0]
    outs = []
    for ci in range(n_ch):
        sl = slice(ci * H_B, (ci + 1) * H_B)
        inter = b_c[sl] + m_prev
        m_t = jnp.maximum(inter, m_intra[sl])
        nd = jnp.exp(inter - m_t) * _bmm(qb[sl], state) + jnp.exp(m_intra[sl] - m_t) * nd_intra[sl]
        outs.append(nd[:, :, :DV_B] / jnp.maximum(jnp.abs(nd[:, :, DV_B:DV_B + 1]), jnp.exp(-m_t)))
        m_new = jnp.maximum(b_last[sl] + m_prev, e_max[sl])
        state = jnp.exp(b_last[sl] + m_prev - m_new) * state + jnp.exp(e_max[sl] - m_new) * kv_end[sl]
        m_prev = m_new
    c_ref[0] = state
    for h in range(H_B):
        mrow = jnp.where(lane_row == h, m_prev[h], mrow)
    m_ref[0] = mrow
    hh = jnp.concatenate(outs, axis=0)
    hh = hh * lax.rsqrt(jnp.mean(hh * hh, axis=-1, keepdims=True) + NORM_EPS) * ng_ref[...]
    hh = jax.nn.sigmoid(_heads(og_ref[...], H_B, DV_B)) * hh
    mix_ref[...] = _unheads(hh, H_B)


def mlstm_prompt(q, k, v, og, gates, alog_row, prow, norm_g, bsz):
    m = q.shape[0]
    rows = AB_CHUNKS_PER_STEP * CHUNK
    nc = m // bsz // rows
    row = lambda w: pl.BlockSpec((rows, w), lambda b, c: (b * nc + c, 0))
    full = lambda shape: pl.BlockSpec(shape, lambda b, c: (0,) * len(shape))
    return pl.pallas_call(
        _mlstm_prompt_kernel, grid=(bsz, nc),
        in_specs=[row(H_B * DQK_B), row(H_B * DQK_B), row(H_B * DV_B), row(H_B * DV_B), row(LANES),
                  full((1, LANES)), full((1, LANES)), full((1, DV_B))],
        out_specs=[row(H_B * DV_B), pl.BlockSpec((1, H_B, DQK_B, DV_B + LANES), lambda b, c: (b, 0, 0, 0)),
                   pl.BlockSpec((1, 1, LANES), lambda b, c: (b, 0, 0))],
        out_shape=[jax.ShapeDtypeStruct((m, H_B * DV_B), F32),
                   jax.ShapeDtypeStruct((bsz, H_B, DQK_B, DV_B + LANES), F32),
                   jax.ShapeDtypeStruct((bsz, 1, LANES), F32)],
        compiler_params=_cparams("parallel", "arbitrary"), name="mlstm_prompt")(
            q, k, v, og, gates, alog_row, prow, norm_g.reshape(1, DV_B))


def _columns(x8):
    n = x8.shape[1]
    r, c = _tri_masks(n)
    return _nt((r == c).astype(F32), x8, HI)


def _ab_sample_kernel(qn_ref, kn_ref, vn_ref, cq_ref, ck_ref, cv_ref, wq_ref, wk_ref, wv_ref, z_ref,
                      qb_ref, kb_ref, vb_ref, og_ref, gt_ref, alog_ref, prow_ref, nga_ref, ngb_ref,
                      s_in, c_in, n_in, m_in,
                      oa_ref, ob_ref, s_out, c_out, n_out, m_out):
    def conv(new_ref, prev_ref, w_ref):
        acc = w_ref[CONV_W - 1] * new_ref[0]
        for i in range(CONV_W - 1):
            acc = acc + w_ref[i] * prev_ref[0, i]
        return _silu(acc)

    q8 = conv(qn_ref, cq_ref, wq_ref)
    k8 = conv(kn_ref, ck_ref, wk_ref)
    v8 = conv(vn_ref, cv_ref, wv_ref)
    q8 = q8 * lax.rsqrt(jnp.sum(q8 * q8, axis=-1, keepdims=True) + NORM_EPS) * (DK_A ** -0.5)
    k8 = k8 * lax.rsqrt(jnp.sum(k8 * k8, axis=-1, keepdims=True) + NORM_EPS)
    beta, g, ipre, logf = _gate_tile(gt_ref[0], alog_ref[...], prow_ref[...])
    q_cols, k_cols = _columns(q8), _columns(k8)
    z8 = z_ref[0]
    outs = []
    for h in range(H_A):
        s = s_in[0, h] * jnp.exp(g[:, LANE_A + h:LANE_A + h + 1])
        kc = k_cols[:, h:h + 1]
        err = v8[h:h + 1] - jnp.sum(kc * s, axis=0, keepdims=True)
        s = s + kc * (beta[:, LANE_BETA + h:LANE_BETA + h + 1] * err)
        s_out[0, h] = s
        outs.append(jnp.sum(q_cols[:, h:h + 1] * s, axis=0, keepdims=True))
    o = jnp.concatenate(outs, axis=0)
    o = o * lax.rsqrt(jnp.mean(o * o, axis=-1, keepdims=True) + NORM_EPS) * nga_ref[...]
    oa_ref[0] = o * _silu(z8)

    zeros4 = jnp.zeros((8 - H_B, DQK_B), F32)
    qb_cols = _columns(jnp.concatenate([qb_ref[0], zeros4], axis=0))
    kb_cols = _columns(jnp.concatenate([kb_ref[0] * (DQK_B ** -0.5), zeros4], axis=0))
    vb = vb_ref[0]
    n_cols = n_in[0]
    m_row = m_in[0]
    lane_n = lax.broadcasted_iota(jnp.int32, n_cols.shape, 1)
    lane_m = lax.broadcasted_iota(jnp.int32, m_row.shape, 1)
    outs = []
    for h in range(H_B):
        lf = logf[:, LANE_F + h:LANE_F + h + 1]
        it = ipre[:, LANE_I + h:LANE_I + h + 1]
        m_prev = m_row[:, h:h + 1]
        m_new = jnp.maximum(lf + m_prev, it)
        f_sc = jnp.exp(lf + m_prev - m_new)
        i_sc = jnp.exp(it - m_new)
        kc = kb_cols[:, h:h + 1]
        qc = qb_cols[:, h:h + 1]
        cm = f_sc * c_in[0, h] + i_sc * (kc * vb[h:h + 1])
        nn = f_sc * n_cols[:, h:h + 1] + i_sc * kc
        c_out[0, h] = cm
        n_cols = jnp.where(lane_n == h, nn, n_cols)
        m_row = jnp.where(lane_m == h, m_new, m_row)
        num = jnp.sum(qc * cm, axis=0, keepdims=True)
        den = jnp.sum(qc * nn, axis=0, keepdims=True)
        outs.append(num / jnp.maximum(jnp.abs(den), jnp.exp(-m_new)))
    hb = jnp.concatenate(outs, axis=0)
    hb = hb * lax.rsqrt(jnp.mean(hb * hb, axis=-1, keepdims=True) + NORM_EPS) * ngb_ref[...]
    ob_ref[0] = jax.nn.sigmoid(og_ref[0]) * hb
    n_out[0] = n_cols
    m_out[0] = m_row


def ab_sample(qkv, z, q_b, k_b, v_b, o_b, gates, conv_prev, conv_w, alog_row, prow, norm_g_a, norm_g_b,
              s_prev, c_prev, n_prev, m_prev):
    bsz = qkv.shape[0]
    hk = H_A * DK_A
    part = lambda x, i, w: x[..., i * hk:i * hk + H_A * w].reshape(x.shape[:-1] + (H_A, w))
    new_parts = [part(qkv, 0, DK_A), part(qkv, 1, DK_A), part(qkv, 2, DV_A)]
    prev_parts = [part(conv_prev, 0, DK_A), part(conv_prev, 1, DK_A), part(conv_prev, 2, DV_A)]
    w_parts = [part(conv_w, 0, DK_A), part(conv_w, 1, DK_A), part(conv_w, 2, DV_A)]
    args = new_parts + prev_parts + w_parts + [
        z.reshape(bsz, H_A, DV_A), q_b.reshape(bsz, H_B, DQK_B), k_b.reshape(bsz, H_B, DQK_B),
        v_b.reshape(bsz, H_B, DV_B), o_b.reshape(bsz, H_B, DV_B), gates.reshape(bsz, 1, LANES),
        alog_row, prow, norm_g_a.reshape(1, DV_A), norm_g_b.reshape(1, DV_B),
        s_prev, c_prev, jnp.swapaxes(n_prev, 1, 2), m_prev.reshape(bsz, 1, H_B)]

    def spec(x, batched):
        nd = x.ndim
        if batched:
            return pl.BlockSpec((1,) + x.shape[1:], lambda b: (b,) + (0,) * (nd - 1))
        return pl.BlockSpec(x.shape, lambda b: (0,) * nd)

    batched = [True] * 6 + [False] * 3 + [True] * 6 + [False] * 4 + [True] * 4
    out_shape = [jax.ShapeDtypeStruct((bsz, H_A, DV_A), F32), jax.ShapeDtypeStruct((bsz, H_B, DV_B), F32),
                 jax.ShapeDtypeStruct(s_prev.shape, F32), jax.ShapeDtypeStruct(c_prev.shape, F32),
                 jax.ShapeDtypeStruct((bsz, DQK_B, H_B), F32), jax.ShapeDtypeStruct((bsz, 1, H_B), F32)]
    oa, ob, s_new, c_new, n_new, m_new = pl.pallas_call(
        _ab_sample_kernel, grid=(bsz,),
        in_specs=[spec(x, bt) for x, bt in zip(args, batched)],
        out_specs=[spec(x, True) for x in out_shape], out_shape=out_shape,
        compiler_params=_cparams("parallel"), name="ab_sample")(*args)
    mix = jnp.concatenate([oa.reshape(bsz, H_A * DV_A), ob.reshape(bsz, H_B * DV_B)], axis=-1)
    return mix, s_new, c_new, jnp.swapaxes(n_new, 1, 2), m_new.reshape(bsz, H_B)


MOE_BLOCK_PROMPT = 256
MOE_BLOCK_SAMPLE = 32


def _diff_qkv_weights(w_qkv):
    return tuple(w_qkv[:, i * C_W:(i + 1) * C_W].astype(BF16) for i in range(3))


def _ab_weights(w_in, a_log, dt_bias, b_i, b_f):
    sizes = (QKV_A, H_A * DV_A, H_A, H_A, H_B * DQK_B, H_B * DQK_B, H_B * DV_B, H_B * DV_B, H_B, H_B)
    offs = [0]
    for s in sizes:
        offs.append(offs[-1] + s)
    col = lambda i: w_in[:, offs[i]:offs[i + 1]]
    w_lo = tuple(col(i).astype(BF16) for i in (0, 1, 4, 5, 6, 7))
    w_gate = jnp.concatenate([col(2), col(3), col(8), col(9)], axis=1)
    w_gate = jnp.pad(w_gate, ((0, 0), (0, LANES - w_gate.shape[1])))
    zeros = lambda n: jnp.zeros((n,), F32)
    pad = LANES - 2 * H_A - 2 * H_B
    alog_row = jnp.concatenate([zeros(H_A), a_log.astype(F32), zeros(2 * H_B + pad)]).reshape(1, LANES)
    prow = jnp.concatenate([zeros(H_A), dt_bias.astype(F32), b_i.astype(F32), b_f.astype(F32),
                            zeros(pad)]).reshape(1, LANES)
    return w_lo, w_gate, alog_row, prow


def kernel(x_prompt, x_sample, state_delta_S, state_delta_conv, state_mlstm_C, state_mlstm_n, state_mlstm_m,
           cache_diff_k, cache_diff_v, cache_mem_k, cache_mem_v, page_table, mem_prompt,
           w_in_ab, conv_w_a, a_log_a, dt_bias_a, norm_g_a, b_i_b, b_f_b, norm_g_b, w_out_ab,
           w_qkv_c, lam_q1, lam_k1, lam_q2, lam_k2, subln_g_c, w_o_c, rel_bias,
           w_xq, w_xkv, w_xo, ln_g, ln_b, w_router, b_router, w_moe_in, b_moe_in, w_moe_out, b_moe_out):
    bp, t, d = x_prompt.shape
    bs = x_sample.shape[0]
    xp = x_prompt.reshape(bp * t, d)
    xs = x_sample.reshape(bs, d)
    mem2d = mem_prompt.reshape(bp * N_MEM, d)
    cmk = cache_mem_k.reshape(DEPTH * bs, N_MEM * H_X, DH_X)
    cmv = cache_mem_v.reshape(DEPTH * bs, N_MEM * H_X, DH_X)
    wm_in = w_moe_in.reshape(DEPTH * N_EXPERTS, d, 2 * D_FF)
    bm_in = b_moe_in.reshape(DEPTH * N_EXPERTS, 2 * D_FF)
    wm_out = w_moe_out.reshape(DEPTH * N_EXPERTS, D_FF, d)
    bm_out = b_moe_out.reshape(DEPTH * N_EXPERTS, d)
    p_S, p_conv, p_C, p_n, p_m, p_k, p_v, p_mk, p_mv = [], [], [], [], [], [], [], [], []
    s_S, s_conv, s_C, s_n, s_m, s_k, s_v = [], [], [], [], [], [], []
    prompt_proj = ()
    for layer in range(DEPTH):
        j = layer // 2
        g0, b0 = ln_g[layer, 0].reshape(1, d), ln_b[layer, 0].reshape(1, d)
        g1, b1 = ln_g[layer, 1].reshape(1, d), ln_b[layer, 1].reshape(1, d)
        g2, b2 = ln_g[layer, 2].reshape(1, d), ln_b[layer, 2].reshape(1, d)
        if layer % 2 == 0:
            w_lo, w_gate, alog_row, prow = _ab_weights(w_in_ab[j], a_log_a[j], dt_bias_a[j], b_i_b[j], b_f_b[j])
            w_out = w_out_ab[j].astype(BF16)
            w_out_a, w_out_b = w_out[:H_A * DV_A], w_out[H_A * DV_A:]
            qkv, z, q_b, k_b, v_b, o_b, gates = mm_multi(xp, w_lo, (w_gate,), tm=256)
            mix_a, st_S = gdn_prompt(qkv, z, gates, conv_w_a[j], alog_row, prow, norm_g_a[j], bp)
            mix_b, c_ext, m_row = mlstm_prompt(q_b, k_b, v_b, o_b, gates, alog_row, prow, norm_g_b[j], bp)
            xp = proj_ln([mix_a, mix_b], [w_out_a, w_out_b], xp, g0, b0)
            p_S.append(st_S)
            p_conv.append(qkv.reshape(bp, t, QKV_A)[:, t - (CONV_W - 1):])
            p_C.append(c_ext[..., :DV_B])
            p_n.append(c_ext[..., DV_B])
            p_m.append(m_row[:, 0, :H_B])
            qkv, z, q_b, k_b, v_b, o_b, gates = mm_multi(xs, w_lo, (w_gate,))
            mix, st_S, st_C, st_n, st_m = ab_sample(
                qkv, z, q_b, k_b, v_b, o_b, gates, state_delta_conv[j], conv_w_a[j], alog_row, prow,
                norm_g_a[j], norm_g_b[j], state_delta_S[j].astype(F32), state_mlstm_C[j].astype(F32),
                state_mlstm_n[j].astype(F32), state_mlstm_m[j].astype(F32))
            xs = proj_ln([mix], [w_out], xs, g0, b0)
            s_S.append(st_S)
            s_conv.append(jnp.concatenate([state_delta_conv[j][:, 1:].astype(F32), qkv[:, None, :]], axis=1))
            s_C.append(st_C)
            s_n.append(st_n)
            s_m.append(st_m)
        else:
            lam_init = 0.8 - 0.6 * math.exp(-0.3 * layer)
            lam = (jnp.exp(jnp.sum(lam_q1[j].astype(F32) * lam_k1[j].astype(F32)))
                   - jnp.exp(jnp.sum(lam_q2[j].astype(F32) * lam_k2[j].astype(F32))) + lam_init)
            w_qkv = _diff_qkv_weights(w_qkv_c[j])
            w_o = w_o_c[j].astype(BF16)
            q, k, v = prompt_proj if prompt_proj else mm_multi(xp, w_qkv, tm=512)
            o = diff_attn_prompt_core(q, k, v, bp, rel_bias, lam, lam_init, subln_g_c[j])
            xp = proj_ln([o], [w_o], xp, g0, b0)
            p_k.append(k.reshape(bp, t, H_C, 2 * DH_C))
            p_v.append(v.reshape(bp, t, H_C, VD_C))
            q, k, v = mm_multi(xs, w_qkv)
            o = diff_attn_sample_core(q, k, v, cache_diff_k, cache_diff_v, page_table, j, rel_bias, lam, lam_init,
                                      subln_g_c[j])
            xs = proj_ln([o], [w_o], xs, g0, b0)
            s_k.append(k.reshape(bs, 1, H_C, 2 * DH_C))
            s_v.append(v.reshape(bs, 1, H_C, VD_C))
        w_q, w_o = w_xq[layer].astype(BF16), w_xo[layer].astype(BF16)
        w_r = jnp.pad(w_router[layer].astype(F32), ((0, 0), (0, LANES - N_EXPERTS)))
        w_r_hi = w_r.astype(BF16)
        w_r = jnp.stack([w_r_hi, (w_r - w_r_hi.astype(F32)).astype(BF16)])
        b_r = jnp.pad(b_router[layer].astype(F32), (0, LANES - N_EXPERTS)).reshape(1, LANES)
        mk, mv = mm_multi(mem2d, (w_xkv[layer][:, :X_W].astype(BF16), w_xkv[layer][:, X_W:].astype(BF16)))
        p_mk.append(mk.reshape(bp, N_MEM, H_X, DH_X))
        p_mv.append(mv.reshape(bp, N_MEM, H_X, DH_X))
        xp, xp_tiles, idx_p, gate_p = xattn_prompt(xp, mk.reshape(bp, N_MEM, X_W), mv.reshape(bp, N_MEM, X_W),
                                                   w_q, w_o, g1, b1, w_r, b_r)
        (q,) = mm_multi(xs, (w_q,))
        o = xattn_sample_core(q, cmk, cmv, off=layer * bs)
        xs, xs_tiles, idx_s, gate_s = proj_ln_route(o, w_o, xs, g1, b1, w_r, b_r)
        nxt = layer + 1
        w_next = _diff_qkv_weights(w_qkv_c[nxt // 2]) if nxt < DEPTH and nxt % 2 == 1 else ()
        xp, prompt_proj = moe_ln(xp, xp_tiles, idx_p, gate_p, wm_in, bm_in, wm_out, bm_out, g2, b2, MOE_BLOCK_PROMPT,
                                 e_off=layer * N_EXPERTS, w_next=w_next)
        xs, _ = moe_ln(xs, xs_tiles, idx_s, gate_s, wm_in, bm_in, wm_out, bm_out, g2, b2, MOE_BLOCK_SAMPLE,
                       e_off=layer * N_EXPERTS)

    return (xp.reshape(bp, t, d), xs.reshape(bs, 1, d),
            jnp.stack(p_S), jnp.stack(p_conv), jnp.stack(p_C), jnp.stack(p_n), jnp.stack(p_m),
            jnp.stack(p_k, axis=1), jnp.stack(p_v, axis=1), jnp.stack(p_mk), jnp.stack(p_mv),
            jnp.stack(s_S), jnp.stack(s_conv), jnp.stack(s_C), jnp.stack(s_n), jnp.stack(s_m),
            jnp.stack(s_k, axis=1), jnp.stack(s_v, axis=1))
```
